```python
import math
import jax, jax.numpy as jnp
from jax import lax
import numpy as np

D_MODEL = 1024
BATCH = 4
SEQ = 8192
DEPTH = 2

CHUNK = 64
EPS = 1e-6
NEG = -1e30

A_HEADS = 4
A_DH = 128
A_WIDTH = A_HEADS * A_DH
CONV_W = 4
GATE_CAP = 15.0

B_HEADS = 8
B_DH = 64
B_WIDTH = B_HEADS * B_DH
B_LEFT_CHUNKS = 8
B_BAND = (B_LEFT_CHUNKS + 1) * CHUNK
B_MAX_REL = 256
B_REL_SIZE = 2 * B_MAX_REL + 1

C_HEADS = 4
C_DQK = 64
C_DV = 2 * C_DQK
C_WIDTH = C_HEADS * C_DV
Q_BLOCK = 128

T5_BUCKETS = 32
T5_MAX_DIST = 1024

N_BRANCH = 3
BRANCH_WIDTH = 512

N_GROUPS = 4
EXPERTS_PER_GROUP = 8
N_EXPERTS = N_GROUPS * EXPERTS_PER_GROUP
TOP_K = 2
D_EXPERT = D_MODEL // 4

IN_SIZES = (A_WIDTH, A_WIDTH, A_WIDTH, A_WIDTH, A_HEADS, A_HEADS,
            B_WIDTH, B_WIDTH, B_WIDTH,
            2 * C_HEADS * C_DQK, 2 * C_HEADS * C_DQK, C_WIDTH,
            N_BRANCH * D_MODEL)
IN_COLS = sum(IN_SIZES)
IN_SPLITS = tuple(int(v) for v in np.cumsum(IN_SIZES)[:-1])

kernel_name = 'hybrid_mlstm_bandattn_diffattn_hiermoe'


def rmsnorm(x, g):
    xf = x.astype(jnp.float32)
    xf = xf * lax.rsqrt(jnp.mean(xf * xf, axis=-1, keepdims=True) + EPS)
    return xf.astype(x.dtype) * g


def causal_conv(u, w, b):
    K = w.shape[0]
    S = u.shape[1]
    up = jnp.pad(u, ((0, 0), (K - 1, 0), (0, 0)))
    y = b + w[0] * u
    for j in range(1, K):
        y = y + w[j] * up[:, K - 1 - j:K - 1 - j + S]
    return y


def to_chunks(t):
    Bn, S, H = t.shape[:3]
    t = t.reshape((Bn, S // CHUNK, CHUNK, H) + t.shape[3:])
    return jnp.moveaxis(t, (1, 3), (0, 2))


def mlstm_chunkwise(q, k, v, ig, lf):
    Bn, S, H, D = q.shape
    in_dtype = v.dtype
    f32 = jnp.float32
    q = q.astype(f32)
    k = k.astype(f32) * (D ** -0.5)
    v = v.astype(f32)
    causal = jnp.tril(jnp.ones((CHUNK, CHUNK), dtype=bool))

    def step(carry, inp):
        C, n, m = carry
        qc, kc, vc, igc, lfc = inp
        b = jnp.cumsum(lfc, axis=-1)
        b_last = b[..., -1]
        dmat = jnp.where(causal, b[..., :, None] - b[..., None, :] + igc[..., None, :], NEG)
        inter = b + m[..., None]
        m_t = jnp.maximum(inter, jnp.max(dmat, axis=-1))
        w_inter = jnp.exp(inter - m_t)
        s = jnp.einsum('bhld,bhsd->bhls', qc, kc) * jnp.exp(dmat - m_t[..., None])
        num = w_inter[..., None] * jnp.einsum('bhld,bhde->bhle', qc, C) + jnp.einsum('bhls,bhse->bhle', s, vc)
        den = w_inter * jnp.einsum('bhld,bhd->bhl', qc, n) + jnp.sum(s, axis=-1)
        h = num / jnp.maximum(jnp.abs(den), jnp.exp(-m_t))[..., None]
        g_end = b_last[..., None] - b + igc
        m_new = jnp.maximum(b_last + m, jnp.max(g_end, axis=-1))
        dec = jnp.exp(b_last + m - m_new)
        w_end = jnp.exp(g_end - m_new[..., None])
        C_new = dec[..., None, None] * C + jnp.einsum('bhl,bhld,bhle->bhde', w_end, kc, vc)
        n_new = dec[..., None] * n + jnp.einsum('bhl,bhld->bhd', w_end, kc)
        return (C_new, n_new, m_new), h

    init = (jnp.zeros((Bn, H, D, D), f32), jnp.zeros((Bn, H, D), f32), jnp.zeros((Bn, H), f32))
    xs = (to_chunks(q), to_chunks(k), to_chunks(v), to_chunks(ig), to_chunks(lf))
    _, h = lax.scan(step, init, xs)
    h = jnp.moveaxis(h, (0, 2), (1, 3))
    return h.reshape(Bn, S, H, D).astype(in_dtype)


def chunk_band_attention(q, k, v, rel_table):
    Bn, S, H, D = q.shape
    nc = S // CHUNK
    pad = B_LEFT_CHUNKS * CHUNK
    kp = jnp.pad(k, ((0, 0), (pad, 0), (0, 0), (0, 0)))
    vp = jnp.pad(v, ((0, 0), (pad, 0), (0, 0), (0, 0)))
    qpos = jnp.arange(CHUNK)
    kpos = jnp.arange(B_BAND) - pad
    rel = jnp.clip(qpos[:, None] - kpos[None, :], -B_MAX_REL, B_MAX_REL) + B_MAX_REL
    bias = jnp.transpose(rel_table[rel], (2, 0, 1)).astype(jnp.float32)
    scale = D ** -0.5

    def one_chunk(c):
        start = c * CHUNK
        qc = lax.dynamic_slice_in_dim(q, start, CHUNK, axis=1)
        kc = lax.dynamic_slice_in_dim(kp, start, B_BAND, axis=1)
        vc = lax.dynamic_slice_in_dim(vp, start, B_BAND, axis=1)
        s = jnp.einsum('blhd,bkhd->bhlk', qc, kc).astype(jnp.float32) * scale + bias
        s = jnp.where((start + kpos) >= 0, s, NEG)
        p = jax.nn.softmax(s, axis=-1).astype(v.dtype)
        return jnp.einsum('bhlk,bkhd->blhd', p, vc)

    out = lax.map(one_chunk, jnp.arange(nc))
    return jnp.moveaxis(out, 0, 1).reshape(Bn, S, H * D)


def t5_bucket(rel):
    nb = T5_BUCKETS // 2
    max_exact = nb // 2
    ret = (rel > 0).astype(jnp.int32) * nb
    n = jnp.abs(rel)
    large = max_exact + (jnp.log(jnp.maximum(n, max_exact).astype(jnp.float32) / max_exact)
                         / math.log(T5_MAX_DIST / max_exact) * (nb - max_exact)).astype(jnp.int32)
    large = jnp.minimum(large, nb - 1)
    return ret + jnp.where(n < max_exact, n, large)


def diff_attention(q1, q2, k1, k2, v, t5_table, lam):
    Bn, S, H, D = q1.shape
    nblk = S // Q_BLOCK
    kpos = jnp.arange(S)
    scale = D ** -0.5

    def one_block(i):
        start = i * Q_BLOCK
        qp = start + jnp.arange(Q_BLOCK)
        bias = jnp.moveaxis(t5_table[t5_bucket(kpos[None, :] - qp[:, None])], -1, 0).astype(jnp.float32)
        allowed = (kpos[None, :] // CHUNK) <= (qp[:, None] // CHUNK)

        def probs(qf, kf):
            qb = lax.dynamic_slice_in_dim(qf, start, Q_BLOCK, axis=1)
            s = jnp.einsum('bqhd,bkhd->bhqk', qb, kf).astype(jnp.float32) * scale + bias
            return jax.nn.softmax(jnp.where(allowed, s, NEG), axis=-1)

        a = (probs(q1, k1) - lam * probs(q2, k2)).astype(v.dtype)
        return jnp.einsum('bhqk,bkhe->bqhe', a, v)

    out = lax.map(one_block, jnp.arange(nblk))
    return jnp.moveaxis(out, 0, 1).reshape(Bn, S, H, v.shape[-1])


def token_mixers(x, layer, norm_g, w_in, conv_w, conv_b, gate_bias, a_norm_g,
                 b_qk_g, b_rel, c_qk_g, c_lam, c_sub_g, t5_table, w_br, w_o):
    Bn, S, _ = x.shape
    h = rmsnorm(x, norm_g)
    aq, ak, av, ao, ai, af, bq, bk, bv, cq, ck, cv, gates = jnp.split(h @ w_in, IN_SPLITS, axis=-1)

    def heads(t, n):
        return t.reshape(Bn, S, n, -1)

    qk = jax.nn.silu(causal_conv(jnp.concatenate([aq, ak], axis=-1), conv_w, conv_b))
    aq, ak = jnp.split(qk, 2, axis=-1)
    ig = GATE_CAP * jnp.tanh((ai + gate_bias[:A_HEADS]).astype(jnp.float32) / GATE_CAP)
    lf = jax.nn.log_sigmoid((af + gate_bias[A_HEADS:]).astype(jnp.float32))
    ha = mlstm_chunkwise(heads(aq, A_HEADS), heads(ak, A_HEADS), heads(av, A_HEADS), ig, lf)
    ha = rmsnorm(ha, a_norm_g).reshape(Bn, S, A_WIDTH) * jax.nn.sigmoid(ao)

    bq = rmsnorm(heads(bq, B_HEADS), b_qk_g[0])
    bk = rmsnorm(heads(bk, B_HEADS), b_qk_g[1])
    hb = chunk_band_attention(bq, bk, heads(bv, B_HEADS), b_rel)

    cq = rmsnorm(cq.reshape(Bn, S, C_HEADS, 2, C_DQK), c_qk_g[0])
    ck = rmsnorm(ck.reshape(Bn, S, C_HEADS, 2, C_DQK), c_qk_g[1])
    lam_init = 0.8 - 0.6 * math.exp(-0.3 * layer)
    lf32 = c_lam.astype(jnp.float32)
    lam = jnp.exp(jnp.sum(lf32[0] * lf32[1])) - jnp.exp(jnp.sum(lf32[2] * lf32[3])) + lam_init
    hc = diff_attention(cq[..., 0, :], cq[..., 1, :], ck[..., 0, :], ck[..., 1, :],
                        heads(cv, C_HEADS), t5_table, lam)
    hc = (rmsnorm(hc, c_sub_g) * (1.0 - lam_init)).reshape(Bn, S, C_WIDTH)

    g = jax.nn.sigmoid(gates).reshape(Bn, S, N_BRANCH, D_MODEL)
    y = g[..., 0, :] * (ha @ w_br[0]) + g[..., 1, :] * (hb @ w_br[1]) + g[..., 2, :] * (hc @ w_br[2])
    return y @ w_o


def hier_moe(h, w_group, b_group, w_router, b_router, w_gate, w_up, w_down):
    Bn, S, D = h.shape
    N = Bn * S
    t = h.reshape(N, D)
    rows = jnp.arange(N)
    glog = (t @ w_group + b_group).astype(jnp.float32)
    g_idx = jnp.argmax(glog, axis=-1)
    p_g = jax.nn.softmax(glog, axis=-1)[rows, g_idx]
    elog = (t @ w_router + b_router).astype(jnp.float32).reshape(N, N_GROUPS, EXPERTS_PER_GROUP)
    ep = jax.nn.softmax(elog[rows, g_idx], axis=-1)
    top_p, top_i = lax.top_k(ep, TOP_K)
    top_p = top_p / jnp.sum(top_p, axis=-1, keepdims=True)
    weights = p_g[:, None] * top_p
    expert_id = g_idx[:, None] * EXPERTS_PER_GROUP + top_i
    combine = jnp.sum(jax.nn.one_hot(expert_id, N_EXPERTS, dtype=jnp.float32) * weights[..., None], axis=1)
    combine = combine.astype(t.dtype)
    out = jnp.zeros_like(t)
    for e in range(N_EXPERTS):
        he = jax.nn.silu(t @ w_gate[e]) * (t @ w_up[e])
        out = out + combine[:, e:e + 1] * (he @ w_down[e])
    return out.reshape(Bn, S, D)


def setup_inputs(seed: int = 0) -> dict:
    key = jax.random.key(seed)
    ks = jax.random.split(key, 26)
    L = DEPTH

    def nrm(k, shape, scale):
        return scale * jax.random.normal(k, shape, jnp.float32)

    def gain(k, shape):
        return 1.0 + nrm(k, shape, 0.02)

    i_bias = nrm(ks[5], (L, A_HEADS), 0.1) - 1.0
    f_bias = jnp.linspace(3.0, 6.0, A_HEADS)[None, :] + nrm(ks[6], (L, A_HEADS), 0.1)
    return {
        'x': nrm(ks[0], (BATCH, SEQ, D_MODEL), 1.0),
        'norm1_g': gain(ks[1], (L, D_MODEL)),
        'w_in': nrm(ks[2], (L, D_MODEL, IN_COLS), D_MODEL ** -0.5),
        'a_conv_w': nrm(ks[3], (L, CONV_W, 2 * A_WIDTH), CONV_W ** -0.5),
        'a_conv_b': nrm(ks[4], (L, 2 * A_WIDTH), 0.02),
        'a_gate_bias': jnp.concatenate([i_bias, f_bias], axis=-1),
        'a_out_norm_g': gain(ks[7], (L, A_HEADS, A_DH)),
        'b_qk_norm_g': gain(ks[8], (L, 2, B_DH)),
        'b_rel_bias': nrm(ks[9], (L, B_REL_SIZE, B_HEADS), 0.2),
        'c_qk_norm_g': gain(ks[10], (L, 2, C_DQK)),
        'c_lambda': nrm(ks[11], (L, 4, C_DQK), 0.1),
        'c_sub_norm_g': gain(ks[12], (L, C_DV)),
        't5_bias': nrm(ks[13], (T5_BUCKETS, C_HEADS), 0.2),
        'w_branch': nrm(ks[14], (L, N_BRANCH, BRANCH_WIDTH, D_MODEL), BRANCH_WIDTH ** -0.5),
        'w_out': nrm(ks[15], (L, D_MODEL, D_MODEL), D_MODEL ** -0.5),
        'norm2_g': gain(ks[16], (L, D_MODEL)),
        'w_group': nrm(ks[17], (L, D_MODEL, N_GROUPS), D_MODEL ** -0.5),
        'b_group': nrm(ks[18], (L, N_GROUPS), 0.01),
        'w_router': nrm(ks[19], (L, D_MODEL, N_EXPERTS), D_MODEL ** -0.5),
        'b_router': nrm(ks[20], (L, N_EXPERTS), 0.01),
        'w_e_gate': nrm(ks[21], (L, N_EXPERTS, D_MODEL, D_EXPERT), D_MODEL ** -0.5),
        'w_e_up': nrm(ks[22], (L, N_EXPERTS, D_MODEL, D_EXPERT), D_MODEL ** -0.5),
        'w_e_down': nrm(ks[23], (L, N_EXPERTS, D_EXPERT, D_MODEL), D_EXPERT ** -0.5),
    }


def reference(x, norm1_g, w_in, a_conv_w, a_conv_b, a_gate_bias, a_out_norm_g,
              b_qk_norm_g, b_rel_bias, c_qk_norm_g, c_lambda, c_sub_norm_g, t5_bias,
              w_branch, w_out, norm2_g, w_group, b_group, w_router, b_router,
              w_e_gate, w_e_up, w_e_down):
    for l in range(DEPTH):
        x = x + token_mixers(x, l, norm1_g[l], w_in[l], a_conv_w[l], a_conv_b[l], a_gate_bias[l],
                             a_out_norm_g[l], b_qk_norm_g[l], b_rel_bias[l], c_qk_norm_g[l],
                             c_lambda[l], c_sub_norm_g[l], t5_bias, w_branch[l], w_out[l])
        x = x + hier_moe(rmsnorm(x, norm2_g[l]), w_group[l], b_group[l], w_router[l], b_router[l],
                         w_e_gate[l], w_e_up[l], w_e_down[l])
    return x
```

```python
import functools
import math

import numpy as np
import jax
import jax.numpy as jnp
from jax import lax
from jax.experimental import pallas as pl
from jax.experimental.pallas import tpu as pltpu

F32 = jnp.float32
BF16 = jnp.bfloat16

D_MODEL = 1024
CHUNK = 64
EPS = 1e-6
NEG = -1e30

A_HEADS = 4
A_DH = 128
A_WIDTH = A_HEADS * A_DH
CONV_W = 4
GATE_CAP = 15.0

B_HEADS = 8
B_DH = 64
B_WIDTH = B_HEADS * B_DH
B_LEFT_CHUNKS = 8
B_MAX_REL = 256

C_HEADS = 4
C_DQK = 64
C_DV = 2 * C_DQK
C_WIDTH = C_HEADS * C_DV

T5_BUCKETS = 32
T5_MAX_DIST = 1024

N_BRANCH = 3
BRANCH_WIDTH = 512

N_GROUPS = 4
EXPERTS_PER_GROUP = 8
N_EXPERTS = N_GROUPS * EXPERTS_PER_GROUP
D_EXPERT = D_MODEL // 4

LANES = 128
SEG = 512
N_SEG = 16
VMEM_LIMIT = 48 * 1024 * 1024

SEG_AO = 3
SEG_BQ, SEG_BK, SEG_BV = 4, 5, 6
SEG_CQ, SEG_CK, SEG_CV = 7, 8, 9
SEG_GATES = 10

MLSTM_L = 128
BAND_TQ = 128
BAND_NKB = 1 + (B_LEFT_CHUNKS * CHUNK) // BAND_TQ
DIFF_T = 512


def _cparams(sem):
    return pltpu.CompilerParams(dimension_semantics=sem, vmem_limit_bytes=VMEM_LIMIT)


def _inproj_kernel(x_ref, g_ref, w_ref, wif_ref, gain_ref, bd_ref, p_ref, gif_ref, xn_ref):
    j = pl.program_id(1)

    @pl.when(j == 0)
    def _():
        xf = x_ref[...]
        xn = xf * lax.rsqrt(jnp.mean(xf * xf, axis=-1, keepdims=True) + EPS) * g_ref[...]
        xn_ref[...] = xn.astype(BF16)
        gif_ref[...] = jnp.dot(xn_ref[...], wif_ref[...], preferred_element_type=F32)

    acc = jnp.dot(xn_ref[...], w_ref[...], preferred_element_type=F32)
    is_norm = (j == SEG_BQ) | (j == SEG_BK) | (j == SEG_CQ) | (j == SEG_CK)
    is_sig = (j == SEG_AO) | (j >= SEG_GATES)

    @pl.when(is_norm)
    def _():
        ssq = jnp.dot((acc * acc).astype(BF16), bd_ref[...], preferred_element_type=F32)
        p_ref[...] = (acc * lax.rsqrt(ssq * (1.0 / 64.0) + EPS) * gain_ref[0]).astype(BF16)

    @pl.when(is_sig)
    def _():
        p_ref[...] = jax.nn.sigmoid(acc).astype(BF16)

    @pl.when(jnp.logical_not(is_norm | is_sig))
    def _():
        p_ref[...] = acc.astype(BF16)


def _inproj(x2, g, w, wif, gain, bd, tm):
    n = x2.shape[0]
    return pl.pallas_call(
        _inproj_kernel,
        grid=(n // tm, N_SEG),
        in_specs=[
            pl.BlockSpec((tm, D_MODEL), lambda i, j: (i, 0)),
            pl.BlockSpec((1, D_MODEL), lambda i, j: (0, 0)),
            pl.BlockSpec((D_MODEL, SEG), lambda i, j: (0, j)),
            pl.BlockSpec((D_MODEL, LANES), lambda i, j: (0, 0)),
            pl.BlockSpec((1, 1, SEG), lambda i, j: (j, 0, 0)),
            pl.BlockSpec((SEG, SEG), lambda i, j: (0, 0)),
        ],
        out_specs=[
            pl.BlockSpec((tm, SEG), lambda i, j: (i, j)),
            pl.BlockSpec((tm, LANES), lambda i, j: (i, 0)),
        ],
        out_shape=[
            jax.ShapeDtypeStruct((n, N_SEG * SEG), BF16),
            jax.ShapeDtypeStruct((n, LANES), F32),
        ],
        scratch_shapes=[pltpu.VMEM((tm, D_MODEL), BF16)],
        compiler_params=_cparams(("parallel", "arbitrary")),
        name="inproj",
    )(x2, g, w, wif, gain, bd)


def _log_sigmoid(z):
    return jnp.minimum(z, 0.0) - jnp.log(1.0 + jnp.exp(-jnp.abs(z)))


def _split3(a):
    hi = a.astype(BF16)
    r1 = a - hi.astype(F32)
    mid = r1.astype(BF16)
    lo = (r1 - mid.astype(F32)).astype(BF16)
    return hi, mid, lo


def _mlstm_kernel(aq_ref, ak_ref, av_ref, ao_ref, gif_ref, gift_ref, cw_ref, cb_ref,
                  gbr_ref, gbc_ref, ag_ref, out_ref, ubuf, c_ref, n_ref, m_ref):
    L = MLSTM_L
    c = pl.program_id(1)

    @pl.when(c == 0)
    def _():
        ubuf[0:8, :] = jnp.zeros((8, 2 * A_WIDTH), F32)
        c_ref[...] = jnp.zeros_like(c_ref)
        n_ref[...] = jnp.zeros_like(n_ref)
        m_ref[...] = jnp.zeros_like(m_ref)

    @pl.when(c > 0)
    def _():
        ubuf[0:8, :] = ubuf[L:L + 8, :]

    ubuf[8:L + 8, 0:A_WIDTH] = aq_ref[...].astype(F32)
    ubuf[8:L + 8, A_WIDTH:2 * A_WIDTH] = ak_ref[...].astype(F32)
    y = cb_ref[...] + cw_ref[0:1, :] * ubuf[8:L + 8, :]
    for t in range(1, CONV_W):
        y = y + cw_ref[t:t + 1, :] * ubuf[8 - t:8 - t + L, :]
    qk = y * jax.nn.sigmoid(y)

    zc = gif_ref[...] + gbr_ref[...]
    ig_c = GATE_CAP * jnp.tanh(zc * (1.0 / GATE_CAP))
    lf_c = _log_sigmoid(zc)
    zr = gift_ref[...] + gbc_ref[...]
    ig_r = GATE_CAP * jnp.tanh(zr * (1.0 / GATE_CAP))
    lf_r = _log_sigmoid(zr)

    row = lax.broadcasted_iota(jnp.int32, (L, L), 0)
    col = lax.broadcasted_iota(jnp.int32, (L, L), 1)
    causal = col <= row
    tril = jnp.where(causal, 1.0, 0.0).astype(BF16)
    triu = jnp.where(row <= col, 1.0, 0.0).astype(BF16)
    b_c = sum(jnp.dot(tril, piece, preferred_element_type=F32) for piece in _split3(lf_c))
    b_r = sum(jnp.dot(piece, triu, preferred_element_type=F32) for piece in _split3(lf_r))

    for h in range(A_HEADS):
        sl = slice(h * A_DH, (h + 1) * A_DH)
        q = qk[:, sl]
        k = qk[:, A_WIDTH + h * A_DH:A_WIDTH + (h + 1) * A_DH] * (A_DH ** -0.5)
        qb = q.astype(BF16)
        kb = k.astype(BF16)
        v = av_ref[:, sl]
        bcol = b_c[:, A_HEADS + h:A_HEADS + h + 1]
        brow = b_r[A_HEADS + h:A_HEADS + h + 1, :]
        igcol = ig_c[:, h:h + 1]
        igrow = ig_r[h:h + 1, :]
        b_last = bcol[L - 1:L, :]
        C = c_ref[h]
        nvec = n_ref[h]
        m_prev = m_ref[h][:, 0:1]

        dmat = jnp.where(causal, bcol - brow + igrow, NEG)
        inter = bcol + m_prev
        m_t = jnp.maximum(inter, jnp.max(dmat, axis=-1, keepdims=True))
        w_inter = jnp.exp(inter - m_t)
        s = lax.dot_general(qb, kb, (((1,), (1,)), ((), ())), preferred_element_type=F32)
        s = s * jnp.exp(dmat - m_t)
        num = w_inter * jnp.dot(qb, C.astype(BF16), preferred_element_type=F32)
        num = num + jnp.dot(s.astype(BF16), v, preferred_element_type=F32)
        den = w_inter * jnp.sum(q * nvec, axis=-1, keepdims=True) + jnp.sum(s, axis=-1, keepdims=True)
        hh = num / jnp.maximum(jnp.abs(den), jnp.exp(-m_t))

        g_end = b_last - bcol + igcol
        m_new = jnp.maximum(b_last + m_prev, jnp.max(g_end, axis=0, keepdims=True))
        dec = jnp.exp(b_last + m_prev - m_new)
        kw = jnp.exp(g_end - m_new) * k
        c_ref[h] = dec * C + lax.dot_general(kw.astype(BF16), v, (((0,), (0,)), ((), ())),
                                             preferred_element_type=F32)
        n_ref[h] = dec * nvec + jnp.sum(kw, axis=0, keepdims=True)
        m_ref[h] = jnp.broadcast_to(m_new, (1, LANES))

        hn = hh * lax.rsqrt(jnp.mean(hh * hh, axis=-1, keepdims=True) + EPS) * ag_ref[:, sl]
        out_ref[:, sl] = (hn * ao_ref[:, sl].astype(F32)).astype(BF16)


def _mlstm(p, gif, gift, cw, cb, gbr, gbc, ag, bsz, seq):
    L = MLSTM_L
    nc = seq // L
    n = bsz * seq

    def seg(j):
        return pl.BlockSpec((L, SEG), lambda b, c: (b * nc + c, j))

    def full(shape):
        return pl.BlockSpec(shape, lambda b, c: (0,) * len(shape))

    return pl.pallas_call(
        _mlstm_kernel,
        grid=(bsz, nc),
        in_specs=[
            seg(0), seg(1), seg(2), seg(SEG_AO),
            pl.BlockSpec((L, LANES), lambda b, c: (b * nc + c, 0)),
            pl.BlockSpec((8, L), lambda b, c: (0, b * nc + c)),
            full((CONV_W, 2 * A_WIDTH)), full((1, 2 * A_WIDTH)),
            full((1, LANES)), full((8, 1)), full((1, A_WIDTH)),
        ],
        out_specs=pl.BlockSpec((L, A_WIDTH), lambda b, c: (b * nc + c, 0)),
        out_shape=jax.ShapeDtypeStruct((n, A_WIDTH), BF16),
        scratch_shapes=[
            pltpu.VMEM((L + 8, 2 * A_WIDTH), F32),
            pltpu.VMEM((A_HEADS, A_DH, A_DH), F32),
            pltpu.VMEM((A_HEADS, 1, A_DH), F32),
            pltpu.VMEM((A_HEADS, 1, LANES), F32),
        ],
        compiler_params=_cparams(("parallel", "arbitrary")),
        name="mlstm",
    )(p, p, p, p, gif, gift, cw, cb, gbr, gbc, ag)


def _band_kernel(*refs):
    nkb = BAND_NKB
    q_ref = refs[0]
    k_refs = refs[1:1 + nkb]
    v_refs = refs[1 + nkb:1 + 2 * nkb]
    bias_ref = refs[1 + 2 * nkb]
    out_ref = refs[2 + 2 * nkb]
    tq = BAND_TQ
    nk = nkb * tq
    i = pl.program_id(1)

    k_all = jnp.concatenate([r[...] for r in k_refs], axis=0)
    v_all = jnp.concatenate([r[...] for r in v_refs], axis=0)
    kidx = lax.broadcasted_iota(jnp.int32, (1, nk), 1)
    valid = (kidx + (i - (nkb - 1)) * tq) >= 0
    lane = lax.broadcasted_iota(jnp.int32, (tq, LANES), 1)
    lo = lane < B_DH

    for p in range(B_HEADS // 2):
        sl = slice(p * LANES, (p + 1) * LANES)
        qp = q_ref[:, sl]
        zero = jnp.zeros_like(qp)
        qbd = jnp.concatenate([jnp.where(lo, qp, zero), jnp.where(lo, zero, qp)], axis=0)
        s = lax.dot_general(qbd, k_all[:, sl], (((1,), (1,)), ((), ())), preferred_element_type=F32)
        s = jnp.where(valid, s + bias_ref[p], NEG)
        m = jnp.max(s, axis=-1, keepdims=True)
        e = jnp.exp(s - m)
        l = jnp.sum(e, axis=-1, keepdims=True)
        pv = jnp.dot(e.astype(BF16), v_all[:, sl], preferred_element_type=F32)
        o = pv / l
        out_ref[:, sl] = jnp.where(lo, o[0:tq], o[tq:2 * tq]).astype(BF16)


def _band(p, bias, bsz, seq):
    tq = BAND_TQ
    nkb = BAND_NKB
    nq = seq // tq
    n = bsz * seq

    def kv(j, d):
        return pl.BlockSpec((tq, SEG), lambda b, i: (b * nq + jnp.maximum(i - d, 0), j))

    in_specs = [pl.BlockSpec((tq, SEG), lambda b, i: (b * nq + i, SEG_BQ))]
    in_specs += [kv(SEG_BK, d) for d in range(nkb - 1, -1, -1)]
    in_specs += [kv(SEG_BV, d) for d in range(nkb - 1, -1, -1)]
    in_specs += [pl.BlockSpec(bias.shape, lambda b, i: (0, 0, 0))]
    return pl.pallas_call(
        _band_kernel,
        grid=(bsz, nq),
        in_specs=in_specs,
        out_specs=pl.BlockSpec((tq, B_WIDTH), lambda b, i: (b * nq + i, 0)),
        out_shape=jax.ShapeDtypeStruct((n, B_WIDTH), BF16),
        compiler_params=_cparams(("parallel", "parallel")),
        name="band_attn",
    )(*([p] * (1 + 2 * nkb)), bias)


def _band_bias(b_rel):
    tq = BAND_TQ
    nk = BAND_NKB * tq
    qpos = np.arange(tq)
    kpos = np.arange(nk) - (nk - tq)
    rel = np.clip(qpos[:, None] - kpos[None, :], -B_MAX_REL, B_MAX_REL) + B_MAX_REL
    qc = qpos[:, None] // CHUNK
    kc = np.floor_divide(kpos[None, :], CHUNK)
    allowed = (kc <= qc) & (kc >= qc - B_LEFT_CHUNKS)
    bias = jnp.transpose(b_rel[rel], (2, 0, 1)).astype(F32)
    bias = jnp.where(allowed[None], bias, NEG)
    return bias.reshape(B_HEADS // 2, 2 * tq, nk)


def _diff_kernel(lam_ref, q_ref, k_ref, vt_ref, bias_ref, g_ref, out_ref,
                 qbd_ref, acc_ref, m_ref, l_ref, *, nd):
    T = DIFF_T
    qi = pl.program_id(2)
    q = q_ref[...]
    lane = lax.broadcasted_iota(jnp.int32, (T, LANES), 1)
    zero = jnp.zeros_like(q)
    qbd_ref[0:T, :] = jnp.where(lane < C_DQK, q, zero)
    qbd_ref[T:2 * T, :] = jnp.where(lane < C_DQK, zero, q)
    m_ref[...] = jnp.full(m_ref.shape, NEG, F32)
    l_ref[...] = jnp.zeros_like(l_ref)
    acc_ref[...] = jnp.zeros_like(acc_ref)

    def body(j, carry):
        k = k_ref[pl.ds(pl.multiple_of(j * T, T), T), :]
        s = lax.dot_general(k, qbd_ref[...], (((1,), (1,)), ((), ())), preferred_element_type=F32)
        bias = bias_ref[jnp.minimum(qi - j, nd - 1)]
        vt = vt_ref[j]
        for mp in range(2):
            st = s[:, mp * T:(mp + 1) * T] + bias
            m_old = m_ref[mp]
            m_new = jnp.maximum(m_old, jnp.max(st, axis=0, keepdims=True))
            alpha = jnp.exp(m_old - m_new)
            pr = jnp.exp(st - m_new)
            l_ref[mp] = alpha * l_ref[mp] + jnp.sum(pr, axis=0, keepdims=True)
            acc_ref[mp] = alpha * acc_ref[mp] + jnp.dot(vt, pr.astype(BF16), preferred_element_type=F32)
            m_ref[mp] = m_new
        return carry

    lax.fori_loop(0, qi + 1, body, 0)
    o = acc_ref[0] / l_ref[0] - lam_ref[0] * (acc_ref[1] / l_ref[1])
    on = o * lax.rsqrt(jnp.mean(o * o, axis=0, keepdims=True) + EPS) * g_ref[...]
    out_ref[...] = on.astype(BF16)


def _diff_nd(nq):
    d_sat = -(-(T5_MAX_DIST - 1 + DIFF_T) // DIFF_T)
    return min(d_sat, nq - 1) + 1


def _diff(p, vt, bias, lam, gsub, bsz, seq):
    T = DIFF_T
    nq = seq // T
    nd = bias.shape[1]
    qcol = SEG_CQ * SEG // LANES
    kcol = SEG_CK * SEG // LANES
    return pl.pallas_call(
        functools.partial(_diff_kernel, nd=nd),
        grid=(bsz, C_HEADS, nq),
        in_specs=[
            pl.BlockSpec(memory_space=pltpu.SMEM),
            pl.BlockSpec((T, LANES), lambda b, h, i: (b * nq + i, qcol + h)),
            pl.BlockSpec((seq, LANES), lambda b, h, i: (b, kcol + h)),
            pl.BlockSpec((None, None, nq, C_DV, T), lambda b, h, i: (b, h, 0, 0, 0)),
            pl.BlockSpec((None, nd, T, T), lambda b, h, i: (h, 0, 0, 0)),
            pl.BlockSpec((C_DV, 1), lambda b, h, i: (0, 0)),
        ],
        out_specs=pl.BlockSpec((None, C_DV, T), lambda b, h, i: (b, h, i)),
        out_shape=jax.ShapeDtypeStruct((bsz, C_WIDTH, seq), BF16),
        scratch_shapes=[
            pltpu.VMEM((2 * T, LANES), BF16),
            pltpu.VMEM((2, C_DV, T), F32),
            pltpu.VMEM((2, 1, T), F32),
            pltpu.VMEM((2, 1, T), F32),
        ],
        compiler_params=_cparams(("parallel", "parallel", "arbitrary")),
        name="diff_attn",
    )(lam, p, p, vt, bias, gsub)


def _t5_bucket(rel):
    nb = T5_BUCKETS // 2
    max_exact = nb // 2
    ret = (rel > 0).astype(jnp.int32) * nb
    n = jnp.abs(rel)
    large = max_exact + (jnp.log(jnp.maximum(n, max_exact).astype(F32) / max_exact)
                         / math.log(T5_MAX_DIST / max_exact) * (nb - max_exact)).astype(jnp.int32)
    large = jnp.minimum(large, nb - 1)
    return ret + jnp.where(n < max_exact, n, large)


def _diff_bias(t5_table, nd):
    T = DIFF_T
    a = np.arange(T)[:, None]
    c = np.arange(T)[None, :]
    tiles = []
    for d in range(nd):
        rel = jnp.asarray(-d * T + a - c, jnp.int32)
        tile = jnp.moveaxis(t5_table[_t5_bucket(rel)], -1, 0).astype(F32)
        if d == 0:
            tile = jnp.where(jnp.asarray((a // CHUNK) <= (c // CHUNK))[None], tile, NEG)
        tiles.append(tile)
    return jnp.stack(tiles, axis=1)


def _merge_kernel(ha_ref, hb_ref, hc_ref, g0_ref, g1_ref, g2_ref, x_ref, wb_ref, wo_ref,
                  n2_ref, wr_ref, br_ref, x1_ref, h2_ref, lg_ref):
    y = g0_ref[...].astype(F32) * jnp.dot(ha_ref[...], wb_ref[0], preferred_element_type=F32)
    y = y + g1_ref[...].astype(F32) * jnp.dot(hb_ref[...], wb_ref[1], preferred_element_type=F32)
    y = y + g2_ref[...].astype(F32) * jnp.dot(hc_ref[...], wb_ref[2], preferred_element_type=F32)
    x1 = x_ref[...] + jnp.dot(y.astype(BF16), wo_ref[...], preferred_element_type=F32)
    x1_ref[...] = x1
    h2 = x1 * lax.rsqrt(jnp.mean(x1 * x1, axis=-1, keepdims=True) + EPS) * n2_ref[...]
    h2b = h2.astype(BF16)
    h2_ref[...] = h2b
    lg_ref[...] = jnp.dot(h2b, wr_ref[...], preferred_element_type=F32) + br_ref[...]


def _merge(ha, hb, hc, p, x2, wb, wo, n2, wr, br, tm):
    n = x2.shape[0]
    gcol = SEG_GATES * SEG // D_MODEL

    def rows(width, col=0):
        return pl.BlockSpec((tm, width), lambda i: (i, col))

    def full(shape):
        return pl.BlockSpec(shape, lambda i: (0,) * len(shape))

    return pl.pallas_call(
        _merge_kernel,
        grid=(n // tm,),
        in_specs=[
            rows(BRANCH_WIDTH), rows(BRANCH_WIDTH), rows(BRANCH_WIDTH),
            rows(D_MODEL, gcol), rows(D_MODEL, gcol + 1), rows(D_MODEL, gcol + 2),
            rows(D_MODEL),
            full((N_BRANCH, BRANCH_WIDTH, D_MODEL)), full((D_MODEL, D_MODEL)),
            full((1, D_MODEL)), full((D_MODEL, LANES)), full((1, LANES)),
        ],
        out_specs=[rows(D_MODEL), rows(D_MODEL), rows(LANES)],
        out_shape=[
            jax.ShapeDtypeStruct((n, D_MODEL), F32),
            jax.ShapeDtypeStruct((n, D_MODEL), BF16),
            jax.ShapeDtypeStruct((n, LANES), F32),
        ],
        compiler_params=_cparams(("parallel",)),
        name="merge",
    )(ha, hb, hc, p, p, p, x2, wb, wo, n2, wr, br)


def _combine_weights(lg):
    lanef = lax.broadcasted_iota(jnp.int32, lg.shape, 1).astype(F32)
    big = 1e9
    is_g = (lanef >= N_EXPERTS) & (lanef < N_EXPERTS + N_GROUPS)
    gl = jnp.where(is_g, lg, -jnp.inf)
    gmax = jnp.max(gl, axis=-1, keepdims=True)
    g_idx = jnp.min(jnp.where(gl == gmax, lanef - N_EXPERTS, big), axis=-1, keepdims=True)
    p_g = 1.0 / jnp.sum(jnp.exp(gl - gmax), axis=-1, keepdims=True)
    in_grp = (lanef >= g_idx * EXPERTS_PER_GROUP) & (lanef < (g_idx + 1.0) * EXPERTS_PER_GROUP)
    el = jnp.where(in_grp, lg, -jnp.inf)
    ee = jnp.exp(el - jnp.max(el, axis=-1, keepdims=True))
    ep = ee / jnp.sum(ee, axis=-1, keepdims=True)
    ep = jnp.where(in_grp, ep, -1.0)
    v1 = jnp.max(ep, axis=-1, keepdims=True)
    i1 = jnp.min(jnp.where(ep == v1, lanef, big), axis=-1, keepdims=True)
    ep2 = jnp.where(lanef == i1, -1.0, ep)
    v2 = jnp.max(ep2, axis=-1, keepdims=True)
    i2 = jnp.min(jnp.where(ep2 == v2, lanef, big), axis=-1, keepdims=True)
    tot = v1 + v2
    return jnp.where(lanef == i1, p_g * (v1 / tot), 0.0) + jnp.where(lanef == i2, p_g * (v2 / tot), 0.0)


def _moe_kernel(x1_ref, h2_ref, lg_ref, wgu_ref, wd_ref, out_ref, acc_ref, comb_ref):
    e = pl.program_id(1)

    @pl.when(e == 0)
    def _():
        comb_ref[...] = _combine_weights(lg_ref[...])
        acc_ref[...] = x1_ref[...]

    gu = jnp.dot(h2_ref[...], wgu_ref[...], preferred_element_type=F32)
    g = gu[:, 0:D_EXPERT]
    he = (g * jax.nn.sigmoid(g) * gu[:, D_EXPERT:2 * D_EXPERT]).astype(BF16)
    contrib = jnp.dot(he, wd_ref[...], preferred_element_type=F32)
    lane = lax.broadcasted_iota(jnp.int32, comb_ref.shape, 1)
    w_e = jnp.sum(jnp.where(lane == e, comb_ref[...], 0.0), axis=-1, keepdims=True)
    acc_ref[...] += w_e * contrib

    @pl.when(e == N_EXPERTS - 1)
    def _():
        out_ref[...] = acc_ref[...]


def _moe(x1, h2, lg, wgu, wd, tm):
    n = x1.shape[0]
    return pl.pallas_call(
        _moe_kernel,
        grid=(n // tm, N_EXPERTS),
        in_specs=[
            pl.BlockSpec((tm, D_MODEL), lambda i, e: (i, 0)),
            pl.BlockSpec((tm, D_MODEL), lambda i, e: (i, 0)),
            pl.BlockSpec((tm, LANES), lambda i, e: (i, 0)),
            pl.BlockSpec((None, D_MODEL, 2 * D_EXPERT), lambda i, e: (e, 0, 0)),
            pl.BlockSpec((None, D_EXPERT, D_MODEL), lambda i, e: (e, 0, 0)),
        ],
        out_specs=pl.BlockSpec((tm, D_MODEL), lambda i, e: (i, 0)),
        out_shape=jax.ShapeDtypeStruct((n, D_MODEL), F32),
        scratch_shapes=[pltpu.VMEM((tm, D_MODEL), F32), pltpu.VMEM((tm, LANES), F32)],
        compiler_params=_cparams(("parallel", "arbitrary")),
        name="moe",
    )(x1, h2, lg, wgu, wd)


def _tile(n, pref):
    t = pref
    while n % t:
        t //= 2
    return t


def _mixer_params(layer, norm1_g, w_in, a_conv_w, a_conv_b, a_gate_bias, a_out_norm_g,
                  b_qk_norm_g, b_rel_bias, c_qk_norm_g, c_lambda, c_sub_norm_g, t5_bias,
                  w_branch, w_out, nq_diff):
    n_small = 2 * A_HEADS
    cut = 4 * A_WIDTH
    w_main = jnp.concatenate([w_in[:, :cut], w_in[:, cut + n_small:]], axis=1).astype(BF16)
    w_if = jnp.pad(w_in[:, cut:cut + n_small], ((0, 0), (0, LANES - n_small))).astype(BF16)
    gain = jnp.ones((N_SEG, SEG), F32)
    gain = gain.at[SEG_BQ].set(jnp.tile(b_qk_norm_g[0], B_HEADS) * (B_DH ** -0.5))
    gain = gain.at[SEG_BK].set(jnp.tile(b_qk_norm_g[1], B_HEADS))
    gain = gain.at[SEG_CQ].set(jnp.tile(c_qk_norm_g[0], 2 * C_HEADS) * (C_DQK ** -0.5))
    gain = gain.at[SEG_CK].set(jnp.tile(c_qk_norm_g[1], 2 * C_HEADS))
    blk = np.arange(SEG) // 64
    bd = jnp.asarray(blk[:, None] == blk[None, :], BF16)
    lam_init = 0.8 - 0.6 * math.exp(-0.3 * layer)
    lf32 = c_lambda.astype(F32)
    lam = jnp.exp(jnp.sum(lf32[0] * lf32[1])) - jnp.exp(jnp.sum(lf32[2] * lf32[3])) + lam_init
    return dict(
        g1=norm1_g.reshape(1, D_MODEL), w_main=w_main, w_if=w_if,
        gain=gain.reshape(N_SEG, 1, SEG), bd=bd,
        cw=a_conv_w, cb=a_conv_b.reshape(1, -1),
        gbr=jnp.pad(a_gate_bias, (0, LANES - n_small)).reshape(1, LANES),
        gbc=a_gate_bias.reshape(n_small, 1),
        ag=a_out_norm_g.reshape(1, A_WIDTH),
        band_bias=_band_bias(b_rel_bias),
        diff_bias=_diff_bias(t5_bias, _diff_nd(nq_diff)),
        lam=lam.reshape(1).astype(F32),
        gsub=(c_sub_norm_g * (1.0 - lam_init)).reshape(C_DV, 1),
        wb=w_branch.astype(BF16), wo=w_out.astype(BF16),
    )


def _layer(x2, bsz, seq, mp, norm2_g, w_group, b_group, w_router, b_router, w_e_gate, w_e_up, w_e_down):
    n = bsz * seq
    p, gif = _inproj(x2, mp["g1"], mp["w_main"], mp["w_if"], mp["gain"], mp["bd"], _tile(n, 1024))
    gift = jnp.transpose(gif[:, :2 * A_HEADS])
    ha = _mlstm(p, gif, gift, mp["cw"], mp["cb"], mp["gbr"], mp["gbc"], mp["ag"], bsz, seq)
    hb = _band(p, mp["band_bias"], bsz, seq)
    nq = seq // DIFF_T
    cv = p[:, SEG_CV * SEG:(SEG_CV + 1) * SEG].reshape(bsz, nq, DIFF_T, C_HEADS, C_DV)
    vt = jnp.transpose(cv, (0, 3, 1, 4, 2))
    hct = _diff(p, vt, mp["diff_bias"], mp["lam"], mp["gsub"], bsz, seq)
    hc = jnp.transpose(hct, (0, 2, 1)).reshape(n, C_WIDTH)

    wr = jnp.concatenate([w_router, w_group], axis=1)
    wr = jnp.pad(wr, ((0, 0), (0, LANES - wr.shape[1]))).astype(BF16)
    br = jnp.pad(jnp.concatenate([b_router, b_group]), (0, LANES - N_EXPERTS - N_GROUPS)).reshape(1, LANES)
    x1, h2, lg = _merge(ha, hb, hc, p, x2, mp["wb"], mp["wo"], norm2_g.reshape(1, D_MODEL), wr, br,
                        _tile(n, 512))
    wgu = jnp.concatenate([w_e_gate, w_e_up], axis=-1).astype(BF16)
    return _moe(x1, h2, lg, wgu, w_e_down.astype(BF16), _tile(n, 1024))


def kernel(x, norm1_g, w_in, a_conv_w, a_conv_b, a_gate_bias, a_out_norm_g, b_qk_norm_g, b_rel_bias,
           c_qk_norm_g, c_lambda, c_sub_norm_g, t5_bias, w_branch, w_out, norm2_g, w_group, b_group,
           w_router, b_router, w_e_gate, w_e_up, w_e_down):
    bsz, seq, _ = x.shape
    assert seq % DIFF_T == 0 and seq % MLSTM_L == 0 and seq % BAND_TQ == 0
    x2 = x.reshape(bsz * seq, D_MODEL)
    for l in range(norm1_g.shape[0]):
        mp = _mixer_params(l, norm1_g[l], w_in[l], a_conv_w[l], a_conv_b[l], a_gate_bias[l],
                           a_out_norm_g[l], b_qk_norm_g[l], b_rel_bias[l], c_qk_norm_g[l], c_lambda[l],
                           c_sub_norm_g[l], t5_bias, w_branch[l], w_out[l], seq // DIFF_T)
        x2 = _layer(x2, bsz, seq, mp, norm2_g[l], w_group[l], b_group[l], w_router[l], b_router[l],
                    w_e_gate[l], w_e_up[l], w_e_down[l])
    return x2.reshape(bsz, seq, D_MODEL)
```

```python
import functools
import math

import numpy as np
import jax
import jax.numpy as jnp
from jax import lax
from jax.experimental import pallas as pl
from jax.experimental.pallas import tpu as pltpu

F32 = jnp.float32
BF16 = jnp.bfloat16

D_MODEL = 1024
CHUNK = 64
EPS = 1e-6
NEG = -1e30
LOG2E = math.log2(math.e)

A_HEADS = 4
A_DH = 128
A_WIDTH = A_HEADS * A_DH
CONV_W = 4
GATE_CAP = 15.0

B_HEADS = 8
B_DH = 64
B_WIDTH = B_HEADS * B_DH
B_LEFT_CHUNKS = 8
B_MAX_REL = 256

C_HEADS = 4
C_DQK = 64
C_DV = 2 * C_DQK
C_WIDTH = C_HEADS * C_DV

T5_BUCKETS = 32
T5_MAX_DIST = 1024

N_BRANCH = 3
BRANCH_WIDTH = 512

N_GROUPS = 4
EXPERTS_PER_GROUP = 8
N_EXPERTS = N_GROUPS * EXPERTS_PER_GROUP
D_EXPERT = D_MODEL // 4

LANES = 128
SEG = 512
N_SEG = 16
VMEM_LIMIT = 48 * 1024 * 1024

SEG_AO = 3
SEG_BQ, SEG_BK, SEG_BV = 4, 5, 6
SEG_CQ, SEG_CK, SEG_CV = 7, 8, 9
SEG_GATES = 10

MLSTM_L = 128
BAND_TQ = 128
BAND_NKB = 1 + (B_LEFT_CHUNKS * CHUNK) // BAND_TQ
DIFF_T = 512
DIFF_ONES = 16

def _cparams(sem, flags=None):
    return pltpu.CompilerParams(dimension_semantics=sem, vmem_limit_bytes=VMEM_LIMIT, flags=flags)


def _inproj_kernel(x_ref, g_ref, w_ref, wif_ref, gain_ref, bd_ref, p_ref, gif_ref, xn_ref):
    j = pl.program_id(1)

    @pl.when(j == 0)
    def _():
        xf = x_ref[...]
        xn = xf * lax.rsqrt(jnp.mean(xf * xf, axis=-1, keepdims=True) + EPS) * g_ref[...]
        xn_ref[...] = xn.astype(BF16)
        gif_ref[...] = jnp.dot(xn_ref[...], wif_ref[...], preferred_element_type=F32)

    acc = jnp.dot(xn_ref[...], w_ref[...], preferred_element_type=F32)
    is_norm = (j == SEG_BQ) | (j == SEG_BK) | (j == SEG_CQ) | (j == SEG_CK)
    is_sig = (j == SEG_AO) | (j >= SEG_GATES)

    @pl.when(is_norm)
    def _():
        ssq = jnp.dot((acc * acc).astype(BF16), bd_ref[...], preferred_element_type=F32)
        p_ref[...] = (acc * lax.rsqrt(ssq * (1.0 / 64.0) + EPS) * gain_ref[0]).astype(BF16)

    @pl.when(is_sig)
    def _():
        p_ref[...] = jax.nn.sigmoid(acc).astype(BF16)

    @pl.when(jnp.logical_not(is_norm | is_sig))
    def _():
        p_ref[...] = acc.astype(BF16)


def _inproj(x2, g, w, wif, gain, bd, tm):
    n = x2.shape[0]
    return pl.pallas_call(
        _inproj_kernel,
        grid=(n // tm, N_SEG),
        in_specs=[
            pl.BlockSpec((tm, D_MODEL), lambda i, j: (i, 0)),
            pl.BlockSpec((1, D_MODEL), lambda i, j: (0, 0)),
            pl.BlockSpec((D_MODEL, SEG), lambda i, j: (0, j)),
            pl.BlockSpec((D_MODEL, LANES), lambda i, j: (0, 0)),
            pl.BlockSpec((1, 1, SEG), lambda i, j: (j, 0, 0)),
            pl.BlockSpec((SEG, SEG), lambda i, j: (0, 0)),
        ],
        out_specs=[
            pl.BlockSpec((tm, SEG), lambda i, j: (i, j)),
            pl.BlockSpec((tm, LANES), lambda i, j: (i, 0)),
        ],
        out_shape=[
            jax.ShapeDtypeStruct((n, N_SEG * SEG), BF16),
            jax.ShapeDtypeStruct((n, LANES), F32),
        ],
        scratch_shapes=[pltpu.VMEM((tm, D_MODEL), BF16)],
        compiler_params=_cparams(("parallel", "arbitrary")),
        name="inproj",
    )(x2, g, w, wif, gain, bd)


def _log_sigmoid(z):
    return jnp.minimum(z, 0.0) - jnp.log(1.0 + jnp.exp(-jnp.abs(z)))


def _split3(a):
    hi = a.astype(BF16)
    r1 = a - hi.astype(F32)
    mid = r1.astype(BF16)
    lo = (r1 - mid.astype(F32)).astype(BF16)
    return hi, mid, lo


def _mlstm_kernel(aq_ref, ak_ref, av_ref, ao_ref, gif_ref, gift_ref, cw_ref, cb_ref,
                  gbr_ref, gbc_ref, ag_ref, out_ref, ubuf, c_ref, n_ref, m_ref):
    L = MLSTM_L
    c = pl.program_id(1)

    @pl.when(c == 0)
    def _():
        ubuf[0:8, :] = jnp.zeros((8, 2 * A_WIDTH), F32)
        c_ref[...] = jnp.zeros_like(c_ref)
        n_ref[...] = jnp.zeros_like(n_ref)
        m_ref[...] = jnp.zeros_like(m_ref)

    @pl.when(c > 0)
    def _():
        ubuf[0:8, :] = ubuf[L:L + 8, :]

    ubuf[8:L + 8, 0:A_WIDTH] = aq_ref[...].astype(F32)
    ubuf[8:L + 8, A_WIDTH:2 * A_WIDTH] = ak_ref[...].astype(F32)
    y = cb_ref[...] + cw_ref[0:1, :] * ubuf[8:L + 8, :]
    for t in range(1, CONV_W):
        y = y + cw_ref[t:t + 1, :] * ubuf[8 - t:8 - t + L, :]
    qk = y * jax.nn.sigmoid(y)

    zc = gif_ref[...] + gbr_ref[...]
    ig_c = GATE_CAP * jnp.tanh(zc * (1.0 / GATE_CAP))
    lf_c = _log_sigmoid(zc)
    zr = gift_ref[...] + gbc_ref[...]
    ig_r = GATE_CAP * jnp.tanh(zr * (1.0 / GATE_CAP))
    lf_r = _log_sigmoid(zr)

    row = lax.broadcasted_iota(jnp.int32, (L, L), 0)
    col = lax.broadcasted_iota(jnp.int32, (L, L), 1)
    causal = col <= row
    tril = jnp.where(causal, 1.0, 0.0).astype(BF16)
    triu = jnp.where(row <= col, 1.0, 0.0).astype(BF16)
    b_c = sum(jnp.dot(tril, piece, preferred_element_type=F32) for piece in _split3(lf_c))
    b_r = sum(jnp.dot(piece, triu, preferred_element_type=F32) for piece in _split3(lf_r))

    for h in range(A_HEADS):
        sl = slice(h * A_DH, (h + 1) * A_DH)
        q = qk[:, sl]
        k = qk[:, A_WIDTH + h * A_DH:A_WIDTH + (h + 1) * A_DH] * (A_DH ** -0.5)
        qb = q.astype(BF16)
        kb = k.astype(BF16)
        v = av_ref[:, sl]
        bcol = b_c[:, A_HEADS + h:A_HEADS + h + 1]
        brow = b_r[A_HEADS + h:A_HEADS + h + 1, :]
        igcol = ig_c[:, h:h + 1]
        igrow = ig_r[h:h + 1, :]
        b_last = bcol[L - 1:L, :]
        C = c_ref[h]
        nvec = n_ref[h]
        m_prev = m_ref[h][:, 0:1]

        dmat = jnp.where(causal, bcol - brow + igrow, NEG)
        inter = bcol + m_prev
        m_t = jnp.maximum(inter, jnp.max(dmat, axis=-1, keepdims=True))
        w_inter = jnp.exp(inter - m_t)
        s = lax.dot_general(qb, kb, (((1,), (1,)), ((), ())), preferred_element_type=F32)
        s = s * jnp.exp(dmat - m_t)
        num = w_inter * jnp.dot(qb, C.astype(BF16), preferred_element_type=F32)
        num = num + jnp.dot(s.astype(BF16), v, preferred_element_type=F32)
        den = w_inter * jnp.sum(q * nvec, axis=-1, keepdims=True) + jnp.sum(s, axis=-1, keepdims=True)
        hh = num / jnp.maximum(jnp.abs(den), jnp.exp(-m_t))

        g_end = b_last - bcol + igcol
        m_new = jnp.maximum(b_last + m_prev, jnp.max(g_end, axis=0, keepdims=True))
        dec = jnp.exp(b_last + m_prev - m_new)
        kw = jnp.exp(g_end - m_new) * k
        c_ref[h] = dec * C + lax.dot_general(kw.astype(BF16), v, (((0,), (0,)), ((), ())),
                                             preferred_element_type=F32)
        n_ref[h] = dec * nvec + jnp.sum(kw, axis=0, keepdims=True)
        m_ref[h] = jnp.broadcast_to(m_new, (1, LANES))

        hn = hh * lax.rsqrt(jnp.mean(hh * hh, axis=-1, keepdims=True) + EPS) * ag_ref[:, sl]
        out_ref[:, sl] = (hn * ao_ref[:, sl].astype(F32)).astype(BF16)


def _mlstm(p, gif, gift, cw, cb, gbr, gbc, ag, bsz, seq):
    L = MLSTM_L
    nc = seq // L
    n = bsz * seq

    def seg(j):
        return pl.BlockSpec((L, SEG), lambda b, c: (b * nc + c, j))

    def full(shape):
        return pl.BlockSpec(shape, lambda b, c: (0,) * len(shape))

    return pl.pallas_call(
        _mlstm_kernel,
        grid=(bsz, nc),
        in_specs=[
            seg(0), seg(1), seg(2), seg(SEG_AO),
            pl.BlockSpec((L, LANES), lambda b, c: (b * nc + c, 0)),
            pl.BlockSpec((8, L), lambda b, c: (0, b * nc + c)),
            full((CONV_W, 2 * A_WIDTH)), full((1, 2 * A_WIDTH)),
            full((1, LANES)), full((8, 1)), full((1, A_WIDTH)),
        ],
        out_specs=pl.BlockSpec((L, A_WIDTH), lambda b, c: (b * nc + c, 0)),
        out_shape=jax.ShapeDtypeStruct((n, A_WIDTH), BF16),
        scratch_shapes=[
            pltpu.VMEM((L + 8, 2 * A_WIDTH), F32),
            pltpu.VMEM((A_HEADS, A_DH, A_DH), F32),
            pltpu.VMEM((A_HEADS, 1, A_DH), F32),
            pltpu.VMEM((A_HEADS, 1, LANES), F32),
        ],
        compiler_params=_cparams(("parallel", "arbitrary")),
        name="mlstm",
    )(p, p, p, p, gif, gift, cw, cb, gbr, gbc, ag)


def _band_kernel(*refs):
    nkb = BAND_NKB
    q_ref = refs[0]
    k_refs = refs[1:1 + nkb]
    v_refs = refs[1 + nkb:1 + 2 * nkb]
    bias_ref = refs[1 + 2 * nkb]
    out_ref = refs[2 + 2 * nkb]
    tq = BAND_TQ
    nk = nkb * tq
    i = pl.program_id(1)

    k_all = jnp.concatenate([r[...] for r in k_refs], axis=0)
    v_all = jnp.concatenate([r[...] for r in v_refs], axis=0)
    kidx = lax.broadcasted_iota(jnp.int32, (1, nk), 1)
    valid = (kidx + (i - (nkb - 1)) * tq) >= 0
    lane = lax.broadcasted_iota(jnp.int32, (tq, LANES), 1)
    lo = lane < B_DH

    for p in range(B_HEADS // 2):
        sl = slice(p * LANES, (p + 1) * LANES)
        qp = q_ref[:, sl]
        zero = jnp.zeros_like(qp)
        qbd = jnp.concatenate([jnp.where(lo, qp, zero), jnp.where(lo, zero, qp)], axis=0)
        s = lax.dot_general(qbd, k_all[:, sl], (((1,), (1,)), ((), ())), preferred_element_type=F32)
        s = jnp.where(valid, s + bias_ref[p], NEG)
        m = jnp.max(s, axis=-1, keepdims=True)
        e = jnp.exp(s - m)
        l = jnp.sum(e, axis=-1, keepdims=True)
        pv = jnp.dot(e.astype(BF16), v_all[:, sl], preferred_element_type=F32)
        o = pv / l
        out_ref[:, sl] = jnp.where(lo, o[0:tq], o[tq:2 * tq]).astype(BF16)


def _band(p, bias, bsz, seq):
    tq = BAND_TQ
    nkb = BAND_NKB
    nq = seq // tq
    n = bsz * seq

    def kv(j, d):
        return pl.BlockSpec((tq, SEG), lambda b, i: (b * nq + jnp.maximum(i - d, 0), j))

    in_specs = [pl.BlockSpec((tq, SEG), lambda b, i: (b * nq + i, SEG_BQ))]
    in_specs += [kv(SEG_BK, d) for d in range(nkb - 1, -1, -1)]
    in_specs += [kv(SEG_BV, d) for d in range(nkb - 1, -1, -1)]
    in_specs += [pl.BlockSpec(bias.shape, lambda b, i: (0, 0, 0))]
    return pl.pallas_call(
        _band_kernel,
        grid=(bsz, nq),
        in_specs=in_specs,
        out_specs=pl.BlockSpec((tq, B_WIDTH), lambda b, i: (b * nq + i, 0)),
        out_shape=jax.ShapeDtypeStruct((n, B_WIDTH), BF16),
        compiler_params=_cparams(("parallel", "parallel")),
        name="band_attn",
    )(*([p] * (1 + 2 * nkb)), bias)


def _toeplitz(base, m, n):
    period = base.shape[-1]
    assert n <= period - 1
    reps = (1,) * (base.ndim - 1) + (m,)
    big = jnp.tile(base, reps)[..., :m * (period - 1)]
    return big.reshape(base.shape[:-1] + (m, period - 1))[..., :n]


def _band_bias(b_rel):
    tq = BAND_TQ
    nk = BAND_NKB * tq
    period = tq + nk
    e = np.arange(period)
    e = np.where(e < nk, e, e - period)
    rel = np.clip((nk - tq) - e, -B_MAX_REL, B_MAX_REL) + B_MAX_REL
    bias = _toeplitz(jnp.transpose(b_rel[rel]).astype(F32), tq, nk)
    qpos = np.arange(tq)
    kpos = np.arange(nk) - (nk - tq)
    qc = qpos[:, None] // CHUNK
    kc = np.floor_divide(kpos[None, :], CHUNK)
    allowed = (kc <= qc) & (kc >= qc - B_LEFT_CHUNKS)
    bias = jnp.where(allowed[None], bias, NEG)
    return bias.reshape(B_HEADS // 2, 2 * tq, nk)


def _diff_kernel(lam_ref, cfar_ref, q_ref, k_ref, vt_ref, bias_ref, g_ref, out_ref,
                 qbd_ref, s_ref, mx_ref, p_ref, acc_ref, m_ref, *, nnear):
    T = DIFF_T
    h = pl.program_id(1)
    qi = pl.program_id(2)
    cfar = cfar_ref[h]
    q = q_ref[...]
    lane = lax.broadcasted_iota(jnp.int32, (T, LANES), 1)
    zero = jnp.zeros_like(q)
    qbd_ref[0:T, :] = jnp.where(lane < C_DQK, q, zero)
    qbd_ref[T:2 * T, :] = jnp.where(lane < C_DQK, zero, q)
    m_ref[...] = jnp.full(m_ref.shape, NEG, F32)
    acc_ref[...] = jnp.zeros_like(acc_ref)

    def near_bias(t):
        return bias_ref[t] if t < nnear else None

    def stage_a(t, slot, bias):
        j = jnp.maximum(qi - t, 0)
        k = k_ref[pl.ds(pl.multiple_of(j * T, T), T), :]
        s = lax.dot_general(k, qbd_ref[...], (((1,), (1,)), ((), ())), preferred_element_type=F32)
        for mp in range(2):
            sm = s[:, mp * T:(mp + 1) * T]
            if bias is not None:
                sm = sm + bias
            s_ref[slot, mp] = sm
            mx = jnp.max(sm, axis=0, keepdims=True)
            mx_ref[slot, mp] = mx + cfar if bias is None else mx

    def stage_bc(t, slot, far):
        vt = vt_ref[qi - t]
        for mp in range(2):
            m_old = m_ref[mp]
            m_new = jnp.maximum(m_old, mx_ref[slot, mp])
            shift = m_new - cfar if far else m_new
            p_ref[slot, mp] = jnp.exp2((s_ref[slot, mp] - shift).astype(BF16))
            m_ref[mp] = m_new
            acc_ref[mp] = jnp.exp2(m_old - m_new) * acc_ref[mp] + jnp.dot(
                vt, p_ref[slot, mp], preferred_element_type=F32)

    def full_step(t, slot, far, next_bias):
        stage_a(t + 1, 1 - slot, next_bias)
        stage_bc(t, slot, far)

    stage_a(0, 0, near_bias(0))
    for t in range(nnear):
        @pl.when(qi >= t)
        def _(t=t):
            full_step(t, t & 1, False, near_bias(t + 1))

    @pl.when(qi >= nnear)
    def _():
        n_full = qi - nnear
        s0 = nnear & 1

        def pair(i, carry):
            full_step(nnear + 2 * i, s0, True, None)
            full_step(nnear + 2 * i + 1, 1 - s0, True, None)
            return carry

        lax.fori_loop(0, n_full // 2, pair, 0)

        @pl.when(n_full % 2 == 1)
        def _():
            full_step(qi - 1, s0, True, None)

        stage_bc(qi, qi & 1, True)

    o1 = acc_ref[0, 0:C_DV, :] / acc_ref[0, C_DV:C_DV + 1, :]
    o2 = acc_ref[1, 0:C_DV, :] / acc_ref[1, C_DV:C_DV + 1, :]
    o = o1 - lam_ref[0] * o2
    on = o * lax.rsqrt(jnp.mean(o * o, axis=0, keepdims=True) + EPS) * g_ref[...]
    out_ref[...] = on.astype(BF16)


def _diff_nnear(nq):
    d_sat = -(-(T5_MAX_DIST - 1 + DIFF_T) // DIFF_T)
    return min(d_sat, nq)


def _diff(p, vt, bias, cfar, lam, gsub, bsz, seq):
    T = DIFF_T
    nq = seq // T
    nnear = bias.shape[1]
    qcol = SEG_CQ * SEG // LANES
    kcol = SEG_CK * SEG // LANES
    return pl.pallas_call(
        functools.partial(_diff_kernel, nnear=nnear),
        grid=(bsz, C_HEADS, nq),
        in_specs=[
            pl.BlockSpec(memory_space=pltpu.SMEM),
            pl.BlockSpec(memory_space=pltpu.SMEM),
            pl.BlockSpec((T, LANES), lambda b, h, i: (b * nq + i, qcol + h)),
            pl.BlockSpec((seq, LANES), lambda b, h, i: (b, kcol + h)),
            pl.BlockSpec((None, None, nq, C_DV + DIFF_ONES, T), lambda b, h, i: (b, h, 0, 0, 0)),
            pl.BlockSpec((None, nnear, T, T), lambda b, h, i: (h, 0, 0, 0)),
            pl.BlockSpec((C_DV, 1), lambda b, h, i: (0, 0)),
        ],
        out_specs=pl.BlockSpec((None, C_DV, T), lambda b, h, i: (b, h, i)),
        out_shape=jax.ShapeDtypeStruct((bsz, C_WIDTH, seq), BF16),
        scratch_shapes=[
            pltpu.VMEM((2 * T, LANES), BF16),
            pltpu.VMEM((2, 2, T, T), F32),
            pltpu.VMEM((2, 2, 1, T), F32),
            pltpu.VMEM((2, 2, T, T), BF16),
            pltpu.VMEM((2, C_DV + DIFF_ONES, T), F32),
            pltpu.VMEM((2, 1, T), F32),
        ],
        compiler_params=_cparams(("parallel", "parallel", "arbitrary")),
        name="diff_attn",
    )(lam, cfar, p, p, vt, bias, gsub)


def _t5_bucket(rel):
    nb = T5_BUCKETS // 2
    max_exact = nb // 2
    ret = (rel > 0).astype(jnp.int32) * nb
    n = jnp.abs(rel)
    large = max_exact + (jnp.log(jnp.maximum(n, max_exact).astype(F32) / max_exact)
                         / math.log(T5_MAX_DIST / max_exact) * (nb - max_exact)).astype(jnp.int32)
    large = jnp.minimum(large, nb - 1)
    return ret + jnp.where(n < max_exact, n, large)


def _diff_bias(t5_table, nnear):
    T = DIFF_T
    e = np.arange(2 * T)
    amc = np.where(e < T, -e, 2 * T - e)
    rel = jnp.asarray(-np.arange(nnear)[:, None] * T + amc[None, :], jnp.int32)
    base = jnp.moveaxis(t5_table[_t5_bucket(rel)], -1, 0).astype(F32) * LOG2E
    tiles = _toeplitz(base, T, T)
    a = np.arange(T)[:, None]
    c = np.arange(T)[None, :]
    allowed = np.ones((nnear, T, T), bool)
    allowed[0] = (a // CHUNK) <= (c // CHUNK)
    far = t5_table[_t5_bucket(jnp.asarray(-T5_MAX_DIST, jnp.int32))].astype(F32) * LOG2E
    return jnp.where(allowed[None], tiles, NEG), far


def _merge_kernel(ha_ref, hb_ref, hc_ref, g0_ref, g1_ref, g2_ref, x_ref, wb_ref, wo_ref,
                  n2_ref, wr_ref, br_ref, x1_ref, h2_ref, lg_ref):
    y = g0_ref[...].astype(F32) * jnp.dot(ha_ref[...], wb_ref[0], preferred_element_type=F32)
    y = y + g1_ref[...].astype(F32) * jnp.dot(hb_ref[...], wb_ref[1], preferred_element_type=F32)
    y = y + g2_ref[...].astype(F32) * jnp.dot(hc_ref[...], wb_ref[2], preferred_element_type=F32)
    x1 = x_ref[...] + jnp.dot(y.astype(BF16), wo_ref[...], preferred_element_type=F32)
    x1_ref[...] = x1
    h2 = x1 * lax.rsqrt(jnp.mean(x1 * x1, axis=-1, keepdims=True) + EPS) * n2_ref[...]
    h2b = h2.astype(BF16)
    h2_ref[...] = h2b
    lg_ref[...] = jnp.dot(h2b, wr_ref[...], preferred_element_type=F32) + br_ref[...]


def _merge(ha, hb, hc, p, x2, wb, wo, n2, wr, br, tm):
    n = x2.shape[0]
    gcol = SEG_GATES * SEG // D_MODEL

    def rows(width, col=0):
        return pl.BlockSpec((tm, width), lambda i: (i, col))

    def full(shape):
        return pl.BlockSpec(shape, lambda i: (0,) * len(shape))

    return pl.pallas_call(
        _merge_kernel,
        grid=(n // tm,),
        in_specs=[
            rows(BRANCH_WIDTH), rows(BRANCH_WIDTH), rows(BRANCH_WIDTH),
            rows(D_MODEL, gcol), rows(D_MODEL, gcol + 1), rows(D_MODEL, gcol + 2),
            rows(D_MODEL),
            full((N_BRANCH, BRANCH_WIDTH, D_MODEL)), full((D_MODEL, D_MODEL)),
            full((1, D_MODEL)), full((D_MODEL, LANES)), full((1, LANES)),
        ],
        out_specs=[rows(D_MODEL), rows(D_MODEL), rows(LANES)],
        out_shape=[
            jax.ShapeDtypeStruct((n, D_MODEL), F32),
            jax.ShapeDtypeStruct((n, D_MODEL), BF16),
            jax.ShapeDtypeStruct((n, LANES), F32),
        ],
        compiler_params=_cparams(("parallel",)),
        name="merge",
    )(ha, hb, hc, p, p, p, x2, wb, wo, n2, wr, br)


def _combine_weights(lg):
    lanef = lax.broadcasted_iota(jnp.int32, lg.shape, 1).astype(F32)
    big = 1e9
    is_g = (lanef >= N_EXPERTS) & (lanef < N_EXPERTS + N_GROUPS)
    gl = jnp.where(is_g, lg, -jnp.inf)
    gmax = jnp.max(gl, axis=-1, keepdims=True)
    g_idx = jnp.min(jnp.where(gl == gmax, lanef - N_EXPERTS, big), axis=-1, keepdims=True)
    p_g = 1.0 / jnp.sum(jnp.exp(gl - gmax), axis=-1, keepdims=True)
    in_grp = (lanef >= g_idx * EXPERTS_PER_GROUP) & (lanef < (g_idx + 1.0) * EXPERTS_PER_GROUP)
    el = jnp.where(in_grp, lg, -jnp.inf)
    ee = jnp.exp(el - jnp.max(el, axis=-1, keepdims=True))
    ep = ee / jnp.sum(ee, axis=-1, keepdims=True)
    ep = jnp.where(in_grp, ep, -1.0)
    v1 = jnp.max(ep, axis=-1, keepdims=True)
    i1 = jnp.min(jnp.where(ep == v1, lanef, big), axis=-1, keepdims=True)
    ep2 = jnp.where(lanef == i1, -1.0, ep)
    v2 = jnp.max(ep2, axis=-1, keepdims=True)
    i2 = jnp.min(jnp.where(ep2 == v2, lanef, big), axis=-1, keepdims=True)
    tot = v1 + v2
    return jnp.where(lanef == i1, p_g * (v1 / tot), 0.0) + jnp.where(lanef == i2, p_g * (v2 / tot), 0.0)


def _moe_kernel(x1_ref, h2_ref, lg_ref, wgu_ref, wd_ref, out_ref, acc_ref, comb_ref):
    e = pl.program_id(1)

    @pl.when(e == 0)
    def _():
        comb_ref[...] = _combine_weights(lg_ref[...])
        acc_ref[...] = x1_ref[...]

    gu = jnp.dot(h2_ref[...], wgu_ref[...], preferred_element_type=F32)
    g = gu[:, 0:D_EXPERT]
    he = (g * jax.nn.sigmoid(g) * gu[:, D_EXPERT:2 * D_EXPERT]).astype(BF16)
    contrib = jnp.dot(he, wd_ref[...], preferred_element_type=F32)
    lane = lax.broadcasted_iota(jnp.int32, comb_ref.shape, 1)
    w_e = jnp.sum(jnp.where(lane == e, comb_ref[...], 0.0), axis=-1, keepdims=True)
    acc_ref[...] += w_e * contrib

    @pl.when(e == N_EXPERTS - 1)
    def _():
        out_ref[...] = acc_ref[...]


def _moe(x1, h2, lg, wgu, wd, tm):
    n = x1.shape[0]
    return pl.pallas_call(
        _moe_kernel,
        grid=(n // tm, N_EXPERTS),
        in_specs=[
            pl.BlockSpec((tm, D_MODEL), lambda i, e: (i, 0)),
            pl.BlockSpec((tm, D_MODEL), lambda i, e: (i, 0)),
            pl.BlockSpec((tm, LANES), lambda i, e: (i, 0)),
            pl.BlockSpec((None, D_MODEL, 2 * D_EXPERT), lambda i, e: (e, 0, 0)),
            pl.BlockSpec((None, D_EXPERT, D_MODEL), lambda i, e: (e, 0, 0)),
        ],
        out_specs=pl.BlockSpec((tm, D_MODEL), lambda i, e: (i, 0)),
        out_shape=jax.ShapeDtypeStruct((n, D_MODEL), F32),
        scratch_shapes=[pltpu.VMEM((tm, D_MODEL), F32), pltpu.VMEM((tm, LANES), F32)],
        compiler_params=_cparams(("parallel", "arbitrary")),
        name="moe",
    )(x1, h2, lg, wgu, wd)


def _tile(n, pref):
    t = pref
    while n % t:
        t //= 2
    return t


def _mixer_params(layer, norm1_g, w_in, a_conv_w, a_conv_b, a_gate_bias, a_out_norm_g,
                  b_qk_norm_g, b_rel_bias, c_qk_norm_g, c_lambda, c_sub_norm_g, t5_bias,
                  w_branch, w_out, nq_diff):
    n_small = 2 * A_HEADS
    cut = 4 * A_WIDTH
    w_main = jnp.concatenate([w_in[:, :cut], w_in[:, cut + n_small:]], axis=1).astype(BF16)
    w_if = jnp.pad(w_in[:, cut:cut + n_small], ((0, 0), (0, LANES - n_small))).astype(BF16)
    gain = jnp.ones((N_SEG, SEG), F32)
    gain = gain.at[SEG_BQ].set(jnp.tile(b_qk_norm_g[0], B_HEADS) * (B_DH ** -0.5))
    gain = gain.at[SEG_BK].set(jnp.tile(b_qk_norm_g[1], B_HEADS))
    gain = gain.at[SEG_CQ].set(jnp.tile(c_qk_norm_g[0], 2 * C_HEADS) * (C_DQK ** -0.5 * LOG2E))
    diff_bias, diff_far = _diff_bias(t5_bias, _diff_nnear(nq_diff))
    gain = gain.at[SEG_CK].set(jnp.tile(c_qk_norm_g[1], 2 * C_HEADS))
    blk = np.arange(SEG) // 64
    bd = jnp.asarray(blk[:, None] == blk[None, :], BF16)
    lam_init = 0.8 - 0.6 * math.exp(-0.3 * layer)
    lf32 = c_lambda.astype(F32)
    lam = jnp.exp(jnp.sum(lf32[0] * lf32[1])) - jnp.exp(jnp.sum(lf32[2] * lf32[3])) + lam_init
    return dict(
        g1=norm1_g.reshape(1, D_MODEL), w_main=w_main, w_if=w_if,
        gain=gain.reshape(N_SEG, 1, SEG), bd=bd,
        cw=a_conv_w, cb=a_conv_b.reshape(1, -1),
        gbr=jnp.pad(a_gate_bias, (0, LANES - n_small)).reshape(1, LANES),
        gbc=a_gate_bias.reshape(n_small, 1),
        ag=a_out_norm_g.reshape(1, A_WIDTH),
        band_bias=_band_bias(b_rel_bias),
        diff_bias=diff_bias, diff_far=diff_far,
        lam=lam.reshape(1).astype(F32),
        gsub=(c_sub_norm_g * (1.0 - lam_init)).reshape(C_DV, 1),
        wb=w_branch.astype(BF16), wo=w_out.astype(BF16),
    )


def _layer(x2, bsz, seq, mp, norm2_g, w_group, b_group, w_router, b_router, w_e_gate, w_e_up, w_e_down):
    n = bsz * seq
    p, gif = _inproj(x2, mp["g1"], mp["w_main"], mp["w_if"], mp["gain"], mp["bd"], _tile(n, 1024))
    gift = jnp.transpose(gif[:, :2 * A_HEADS])
    ha = _mlstm(p, gif, gift, mp["cw"], mp["cb"], mp["gbr"], mp["gbc"], mp["ag"], bsz, seq)
    hb = _band(p, mp["band_bias"], bsz, seq)
    nq = seq // DIFF_T
    cv = p[:, SEG_CV * SEG:(SEG_CV + 1) * SEG].reshape(bsz, nq, DIFF_T, C_HEADS, C_DV)
    vt = jnp.transpose(cv, (0, 3, 1, 4, 2))
    vt = jnp.concatenate([vt, jnp.ones(vt.shape[:3] + (DIFF_ONES, DIFF_T), vt.dtype)], axis=3)
    hct = _diff(p, vt, mp["diff_bias"], mp["diff_far"], mp["lam"], mp["gsub"], bsz, seq)
    hc = jnp.transpose(hct, (0, 2, 1)).reshape(n, C_WIDTH)

    wr = jnp.concatenate([w_router, w_group], axis=1)
    wr = jnp.pad(wr, ((0, 0), (0, LANES - wr.shape[1]))).astype(BF16)
    br = jnp.pad(jnp.concatenate([b_router, b_group]), (0, LANES - N_EXPERTS - N_GROUPS)).reshape(1, LANES)
    x1, h2, lg = _merge(ha, hb, hc, p, x2, mp["wb"], mp["wo"], norm2_g.reshape(1, D_MODEL), wr, br,
                        _tile(n, 512))
    wgu = jnp.concatenate([w_e_gate, w_e_up], axis=-1).astype(BF16)
    return _moe(x1, h2, lg, wgu, w_e_down.astype(BF16), _tile(n, 1024))


def kernel(x, norm1_g, w_in, a_conv_w, a_conv_b, a_gate_bias, a_out_norm_g, b_qk_norm_g, b_rel_bias,
           c_qk_norm_g, c_lambda, c_sub_norm_g, t5_bias, w_branch, w_out, norm2_g, w_group, b_group,
           w_router, b_router, w_e_gate, w_e_up, w_e_down):
    bsz, seq, _ = x.shape
    assert seq % DIFF_T == 0 and seq % MLSTM_L == 0 and seq % BAND_TQ == 0
    x2 = x.reshape(bsz * seq, D_MODEL)
    for l in range(norm1_g.shape[0]):
        mp = _mixer_params(l, norm1_g[l], w_in[l], a_conv_w[l], a_conv_b[l], a_gate_bias[l],
                           a_out_norm_g[l], b_qk_norm_g[l], b_rel_bias[l], c_qk_norm_g[l], c_lambda[l],
                           c_sub_norm_g[l], t5_bias, w_branch[l], w_out[l], seq // DIFF_T)
        x2 = _layer(x2, bsz, seq, mp, norm2_g[l], w_group[l], b_group[l], w_router[l], b_router[l],
                    w_e_gate[l], w_e_up[l], w_e_down[l])
    return x2.reshape(bsz, seq, D_MODEL)
```

```python
import functools
import math

import numpy as np
import jax
import jax.numpy as jnp
from jax import lax
from jax.experimental import pallas as pl
from jax.experimental.pallas import tpu as pltpu

F32 = jnp.float32
BF16 = jnp.bfloat16

D_MODEL = 1024
CHUNK = 64
EPS = 1e-6
NEG = -1e30
LOG2E = math.log2(math.e)

A_HEADS = 4
A_DH = 128
A_WIDTH = A_HEADS * A_DH
CONV_W = 4
GATE_CAP = 15.0

B_HEADS = 8
B_DH = 64
B_WIDTH = B_HEADS * B_DH
B_LEFT_CHUNKS = 8
B_MAX_REL = 256

C_HEADS = 4
C_DQK = 64
C_DV = 2 * C_DQK
C_WIDTH = C_HEADS * C_DV

T5_BUCKETS = 32
T5_MAX_DIST = 1024

N_BRANCH = 3
BRANCH_WIDTH = 512

N_GROUPS = 4
EXPERTS_PER_GROUP = 8
N_EXPERTS = N_GROUPS * EXPERTS_PER_GROUP
D_EXPERT = D_MODEL // 4

LANES = 128
SEG = 512
N_SEG = 16
VMEM_LIMIT = 48 * 1024 * 1024

SEG_AO = 3
SEG_BQ, SEG_BK, SEG_BV = 4, 5, 6
SEG_CQ, SEG_CK, SEG_CV = 7, 8, 9
SEG_GATES = 10

MLSTM_L = 128
BAND_TQ = 128
BAND_NKB = 1 + (B_LEFT_CHUNKS * CHUNK) // BAND_TQ
DIFF_T = 512
DIFF_ONES = 16
MOE_TM = 1024
MOE_C = 128
MOE_EPS = 4
MOE_VMEM_LIMIT = 56 * 1024 * 1024

def _cparams(sem, flags=None):
    return pltpu.CompilerParams(dimension_semantics=sem, vmem_limit_bytes=VMEM_LIMIT, flags=flags)


NORM_SEGS = (SEG_BQ, SEG_BK, SEG_CQ, SEG_CK)


def _inproj_kernel(x_ref, g_ref, w_ref, wif_ref, gain_ref, bd_ref, p_ref, gif_ref):
    xf = x_ref[...]
    xn = (xf * lax.rsqrt(jnp.mean(xf * xf, axis=-1, keepdims=True) + EPS) * g_ref[...]).astype(BF16)
    gif_ref[...] = jnp.dot(xn, wif_ref[...], preferred_element_type=F32)
    for j in range(N_SEG):
        cols = slice(j * SEG, (j + 1) * SEG)
        acc = jnp.dot(xn, w_ref[:, cols], preferred_element_type=F32)
        if j in NORM_SEGS:
            ssq = jnp.dot((acc * acc).astype(BF16), bd_ref[...], preferred_element_type=F32)
            acc = acc * lax.rsqrt(ssq * (1.0 / 64.0) + EPS) * gain_ref[j]
        elif j == SEG_AO or j >= SEG_GATES:
            acc = jax.nn.sigmoid(acc)
        p_ref[:, cols] = acc.astype(BF16)


def _inproj(x2, g, w, wif, gain, bd, tm):
    n = x2.shape[0]

    def const(shape):
        return pl.BlockSpec(shape, lambda i: (0,) * len(shape), pipeline_mode=pl.Buffered(1))

    return pl.pallas_call(
        _inproj_kernel,
        grid=(n // tm,),
        in_specs=[
            pl.BlockSpec((tm, D_MODEL), lambda i: (i, 0)),
            const((1, D_MODEL)),
            const((D_MODEL, N_SEG * SEG)),
            const((D_MODEL, LANES)),
            const((N_SEG, 1, SEG)),
            const((SEG, SEG)),
        ],
        out_specs=[
            pl.BlockSpec((tm, N_SEG * SEG), lambda i: (i, 0)),
            pl.BlockSpec((tm, LANES), lambda i: (i, 0)),
        ],
        out_shape=[
            jax.ShapeDtypeStruct((n, N_SEG * SEG), BF16),
            jax.ShapeDtypeStruct((n, LANES), F32),
        ],
        compiler_params=_cparams(("parallel",)),
        name="inproj",
    )(x2, g, w, wif, gain, bd)


def _log_sigmoid(z):
    return jnp.minimum(z, 0.0) - jnp.log(1.0 + jnp.exp(-jnp.abs(z)))


def _split3(a):
    hi = a.astype(BF16)
    r1 = a - hi.astype(F32)
    mid = r1.astype(BF16)
    lo = (r1 - mid.astype(F32)).astype(BF16)
    return hi, mid, lo


def _mlstm_kernel(aq_ref, ak_ref, av_ref, ao_ref, gif_ref, gift_ref, cw_ref, cb_ref,
                  gbr_ref, gbc_ref, ag_ref, out_ref, ubuf, c_ref, n_ref, m_ref):
    L = MLSTM_L
    c = pl.program_id(1)

    @pl.when(c == 0)
    def _():
        ubuf[0:8, :] = jnp.zeros((8, 2 * A_WIDTH), F32)
        c_ref[...] = jnp.zeros_like(c_ref)
        n_ref[...] = jnp.zeros_like(n_ref)
        m_ref[...] = jnp.zeros_like(m_ref)

    @pl.when(c > 0)
    def _():
        ubuf[0:8, :] = ubuf[L:L + 8, :]

    ubuf[8:L + 8, 0:A_WIDTH] = aq_ref[...].astype(F32)
    ubuf[8:L + 8, A_WIDTH:2 * A_WIDTH] = ak_ref[...].astype(F32)
    y = cb_ref[...] + cw_ref[0:1, :] * ubuf[8:L + 8, :]
    for t in range(1, CONV_W):
        y = y + cw_ref[t:t + 1, :] * ubuf[8 - t:8 - t + L, :]
    qk = y * jax.nn.sigmoid(y)

    zc = gif_ref[...] + gbr_ref[...]
    ig_c = GATE_CAP * jnp.tanh(zc * (1.0 / GATE_CAP))
    lf_c = _log_sigmoid(zc)
    zr = gift_ref[...] + gbc_ref[...]
    ig_r = GATE_CAP * jnp.tanh(zr * (1.0 / GATE_CAP))
    lf_r = _log_sigmoid(zr)

    row = lax.broadcasted_iota(jnp.int32, (L, L), 0)
    col = lax.broadcasted_iota(jnp.int32, (L, L), 1)
    causal = col <= row
    tril = jnp.where(causal, 1.0, 0.0).astype(BF16)
    triu = jnp.where(row <= col, 1.0, 0.0).astype(BF16)
    b_c = sum(jnp.dot(tril, piece, preferred_element_type=F32) for piece in _split3(lf_c))
    b_r = sum(jnp.dot(piece, triu, preferred_element_type=F32) for piece in _split3(lf_r))

    for h in range(A_HEADS):
        sl = slice(h * A_DH, (h + 1) * A_DH)
        q = qk[:, sl]
        k = qk[:, A_WIDTH + h * A_DH:A_WIDTH + (h + 1) * A_DH] * (A_DH ** -0.5)
        qb = q.astype(BF16)
        kb = k.astype(BF16)
        v = av_ref[:, sl]
        bcol = b_c[:, A_HEADS + h:A_HEADS + h + 1]
        brow = b_r[A_HEADS + h:A_HEADS + h + 1, :]
        igcol = ig_c[:, h:h + 1]
        igrow = ig_r[h:h + 1, :]
        b_last = bcol[L - 1:L, :]
        C = c_ref[h]
        nvec = n_ref[h]
        m_prev = m_ref[h][:, 0:1]

        dmat = jnp.where(causal, bcol - brow + igrow, NEG)
        inter = bcol + m_prev
        m_t = jnp.maximum(inter, jnp.max(dmat, axis=-1, keepdims=True))
        w_inter = jnp.exp(inter - m_t)
        s = lax.dot_general(qb, kb, (((1,), (1,)), ((), ())), preferred_element_type=F32)
        s = s * jnp.exp(dmat - m_t)
        num = w_inter * jnp.dot(qb, C.astype(BF16), preferred_element_type=F32)
        num = num + jnp.dot(s.astype(BF16), v, preferred_element_type=F32)
        den = w_inter * jnp.sum(q * nvec, axis=-1, keepdims=True) + jnp.sum(s, axis=-1, keepdims=True)
        hh = num / jnp.maximum(jnp.abs(den), jnp.exp(-m_t))

        g_end = b_last - bcol + igcol
        m_new = jnp.maximum(b_last + m_prev, jnp.max(g_end, axis=0, keepdims=True))
        dec = jnp.exp(b_last + m_prev - m_new)
        kw = jnp.exp(g_end - m_new) * k
        c_ref[h] = dec * C + lax.dot_general(kw.astype(BF16), v, (((0,), (0,)), ((), ())),
                                             preferred_element_type=F32)
        n_ref[h] = dec * nvec + jnp.sum(kw, axis=0, keepdims=True)
        m_ref[h] = jnp.broadcast_to(m_new, (1, LANES))

        hn = hh * lax.rsqrt(jnp.mean(hh * hh, axis=-1, keepdims=True) + EPS) * ag_ref[:, sl]
        out_ref[:, sl] = (hn * ao_ref[:, sl].astype(F32)).astype(BF16)


def _mlstm(p, gif, gift, cw, cb, gbr, gbc, ag, bsz, seq):
    L = MLSTM_L
    nc = seq // L
    n = bsz * seq

    def seg(j):
        return pl.BlockSpec((L, SEG), lambda b, c: (b * nc + c, j))

    def full(shape):
        return pl.BlockSpec(shape, lambda b, c: (0,) * len(shape))

    return pl.pallas_call(
        _mlstm_kernel,
        grid=(bsz, nc),
        in_specs=[
            seg(0), seg(1), seg(2), seg(SEG_AO),
            pl.BlockSpec((L, LANES), lambda b, c: (b * nc + c, 0)),
            pl.BlockSpec((8, L), lambda b, c: (0, b * nc + c)),
            full((CONV_W, 2 * A_WIDTH)), full((1, 2 * A_WIDTH)),
            full((1, LANES)), full((8, 1)), full((1, A_WIDTH)),
        ],
        out_specs=pl.BlockSpec((L, A_WIDTH), lambda b, c: (b * nc + c, 0)),
        out_shape=jax.ShapeDtypeStruct((n, A_WIDTH), BF16),
        scratch_shapes=[
            pltpu.VMEM((L + 8, 2 * A_WIDTH), F32),
            pltpu.VMEM((A_HEADS, A_DH, A_DH), F32),
            pltpu.VMEM((A_HEADS, 1, A_DH), F32),
            pltpu.VMEM((A_HEADS, 1, LANES), F32),
        ],
        compiler_params=_cparams(("parallel", "arbitrary")),
        name="mlstm",
    )(p, p, p, p, gif, gift, cw, cb, gbr, gbc, ag)


def _band_kernel(*refs):
    nkb = BAND_NKB
    q_ref = refs[0]
    k_refs = refs[1:1 + nkb]
    v_refs = refs[1 + nkb:1 + 2 * nkb]
    bias_ref = refs[1 + 2 * nkb]
    out_ref = refs[2 + 2 * nkb]
    tq = BAND_TQ
    nk = nkb * tq
    i = pl.program_id(1)

    k_all = jnp.concatenate([r[...] for r in k_refs], axis=0)
    v_all = jnp.concatenate([r[...] for r in v_refs], axis=0)
    kidx = lax.broadcasted_iota(jnp.int32, (1, nk), 1)
    valid = (kidx + (i - (nkb - 1)) * tq) >= 0
    lane = lax.broadcasted_iota(jnp.int32, (tq, LANES), 1)
    lo = lane < B_DH

    for p in range(B_HEADS // 2):
        sl = slice(p * LANES, (p + 1) * LANES)
        qp = q_ref[:, sl]
        zero = jnp.zeros_like(qp)
        qbd = jnp.concatenate([jnp.where(lo, qp, zero), jnp.where(lo, zero, qp)], axis=0)
        s = lax.dot_general(qbd, k_all[:, sl], (((1,), (1,)), ((), ())), preferred_element_type=F32)
        s = jnp.where(valid, s + bias_ref[p], NEG)
        m = jnp.max(s, axis=-1, keepdims=True)
        e = jnp.exp(s - m)
        l = jnp.sum(e, axis=-1, keepdims=True)
        pv = jnp.dot(e.astype(BF16), v_all[:, sl], preferred_element_type=F32)
        o = pv / l
        out_ref[:, sl] = jnp.where(lo, o[0:tq], o[tq:2 * tq]).astype(BF16)


def _band(p, bias, bsz, seq):
    tq = BAND_TQ
    nkb = BAND_NKB
    nq = seq // tq
    n = bsz * seq

    def kv(j, d):
        return pl.BlockSpec((tq, SEG), lambda b, i: (b * nq + jnp.maximum(i - d, 0), j))

    in_specs = [pl.BlockSpec((tq, SEG), lambda b, i: (b * nq + i, SEG_BQ))]
    in_specs += [kv(SEG_BK, d) for d in range(nkb - 1, -1, -1)]
    in_specs += [kv(SEG_BV, d) for d in range(nkb - 1, -1, -1)]
    in_specs += [pl.BlockSpec(bias.shape, lambda b, i: (0, 0, 0))]
    return pl.pallas_call(
        _band_kernel,
        grid=(bsz, nq),
        in_specs=in_specs,
        out_specs=pl.BlockSpec((tq, B_WIDTH), lambda b, i: (b * nq + i, 0)),
        out_shape=jax.ShapeDtypeStruct((n, B_WIDTH), BF16),
        compiler_params=_cparams(("parallel", "parallel")),
        name="band_attn",
    )(*([p] * (1 + 2 * nkb)), bias)


def _toeplitz(base, m, n):
    period = base.shape[-1]
    assert n <= period - 1
    reps = (1,) * (base.ndim - 1) + (m,)
    big = jnp.tile(base, reps)[..., :m * (period - 1)]
    return big.reshape(base.shape[:-1] + (m, period - 1))[..., :n]


def _band_bias(b_rel):
    tq = BAND_TQ
    nk = BAND_NKB * tq
    period = tq + nk
    e = np.arange(period)
    e = np.where(e < nk, e, e - period)
    rel = np.clip((nk - tq) - e, -B_MAX_REL, B_MAX_REL) + B_MAX_REL
    bias = _toeplitz(jnp.transpose(b_rel[rel]).astype(F32), tq, nk)
    qpos = np.arange(tq)
    kpos = np.arange(nk) - (nk - tq)
    qc = qpos[:, None] // CHUNK
    kc = np.floor_divide(kpos[None, :], CHUNK)
    allowed = (kc <= qc) & (kc >= qc - B_LEFT_CHUNKS)
    bias = jnp.where(allowed[None], bias, NEG)
    return bias.reshape(B_HEADS // 2, 2 * tq, nk)


def _diff_kernel(lam_ref, cfar_ref, q_ref, k_ref, vt_ref, bias_ref, g_ref, out_ref,
                 qbd_ref, s_ref, mx_ref, p_ref, acc_ref, m_ref, *, nnear):
    T = DIFF_T
    h = pl.program_id(1)
    qi = pl.program_id(2)
    cfar = cfar_ref[h]
    q = q_ref[...]
    lane = lax.broadcasted_iota(jnp.int32, (T, LANES), 1)
    zero = jnp.zeros_like(q)
    qbd_ref[0:T, :] = jnp.where(lane < C_DQK, q, zero)
    qbd_ref[T:2 * T, :] = jnp.where(lane < C_DQK, zero, q)
    m_ref[...] = jnp.full(m_ref.shape, NEG, F32)
    acc_ref[...] = jnp.zeros_like(acc_ref)

    def near_bias(t):
        return bias_ref[t] if t < nnear else None

    def stage_a(t, slot, bias):
        j = jnp.maximum(qi - t, 0)
        k = k_ref[pl.ds(pl.multiple_of(j * T, T), T), :]
        s = lax.dot_general(k, qbd_ref[...], (((1,), (1,)), ((), ())), preferred_element_type=F32)
        for mp in range(2):
            sm = s[:, mp * T:(mp + 1) * T]
            if bias is not None:
                sm = sm + bias
            s_ref[slot, mp] = sm
            mx = jnp.max(sm, axis=0, keepdims=True)
            mx_ref[slot, mp] = mx + cfar if bias is None else mx

    def stage_bc(t, slot, far):
        vt = vt_ref[qi - t]
        for mp in range(2):
            m_old = m_ref[mp]
            m_new = jnp.maximum(m_old, mx_ref[slot, mp])
            shift = m_new - cfar if far else m_new
            p_ref[slot, mp] = jnp.exp2((s_ref[slot, mp] - shift).astype(BF16))
            m_ref[mp] = m_new
            acc_ref[mp] = jnp.exp2(m_old - m_new) * acc_ref[mp] + jnp.dot(
                vt, p_ref[slot, mp], preferred_element_type=F32)

    def full_step(t, slot, far, next_bias):
        stage_a(t + 1, 1 - slot, next_bias)
        stage_bc(t, slot, far)

    stage_a(0, 0, near_bias(0))
    for t in range(nnear):
        @pl.when(qi >= t)
        def _(t=t):
            full_step(t, t & 1, False, near_bias(t + 1))

    @pl.when(qi >= nnear)
    def _():
        n_full = qi - nnear
        s0 = nnear & 1

        def pair(i, carry):
            full_step(nnear + 2 * i, s0, True, None)
            full_step(nnear + 2 * i + 1, 1 - s0, True, None)
            return carry

        lax.fori_loop(0, n_full // 2, pair, 0)

        @pl.when(n_full % 2 == 1)
        def _():
            full_step(qi - 1, s0, True, None)

        stage_bc(qi, qi & 1, True)

    o1 = acc_ref[0, 0:C_DV, :] / acc_ref[0, C_DV:C_DV + 1, :]
    o2 = acc_ref[1, 0:C_DV, :] / acc_ref[1, C_DV:C_DV + 1, :]
    o = o1 - lam_ref[0] * o2
    on = o * lax.rsqrt(jnp.mean(o * o, axis=0, keepdims=True) + EPS) * g_ref[...]
    out_ref[...] = on.astype(BF16)


def _diff_nnear(nq):
    d_sat = -(-(T5_MAX_DIST - 1 + DIFF_T) // DIFF_T)
    return min(d_sat, nq)


def _diff(p, vt, bias, cfar, lam, gsub, bsz, seq):
    T = DIFF_T
    nq = seq // T
    nnear = bias.shape[1]
    qcol = SEG_CQ * SEG // LANES
    kcol = SEG_CK * SEG // LANES
    return pl.pallas_call(
        functools.partial(_diff_kernel, nnear=nnear),
        grid=(bsz, C_HEADS, nq),
        in_specs=[
            pl.BlockSpec(memory_space=pltpu.SMEM),
            pl.BlockSpec(memory_space=pltpu.SMEM),
            pl.BlockSpec((T, LANES), lambda b, h, i: (b * nq + i, qcol + h)),
            pl.BlockSpec((seq, LANES), lambda b, h, i: (b, kcol + h)),
            pl.BlockSpec((None, None, nq, C_DV + DIFF_ONES, T), lambda b, h, i: (b, h, 0, 0, 0)),
            pl.BlockSpec((None, nnear, T, T), lambda b, h, i: (h, 0, 0, 0)),
            pl.BlockSpec((C_DV, 1), lambda b, h, i: (0, 0)),
        ],
        out_specs=pl.BlockSpec((None, C_DV, T), lambda b, h, i: (b, h, i)),
        out_shape=jax.ShapeDtypeStruct((bsz, C_WIDTH, seq), BF16),
        scratch_shapes=[
            pltpu.VMEM((2 * T, LANES), BF16),
            pltpu.VMEM((2, 2, T, T), F32),
            pltpu.VMEM((2, 2, 1, T), F32),
            pltpu.VMEM((2, 2, T, T), BF16),
            pltpu.VMEM((2, C_DV + DIFF_ONES, T), F32),
            pltpu.VMEM((2, 1, T), F32),
        ],
        compiler_params=_cparams(("parallel", "parallel", "arbitrary")),
        name="diff_attn",
    )(lam, cfar, p, p, vt, bias, gsub)


def _t5_bucket(rel):
    nb = T5_BUCKETS // 2
    max_exact = nb // 2
    ret = (rel > 0).astype(jnp.int32) * nb
    n = jnp.abs(rel)
    large = max_exact + (jnp.log(jnp.maximum(n, max_exact).astype(F32) / max_exact)
                         / math.log(T5_MAX_DIST / max_exact) * (nb - max_exact)).astype(jnp.int32)
    large = jnp.minimum(large, nb - 1)
    return ret + jnp.where(n < max_exact, n, large)


def _diff_bias(t5_table, nnear):
    T = DIFF_T
    e = np.arange(2 * T)
    amc = np.where(e < T, -e, 2 * T - e)
    rel = jnp.asarray(-np.arange(nnear)[:, None] * T + amc[None, :], jnp.int32)
    base = jnp.moveaxis(t5_table[_t5_bucket(rel)], -1, 0).astype(F32) * LOG2E
    tiles = _toeplitz(base, T, T)
    a = np.arange(T)[:, None]
    c = np.arange(T)[None, :]
    allowed = np.ones((nnear, T, T), bool)
    allowed[0] = (a // CHUNK) <= (c // CHUNK)
    far = t5_table[_t5_bucket(jnp.asarray(-T5_MAX_DIST, jnp.int32))].astype(F32) * LOG2E
    return jnp.where(allowed[None], tiles, NEG), far


def _merge_kernel(ha_ref, hb_ref, hc_ref, g0_ref, g1_ref, g2_ref, x_ref, wb_ref, wo_ref,
                  n2_ref, wr_ref, br_ref, x1_ref, h2_ref, lg_ref):
    y = g0_ref[...].astype(F32) * jnp.dot(ha_ref[...], wb_ref[0], preferred_element_type=F32)
    y = y + g1_ref[...].astype(F32) * jnp.dot(hb_ref[...], wb_ref[1], preferred_element_type=F32)
    y = y + g2_ref[...].astype(F32) * jnp.dot(hc_ref[...], wb_ref[2], preferred_element_type=F32)
    x1 = x_ref[...] + jnp.dot(y.astype(BF16), wo_ref[...], preferred_element_type=F32)
    x1_ref[...] = x1
    h2 = x1 * lax.rsqrt(jnp.mean(x1 * x1, axis=-1, keepdims=True) + EPS) * n2_ref[...]
    h2b = h2.astype(BF16)
    h2_ref[...] = h2b
    lg_ref[...] = jnp.dot(h2b, wr_ref[...], preferred_element_type=F32) + br_ref[...]


def _merge(ha, hb, hc, p, x2, wb, wo, n2, wr, br, tm):
    n = x2.shape[0]
    gcol = SEG_GATES * SEG // D_MODEL

    def rows(width, col=0):
        return pl.BlockSpec((tm, width), lambda i: (i, col))

    def full(shape):
        return pl.BlockSpec(shape, lambda i: (0,) * len(shape))

    return pl.pallas_call(
        _merge_kernel,
        grid=(n // tm,),
        in_specs=[
            rows(BRANCH_WIDTH), rows(BRANCH_WIDTH), rows(BRANCH_WIDTH),
            rows(D_MODEL, gcol), rows(D_MODEL, gcol + 1), rows(D_MODEL, gcol + 2),
            rows(D_MODEL),
            full((N_BRANCH, BRANCH_WIDTH, D_MODEL)), full((D_MODEL, D_MODEL)),
            full((1, D_MODEL)), full((D_MODEL, LANES)), full((1, LANES)),
        ],
        out_specs=[rows(D_MODEL), rows(D_MODEL), rows(LANES)],
        out_shape=[
            jax.ShapeDtypeStruct((n, D_MODEL), F32),
            jax.ShapeDtypeStruct((n, D_MODEL), BF16),
            jax.ShapeDtypeStruct((n, LANES), F32),
        ],
        compiler_params=_cparams(("parallel",)),
        name="merge",
    )(ha, hb, hc, p, p, p, x2, wb, wo, n2, wr, br)


def _combine_weights(lg):
    lanef = lax.broadcasted_iota(jnp.int32, lg.shape, 1).astype(F32)
    big = 1e9
    is_g = (lanef >= N_EXPERTS) & (lanef < N_EXPERTS + N_GROUPS)
    gl = jnp.where(is_g, lg, -jnp.inf)
    gmax = jnp.max(gl, axis=-1, keepdims=True)
    g_idx = jnp.min(jnp.where(gl == gmax, lanef - N_EXPERTS, big), axis=-1, keepdims=True)
    p_g = 1.0 / jnp.sum(jnp.exp(gl - gmax), axis=-1, keepdims=True)
    in_grp = (lanef >= g_idx * EXPERTS_PER_GROUP) & (lanef < (g_idx + 1.0) * EXPERTS_PER_GROUP)
    el = jnp.where(in_grp, lg, -jnp.inf)
    ee = jnp.exp(el - jnp.max(el, axis=-1, keepdims=True))
    ep = ee / jnp.sum(ee, axis=-1, keepdims=True)
    ep = jnp.where(in_grp, ep, -1.0)
    v1 = jnp.max(ep, axis=-1, keepdims=True)
    i1 = jnp.min(jnp.where(ep == v1, lanef, big), axis=-1, keepdims=True)
    ep2 = jnp.where(lanef == i1, -1.0, ep)
    v2 = jnp.max(ep2, axis=-1, keepdims=True)
    i2 = jnp.min(jnp.where(ep2 == v2, lanef, big), axis=-1, keepdims=True)
    tot = v1 + v2
    comb = jnp.where(lanef == i1, p_g * (v1 / tot), 0.0) + jnp.where(lanef == i2, p_g * (v2 / tot), 0.0)
    return comb, g_idx


def _moe_kernel(x1_ref, h2_ref, lg_ref, tri_ref, wgu_ref, wd_ref, out_ref,
                pt_ref, xs_ref, ws_ref, ys_ref, meta_ref, *, rcap):
    C = MOE_C
    s = pl.program_id(1)
    tm = h2_ref.shape[0]

    @pl.when(s == 0)
    def _():
        comb, g_idx = _combine_weights(lg_ref[...])
        lanef = lax.broadcasted_iota(jnp.int32, comb.shape, 1).astype(F32)
        mine = lanef == g_idx
        onehot = jnp.where(mine, 1.0, 0.0)
        ranks = jnp.dot(tri_ref[...], onehot.astype(BF16), preferred_element_type=F32)
        dest = jnp.sum(jnp.where(mine, ranks, 0.0), axis=-1, keepdims=True)
        off = jnp.int32(0)
        for g in range(N_GROUPS):
            cnt = jnp.sum(onehot[:, g:g + 1]).astype(jnp.int32)
            nchunk = (cnt + (C - 1)) // C
            meta_ref[g] = off
            meta_ref[N_GROUPS + g] = nchunk
            dest = dest + jnp.where(g_idx == float(g), (off * C).astype(F32), 0.0)
            off = off + nchunk
        slot = lax.broadcasted_iota(jnp.int32, (tm, rcap), 1).astype(F32)
        pt = jnp.where(dest == slot, 1.0, 0.0).astype(BF16)
        pt_ref[...] = pt
        tn = (((0,), (0,)), ((), ()))
        xs_ref[...] = lax.dot_general(pt, h2_ref[...], tn, preferred_element_type=F32).astype(BF16)
        comb_hi = comb.astype(BF16)
        comb_lo = (comb - comb_hi.astype(F32)).astype(BF16)
        ws_ref[...] = (lax.dot_general(pt, comb_hi, tn, preferred_element_type=F32)
                       + lax.dot_general(pt, comb_lo, tn, preferred_element_type=F32))
        ys_ref[...] = jnp.zeros_like(ys_ref)

    def do_chunk(r0, m):
        r0 = pl.multiple_of(r0, C)
        xc = xs_ref[pl.ds(r0, m), :]
        wsc = ws_ref[pl.ds(r0, m), :]
        lane = lax.broadcasted_iota(jnp.int32, (m, LANES), 1)
        y = None
        for k in range(MOE_EPS):
            gu = jnp.dot(xc, wgu_ref[k], preferred_element_type=F32)
            gate = gu[:, 0:D_EXPERT]
            wk = jnp.sum(jnp.where(lane == s * MOE_EPS + k, wsc, 0.0), axis=-1, keepdims=True)
            he = (gate * jax.nn.sigmoid(gate) * gu[:, D_EXPERT:2 * D_EXPERT] * wk).astype(BF16)
            yk = jnp.dot(he, wd_ref[k], preferred_element_type=F32)
            y = yk if y is None else y + yk
        ys_ref[pl.ds(r0, m), :] += y

    g = s // (EXPERTS_PER_GROUP // MOE_EPS)
    start = meta_ref[g]
    nchunk = meta_ref[N_GROUPS + g]

    def pair(i, carry):
        do_chunk((start + 2 * i) * C, 2 * C)
        return carry

    lax.fori_loop(0, nchunk // 2, pair, 0)

    @pl.when(nchunk % 2 == 1)
    def _():
        do_chunk((start + nchunk - 1) * C, C)

    @pl.when(s == pl.num_programs(1) - 1)
    def _():
        out_ref[...] = x1_ref[...] + jnp.dot(pt_ref[...], ys_ref[...].astype(BF16),
                                             preferred_element_type=F32)


def _moe(x1, h2, lg, wgu, wd, tm):
    n = x1.shape[0]
    rcap = tm + N_GROUPS * MOE_C
    idx = np.arange(tm)
    tri = jnp.asarray(idx[None, :] < idx[:, None], BF16)
    return pl.pallas_call(
        functools.partial(_moe_kernel, rcap=rcap),
        grid=(n // tm, N_EXPERTS // MOE_EPS),
        in_specs=[
            pl.BlockSpec((tm, D_MODEL), lambda i, s: (i, 0), pipeline_mode=pl.Buffered(1)),
            pl.BlockSpec((tm, D_MODEL), lambda i, s: (i, 0), pipeline_mode=pl.Buffered(1)),
            pl.BlockSpec((tm, LANES), lambda i, s: (i, 0), pipeline_mode=pl.Buffered(1)),
            pl.BlockSpec((tm, tm), lambda i, s: (0, 0), pipeline_mode=pl.Buffered(1)),
            pl.BlockSpec((MOE_EPS, D_MODEL, 2 * D_EXPERT), lambda i, s: (s, 0, 0)),
            pl.BlockSpec((MOE_EPS, D_EXPERT, D_MODEL), lambda i, s: (s, 0, 0)),
        ],
        out_specs=pl.BlockSpec((tm, D_MODEL), lambda i, s: (i, 0)),
        out_shape=jax.ShapeDtypeStruct((n, D_MODEL), F32),
        scratch_shapes=[
            pltpu.VMEM((tm, rcap), BF16),
            pltpu.VMEM((rcap, D_MODEL), BF16),
            pltpu.VMEM((rcap, LANES), F32),
            pltpu.VMEM((rcap, D_MODEL), F32),
            pltpu.SMEM((2 * N_GROUPS,), jnp.int32),
        ],
        compiler_params=pltpu.CompilerParams(dimension_semantics=("parallel", "arbitrary"),
                                             vmem_limit_bytes=MOE_VMEM_LIMIT),
        name="moe",
    )(x1, h2, lg, tri, wgu, wd)


def _tile(n, pref):
    t = pref
    while n % t:
        t //= 2
    return t


def _mixer_params(layer, norm1_g, w_in, a_conv_w, a_conv_b, a_gate_bias, a_out_norm_g,
                  b_qk_norm_g, b_rel_bias, c_qk_norm_g, c_lambda, c_sub_norm_g, t5_bias,
                  w_branch, w_out, nq_diff):
    n_small = 2 * A_HEADS
    cut = 4 * A_WIDTH
    w_main = jnp.concatenate([w_in[:, :cut], w_in[:, cut + n_small:]], axis=1).astype(BF16)
    w_if = jnp.pad(w_in[:, cut:cut + n_small], ((0, 0), (0, LANES - n_small))).astype(BF16)
    gain = jnp.ones((N_SEG, SEG), F32)
    gain = gain.at[SEG_BQ].set(jnp.tile(b_qk_norm_g[0], B_HEADS) * (B_DH ** -0.5))
    gain = gain.at[SEG_BK].set(jnp.tile(b_qk_norm_g[1], B_HEADS))
    gain = gain.at[SEG_CQ].set(jnp.tile(c_qk_norm_g[0], 2 * C_HEADS) * (C_DQK ** -0.5 * LOG2E))
    diff_bias, diff_far = _diff_bias(t5_bias, _diff_nnear(nq_diff))
    gain = gain.at[SEG_CK].set(jnp.tile(c_qk_norm_g[1], 2 * C_HEADS))
    blk = np.arange(SEG) // 64
    bd = jnp.asarray(blk[:, None] == blk[None, :], BF16)
    lam_init = 0.8 - 0.6 * math.exp(-0.3 * layer)
    lf32 = c_lambda.astype(F32)
    lam = jnp.exp(jnp.sum(lf32[0] * lf32[1])) - jnp.exp(jnp.sum(lf32[2] * lf32[3])) + lam_init
    return dict(
        g1=norm1_g.reshape(1, D_MODEL), w_main=w_main, w_if=w_if,
        gain=gain.reshape(N_SEG, 1, SEG), bd=bd,
        cw=a_conv_w, cb=a_conv_b.reshape(1, -1),
        gbr=jnp.pad(a_gate_bias, (0, LANES - n_small)).reshape(1, LANES),
        gbc=a_gate_bias.reshape(n_small, 1),
        ag=a_out_norm_g.reshape(1, A_WIDTH),
        band_bias=_band_bias(b_rel_bias),
        diff_bias=diff_bias, diff_far=diff_far,
        lam=lam.reshape(1).astype(F32),
        gsub=(c_sub_norm_g * (1.0 - lam_init)).reshape(C_DV, 1),
        wb=w_branch.astype(BF16), wo=w_out.astype(BF16),
    )


def _layer(x2, bsz, seq, mp, norm2_g, w_group, b_group, w_router, b_router, w_e_gate, w_e_up, w_e_down):
    n = bsz * seq
    p, gif = _inproj(x2, mp["g1"], mp["w_main"], mp["w_if"], mp["gain"], mp["bd"], _tile(n, 512))
    gift = jnp.transpose(gif[:, :2 * A_HEADS])
    ha = _mlstm(p, gif, gift, mp["cw"], mp["cb"], mp["gbr"], mp["gbc"], mp["ag"], bsz, seq)
    hb = _band(p, mp["band_bias"], bsz, seq)
    nq = seq // DIFF_T
    cv = p[:, SEG_CV * SEG:(SEG_CV + 1) * SEG].reshape(bsz, nq, DIFF_T, C_HEADS, C_DV)
    vt = jnp.transpose(cv, (0, 3, 1, 4, 2))
    vt = jnp.concatenate([vt, jnp.ones(vt.shape[:3] + (DIFF_ONES, DIFF_T), vt.dtype)], axis=3)
    hct = _diff(p, vt, mp["diff_bias"], mp["diff_far"], mp["lam"], mp["gsub"], bsz, seq)
    hc = jnp.transpose(hct, (0, 2, 1)).reshape(n, C_WIDTH)

    wr = jnp.concatenate([w_router, w_group], axis=1)
    wr = jnp.pad(wr, ((0, 0), (0, LANES - wr.shape[1]))).astype(BF16)
    br = jnp.pad(jnp.concatenate([b_router, b_group]), (0, LANES - N_EXPERTS - N_GROUPS)).reshape(1, LANES)
    x1, h2, lg = _merge(ha, hb, hc, p, x2, mp["wb"], mp["wo"], norm2_g.reshape(1, D_MODEL), wr, br,
                        _tile(n, 512))
    wgu = jnp.concatenate([w_e_gate, w_e_up], axis=-1).astype(BF16)
    return _moe(x1, h2, lg, wgu, w_e_down.astype(BF16), _tile(n, MOE_TM))


def kernel(x, norm1_g, w_in, a_conv_w, a_conv_b, a_gate_bias, a_out_norm_g, b_qk_norm_g, b_rel_bias,
           c_qk_norm_g, c_lambda, c_sub_norm_g, t5_bias, w_branch, w_out, norm2_g, w_group, b_group,
           w_router, b_router, w_e_gate, w_e_up, w_e_down):
    bsz, seq, _ = x.shape
    assert seq % DIFF_T == 0 and seq % MLSTM_L == 0 and seq % BAND_TQ == 0
    x2 = x.reshape(bsz * seq, D_MODEL)
    for l in range(norm1_g.shape[0]):
        mp = _mixer_params(l, norm1_g[l], w_in[l], a_conv_w[l], a_conv_b[l], a_gate_bias[l],
                           a_out_norm_g[l], b_qk_norm_g[l], b_rel_bias[l], c_qk_norm_g[l], c_lambda[l],
                           c_sub_norm_g[l], t5_bias, w_branch[l], w_out[l], seq // DIFF_T)
        x2 = _layer(x2, bsz, seq, mp, norm2_g[l], w_group[l], b_group[l], w_router[l], b_router[l],
                    w_e_gate[l], w_e_up[l], w_e_down[l])
    return x2.reshape(bsz, seq, D_MODEL)
```

```python
import functools
import math

import numpy as np
import jax
import jax.numpy as jnp
from jax import lax
from jax.experimental import pallas as pl
from jax.experimental.pallas import tpu as pltpu

F32 = jnp.float32
BF16 = jnp.bfloat16

D_MODEL = 1024
CHUNK = 64
EPS = 1e-6
NEG = -1e30
LOG2E = math.log2(math.e)

A_HEADS = 4
A_DH = 128
A_WIDTH = A_HEADS * A_DH
CONV_W = 4
GATE_CAP = 15.0

B_HEADS = 8
B_DH = 64
B_WIDTH = B_HEADS * B_DH
B_LEFT_CHUNKS = 8
B_MAX_REL = 256

C_HEADS = 4
C_DQK = 64
C_DV = 2 * C_DQK
C_WIDTH = C_HEADS * C_DV

T5_BUCKETS = 32
T5_MAX_DIST = 1024

N_BRANCH = 3
BRANCH_WIDTH = 512

N_GROUPS = 4
EXPERTS_PER_GROUP = 8
N_EXPERTS = N_GROUPS * EXPERTS_PER_GROUP
D_EXPERT = D_MODEL // 4

LANES = 128
SEG = 512
N_SEG = 16
VMEM_LIMIT = 48 * 1024 * 1024

SEG_AQ, SEG_AK, SEG_AV, SEG_AO, SEG_BK, SEG_CK, SEG_GATES = 0, 1, 2, 3, 4, 5, 6
N_ROW_SEG = 12
TSEG_BQ, TSEG_BV, TSEG_CQ, TSEG_CV = 0, 1, 2, 3
N_T_SEG = N_SEG - N_ROW_SEG
SEG_PERM = (0, 1, 2, 3, 5, 8, 10, 11, 12, 13, 14, 15, 4, 6, 7, 9)

MLSTM_L = 128
BAND_TQ = 128
BAND_NKB = 1 + (B_LEFT_CHUNKS * CHUNK) // BAND_TQ
BAND_ONES = 16
DIFF_T = 512
DIFF_ONES = 16
MOE_TM = 1024
MOE_C = 128
MOE_EPS = 4
MOE_VMEM_LIMIT = 56 * 1024 * 1024

def _cparams(sem, flags=None):
    return pltpu.CompilerParams(dimension_semantics=sem, vmem_limit_bytes=VMEM_LIMIT, flags=flags)


NORM_SEGS = (SEG_BK, SEG_CK, N_ROW_SEG + TSEG_BQ, N_ROW_SEG + TSEG_CQ)
SIGMOID_SEGS = (SEG_AO,) + tuple(range(SEG_GATES, N_ROW_SEG))


def _inproj_kernel(x_ref, g_ref, w_ref, wif_ref, gain_ref, bd_ref, p_ref, pt_ref, gif_ref):
    xf = x_ref[...]
    xn = (xf * lax.rsqrt(jnp.mean(xf * xf, axis=-1, keepdims=True) + EPS) * g_ref[...]).astype(BF16)
    gif_ref[...] = jnp.dot(xn, wif_ref[...], preferred_element_type=F32)
    for j in range(N_SEG):
        cols = slice(j * SEG, (j + 1) * SEG)
        acc = jnp.dot(xn, w_ref[:, cols], preferred_element_type=F32)
        if j in NORM_SEGS:
            ssq = jnp.dot((acc * acc).astype(BF16), bd_ref[...], preferred_element_type=F32)
            acc = acc * lax.rsqrt(ssq * (1.0 / 64.0) + EPS) * gain_ref[j]
        elif j in SIGMOID_SEGS:
            acc = jax.nn.sigmoid(acc)
        if j < N_ROW_SEG:
            p_ref[:, cols] = acc.astype(BF16)
        else:
            pt_ref[j - N_ROW_SEG] = jnp.transpose(acc).astype(BF16)


def _inproj(x2, g, w, wif, gain, bd, tm):
    n = x2.shape[0]

    def const(shape):
        return pl.BlockSpec(shape, lambda i: (0,) * len(shape), pipeline_mode=pl.Buffered(1))

    return pl.pallas_call(
        _inproj_kernel,
        grid=(n // tm,),
        in_specs=[
            pl.BlockSpec((tm, D_MODEL), lambda i: (i, 0)),
            const((1, D_MODEL)),
            const((D_MODEL, N_SEG * SEG)),
            const((D_MODEL, LANES)),
            const((N_SEG, 1, SEG)),
            const((SEG, SEG)),
        ],
        out_specs=[
            pl.BlockSpec((tm, N_ROW_SEG * SEG), lambda i: (i, 0)),
            pl.BlockSpec((N_T_SEG, SEG, tm), lambda i: (0, 0, i)),
            pl.BlockSpec((tm, LANES), lambda i: (i, 0)),
        ],
        out_shape=[
            jax.ShapeDtypeStruct((n, N_ROW_SEG * SEG), BF16),
            jax.ShapeDtypeStruct((N_T_SEG, SEG, n), BF16),
            jax.ShapeDtypeStruct((n, LANES), F32),
        ],
        compiler_params=_cparams(("parallel",)),
        name="inproj",
    )(x2, g, w, wif, gain, bd)


def _log_sigmoid(z):
    return jnp.minimum(z, 0.0) - jnp.log(1.0 + jnp.exp(-jnp.abs(z)))


def _split3(a):
    hi = a.astype(BF16)
    r1 = a - hi.astype(F32)
    mid = r1.astype(BF16)
    lo = (r1 - mid.astype(F32)).astype(BF16)
    return hi, mid, lo


def _mlstm_kernel(aq_ref, ak_ref, av_ref, ao_ref, gif_ref, gift_ref, cw_ref, cb_ref,
                  gbr_ref, gbc_ref, ag_ref, out_ref, ubuf, c_ref, n_ref, m_ref):
    L = MLSTM_L
    c = pl.program_id(1)

    @pl.when(c == 0)
    def _():
        ubuf[0:8, :] = jnp.zeros((8, 2 * A_WIDTH), F32)
        c_ref[...] = jnp.zeros_like(c_ref)
        n_ref[...] = jnp.zeros_like(n_ref)
        m_ref[...] = jnp.zeros_like(m_ref)

    @pl.when(c > 0)
    def _():
        ubuf[0:8, :] = ubuf[L:L + 8, :]

    ubuf[8:L + 8, 0:A_WIDTH] = aq_ref[...].astype(F32)
    ubuf[8:L + 8, A_WIDTH:2 * A_WIDTH] = ak_ref[...].astype(F32)
    y = cb_ref[...] + cw_ref[0:1, :] * ubuf[8:L + 8, :]
    for t in range(1, CONV_W):
        y = y + cw_ref[t:t + 1, :] * ubuf[8 - t:8 - t + L, :]
    qk = y * jax.nn.sigmoid(y)

    zc = gif_ref[...] + gbr_ref[...]
    ig_c = GATE_CAP * jnp.tanh(zc * (1.0 / GATE_CAP))
    lf_c = _log_sigmoid(zc)
    zr = gift_ref[...] + gbc_ref[...]
    ig_r = GATE_CAP * jnp.tanh(zr * (1.0 / GATE_CAP))
    lf_r = _log_sigmoid(zr)

    row = lax.broadcasted_iota(jnp.int32, (L, L), 0)
    col = lax.broadcasted_iota(jnp.int32, (L, L), 1)
    causal = col <= row
    tril = jnp.where(causal, 1.0, 0.0).astype(BF16)
    triu = jnp.where(row <= col, 1.0, 0.0).astype(BF16)
    b_c = sum(jnp.dot(tril, piece, preferred_element_type=F32) for piece in _split3(lf_c))
    b_r = sum(jnp.dot(piece, triu, preferred_element_type=F32) for piece in _split3(lf_r))

    for h in range(A_HEADS):
        sl = slice(h * A_DH, (h + 1) * A_DH)
        q = qk[:, sl]
        k = qk[:, A_WIDTH + h * A_DH:A_WIDTH + (h + 1) * A_DH] * (A_DH ** -0.5)
        qb = q.astype(BF16)
        kb = k.astype(BF16)
        v = av_ref[:, sl]
        bcol = b_c[:, A_HEADS + h:A_HEADS + h + 1]
        brow = b_r[A_HEADS + h:A_HEADS + h + 1, :]
        igcol = ig_c[:, h:h + 1]
        igrow = ig_r[h:h + 1, :]
        b_last = bcol[L - 1:L, :]
        C = c_ref[h]
        nvec = n_ref[h]
        m_prev = m_ref[h][:, 0:1]

        dmat = jnp.where(causal, bcol - brow + igrow, NEG)
        inter = bcol + m_prev
        m_t = jnp.maximum(inter, jnp.max(dmat, axis=-1, keepdims=True))
        w_inter = jnp.exp(inter - m_t)
        s = lax.dot_general(qb, kb, (((1,), (1,)), ((), ())), preferred_element_type=F32)
        s = s * jnp.exp(dmat - m_t)
        num = w_inter * jnp.dot(qb, C.astype(BF16), preferred_element_type=F32)
        num = num + jnp.dot(s.astype(BF16), v, preferred_element_type=F32)
        den = w_inter * jnp.sum(q * nvec, axis=-1, keepdims=True) + jnp.sum(s, axis=-1, keepdims=True)
        hh = num / jnp.maximum(jnp.abs(den), jnp.exp(-m_t))

        g_end = b_last - bcol + igcol
        m_new = jnp.maximum(b_last + m_prev, jnp.max(g_end, axis=0, keepdims=True))
        dec = jnp.exp(b_last + m_prev - m_new)
        kw = jnp.exp(g_end - m_new) * k
        c_ref[h] = dec * C + lax.dot_general(kw.astype(BF16), v, (((0,), (0,)), ((), ())),
                                             preferred_element_type=F32)
        n_ref[h] = dec * nvec + jnp.sum(kw, axis=0, keepdims=True)
        m_ref[h] = jnp.broadcast_to(m_new, (1, LANES))

        hn = hh * lax.rsqrt(jnp.mean(hh * hh, axis=-1, keepdims=True) + EPS) * ag_ref[:, sl]
        out_ref[:, sl] = (hn * ao_ref[:, sl].astype(F32)).astype(BF16)


def _mlstm(p, gif, gift, cw, cb, gbr, gbc, ag, bsz, seq):
    L = MLSTM_L
    nc = seq // L
    n = bsz * seq

    def seg(j):
        return pl.BlockSpec((L, SEG), lambda b, c: (b * nc + c, j))

    def full(shape):
        return pl.BlockSpec(shape, lambda b, c: (0,) * len(shape))

    return pl.pallas_call(
        _mlstm_kernel,
        grid=(bsz, nc),
        in_specs=[
            seg(0), seg(1), seg(2), seg(SEG_AO),
            pl.BlockSpec((L, LANES), lambda b, c: (b * nc + c, 0)),
            pl.BlockSpec((8, L), lambda b, c: (0, b * nc + c)),
            full((CONV_W, 2 * A_WIDTH)), full((1, 2 * A_WIDTH)),
            full((1, LANES)), full((8, 1)), full((1, A_WIDTH)),
        ],
        out_specs=pl.BlockSpec((L, A_WIDTH), lambda b, c: (b * nc + c, 0)),
        out_shape=jax.ShapeDtypeStruct((n, A_WIDTH), BF16),
        scratch_shapes=[
            pltpu.VMEM((L + 8, 2 * A_WIDTH), F32),
            pltpu.VMEM((A_HEADS, A_DH, A_DH), F32),
            pltpu.VMEM((A_HEADS, 1, A_DH), F32),
            pltpu.VMEM((A_HEADS, 1, LANES), F32),
        ],
        compiler_params=_cparams(("parallel", "arbitrary")),
        name="mlstm",
    )(p, p, p, p, gif, gift, cw, cb, gbr, gbc, ag)


def _band_kernel(*refs):
    nkb = BAND_NKB
    qt_ref = refs[0]
    k_refs = refs[1:1 + nkb]
    vt_refs = refs[1 + nkb:1 + 2 * nkb]
    bias_ref = refs[1 + 2 * nkb]
    out_ref = refs[2 + 2 * nkb]
    s_ref, mx_ref = refs[3 + 2 * nkb:5 + 2 * nkb]
    tq = BAND_TQ
    nk = nkb * tq
    i = pl.program_id(1)

    def compute(mask_start):
        k_all = jnp.concatenate([r[...] for r in k_refs], axis=0)
        vt_all = jnp.concatenate([r[...] for r in vt_refs], axis=1)
        ones = jnp.ones((BAND_ONES, nk), BF16)
        row = lax.broadcasted_iota(jnp.int32, (LANES, tq), 0)
        lo = row < B_DH
        if mask_start:
            kidx = lax.broadcasted_iota(jnp.int32, (nk, 1), 0)
            valid = (kidx + (i - (nkb - 1)) * tq) >= 0

        def score(p):
            rows = slice(p * LANES, (p + 1) * LANES)
            qtp = qt_ref[rows, :]
            zero = jnp.zeros_like(qtp)
            qbd = jnp.concatenate([jnp.where(lo, qtp, zero), jnp.where(lo, zero, qtp)], axis=1)
            s = jnp.dot(k_all[:, rows], qbd, preferred_element_type=F32) + bias_ref[p]
            if mask_start:
                s = jnp.where(valid, s, NEG)
            s_ref[p & 1] = s
            mx_ref[p & 1] = jnp.max(s, axis=0, keepdims=True)

        def finish(p):
            rows = slice(p * LANES, (p + 1) * LANES)
            pr = jnp.exp2((s_ref[p & 1] - mx_ref[p & 1]).astype(BF16))
            o = jnp.dot(jnp.concatenate([vt_all[rows, :], ones], axis=0), pr,
                        preferred_element_type=F32)
            o = o[0:LANES, :] / o[LANES:LANES + 1, :]
            out_ref[rows, :] = jnp.where(lo, o[:, 0:tq], o[:, tq:2 * tq]).astype(BF16)

        score(0)
        for p in range(B_HEADS // 2):
            if p + 1 < B_HEADS // 2:
                score(p + 1)
            finish(p)

    @pl.when(i < nkb - 1)
    def _():
        compute(True)

    @pl.when(i >= nkb - 1)
    def _():
        compute(False)


def _band(p, pt, bias, bsz, seq):
    tq = BAND_TQ
    nkb = BAND_NKB
    nq = seq // tq
    n = bsz * seq

    def kblk(d):
        return pl.BlockSpec((tq, SEG), lambda b, i: (b * nq + jnp.maximum(i - d, 0), SEG_BK))

    def vblk(d):
        return pl.BlockSpec((None, SEG, tq), lambda b, i: (TSEG_BV, 0, b * nq + jnp.maximum(i - d, 0)))

    in_specs = [pl.BlockSpec((None, SEG, tq), lambda b, i: (TSEG_BQ, 0, b * nq + i))]
    in_specs += [kblk(d) for d in range(nkb - 1, -1, -1)]
    in_specs += [vblk(d) for d in range(nkb - 1, -1, -1)]
    in_specs += [pl.BlockSpec(bias.shape, lambda b, i: (0, 0, 0))]
    return pl.pallas_call(
        _band_kernel,
        grid=(bsz, nq),
        in_specs=in_specs,
        out_specs=pl.BlockSpec((B_WIDTH, tq), lambda b, i: (0, b * nq + i)),
        out_shape=jax.ShapeDtypeStruct((B_WIDTH, n), BF16),
        scratch_shapes=[
            pltpu.VMEM((2, nkb * tq, 2 * tq), F32),
            pltpu.VMEM((2, 1, 2 * tq), F32),
        ],
        compiler_params=_cparams(("parallel", "parallel")),
        name="band_attn",
    )(pt, *([p] * nkb), *([pt] * nkb), bias)


def _toeplitz(base, m, n):
    period = base.shape[-1]
    assert n <= period - 1
    reps = (1,) * (base.ndim - 1) + (m,)
    big = jnp.tile(base, reps)[..., :m * (period - 1)]
    return big.reshape(base.shape[:-1] + (m, period - 1))[..., :n]


def _band_bias(b_rel):
    tq = BAND_TQ
    nk = BAND_NKB * tq
    period = tq + nk
    e = np.arange(period)
    e = np.where(e < nk, e, e - period)
    rel = np.clip((nk - tq) - e, -B_MAX_REL, B_MAX_REL) + B_MAX_REL
    bias = _toeplitz(jnp.transpose(b_rel[rel]).astype(F32), tq, nk)
    qpos = np.arange(tq)
    kpos = np.arange(nk) - (nk - tq)
    qc = qpos[:, None] // CHUNK
    kc = np.floor_divide(kpos[None, :], CHUNK)
    allowed = (kc <= qc) & (kc >= qc - B_LEFT_CHUNKS)
    bias = jnp.where(allowed[None], bias * LOG2E, NEG)
    return jnp.swapaxes(bias.reshape(B_HEADS // 2, 2 * tq, nk), 1, 2)


def _diff_kernel(lam_ref, cfar_ref, qt_ref, k_ref, vt_ref, bias_ref, g_ref, out_ref,
                 qbd_ref, s_ref, mx_ref, p_ref, acc_ref, m_ref, *, nnear):
    T = DIFF_T
    h = pl.program_id(1)
    qi = pl.program_id(2)
    cfar = cfar_ref[h]
    qt = qt_ref[...]
    row = lax.broadcasted_iota(jnp.int32, (2 * C_DQK, T), 0)
    zero = jnp.zeros_like(qt)
    qbd_ref[:, 0:T] = jnp.where(row < C_DQK, qt, zero)
    qbd_ref[:, T:2 * T] = jnp.where(row < C_DQK, zero, qt)
    m_ref[...] = jnp.full(m_ref.shape, NEG, F32)
    acc_ref[...] = jnp.zeros_like(acc_ref)

    def near_bias(t):
        return bias_ref[t] if t < nnear else None

    def stage_a(t, slot, bias, maps=(0, 1)):
        j = jnp.maximum(qi - t, 0)
        k = k_ref[pl.ds(pl.multiple_of(j * T, T), T), :]
        for mp in maps:
            sm = jnp.dot(k, qbd_ref[:, mp * T:(mp + 1) * T], preferred_element_type=F32)
            if bias is not None:
                sm = sm + bias
            s_ref[slot, mp] = sm
            mx = jnp.max(sm, axis=0, keepdims=True)
            mx_ref[slot, mp] = mx + cfar if bias is None else mx

    def stage_bc(t, slot, far, maps=(0, 1)):
        j = qi - t
        vt = jnp.concatenate([vt_ref[:, pl.ds(pl.multiple_of(j * T, T), T)],
                              jnp.ones((DIFF_ONES, T), BF16)], axis=0)
        for mp in maps:
            m_old = m_ref[mp]
            m_new = jnp.maximum(m_old, mx_ref[slot, mp])
            shift = m_new - cfar if far else m_new
            p_ref[slot, mp] = jnp.exp2((s_ref[slot, mp] - shift).astype(BF16))
            m_ref[mp] = m_new
            acc_ref[mp] = jnp.exp2(m_old - m_new) * acc_ref[mp] + jnp.dot(
                vt, p_ref[slot, mp], preferred_element_type=F32)

    def full_step(t, slot, far, next_bias):
        for mp in range(2):
            stage_a(t + 1, 1 - slot, next_bias, (mp,))
            stage_bc(t, slot, far, (mp,))

    stage_a(0, 0, near_bias(0))
    for t in range(nnear):
        @pl.when(qi >= t)
        def _(t=t):
            full_step(t, t & 1, False, near_bias(t + 1))

    @pl.when(qi >= nnear)
    def _():
        n_full = qi - nnear
        s0 = nnear & 1

        def pair(i, carry):
            full_step(nnear + 2 * i, s0, True, None)
            full_step(nnear + 2 * i + 1, 1 - s0, True, None)
            return carry

        lax.fori_loop(0, n_full // 2, pair, 0)

        @pl.when(n_full % 2 == 1)
        def _():
            full_step(qi - 1, s0, True, None)

        stage_bc(qi, qi & 1, True)

    o1 = acc_ref[0, 0:C_DV, :] / acc_ref[0, C_DV:C_DV + 1, :]
    o2 = acc_ref[1, 0:C_DV, :] / acc_ref[1, C_DV:C_DV + 1, :]
    o = o1 - lam_ref[0] * o2
    on = o * lax.rsqrt(jnp.mean(o * o, axis=0, keepdims=True) + EPS) * g_ref[...]
    out_ref[...] = on.astype(BF16)


def _diff_nnear(nq):
    d_sat = -(-(T5_MAX_DIST - 1 + DIFF_T) // DIFF_T)
    return min(d_sat, nq)


def _diff(p, pt, bias, cfar, lam, gsub, bsz, seq):
    T = DIFF_T
    nq = seq // T
    nnear = bias.shape[1]
    kcol = SEG_CK * SEG // LANES
    n = bsz * seq
    return pl.pallas_call(
        functools.partial(_diff_kernel, nnear=nnear),
        grid=(bsz, C_HEADS, nq),
        in_specs=[
            pl.BlockSpec(memory_space=pltpu.SMEM),
            pl.BlockSpec(memory_space=pltpu.SMEM),
            pl.BlockSpec((None, 2 * C_DQK, T), lambda b, h, i: (TSEG_CQ, h, b * nq + i)),
            pl.BlockSpec((seq, LANES), lambda b, h, i: (b, kcol + h)),
            pl.BlockSpec((None, C_DV, seq), lambda b, h, i: (TSEG_CV, h, b)),
            pl.BlockSpec((None, nnear, T, T), lambda b, h, i: (h, 0, 0, 0)),
            pl.BlockSpec((C_DV, 1), lambda b, h, i: (0, 0)),
        ],
        out_specs=pl.BlockSpec((C_DV, T), lambda b, h, i: (h, b * nq + i)),
        out_shape=jax.ShapeDtypeStruct((C_WIDTH, n), BF16),
        scratch_shapes=[
            pltpu.VMEM((2 * C_DQK, 2 * T), BF16),
            pltpu.VMEM((2, 2, T, T), F32),
            pltpu.VMEM((2, 2, 1, T), F32),
            pltpu.VMEM((2, 2, T, T), BF16),
            pltpu.VMEM((2, C_DV + DIFF_ONES, T), F32),
            pltpu.VMEM((2, 1, T), F32),
        ],
        compiler_params=_cparams(("parallel", "parallel", "arbitrary")),
        name="diff_attn",
    )(lam, cfar, pt, p, pt, bias, gsub)


def _t5_bucket(rel):
    nb = T5_BUCKETS // 2
    max_exact = nb // 2
    ret = (rel > 0).astype(jnp.int32) * nb
    n = jnp.abs(rel)
    large = max_exact + (jnp.log(jnp.maximum(n, max_exact).astype(F32) / max_exact)
                         / math.log(T5_MAX_DIST / max_exact) * (nb - max_exact)).astype(jnp.int32)
    large = jnp.minimum(large, nb - 1)
    return ret + jnp.where(n < max_exact, n, large)


def _diff_bias(t5_table, nnear):
    T = DIFF_T
    e = np.arange(2 * T)
    amc = np.where(e < T, -e, 2 * T - e)
    rel = jnp.asarray(-np.arange(nnear)[:, None] * T + amc[None, :], jnp.int32)
    base = jnp.moveaxis(t5_table[_t5_bucket(rel)], -1, 0).astype(F32) * LOG2E
    tiles = _toeplitz(base, T, T)
    a = np.arange(T)[:, None]
    c = np.arange(T)[None, :]
    allowed = np.ones((nnear, T, T), bool)
    allowed[0] = (a // CHUNK) <= (c // CHUNK)
    far = t5_table[_t5_bucket(jnp.asarray(-T5_MAX_DIST, jnp.int32))].astype(F32) * LOG2E
    return jnp.where(allowed[None], tiles, NEG), far


def _merge_kernel(ha_ref, hbt_ref, hct_ref, g0_ref, g1_ref, g2_ref, x_ref, wb_ref, wo_ref,
                  n2_ref, wr_ref, br_ref, x1_ref, h2_ref, lg_ref):
    tn = (((0,), (0,)), ((), ()))
    y = g0_ref[...].astype(F32) * jnp.dot(ha_ref[...], wb_ref[0], preferred_element_type=F32)
    y = y + g1_ref[...].astype(F32) * lax.dot_general(hbt_ref[...], wb_ref[1], tn,
                                                      preferred_element_type=F32)
    y = y + g2_ref[...].astype(F32) * lax.dot_general(hct_ref[...], wb_ref[2], tn,
                                                      preferred_element_type=F32)
    x1 = x_ref[...] + jnp.dot(y.astype(BF16), wo_ref[...], preferred_element_type=F32)
    x1_ref[...] = x1
    h2 = x1 * lax.rsqrt(jnp.mean(x1 * x1, axis=-1, keepdims=True) + EPS) * n2_ref[...]
    h2b = h2.astype(BF16)
    h2_ref[...] = h2b
    lg_ref[...] = jnp.dot(h2b, wr_ref[...], preferred_element_type=F32) + br_ref[...]


def _merge(ha, hbt, hct, p, x2, wb, wo, n2, wr, br, tm):
    n = x2.shape[0]
    gcol = SEG_GATES * SEG // D_MODEL

    def rows(width, col=0):
        return pl.BlockSpec((tm, width), lambda i: (i, col))

    def cols():
        return pl.BlockSpec((BRANCH_WIDTH, tm), lambda i: (0, i))

    def full(shape):
        return pl.BlockSpec(shape, lambda i: (0,) * len(shape))

    return pl.pallas_call(
        _merge_kernel,
        grid=(n // tm,),
        in_specs=[
            rows(BRANCH_WIDTH), cols(), cols(),
            rows(D_MODEL, gcol), rows(D_MODEL, gcol + 1), rows(D_MODEL, gcol + 2),
            rows(D_MODEL),
            full((N_BRANCH, BRANCH_WIDTH, D_MODEL)), full((D_MODEL, D_MODEL)),
            full((1, D_MODEL)), full((D_MODEL, LANES)), full((1, LANES)),
        ],
        out_specs=[rows(D_MODEL), rows(D_MODEL), rows(LANES)],
        out_shape=[
            jax.ShapeDtypeStruct((n, D_MODEL), F32),
            jax.ShapeDtypeStruct((n, D_MODEL), BF16),
            jax.ShapeDtypeStruct((n, LANES), F32),
        ],
        compiler_params=_cparams(("parallel",)),
        name="merge",
    )(ha, hbt, hct, p, p, p, x2, wb, wo, n2, wr, br)


def _combine_weights(lg):
    lanef = lax.broadcasted_iota(jnp.int32, lg.shape, 1).astype(F32)
    big = 1e9
    is_g = (lanef >= N_EXPERTS) & (lanef < N_EXPERTS + N_GROUPS)
    gl = jnp.where(is_g, lg, -jnp.inf)
    gmax = jnp.max(gl, axis=-1, keepdims=True)
    g_idx = jnp.min(jnp.where(gl == gmax, lanef - N_EXPERTS, big), axis=-1, keepdims=True)
    p_g = 1.0 / jnp.sum(jnp.exp(gl - gmax), axis=-1, keepdims=True)
    in_grp = (lanef >= g_idx * EXPERTS_PER_GROUP) & (lanef < (g_idx + 1.0) * EXPERTS_PER_GROUP)
    el = jnp.where(in_grp, lg, -jnp.inf)
    ee = jnp.exp(el - jnp.max(el, axis=-1, keepdims=True))
    ep = ee / jnp.sum(ee, axis=-1, keepdims=True)
    ep = jnp.where(in_grp, ep, -1.0)
    v1 = jnp.max(ep, axis=-1, keepdims=True)
    i1 = jnp.min(jnp.where(ep == v1, lanef, big), axis=-1, keepdims=True)
    ep2 = jnp.where(lanef == i1, -1.0, ep)
    v2 = jnp.max(ep2, axis=-1, keepdims=True)
    i2 = jnp.min(jnp.where(ep2 == v2, lanef, big), axis=-1, keepdims=True)
    tot = v1 + v2
    comb = jnp.where(lanef == i1, p_g * (v1 / tot), 0.0) + jnp.where(lanef == i2, p_g * (v2 / tot), 0.0)
    return comb, g_idx


def _moe_kernel(x1_ref, h2_ref, lg_ref, tri_ref, wgu_ref, wd_ref, out_ref,
                pt_ref, xs_ref, ws_ref, ys_ref, meta_ref, *, rcap):
    C = MOE_C
    s = pl.program_id(1)
    tm = h2_ref.shape[0]

    @pl.when(s == 0)
    def _():
        comb, g_idx = _combine_weights(lg_ref[...])
        lanef = lax.broadcasted_iota(jnp.int32, comb.shape, 1).astype(F32)
        mine = lanef == g_idx
        onehot = jnp.where(mine, 1.0, 0.0)
        ranks = jnp.dot(tri_ref[...], onehot.astype(BF16), preferred_element_type=F32)
        dest = jnp.sum(jnp.where(mine, ranks, 0.0), axis=-1, keepdims=True)
        off = jnp.int32(0)
        for g in range(N_GROUPS):
            cnt = jnp.sum(onehot[:, g:g + 1]).astype(jnp.int32)
            nchunk = (cnt + (C - 1)) // C
            meta_ref[g] = off
            meta_ref[N_GROUPS + g] = nchunk
            dest = dest + jnp.where(g_idx == float(g), (off * C).astype(F32), 0.0)
            off = off + nchunk
        slot = lax.broadcasted_iota(jnp.int32, (tm, rcap), 1).astype(F32)
        pt = jnp.where(dest == slot, 1.0, 0.0).astype(BF16)
        pt_ref[...] = pt
        tn = (((0,), (0,)), ((), ()))
        xs_ref[...] = lax.dot_general(pt, h2_ref[...], tn, preferred_element_type=F32).astype(BF16)
        comb_hi = comb.astype(BF16)
        comb_lo = (comb - comb_hi.astype(F32)).astype(BF16)
        ws_ref[...] = (lax.dot_general(pt, comb_hi, tn, preferred_element_type=F32)
                       + lax.dot_general(pt, comb_lo, tn, preferred_element_type=F32))
        ys_ref[...] = jnp.zeros_like(ys_ref)

    def do_chunk(r0, m):
        r0 = pl.multiple_of(r0, C)
        xc = xs_ref[pl.ds(r0, m), :]
        wsc = ws_ref[pl.ds(r0, m), :]
        lane = lax.broadcasted_iota(jnp.int32, (m, LANES), 1)
        y = None
        for k in range(MOE_EPS):
            gu = jnp.dot(xc, wgu_ref[k], preferred_element_type=F32)
            gate = gu[:, 0:D_EXPERT]
            wk = jnp.sum(jnp.where(lane == s * MOE_EPS + k, wsc, 0.0), axis=-1, keepdims=True)
            he = (gate * jax.nn.sigmoid(gate) * gu[:, D_EXPERT:2 * D_EXPERT] * wk).astype(BF16)
            yk = jnp.dot(he, wd_ref[k], preferred_element_type=F32)
            y = yk if y is None else y + yk
        ys_ref[pl.ds(r0, m), :] += y

    g = s // (EXPERTS_PER_GROUP // MOE_EPS)
    start = meta_ref[g]
    nchunk = meta_ref[N_GROUPS + g]

    def pair(i, carry):
        do_chunk((start + 2 * i) * C, 2 * C)
        return carry

    lax.fori_loop(0, nchunk // 2, pair, 0)

    @pl.when(nchunk % 2 == 1)
    def _():
        do_chunk((start + nchunk - 1) * C, C)

    @pl.when(s == pl.num_programs(1) - 1)
    def _():
        out_ref[...] = x1_ref[...] + jnp.dot(pt_ref[...], ys_ref[...].astype(BF16),
                                             preferred_element_type=F32)


def _moe(x1, h2, lg, wgu, wd, tm):
    n = x1.shape[0]
    rcap = tm + N_GROUPS * MOE_C
    idx = np.arange(tm)
    tri = jnp.asarray(idx[None, :] < idx[:, None], BF16)
    return pl.pallas_call(
        functools.partial(_moe_kernel, rcap=rcap),
        grid=(n // tm, N_EXPERTS // MOE_EPS),
        in_specs=[
            pl.BlockSpec((tm, D_MODEL), lambda i, s: (i, 0), pipeline_mode=pl.Buffered(1)),
            pl.BlockSpec((tm, D_MODEL), lambda i, s: (i, 0), pipeline_mode=pl.Buffered(1)),
            pl.BlockSpec((tm, LANES), lambda i, s: (i, 0), pipeline_mode=pl.Buffered(1)),
            pl.BlockSpec((tm, tm), lambda i, s: (0, 0), pipeline_mode=pl.Buffered(1)),
            pl.BlockSpec((MOE_EPS, D_MODEL, 2 * D_EXPERT), lambda i, s: (s, 0, 0)),
            pl.BlockSpec((MOE_EPS, D_EXPERT, D_MODEL), lambda i, s: (s, 0, 0)),
        ],
        out_specs=pl.BlockSpec((tm, D_MODEL), lambda i, s: (i, 0)),
        out_shape=jax.ShapeDtypeStruct((n, D_MODEL), F32),
        scratch_shapes=[
            pltpu.VMEM((tm, rcap), BF16),
            pltpu.VMEM((rcap, D_MODEL), BF16),
            pltpu.VMEM((rcap, LANES), F32),
            pltpu.VMEM((rcap, D_MODEL), F32),
            pltpu.SMEM((2 * N_GROUPS,), jnp.int32),
        ],
        compiler_params=pltpu.CompilerParams(dimension_semantics=("parallel", "arbitrary"),
                                             vmem_limit_bytes=MOE_VMEM_LIMIT),
        name="moe",
    )(x1, h2, lg, tri, wgu, wd)


def _tile(n, pref):
    t = pref
    while n % t:
        t //= 2
    return t


def _mixer_params(layer, norm1_g, w_in, a_conv_w, a_conv_b, a_gate_bias, a_out_norm_g,
                  b_qk_norm_g, b_rel_bias, c_qk_norm_g, c_lambda, c_sub_norm_g, t5_bias,
                  w_branch, w_out, nq_diff):
    n_small = 2 * A_HEADS
    cut = 4 * A_WIDTH
    w_main = jnp.concatenate([w_in[:, :cut], w_in[:, cut + n_small:]], axis=1)
    w_main = w_main.reshape(D_MODEL, N_SEG, SEG)[:, np.asarray(SEG_PERM), :]
    w_main = w_main.reshape(D_MODEL, N_SEG * SEG).astype(BF16)
    w_if = jnp.pad(w_in[:, cut:cut + n_small], ((0, 0), (0, LANES - n_small))).astype(BF16)
    gain = jnp.ones((N_SEG, SEG), F32)
    gain = gain.at[N_ROW_SEG + TSEG_BQ].set(jnp.tile(b_qk_norm_g[0], B_HEADS) * (B_DH ** -0.5 * LOG2E))
    gain = gain.at[SEG_BK].set(jnp.tile(b_qk_norm_g[1], B_HEADS))
    gain = gain.at[N_ROW_SEG + TSEG_CQ].set(
        jnp.tile(c_qk_norm_g[0], 2 * C_HEADS) * (C_DQK ** -0.5 * LOG2E))
    gain = gain.at[SEG_CK].set(jnp.tile(c_qk_norm_g[1], 2 * C_HEADS))
    diff_bias, diff_far = _diff_bias(t5_bias, _diff_nnear(nq_diff))
    blk = np.arange(SEG) // 64
    bd = jnp.asarray(blk[:, None] == blk[None, :], BF16)
    lam_init = 0.8 - 0.6 * math.exp(-0.3 * layer)
    lf32 = c_lambda.astype(F32)
    lam = jnp.exp(jnp.sum(lf32[0] * lf32[1])) - jnp.exp(jnp.sum(lf32[2] * lf32[3])) + lam_init
    return dict(
        g1=norm1_g.reshape(1, D_MODEL), w_main=w_main, w_if=w_if,
        gain=gain.reshape(N_SEG, 1, SEG), bd=bd,
        cw=a_conv_w, cb=a_conv_b.reshape(1, -1),
        gbr=jnp.pad(a_gate_bias, (0, LANES - n_small)).reshape(1, LANES),
        gbc=a_gate_bias.reshape(n_small, 1),
        ag=a_out_norm_g.reshape(1, A_WIDTH),
        band_bias=_band_bias(b_rel_bias),
        diff_bias=diff_bias, diff_far=diff_far,
        lam=lam.reshape(1).astype(F32),
        gsub=(c_sub_norm_g * (1.0 - lam_init)).reshape(C_DV, 1),
        wb=w_branch.astype(BF16), wo=w_out.astype(BF16),
    )


def _layer(x2, bsz, seq, mp, norm2_g, w_group, b_group, w_router, b_router, w_e_gate, w_e_up, w_e_down):
    n = bsz * seq
    p, pt, gif = _inproj(x2, mp["g1"], mp["w_main"], mp["w_if"], mp["gain"], mp["bd"], _tile(n, 512))
    gift = jnp.transpose(gif[:, :2 * A_HEADS])
    ha = _mlstm(p, gif, gift, mp["cw"], mp["cb"], mp["gbr"], mp["gbc"], mp["ag"], bsz, seq)
    hbt = _band(p, pt, mp["band_bias"], bsz, seq)
    hct = _diff(p, pt, mp["diff_bias"], mp["diff_far"], mp["lam"], mp["gsub"], bsz, seq)

    wr = jnp.concatenate([w_router, w_group], axis=1)
    wr = jnp.pad(wr, ((0, 0), (0, LANES - wr.shape[1]))).astype(BF16)
    br = jnp.pad(jnp.concatenate([b_router, b_group]), (0, LANES - N_EXPERTS - N_GROUPS)).reshape(1, LANES)
    x1, h2, lg = _merge(ha, hbt, hct, p, x2, mp["wb"], mp["wo"], norm2_g.reshape(1, D_MODEL), wr, br,
                        _tile(n, 512))
    wgu = jnp.concatenate([w_e_gate, w_e_up], axis=-1).astype(BF16)
    return _moe(x1, h2, lg, wgu, w_e_down.astype(BF16), _tile(n, MOE_TM))


def kernel(x, norm1_g, w_in, a_conv_w, a_conv_b, a_gate_bias, a_out_norm_g, b_qk_norm_g, b_rel_bias,
           c_qk_norm_g, c_lambda, c_sub_norm_g, t5_bias, w_branch, w_out, norm2_g, w_group, b_group,
           w_router, b_router, w_e_gate, w_e_up, w_e_down):
    bsz, seq, _ = x.shape
    assert seq % DIFF_T == 0 and seq % MLSTM_L == 0 and seq % BAND_TQ == 0
    x2 = x.reshape(bsz * seq, D_MODEL)
    for l in range(norm1_g.shape[0]):
        mp = _mixer_params(l, norm1_g[l], w_in[l], a_conv_w[l], a_conv_b[l], a_gate_bias[l],
                           a_out_norm_g[l], b_qk_norm_g[l], b_rel_bias[l], c_qk_norm_g[l], c_lambda[l],
                           c_sub_norm_g[l], t5_bias, w_branch[l], w_out[l], seq // DIFF_T)
        x2 = _layer(x2, bsz, seq, mp, norm2_g[l], w_group[l], b_group[l], w_router[l], b_router[l],
                    w_e_gate[l], w_e_up[l], w_e_down[l])
    return x2.reshape(bsz, seq, D_MODEL)
```

```python
import functools
import math

import numpy as np
import jax
import jax.numpy as jnp
from jax import lax
from jax.experimental import pallas as pl
from jax.experimental.pallas import tpu as pltpu

F32 = jnp.float32
BF16 = jnp.bfloat16

D_MODEL = 1024
CHUNK = 64
EPS = 1e-6
NEG = -1e30
LOG2E = math.log2(math.e)

A_HEADS = 4
A_DH = 128
A_WIDTH = A_HEADS * A_DH
CONV_W = 4
GATE_CAP = 15.0

B_HEADS = 8
B_DH = 64
B_WIDTH = B_HEADS * B_DH
B_LEFT_CHUNKS = 8
B_MAX_REL = 256

C_HEADS = 4
C_DQK = 64
C_DV = 2 * C_DQK
C_WIDTH = C_HEADS * C_DV

T5_BUCKETS = 32
T5_MAX_DIST = 1024

N_BRANCH = 3
BRANCH_WIDTH = 512

N_GROUPS = 4
EXPERTS_PER_GROUP = 8
N_EXPERTS = N_GROUPS * EXPERTS_PER_GROUP
D_EXPERT = D_MODEL // 4

LANES = 128
SEG = 512
N_SEG = 16
VMEM_LIMIT = 48 * 1024 * 1024

SEG_AQ, SEG_AK, SEG_AV, SEG_AO, SEG_BK, SEG_CK, SEG_GATES = 0, 1, 2, 3, 4, 5, 6
N_ROW_SEG = 12
TSEG_BQ, TSEG_BV, TSEG_CQ, TSEG_CV = 0, 1, 2, 3
N_T_SEG = N_SEG - N_ROW_SEG
SEG_PERM = (0, 1, 2, 3, 5, 8, 10, 11, 12, 13, 14, 15, 4, 6, 7, 9)

MLSTM_L = 128
BAND_TQ = 128
BAND_NKB = 1 + (B_LEFT_CHUNKS * CHUNK) // BAND_TQ
BAND_ONES = 16
DIFF_T = 512
DIFF_ONES = 16
MOE_TM = 1024
MOE_C = 128
MOE_EPS = 4
MOE_VMEM_LIMIT = 56 * 1024 * 1024

def _cparams(sem, flags=None):
    return pltpu.CompilerParams(dimension_semantics=sem, vmem_limit_bytes=VMEM_LIMIT, flags=flags)


NORM_SEGS = (SEG_BK, SEG_CK, N_ROW_SEG + TSEG_BQ, N_ROW_SEG + TSEG_CQ)
SIGMOID_SEGS = (SEG_AO,) + tuple(range(SEG_GATES, N_ROW_SEG))


def _head_norm_t(acc_t):
    rows, tm = acc_t.shape
    a3 = acc_t.reshape(rows // 64, 64, tm)
    ssq = jnp.sum(a3 * a3, axis=1, keepdims=True)
    return (a3 * lax.rsqrt(ssq * (1.0 / 64.0) + EPS)).reshape(rows, tm)


def _inproj_kernel(x_ref, g_ref, w_ref, wif_ref, gain_ref, gain_t_ref, p_ref, pt_ref, gif_ref):
    xf = x_ref[...]
    xn = (xf * lax.rsqrt(jnp.mean(xf * xf, axis=-1, keepdims=True) + EPS) * g_ref[...]).astype(BF16)
    gif_ref[...] = jnp.dot(xn, wif_ref[...], preferred_element_type=F32)
    for j in range(N_SEG):
        cols = slice(j * SEG, (j + 1) * SEG)
        acc = jnp.dot(xn, w_ref[:, cols], preferred_element_type=F32)
        if j in NORM_SEGS:
            acc_t = _head_norm_t(jnp.transpose(acc))
            if j < N_ROW_SEG:
                p_ref[:, cols] = (jnp.transpose(acc_t) * gain_ref[j]).astype(BF16)
            else:
                pt_ref[j - N_ROW_SEG] = (acc_t * gain_t_ref[NORM_SEGS.index(j) - 2]).astype(BF16)
            continue
        if j in SIGMOID_SEGS:
            acc = jax.nn.sigmoid(acc)
        if j < N_ROW_SEG:
            p_ref[:, cols] = acc.astype(BF16)
        else:
            pt_ref[j - N_ROW_SEG] = jnp.transpose(acc).astype(BF16)


def _inproj(x2, g, w, wif, gain, tm):
    n = x2.shape[0]
    tsegs = np.asarray([N_ROW_SEG + TSEG_BQ, N_ROW_SEG + TSEG_CQ])
    gain_t = jnp.broadcast_to(gain[tsegs, 0, :, None], (2, SEG, tm))

    def const(shape):
        return pl.BlockSpec(shape, lambda i: (0,) * len(shape), pipeline_mode=pl.Buffered(1))

    return pl.pallas_call(
        _inproj_kernel,
        grid=(n // tm,),
        in_specs=[
            pl.BlockSpec((tm, D_MODEL), lambda i: (i, 0)),
            const((1, D_MODEL)),
            const((D_MODEL, N_SEG * SEG)),
            const((D_MODEL, LANES)),
            const((N_SEG, 1, SEG)),
            const((2, SEG, tm)),
        ],
        out_specs=[
            pl.BlockSpec((tm, N_ROW_SEG * SEG), lambda i: (i, 0)),
            pl.BlockSpec((N_T_SEG, SEG, tm), lambda i: (0, 0, i)),
            pl.BlockSpec((tm, LANES), lambda i: (i, 0)),
        ],
        out_shape=[
            jax.ShapeDtypeStruct((n, N_ROW_SEG * SEG), BF16),
            jax.ShapeDtypeStruct((N_T_SEG, SEG, n), BF16),
            jax.ShapeDtypeStruct((n, LANES), F32),
        ],
        compiler_params=_cparams(("parallel",)),
        name="inproj",
    )(x2, g, w, wif, gain, gain_t)


def _log_sigmoid(z):
    return jnp.minimum(z, 0.0) - jnp.log(1.0 + jnp.exp(-jnp.abs(z)))


def _split3(a):
    hi = a.astype(BF16)
    r1 = a - hi.astype(F32)
    mid = r1.astype(BF16)
    lo = (r1 - mid.astype(F32)).astype(BF16)
    return hi, mid, lo


def _mlstm_kernel(aq_ref, ak_ref, av_ref, ao_ref, gif_ref, gift_ref, cw_ref, cb_ref,
                  gbr_ref, gbc_ref, ag_ref, out_ref, ubuf, c_ref, n_ref, m_ref):
    L = MLSTM_L
    c = pl.program_id(1)

    @pl.when(c == 0)
    def _():
        ubuf[0:8, :] = jnp.zeros((8, 2 * A_WIDTH), F32)
        c_ref[...] = jnp.zeros_like(c_ref)
        n_ref[...] = jnp.zeros_like(n_ref)
        m_ref[...] = jnp.zeros_like(m_ref)

    @pl.when(c > 0)
    def _():
        ubuf[0:8, :] = ubuf[L:L + 8, :]

    ubuf[8:L + 8, 0:A_WIDTH] = aq_ref[...].astype(F32)
    ubuf[8:L + 8, A_WIDTH:2 * A_WIDTH] = ak_ref[...].astype(F32)
    y = cb_ref[...] + cw_ref[0:1, :] * ubuf[8:L + 8, :]
    for t in range(1, CONV_W):
        y = y + cw_ref[t:t + 1, :] * ubuf[8 - t:8 - t + L, :]
    qk = y * jax.nn.sigmoid(y)

    zc = gif_ref[...] + gbr_ref[...]
    ig_c = GATE_CAP * jnp.tanh(zc * (1.0 / GATE_CAP))
    lf_c = _log_sigmoid(zc)
    zr = gift_ref[...] + gbc_ref[...]
    ig_r = GATE_CAP * jnp.tanh(zr * (1.0 / GATE_CAP))
    lf_r = _log_sigmoid(zr)

    row = lax.broadcasted_iota(jnp.int32, (L, L), 0)
    col = lax.broadcasted_iota(jnp.int32, (L, L), 1)
    causal = col <= row
    tril = jnp.where(causal, 1.0, 0.0).astype(BF16)
    triu = jnp.where(row <= col, 1.0, 0.0).astype(BF16)
    b_c = sum(jnp.dot(tril, piece, preferred_element_type=F32) for piece in _split3(lf_c))
    b_r = sum(jnp.dot(piece, triu, preferred_element_type=F32) for piece in _split3(lf_r))

    for h in range(A_HEADS):
        sl = slice(h * A_DH, (h + 1) * A_DH)
        q = qk[:, sl]
        k = qk[:, A_WIDTH + h * A_DH:A_WIDTH + (h + 1) * A_DH] * (A_DH ** -0.5)
        qb = q.astype(BF16)
        kb = k.astype(BF16)
        v = av_ref[:, sl]
        bcol = b_c[:, A_HEADS + h:A_HEADS + h + 1]
        brow = b_r[A_HEADS + h:A_HEADS + h + 1, :]
        igcol = ig_c[:, h:h + 1]
        igrow = ig_r[h:h + 1, :]
        b_last = bcol[L - 1:L, :]
        C = c_ref[h]
        nvec = n_ref[h]
        m_prev = m_ref[h][:, 0:1]

        dmat = jnp.where(causal, bcol - brow + igrow, NEG)
        inter = bcol + m_prev
        m_t = jnp.maximum(inter, jnp.max(dmat, axis=-1, keepdims=True))
        w_inter = jnp.exp(inter - m_t)
        s = lax.dot_general(qb, kb, (((1,), (1,)), ((), ())), preferred_element_type=F32)
        s = s * jnp.exp(dmat - m_t)
        num = w_inter * jnp.dot(qb, C.astype(BF16), preferred_element_type=F32)
        num = num + jnp.dot(s.astype(BF16), v, preferred_element_type=F32)
        den = w_inter * jnp.sum(q * nvec, axis=-1, keepdims=True) + jnp.sum(s, axis=-1, keepdims=True)
        hh = num / jnp.maximum(jnp.abs(den), jnp.exp(-m_t))

        g_end = b_last - bcol + igcol
        m_new = jnp.maximum(b_last + m_prev, jnp.max(g_end, axis=0, keepdims=True))
        dec = jnp.exp(b_last + m_prev - m_new)
        kw = jnp.exp(g_end - m_new) * k
        c_ref[h] = dec * C + lax.dot_general(kw.astype(BF16), v, (((0,), (0,)), ((), ())),
                                             preferred_element_type=F32)
        n_ref[h] = dec * nvec + jnp.sum(kw, axis=0, keepdims=True)
        m_ref[h] = jnp.broadcast_to(m_new, (1, LANES))

        hn = hh * lax.rsqrt(jnp.mean(hh * hh, axis=-1, keepdims=True) + EPS) * ag_ref[:, sl]
        out_ref[:, sl] = (hn * ao_ref[:, sl].astype(F32)).astype(BF16)


def _mlstm(p, gif, gift, cw, cb, gbr, gbc, ag, bsz, seq):
    L = MLSTM_L
    nc = seq // L
    n = bsz * seq

    def seg(j):
        return pl.BlockSpec((L, SEG), lambda b, c: (b * nc + c, j))

    def full(shape):
        return pl.BlockSpec(shape, lambda b, c: (0,) * len(shape))

    return pl.pallas_call(
        _mlstm_kernel,
        grid=(bsz, nc),
        in_specs=[
            seg(0), seg(1), seg(2), seg(SEG_AO),
            pl.BlockSpec((L, LANES), lambda b, c: (b * nc + c, 0)),
            pl.BlockSpec((8, L), lambda b, c: (0, b * nc + c)),
            full((CONV_W, 2 * A_WIDTH)), full((1, 2 * A_WIDTH)),
            full((1, LANES)), full((8, 1)), full((1, A_WIDTH)),
        ],
        out_specs=pl.BlockSpec((L, A_WIDTH), lambda b, c: (b * nc + c, 0)),
        out_shape=jax.ShapeDtypeStruct((n, A_WIDTH), BF16),
        scratch_shapes=[
            pltpu.VMEM((L + 8, 2 * A_WIDTH), F32),
            pltpu.VMEM((A_HEADS, A_DH, A_DH), F32),
            pltpu.VMEM((A_HEADS, 1, A_DH), F32),
            pltpu.VMEM((A_HEADS, 1, LANES), F32),
        ],
        compiler_params=_cparams(("parallel", "arbitrary")),
        name="mlstm",
    )(p, p, p, p, gif, gift, cw, cb, gbr, gbc, ag)


def _band_kernel(*refs):
    nkb = BAND_NKB
    qt_ref = refs[0]
    k_refs = refs[1:1 + nkb]
    vt_refs = refs[1 + nkb:1 + 2 * nkb]
    bias_ref = refs[1 + 2 * nkb]
    out_ref = refs[2 + 2 * nkb]
    s_ref, mx_ref = refs[3 + 2 * nkb:5 + 2 * nkb]
    tq = BAND_TQ
    nk = nkb * tq
    i = pl.program_id(1)

    def compute(mask_start):
        k_all = jnp.concatenate([r[...] for r in k_refs], axis=0)
        vt_all = jnp.concatenate([r[...] for r in vt_refs], axis=1)
        ones = jnp.ones((BAND_ONES, nk), BF16)
        row = lax.broadcasted_iota(jnp.int32, (LANES, tq), 0)
        lo = row < B_DH
        if mask_start:
            kidx = lax.broadcasted_iota(jnp.int32, (nk, 1), 0)
            valid = (kidx + (i - (nkb - 1)) * tq) >= 0

        def score(p):
            rows = slice(p * LANES, (p + 1) * LANES)
            qtp = qt_ref[rows, :]
            zero = jnp.zeros_like(qtp)
            qbd = jnp.concatenate([jnp.where(lo, qtp, zero), jnp.where(lo, zero, qtp)], axis=1)
            s = jnp.dot(k_all[:, rows], qbd, preferred_element_type=F32) + bias_ref[p]
            if mask_start:
                s = jnp.where(valid, s, NEG)
            s_ref[p & 1] = s
            mx_ref[p & 1] = jnp.max(s, axis=0, keepdims=True)

        def finish(p):
            rows = slice(p * LANES, (p + 1) * LANES)
            pr = jnp.exp2((s_ref[p & 1] - mx_ref[p & 1]).astype(BF16))
            o = jnp.dot(jnp.concatenate([vt_all[rows, :], ones], axis=0), pr,
                        preferred_element_type=F32)
            o = o[0:LANES, :] / o[LANES:LANES + 1, :]
            out_ref[rows, :] = jnp.where(lo, o[:, 0:tq], o[:, tq:2 * tq]).astype(BF16)

        score(0)
        for p in range(B_HEADS // 2):
            if p + 1 < B_HEADS // 2:
                score(p + 1)
            finish(p)

    @pl.when(i < nkb - 1)
    def _():
        compute(True)

    @pl.when(i >= nkb - 1)
    def _():
        compute(False)


def _band(p, pt, bias, bsz, seq):
    tq = BAND_TQ
    nkb = BAND_NKB
    nq = seq // tq
    n = bsz * seq

    def kblk(d):
        return pl.BlockSpec((tq, SEG), lambda b, i: (b * nq + jnp.maximum(i - d, 0), SEG_BK))

    def vblk(d):
        return pl.BlockSpec((None, SEG, tq), lambda b, i: (TSEG_BV, 0, b * nq + jnp.maximum(i - d, 0)))

    in_specs = [pl.BlockSpec((None, SEG, tq), lambda b, i: (TSEG_BQ, 0, b * nq + i))]
    in_specs += [kblk(d) for d in range(nkb - 1, -1, -1)]
    in_specs += [vblk(d) for d in range(nkb - 1, -1, -1)]
    in_specs += [pl.BlockSpec(bias.shape, lambda b, i: (0, 0, 0))]
    return pl.pallas_call(
        _band_kernel,
        grid=(bsz, nq),
        in_specs=in_specs,
        out_specs=pl.BlockSpec((B_WIDTH, tq), lambda b, i: (0, b * nq + i)),
        out_shape=jax.ShapeDtypeStruct((B_WIDTH, n), BF16),
        scratch_shapes=[
            pltpu.VMEM((2, nkb * tq, 2 * tq), F32),
            pltpu.VMEM((2, 1, 2 * tq), F32),
        ],
        compiler_params=_cparams(("parallel", "parallel")),
        name="band_attn",
    )(pt, *([p] * nkb), *([pt] * nkb), bias)


def _toeplitz(base, m, n):
    period = base.shape[-1]
    assert n <= period - 1
    reps = (1,) * (base.ndim - 1) + (m,)
    big = jnp.tile(base, reps)[..., :m * (period - 1)]
    return big.reshape(base.shape[:-1] + (m, period - 1))[..., :n]


def _band_bias(b_rel):
    tq = BAND_TQ
    nk = BAND_NKB * tq
    period = tq + nk
    e = np.arange(period)
    e = np.where(e < nk, e, e - period)
    rel = np.clip((nk - tq) - e, -B_MAX_REL, B_MAX_REL) + B_MAX_REL
    bias = _toeplitz(jnp.transpose(b_rel[rel]).astype(F32), tq, nk)
    qpos = np.arange(tq)
    kpos = np.arange(nk) - (nk - tq)
    qc = qpos[:, None] // CHUNK
    kc = np.floor_divide(kpos[None, :], CHUNK)
    allowed = (kc <= qc) & (kc >= qc - B_LEFT_CHUNKS)
    bias = jnp.where(allowed[None], bias * LOG2E, NEG)
    return jnp.swapaxes(bias.reshape(B_HEADS // 2, 2 * tq, nk), 1, 2)


def _diff_kernel(lam_ref, cfar_ref, qt_ref, k_ref, vt_ref, bias_ref, g_ref, out_ref,
                 qbd_ref, s_ref, mx_ref, acc_ref, m_ref, *, nnear):
    T = DIFF_T
    h = pl.program_id(1)
    qi = pl.program_id(2)
    cfar = cfar_ref[h]
    qt = qt_ref[...]
    row = lax.broadcasted_iota(jnp.int32, (2 * C_DQK, T), 0)
    zero = jnp.zeros_like(qt)
    qbd_ref[:, 0:T] = jnp.where(row < C_DQK, qt, zero)
    qbd_ref[:, T:2 * T] = jnp.where(row < C_DQK, zero, qt)
    m_ref[...] = jnp.full(m_ref.shape, NEG, F32)
    acc_ref[...] = jnp.zeros_like(acc_ref)

    def near_bias(t):
        return bias_ref[t] if t < nnear else None

    def stage_a(t, slot, bias, maps=(0, 1)):
        j = jnp.maximum(qi - t, 0)
        k = k_ref[pl.ds(pl.multiple_of(j * T, T), T), :]
        for mp in maps:
            sm = jnp.dot(k, qbd_ref[:, mp * T:(mp + 1) * T], preferred_element_type=F32)
            if bias is not None:
                sm = sm + bias
            s_ref[slot, mp] = sm
            mx = jnp.max(sm, axis=0, keepdims=True)
            mx_ref[slot, mp] = mx + cfar if bias is None else mx

    def stage_bc(t, slot, far, maps=(0, 1)):
        j = qi - t
        vt = jnp.concatenate([vt_ref[:, pl.ds(pl.multiple_of(j * T, T), T)],
                              jnp.ones((DIFF_ONES, T), BF16)], axis=0)
        for mp in maps:
            m_old = m_ref[mp]
            m_new = jnp.maximum(m_old, mx_ref[slot, mp])
            shift = m_new - cfar if far else m_new
            pr = jnp.exp2((s_ref[slot, mp] - shift).astype(BF16))
            m_ref[mp] = m_new
            acc_ref[mp] = jnp.exp2(m_old - m_new) * acc_ref[mp] + jnp.dot(
                vt, pr, preferred_element_type=F32)

    def full_step(t, slot, far, next_bias):
        for mp in range(2):
            stage_a(t + 1, 1 - slot, next_bias, (mp,))
            stage_bc(t, slot, far, (mp,))

    stage_a(0, 0, near_bias(0))
    for t in range(nnear):
        @pl.when(qi >= t)
        def _(t=t):
            full_step(t, t & 1, False, near_bias(t + 1))

    @pl.when(qi >= nnear)
    def _():
        n_full = qi - nnear
        s0 = nnear & 1

        def pair(i, carry):
            full_step(nnear + 2 * i, s0, True, None)
            full_step(nnear + 2 * i + 1, 1 - s0, True, None)
            return carry

        lax.fori_loop(0, n_full // 2, pair, 0)

        @pl.when(n_full % 2 == 1)
        def _():
            full_step(qi - 1, s0, True, None)

        stage_bc(qi, qi & 1, True)

    o1 = acc_ref[0, 0:C_DV, :] / acc_ref[0, C_DV:C_DV + 1, :]
    o2 = acc_ref[1, 0:C_DV, :] / acc_ref[1, C_DV:C_DV + 1, :]
    o = o1 - lam_ref[0] * o2
    on = o * lax.rsqrt(jnp.mean(o * o, axis=0, keepdims=True) + EPS) * g_ref[...]
    out_ref[...] = on.astype(BF16)


def _diff_nnear(nq):
    d_sat = -(-(T5_MAX_DIST - 1 + DIFF_T) // DIFF_T)
    return min(d_sat, nq)


def _diff(p, pt, bias, cfar, lam, gsub, bsz, seq):
    T = DIFF_T
    nq = seq // T
    nnear = bias.shape[1]
    kcol = SEG_CK * SEG // LANES
    n = bsz * seq
    return pl.pallas_call(
        functools.partial(_diff_kernel, nnear=nnear),
        grid=(bsz, C_HEADS, nq),
        in_specs=[
            pl.BlockSpec(memory_space=pltpu.SMEM),
            pl.BlockSpec(memory_space=pltpu.SMEM),
            pl.BlockSpec((None, 2 * C_DQK, T), lambda b, h, i: (TSEG_CQ, h, b * nq + i)),
            pl.BlockSpec((seq, LANES), lambda b, h, i: (b, kcol + h)),
            pl.BlockSpec((None, C_DV, seq), lambda b, h, i: (TSEG_CV, h, b)),
            pl.BlockSpec((None, nnear, T, T), lambda b, h, i: (h, 0, 0, 0)),
            pl.BlockSpec((C_DV, 1), lambda b, h, i: (0, 0)),
        ],
        out_specs=pl.BlockSpec((C_DV, T), lambda b, h, i: (h, b * nq + i)),
        out_shape=jax.ShapeDtypeStruct((C_WIDTH, n), BF16),
        scratch_shapes=[
            pltpu.VMEM((2 * C_DQK, 2 * T), BF16),
            pltpu.VMEM((2, 2, T, T), F32),
            pltpu.VMEM((2, 2, 1, T), F32),
            pltpu.VMEM((2, C_DV + DIFF_ONES, T), F32),
            pltpu.VMEM((2, 1, T), F32),
        ],
        compiler_params=_cparams(("parallel", "parallel", "arbitrary")),
        name="diff_attn",
    )(lam, cfar, pt, p, pt, bias, gsub)


def _t5_bucket(rel):
    nb = T5_BUCKETS // 2
    max_exact = nb // 2
    ret = (rel > 0).astype(jnp.int32) * nb
    n = jnp.abs(rel)
    large = max_exact + (jnp.log(jnp.maximum(n, max_exact).astype(F32) / max_exact)
                         / math.log(T5_MAX_DIST / max_exact) * (nb - max_exact)).astype(jnp.int32)
    large = jnp.minimum(large, nb - 1)
    return ret + jnp.where(n < max_exact, n, large)


def _diff_bias(t5_table, nnear):
    T = DIFF_T
    e = np.arange(2 * T)
    amc = np.where(e < T, -e, 2 * T - e)
    rel = jnp.asarray(-np.arange(nnear)[:, None] * T + amc[None, :], jnp.int32)
    base = jnp.moveaxis(t5_table[_t5_bucket(rel)], -1, 0).astype(F32) * LOG2E
    tiles = _toeplitz(base, T, T)
    a = np.arange(T)[:, None]
    c = np.arange(T)[None, :]
    allowed = np.ones((nnear, T, T), bool)
    allowed[0] = (a // CHUNK) <= (c // CHUNK)
    far = t5_table[_t5_bucket(jnp.asarray(-T5_MAX_DIST, jnp.int32))].astype(F32) * LOG2E
    return jnp.where(allowed[None], tiles, NEG), far


def _merge_kernel(ha_ref, hbt_ref, hct_ref, g0_ref, g1_ref, g2_ref, x_ref, wb_ref, wo_ref,
                  n2_ref, wr_ref, br_ref, x1_ref, h2_ref, lg_ref):
    tn = (((0,), (0,)), ((), ()))
    y = g0_ref[...].astype(F32) * jnp.dot(ha_ref[...], wb_ref[0], preferred_element_type=F32)
    y = y + g1_ref[...].astype(F32) * lax.dot_general(hbt_ref[...], wb_ref[1], tn,
                                                      preferred_element_type=F32)
    y = y + g2_ref[...].astype(F32) * lax.dot_general(hct_ref[...], wb_ref[2], tn,
                                                      preferred_element_type=F32)
    x1 = x_ref[...] + jnp.dot(y.astype(BF16), wo_ref[...], preferred_element_type=F32)
    x1_ref[...] = x1
    h2 = x1 * lax.rsqrt(jnp.mean(x1 * x1, axis=-1, keepdims=True) + EPS) * n2_ref[...]
    h2b = h2.astype(BF16)
    h2_ref[...] = h2b
    lg_ref[...] = jnp.dot(h2b, wr_ref[...], preferred_element_type=F32) + br_ref[...]


def _merge(ha, hbt, hct, p, x2, wb, wo, n2, wr, br, tm):
    n = x2.shape[0]
    gcol = SEG_GATES * SEG // D_MODEL

    def rows(width, col=0):
        return pl.BlockSpec((tm, width), lambda i: (i, col))

    def cols():
        return pl.BlockSpec((BRANCH_WIDTH, tm), lambda i: (0, i))

    def full(shape):
        return pl.BlockSpec(shape, lambda i: (0,) * len(shape))

    return pl.pallas_call(
        _merge_kernel,
        grid=(n // tm,),
        in_specs=[
            rows(BRANCH_WIDTH), cols(), cols(),
            rows(D_MODEL, gcol), rows(D_MODEL, gcol + 1), rows(D_MODEL, gcol + 2),
            rows(D_MODEL),
            full((N_BRANCH, BRANCH_WIDTH, D_MODEL)), full((D_MODEL, D_MODEL)),
            full((1, D_MODEL)), full((D_MODEL, LANES)), full((1, LANES)),
        ],
        out_specs=[rows(D_MODEL), rows(D_MODEL), rows(LANES)],
        out_shape=[
            jax.ShapeDtypeStruct((n, D_MODEL), F32),
            jax.ShapeDtypeStruct((n, D_MODEL), BF16),
            jax.ShapeDtypeStruct((n, LANES), F32),
        ],
        compiler_params=_cparams(("parallel",)),
        name="merge",
    )(ha, hbt, hct, p, p, p, x2, wb, wo, n2, wr, br)


def _combine_weights(lg):
    lanef = lax.broadcasted_iota(jnp.int32, lg.shape, 1).astype(F32)
    big = 1e9
    is_g = (lanef >= N_EXPERTS) & (lanef < N_EXPERTS + N_GROUPS)
    gl = jnp.where(is_g, lg, -jnp.inf)
    gmax = jnp.max(gl, axis=-1, keepdims=True)
    g_idx = jnp.min(jnp.where(gl == gmax, lanef - N_EXPERTS, big), axis=-1, keepdims=True)
    p_g = 1.0 / jnp.sum(jnp.exp(gl - gmax), axis=-1, keepdims=True)
    in_grp = (lanef >= g_idx * EXPERTS_PER_GROUP) & (lanef < (g_idx + 1.0) * EXPERTS_PER_GROUP)
    el = jnp.where(in_grp, lg, -jnp.inf)
    ee = jnp.exp(el - jnp.max(el, axis=-1, keepdims=True))
    ep = ee / jnp.sum(ee, axis=-1, keepdims=True)
    ep = jnp.where(in_grp, ep, -1.0)
    v1 = jnp.max(ep, axis=-1, keepdims=True)
    i1 = jnp.min(jnp.where(ep == v1, lanef, big), axis=-1, keepdims=True)
    ep2 = jnp.where(lanef == i1, -1.0, ep)
    v2 = jnp.max(ep2, axis=-1, keepdims=True)
    i2 = jnp.min(jnp.where(ep2 == v2, lanef, big), axis=-1, keepdims=True)
    tot = v1 + v2
    comb = jnp.where(lanef == i1, p_g * (v1 / tot), 0.0) + jnp.where(lanef == i2, p_g * (v2 / tot), 0.0)
    return comb, g_idx


def _moe_kernel(x1_ref, h2_ref, lg_ref, tri_ref, wgu_ref, wd_ref, out_ref,
                pt_ref, xs_ref, ws_ref, ys_ref, meta_ref, *, rcap):
    C = MOE_C
    s = pl.program_id(1)
    tm = h2_ref.shape[0]

    @pl.when(s == 0)
    def _():
        comb, g_idx = _combine_weights(lg_ref[...])
        lanef = lax.broadcasted_iota(jnp.int32, comb.shape, 1).astype(F32)
        mine = lanef == g_idx
        onehot = jnp.where(mine, 1.0, 0.0)
        ranks = jnp.dot(tri_ref[...], onehot.astype(BF16), preferred_element_type=F32)
        dest = jnp.sum(jnp.where(mine, ranks, 0.0), axis=-1, keepdims=True)
        off = jnp.int32(0)
        for g in range(N_GROUPS):
            cnt = jnp.sum(onehot[:, g:g + 1]).astype(jnp.int32)
            nchunk = (cnt + (C - 1)) // C
            meta_ref[g] = off
            meta_ref[N_GROUPS + g] = nchunk
            dest = dest + jnp.where(g_idx == float(g), (off * C).astype(F32), 0.0)
            off = off + nchunk
        slot = lax.broadcasted_iota(jnp.int32, (tm, rcap), 1).astype(F32)
        pt = jnp.where(dest == slot, 1.0, 0.0).astype(BF16)
        pt_ref[...] = pt
        comb_hi = comb.astype(BF16)
        comb_lo = (comb - comb_hi.astype(F32)).astype(BF16)
        packed = jnp.concatenate([h2_ref[...], comb_hi, comb_lo], axis=1)
        srt = lax.dot_general(pt, packed, (((0,), (0,)), ((), ())), preferred_element_type=F32)
        xs_ref[...] = srt[:, 0:D_MODEL].astype(BF16)
        ws_ref[...] = srt[:, D_MODEL:D_MODEL + LANES] + srt[:, D_MODEL + LANES:D_MODEL + 2 * LANES]
        ys_ref[...] = jnp.zeros_like(ys_ref)

    def do_chunk(r0, m):
        r0 = pl.multiple_of(r0, C)
        xc = xs_ref[pl.ds(r0, m), :]
        wsc = ws_ref[pl.ds(r0, m), :]
        lane = lax.broadcasted_iota(jnp.int32, (m, LANES), 1)
        y = None
        for k in range(MOE_EPS):
            gu = jnp.dot(xc, wgu_ref[k], preferred_element_type=F32)
            gate = gu[:, 0:D_EXPERT]
            wk = jnp.sum(jnp.where(lane == s * MOE_EPS + k, wsc, 0.0), axis=-1, keepdims=True)
            he = (gate * jax.nn.sigmoid(gate) * gu[:, D_EXPERT:2 * D_EXPERT] * wk).astype(BF16)
            yk = jnp.dot(he, wd_ref[k], preferred_element_type=F32)
            y = yk if y is None else y + yk
        ys_ref[pl.ds(r0, m), :] += y

    g = s // (EXPERTS_PER_GROUP // MOE_EPS)
    start = meta_ref[g]
    nchunk = meta_ref[N_GROUPS + g]

    def pair(i, carry):
        do_chunk((start + 2 * i) * C, 2 * C)
        return carry

    lax.fori_loop(0, nchunk // 2, pair, 0)

    @pl.when(nchunk % 2 == 1)
    def _():
        do_chunk((start + nchunk - 1) * C, C)

    @pl.when(s == pl.num_programs(1) - 1)
    def _():
        out_ref[...] = x1_ref[...] + jnp.dot(pt_ref[...], ys_ref[...].astype(BF16),
                                             preferred_element_type=F32)


def _moe(x1, h2, lg, wgu, wd, tm):
    n = x1.shape[0]
    rcap = tm + N_GROUPS * MOE_C
    idx = np.arange(tm)
    tri = jnp.asarray(idx[None, :] < idx[:, None], BF16)
    return pl.pallas_call(
        functools.partial(_moe_kernel, rcap=rcap),
        grid=(n // tm, N_EXPERTS // MOE_EPS),
        in_specs=[
            pl.BlockSpec((tm, D_MODEL), lambda i, s: (i, 0), pipeline_mode=pl.Buffered(1)),
            pl.BlockSpec((tm, D_MODEL), lambda i, s: (i, 0), pipeline_mode=pl.Buffered(1)),
            pl.BlockSpec((tm, LANES), lambda i, s: (i, 0), pipeline_mode=pl.Buffered(1)),
            pl.BlockSpec((tm, tm), lambda i, s: (0, 0), pipeline_mode=pl.Buffered(1)),
            pl.BlockSpec((MOE_EPS, D_MODEL, 2 * D_EXPERT), lambda i, s: (s, 0, 0)),
            pl.BlockSpec((MOE_EPS, D_EXPERT, D_MODEL), lambda i, s: (s, 0, 0)),
        ],
        out_specs=pl.BlockSpec((tm, D_MODEL), lambda i, s: (i, 0)),
        out_shape=jax.ShapeDtypeStruct((n, D_MODEL), F32),
        scratch_shapes=[
            pltpu.VMEM((tm, rcap), BF16),
            pltpu.VMEM((rcap, D_MODEL), BF16),
            pltpu.VMEM((rcap, LANES), F32),
            pltpu.VMEM((rcap, D_MODEL), F32),
            pltpu.SMEM((2 * N_GROUPS,), jnp.int32),
        ],
        compiler_params=pltpu.CompilerParams(dimension_semantics=("parallel", "arbitrary"),
                                             vmem_limit_bytes=MOE_VMEM_LIMIT),
        name="moe",
    )(x1, h2, lg, tri, wgu, wd)


def _tile(n, pref):
    t = pref
    while n % t:
        t //= 2
    return t


def _mixer_params(layer, norm1_g, w_in, a_conv_w, a_conv_b, a_gate_bias, a_out_norm_g,
                  b_qk_norm_g, b_rel_bias, c_qk_norm_g, c_lambda, c_sub_norm_g, t5_bias,
                  w_branch, w_out, nq_diff):
    n_small = 2 * A_HEADS
    cut = 4 * A_WIDTH
    w_main = jnp.concatenate([w_in[:, :cut], w_in[:, cut + n_small:]], axis=1)
    w_main = w_main.reshape(D_MODEL, N_SEG, SEG)[:, np.asarray(SEG_PERM), :]
    w_main = w_main.reshape(D_MODEL, N_SEG * SEG).astype(BF16)
    w_if = jnp.pad(w_in[:, cut:cut + n_small], ((0, 0), (0, LANES - n_small))).astype(BF16)
    gain = jnp.ones((N_SEG, SEG), F32)
    gain = gain.at[N_ROW_SEG + TSEG_BQ].set(jnp.tile(b_qk_norm_g[0], B_HEADS) * (B_DH ** -0.5 * LOG2E))
    gain = gain.at[SEG_BK].set(jnp.tile(b_qk_norm_g[1], B_HEADS))
    gain = gain.at[N_ROW_SEG + TSEG_CQ].set(
        jnp.tile(c_qk_norm_g[0], 2 * C_HEADS) * (C_DQK ** -0.5 * LOG2E))
    gain = gain.at[SEG_CK].set(jnp.tile(c_qk_norm_g[1], 2 * C_HEADS))
    diff_bias, diff_far = _diff_bias(t5_bias, _diff_nnear(nq_diff))
    lam_init = 0.8 - 0.6 * math.exp(-0.3 * layer)
    lf32 = c_lambda.astype(F32)
    lam = jnp.exp(jnp.sum(lf32[0] * lf32[1])) - jnp.exp(jnp.sum(lf32[2] * lf32[3])) + lam_init
    return dict(
        g1=norm1_g.reshape(1, D_MODEL), w_main=w_main, w_if=w_if,
        gain=gain.reshape(N_SEG, 1, SEG),
        cw=a_conv_w, cb=a_conv_b.reshape(1, -1),
        gbr=jnp.pad(a_gate_bias, (0, LANES - n_small)).reshape(1, LANES),
        gbc=a_gate_bias.reshape(n_small, 1),
        ag=a_out_norm_g.reshape(1, A_WIDTH),
        band_bias=_band_bias(b_rel_bias),
        diff_bias=diff_bias, diff_far=diff_far,
        lam=lam.reshape(1).astype(F32),
        gsub=(c_sub_norm_g * (1.0 - lam_init)).reshape(C_DV, 1),
        wb=w_branch.astype(BF16), wo=w_out.astype(BF16),
    )


def _layer(x2, bsz, seq, mp, norm2_g, w_group, b_group, w_router, b_router, w_e_gate, w_e_up, w_e_down):
    n = bsz * seq
    p, pt, gif = _inproj(x2, mp["g1"], mp["w_main"], mp["w_if"], mp["gain"], _tile(n, 512))
    gift = jnp.transpose(gif[:, :2 * A_HEADS])
    ha = _mlstm(p, gif, gift, mp["cw"], mp["cb"], mp["gbr"], mp["gbc"], mp["ag"], bsz, seq)
    hbt = _band(p, pt, mp["band_bias"], bsz, seq)
    hct = _diff(p, pt, mp["diff_bias"], mp["diff_far"], mp["lam"], mp["gsub"], bsz, seq)

    wr = jnp.concatenate([w_router, w_group], axis=1)
    wr = jnp.pad(wr, ((0, 0), (0, LANES - wr.shape[1]))).astype(BF16)
    br = jnp.pad(jnp.concatenate([b_router, b_group]), (0, LANES - N_EXPERTS - N_GROUPS)).reshape(1, LANES)
    x1, h2, lg = _merge(ha, hbt, hct, p, x2, mp["wb"], mp["wo"], norm2_g.reshape(1, D_MODEL), wr, br,
                        _tile(n, 512))
    wgu = jnp.concatenate([w_e_gate, w_e_up], axis=-1).astype(BF16)
    return _moe(x1, h2, lg, wgu, w_e_down.astype(BF16), _tile(n, MOE_TM))


def kernel(x, norm1_g, w_in, a_conv_w, a_conv_b, a_gate_bias, a_out_norm_g, b_qk_norm_g, b_rel_bias,
           c_qk_norm_g, c_lambda, c_sub_norm_g, t5_bias, w_branch, w_out, norm2_g, w_group, b_group,
           w_router, b_router, w_e_gate, w_e_up, w_e_down):
    bsz, seq, _ = x.shape
    assert seq % DIFF_T == 0 and seq % MLSTM_L == 0 and seq % BAND_TQ == 0
    x2 = x.reshape(bsz * seq, D_MODEL)
    for l in range(norm1_g.shape[0]):
        mp = _mixer_params(l, norm1_g[l], w_in[l], a_conv_w[l], a_conv_b[l], a_gate_bias[l],
                           a_out_norm_g[l], b_qk_norm_g[l], b_rel_bias[l], c_qk_norm_g[l], c_lambda[l],
                           c_sub_norm_g[l], t5_bias, w_branch[l], w_out[l], seq // DIFF_T)
        x2 = _layer(x2, bsz, seq, mp, norm2_g[l], w_group[l], b_group[l], w_router[l], b_router[l],
                    w_e_gate[l], w_e_up[l], w_e_down[l])
    return x2.reshape(bsz, seq, D_MODEL)
```

```python
import functools
import math

import numpy as np
import jax
import jax.numpy as jnp
from jax import lax
from jax.experimental import pallas as pl
from jax.experimental.pallas import tpu as pltpu

F32 = jnp.float32
BF16 = jnp.bfloat16

D_MODEL = 1024
CHUNK = 64
EPS = 1e-6
NEG = -1e30
LOG2E = math.log2(math.e)

A_HEADS = 4
A_DH = 128
A_WIDTH = A_HEADS * A_DH
CONV_W = 4
GATE_CAP = 15.0

B_HEADS = 8
B_DH = 64
B_WIDTH = B_HEADS * B_DH
B_LEFT_CHUNKS = 8
B_MAX_REL = 256

C_HEADS = 4
C_DQK = 64
C_DV = 2 * C_DQK
C_WIDTH = C_HEADS * C_DV

T5_BUCKETS = 32
T5_MAX_DIST = 1024

N_BRANCH = 3
BRANCH_WIDTH = 512

N_GROUPS = 4
EXPERTS_PER_GROUP = 8
N_EXPERTS = N_GROUPS * EXPERTS_PER_GROUP
D_EXPERT = D_MODEL // 4

LANES = 128
SEG = 512
N_SEG = 16
VMEM_LIMIT = 48 * 1024 * 1024

SEG_GATES, SEG_AQ, SEG_AK, SEG_BK, SEG_CK = 0, 6, 7, 8, 9
N_ROW_SEG = 10
TSEG_AV, TSEG_AO, TSEG_BQ, TSEG_BV, TSEG_CQ, TSEG_CV = 0, 1, 2, 3, 4, 5
N_T_SEG = N_SEG - N_ROW_SEG
SEG_PERM = (10, 11, 12, 13, 14, 15, 0, 1, 5, 8, 2, 3, 4, 6, 7, 9)

MLSTM_L = 128
BAND_TQ = 128
BAND_NKB = 1 + (B_LEFT_CHUNKS * CHUNK) // BAND_TQ
BAND_ONES = 16
DIFF_T = 512
DIFF_ONES = 16
MOE_TM = 1024
MOE_C = 128
MOE_EPS = EXPERTS_PER_GROUP
MOE_VMEM_LIMIT = 56 * 1024 * 1024

def _cparams(sem, flags=None):
    return pltpu.CompilerParams(dimension_semantics=sem, vmem_limit_bytes=VMEM_LIMIT, flags=flags)


NORM_SEGS = (SEG_BK, SEG_CK, N_ROW_SEG + TSEG_BQ, N_ROW_SEG + TSEG_CQ)
SIGMOID_SEGS = (N_ROW_SEG + TSEG_AO,) + tuple(range(SEG_GATES, SEG_GATES + 6))


def _head_norm_t(acc_t):
    rows, tm = acc_t.shape
    a3 = acc_t.reshape(rows // 64, 64, tm)
    ssq = jnp.sum(a3 * a3, axis=1, keepdims=True)
    return (a3 * lax.rsqrt(ssq * (1.0 / 64.0) + EPS)).reshape(rows, tm)


def _inproj_kernel(x_ref, g_ref, w_ref, wif_ref, gain_ref, gain_t_ref, p_ref, pt_ref, gif_ref):
    xf = x_ref[...]
    xn = (xf * lax.rsqrt(jnp.mean(xf * xf, axis=-1, keepdims=True) + EPS) * g_ref[...]).astype(BF16)
    gif_ref[...] = jnp.dot(xn, wif_ref[...], preferred_element_type=F32)
    for j in range(N_SEG):
        cols = slice(j * SEG, (j + 1) * SEG)
        acc = jnp.dot(xn, w_ref[:, cols], preferred_element_type=F32)
        if j in NORM_SEGS:
            acc_t = _head_norm_t(jnp.transpose(acc))
            if j < N_ROW_SEG:
                p_ref[:, cols] = (jnp.transpose(acc_t) * gain_ref[j]).astype(BF16)
            else:
                pt_ref[j - N_ROW_SEG] = (acc_t * gain_t_ref[NORM_SEGS.index(j) - 2]).astype(BF16)
            continue
        if j in SIGMOID_SEGS:
            acc = jax.nn.sigmoid(acc)
        if j < N_ROW_SEG:
            p_ref[:, cols] = acc.astype(BF16)
        else:
            pt_ref[j - N_ROW_SEG] = jnp.transpose(acc).astype(BF16)


def _inproj(x2, g, w, wif, gain, tm):
    n = x2.shape[0]
    tsegs = np.asarray([N_ROW_SEG + TSEG_BQ, N_ROW_SEG + TSEG_CQ])
    gain_t = jnp.broadcast_to(gain[tsegs, 0, :, None], (2, SEG, tm))

    def const(shape):
        return pl.BlockSpec(shape, lambda i: (0,) * len(shape), pipeline_mode=pl.Buffered(1))

    return pl.pallas_call(
        _inproj_kernel,
        grid=(n // tm,),
        in_specs=[
            pl.BlockSpec((tm, D_MODEL), lambda i: (i, 0)),
            const((1, D_MODEL)),
            const((D_MODEL, N_SEG * SEG)),
            const((D_MODEL, LANES)),
            const((N_SEG, 1, SEG)),
            const((2, SEG, tm)),
        ],
        out_specs=[
            pl.BlockSpec((tm, N_ROW_SEG * SEG), lambda i: (i, 0)),
            pl.BlockSpec((N_T_SEG, SEG, tm), lambda i: (0, 0, i)),
            pl.BlockSpec((tm, LANES), lambda i: (i, 0)),
        ],
        out_shape=[
            jax.ShapeDtypeStruct((n, N_ROW_SEG * SEG), BF16),
            jax.ShapeDtypeStruct((N_T_SEG, SEG, n), BF16),
            jax.ShapeDtypeStruct((n, LANES), F32),
        ],
        compiler_params=_cparams(("parallel",)),
        name="inproj",
    )(x2, g, w, wif, gain, gain_t)


def _log_sigmoid(z):
    return jnp.minimum(z, 0.0) - jnp.log(1.0 + jnp.exp(-jnp.abs(z)))


def _split3(a):
    hi = a.astype(BF16)
    r1 = a - hi.astype(F32)
    mid = r1.astype(BF16)
    lo = (r1 - mid.astype(F32)).astype(BF16)
    return hi, mid, lo


def _mlstm_kernel(aq_ref, ak_ref, vt_ref, aot_ref, gif_ref, gift_ref, cw_ref, cb_ref,
                  gbr_ref, gbc_ref, agt_ref, out_ref, ubuf, ct_ref, n_ref, m_ref):
    L = MLSTM_L
    c = pl.program_id(1)

    @pl.when(c == 0)
    def _():
        ubuf[0:8, :] = jnp.zeros((8, 2 * A_WIDTH), F32)
        ct_ref[...] = jnp.zeros_like(ct_ref)
        n_ref[...] = jnp.zeros_like(n_ref)
        m_ref[...] = jnp.zeros_like(m_ref)

    @pl.when(c > 0)
    def _():
        ubuf[0:8, :] = ubuf[L:L + 8, :]

    ubuf[8:L + 8, 0:A_WIDTH] = aq_ref[...].astype(F32)
    ubuf[8:L + 8, A_WIDTH:2 * A_WIDTH] = ak_ref[...].astype(F32)
    y = cb_ref[...] + cw_ref[0:1, :] * ubuf[8:L + 8, :]
    for t in range(1, CONV_W):
        y = y + cw_ref[t:t + 1, :] * ubuf[8 - t:8 - t + L, :]
    qk = y * jax.nn.sigmoid(y)
    q_t = jnp.transpose(qk[:, 0:A_WIDTH]).astype(BF16)
    k_all = (qk[:, A_WIDTH:2 * A_WIDTH] * (A_DH ** -0.5)).astype(BF16)

    zc = gif_ref[...] + gbr_ref[...]
    ig_c = GATE_CAP * jnp.tanh(zc * (1.0 / GATE_CAP))
    lf_c = _log_sigmoid(zc)
    zr = gift_ref[...] + gbc_ref[...]
    ig_r = GATE_CAP * jnp.tanh(zr * (1.0 / GATE_CAP))
    lf_r = _log_sigmoid(zr)

    row = lax.broadcasted_iota(jnp.int32, (L, L), 0)
    col = lax.broadcasted_iota(jnp.int32, (L, L), 1)
    causal = col <= row
    tril = jnp.where(causal, 1.0, 0.0).astype(BF16)
    triu = jnp.where(row <= col, 1.0, 0.0).astype(BF16)
    b_c = sum(jnp.dot(tril, piece, preferred_element_type=F32) for piece in _split3(lf_c))
    b_r = sum(jnp.dot(piece, triu, preferred_element_type=F32) for piece in _split3(lf_r))

    sub8 = lax.broadcasted_iota(jnp.int32, (8, L), 0)
    for h in range(A_HEADS):
        rows = slice(h * A_DH, (h + 1) * A_DH)
        qt = q_t[rows, :]
        k = k_all[:, rows]
        vt = vt_ref[rows, :]
        bcol = b_c[:, A_HEADS + h:A_HEADS + h + 1]
        brow = b_r[A_HEADS + h:A_HEADS + h + 1, :]
        igcol = ig_c[:, h:h + 1]
        igrow = ig_r[h:h + 1, :]
        b_last = brow[:, L - 1:L]
        ct = ct_ref[h]
        n8 = n_ref[h]
        m_prev = m_ref[h][:, 0:1]

        dmat = jnp.where(row <= col, brow + (igcol - bcol), NEG)
        inter = brow + m_prev
        m_t = jnp.maximum(inter, jnp.max(dmat, axis=0, keepdims=True))
        w_inter = jnp.exp(inter - m_t)
        st = jnp.dot(k, qt, preferred_element_type=F32) * jnp.exp(dmat - m_t)
        num = w_inter * jnp.dot(ct.astype(BF16), qt, preferred_element_type=F32)
        num = num + jnp.dot(vt, st.astype(BF16), preferred_element_type=F32)
        nq = jnp.dot(n8.astype(BF16), qt, preferred_element_type=F32)[0:1, :]
        den = w_inter * nq + jnp.sum(st, axis=0, keepdims=True)
        hh = num * (1.0 / jnp.maximum(jnp.abs(den), jnp.exp(-m_t)))

        g_end = b_last - brow + igrow
        m_new = jnp.maximum(b_last + m_prev, jnp.max(g_end, axis=-1, keepdims=True))
        dec = jnp.exp(b_last + m_prev - m_new)
        w_end = jnp.exp(g_end - m_new)
        vw = (vt.astype(F32) * w_end).astype(BF16)
        ct_ref[h] = dec * ct + jnp.dot(vw, k, preferred_element_type=F32)
        w8 = jnp.where(sub8 == 0, w_end, 0.0).astype(BF16)
        n_ref[h] = dec * n8 + jnp.dot(w8, k, preferred_element_type=F32)
        m_ref[h] = jnp.broadcast_to(m_new, (1, LANES))

        hn = hh * lax.rsqrt(jnp.mean(hh * hh, axis=0, keepdims=True) + EPS) * agt_ref[rows, :]
        out_ref[rows, :] = (hn * aot_ref[rows, :].astype(F32)).astype(BF16)


def _mlstm(p, pt, gif, gift, cw, cb, gbr, gbc, ag, bsz, seq):
    L = MLSTM_L
    nc = seq // L
    n = bsz * seq
    agt = jnp.broadcast_to(ag.reshape(A_WIDTH, 1), (A_WIDTH, L))

    def tseg(j):
        return pl.BlockSpec((None, SEG, L), lambda b, c: (j, 0, b * nc + c))

    def full(shape):
        return pl.BlockSpec(shape, lambda b, c: (0,) * len(shape))

    return pl.pallas_call(
        _mlstm_kernel,
        grid=(bsz, nc),
        in_specs=[
            pl.BlockSpec((L, SEG), lambda b, c: (b * nc + c, SEG_AQ)),
            pl.BlockSpec((L, SEG), lambda b, c: (b * nc + c, SEG_AK)),
            tseg(TSEG_AV), tseg(TSEG_AO),
            pl.BlockSpec((L, LANES), lambda b, c: (b * nc + c, 0)),
            pl.BlockSpec((8, L), lambda b, c: (0, b * nc + c)),
            full((CONV_W, 2 * A_WIDTH)), full((1, 2 * A_WIDTH)),
            full((1, LANES)), full((8, 1)), full((A_WIDTH, L)),
        ],
        out_specs=pl.BlockSpec((A_WIDTH, L), lambda b, c: (0, b * nc + c)),
        out_shape=jax.ShapeDtypeStruct((A_WIDTH, n), BF16),
        scratch_shapes=[
            pltpu.VMEM((L + 8, 2 * A_WIDTH), F32),
            pltpu.VMEM((A_HEADS, A_DH, A_DH), F32),
            pltpu.VMEM((A_HEADS, 8, A_DH), F32),
            pltpu.VMEM((A_HEADS, 1, LANES), F32),
        ],
        compiler_params=_cparams(("parallel", "arbitrary")),
        name="mlstm",
    )(p, p, pt, pt, gif, gift, cw, cb, gbr, gbc, agt)


def _band_kernel(*refs):
    nkb = BAND_NKB
    qt_ref = refs[0]
    k_refs = refs[1:1 + nkb]
    vt_refs = refs[1 + nkb:1 + 2 * nkb]
    bias_ref = refs[1 + 2 * nkb]
    out_ref = refs[2 + 2 * nkb]
    s_ref, mx_ref = refs[3 + 2 * nkb:5 + 2 * nkb]
    tq = BAND_TQ
    nk = nkb * tq
    i = pl.program_id(1)

    def compute(mask_start):
        k_all = jnp.concatenate([r[...] for r in k_refs], axis=0)
        vt_all = jnp.concatenate([r[...] for r in vt_refs], axis=1)
        ones = jnp.ones((BAND_ONES, nk), BF16)
        row = lax.broadcasted_iota(jnp.int32, (LANES, tq), 0)
        lo = row < B_DH
        if mask_start:
            kidx = lax.broadcasted_iota(jnp.int32, (nk, 1), 0)
            valid = (kidx + (i - (nkb - 1)) * tq) >= 0

        def score(p):
            rows = slice(p * LANES, (p + 1) * LANES)
            qtp = qt_ref[rows, :]
            zero = jnp.zeros_like(qtp)
            qbd = jnp.concatenate([jnp.where(lo, qtp, zero), jnp.where(lo, zero, qtp)], axis=1)
            s = jnp.dot(k_all[:, rows], qbd, preferred_element_type=F32) + bias_ref[p]
            if mask_start:
                s = jnp.where(valid, s, NEG)
            s_ref[p & 1] = s
            mx_ref[p & 1] = jnp.max(s, axis=0, keepdims=True)

        def finish(p):
            rows = slice(p * LANES, (p + 1) * LANES)
            pr = jnp.exp2((s_ref[p & 1] - mx_ref[p & 1]).astype(BF16))
            o = jnp.dot(jnp.concatenate([vt_all[rows, :], ones], axis=0), pr,
                        preferred_element_type=F32)
            o = o[0:LANES, :] / o[LANES:LANES + 1, :]
            out_ref[rows, :] = jnp.where(lo, o[:, 0:tq], o[:, tq:2 * tq]).astype(BF16)

        score(0)
        for p in range(B_HEADS // 2):
            if p + 1 < B_HEADS // 2:
                score(p + 1)
            finish(p)

    @pl.when(i < nkb - 1)
    def _():
        compute(True)

    @pl.when(i >= nkb - 1)
    def _():
        compute(False)


def _band(p, pt, bias, bsz, seq):
    tq = BAND_TQ
    nkb = BAND_NKB
    nq = seq // tq
    n = bsz * seq

    def kblk(d):
        return pl.BlockSpec((tq, SEG), lambda b, i: (b * nq + jnp.maximum(i - d, 0), SEG_BK))

    def vblk(d):
        return pl.BlockSpec((None, SEG, tq), lambda b, i: (TSEG_BV, 0, b * nq + jnp.maximum(i - d, 0)))

    in_specs = [pl.BlockSpec((None, SEG, tq), lambda b, i: (TSEG_BQ, 0, b * nq + i))]
    in_specs += [kblk(d) for d in range(nkb - 1, -1, -1)]
    in_specs += [vblk(d) for d in range(nkb - 1, -1, -1)]
    in_specs += [pl.BlockSpec(bias.shape, lambda b, i: (0, 0, 0))]
    return pl.pallas_call(
        _band_kernel,
        grid=(bsz, nq),
        in_specs=in_specs,
        out_specs=pl.BlockSpec((B_WIDTH, tq), lambda b, i: (0, b * nq + i)),
        out_shape=jax.ShapeDtypeStruct((B_WIDTH, n), BF16),
        scratch_shapes=[
            pltpu.VMEM((2, nkb * tq, 2 * tq), F32),
            pltpu.VMEM((2, 1, 2 * tq), F32),
        ],
        compiler_params=_cparams(("parallel", "parallel")),
        name="band_attn",
    )(pt, *([p] * nkb), *([pt] * nkb), bias)


def _toeplitz(base, m, n):
    period = base.shape[-1]
    assert n <= period - 1
    reps = (1,) * (base.ndim - 1) + (m,)
    big = jnp.tile(base, reps)[..., :m * (period - 1)]
    return big.reshape(base.shape[:-1] + (m, period - 1))[..., :n]


def _band_bias(b_rel):
    tq = BAND_TQ
    nk = BAND_NKB * tq
    period = tq + nk
    e = np.arange(period)
    e = np.where(e < nk, e, e - period)
    rel = np.clip((nk - tq) - e, -B_MAX_REL, B_MAX_REL) + B_MAX_REL
    bias = _toeplitz(jnp.transpose(b_rel[rel]).astype(F32), tq, nk)
    qpos = np.arange(tq)
    kpos = np.arange(nk) - (nk - tq)
    qc = qpos[:, None] // CHUNK
    kc = np.floor_divide(kpos[None, :], CHUNK)
    allowed = (kc <= qc) & (kc >= qc - B_LEFT_CHUNKS)
    bias = jnp.where(allowed[None], bias * LOG2E, NEG)
    return jnp.swapaxes(bias.reshape(B_HEADS // 2, 2 * tq, nk), 1, 2)


def _diff_kernel(lam_ref, cfar_ref, qt_ref, k_ref, vt_ref, bias_ref, g_ref, out_ref,
                 qbd_ref, s_ref, mx_ref, acc_ref, m_ref, *, nnear):
    T = DIFF_T
    h = pl.program_id(1)
    qi = pl.program_id(2)
    cfar = cfar_ref[h]
    qt = qt_ref[...]
    row = lax.broadcasted_iota(jnp.int32, (2 * C_DQK, T), 0)
    zero = jnp.zeros_like(qt)
    qbd_ref[:, 0:T] = jnp.where(row < C_DQK, qt, zero)
    qbd_ref[:, T:2 * T] = jnp.where(row < C_DQK, zero, qt)
    m_ref[...] = jnp.full(m_ref.shape, NEG, F32)
    acc_ref[...] = jnp.zeros_like(acc_ref)

    def near_bias(t):
        return bias_ref[t] if t < nnear else None

    def stage_a(t, slot, bias, maps=(0, 1)):
        j = jnp.maximum(qi - t, 0)
        k = k_ref[pl.ds(pl.multiple_of(j * T, T), T), :]
        for mp in maps:
            sm = jnp.dot(k, qbd_ref[:, mp * T:(mp + 1) * T], preferred_element_type=F32)
            if bias is not None:
                sm = sm + bias
            s_ref[slot, mp] = sm
            mx = jnp.max(sm, axis=0, keepdims=True)
            mx_ref[slot, mp] = mx + cfar if bias is None else mx

    def stage_bc(t, slot, far, maps=(0, 1)):
        j = qi - t
        vt = jnp.concatenate([vt_ref[:, pl.ds(pl.multiple_of(j * T, T), T)],
                              jnp.ones((DIFF_ONES, T), BF16)], axis=0)
        for mp in maps:
            m_old = m_ref[mp]
            m_new = jnp.maximum(m_old, mx_ref[slot, mp])
            shift = m_new - cfar if far else m_new
            pr = jnp.exp2((s_ref[slot, mp] - shift).astype(BF16))
            m_ref[mp] = m_new
            acc_ref[mp] = jnp.exp2(m_old - m_new) * acc_ref[mp] + jnp.dot(
                vt, pr, preferred_element_type=F32)

    def full_step(t, slot, far, next_bias):
        for mp in range(2):
            stage_a(t + 1, 1 - slot, next_bias, (mp,))
            stage_bc(t, slot, far, (mp,))

    stage_a(0, 0, near_bias(0))
    for t in range(nnear):
        @pl.when(qi >= t)
        def _(t=t):
            full_step(t, t & 1, False, near_bias(t + 1))

    @pl.when(qi >= nnear)
    def _():
        n_full = qi - nnear
        s0 = nnear & 1

        def pair(i, carry):
            full_step(nnear + 2 * i, s0, True, None)
            full_step(nnear + 2 * i + 1, 1 - s0, True, None)
            return carry

        lax.fori_loop(0, n_full // 2, pair, 0)

        @pl.when(n_full % 2 == 1)
        def _():
            full_step(qi - 1, s0, True, None)

        stage_bc(qi, qi & 1, True)

    o1 = acc_ref[0, 0:C_DV, :] / acc_ref[0, C_DV:C_DV + 1, :]
    o2 = acc_ref[1, 0:C_DV, :] / acc_ref[1, C_DV:C_DV + 1, :]
    o = o1 - lam_ref[0] * o2
    on = o * lax.rsqrt(jnp.mean(o * o, axis=0, keepdims=True) + EPS) * g_ref[...]
    out_ref[...] = on.astype(BF16)


def _diff_nnear(nq):
    d_sat = -(-(T5_MAX_DIST - 1 + DIFF_T) // DIFF_T)
    return min(d_sat, nq)


def _diff(p, pt, bias, cfar, lam, gsub, bsz, seq):
    T = DIFF_T
    nq = seq // T
    nnear = bias.shape[1]
    kcol = SEG_CK * SEG // LANES
    n = bsz * seq
    return pl.pallas_call(
        functools.partial(_diff_kernel, nnear=nnear),
        grid=(bsz, C_HEADS, nq),
        in_specs=[
            pl.BlockSpec(memory_space=pltpu.SMEM),
            pl.BlockSpec(memory_space=pltpu.SMEM),
            pl.BlockSpec((None, 2 * C_DQK, T), lambda b, h, i: (TSEG_CQ, h, b * nq + i)),
            pl.BlockSpec((seq, LANES), lambda b, h, i: (b, kcol + h)),
            pl.BlockSpec((None, C_DV, seq), lambda b, h, i: (TSEG_CV, h, b)),
            pl.BlockSpec((None, nnear, T, T), lambda b, h, i: (h, 0, 0, 0)),
            pl.BlockSpec((C_DV, 1), lambda b, h, i: (0, 0)),
        ],
        out_specs=pl.BlockSpec((C_DV, T), lambda b, h, i: (h, b * nq + i)),
        out_shape=jax.ShapeDtypeStruct((C_WIDTH, n), BF16),
        scratch_shapes=[
            pltpu.VMEM((2 * C_DQK, 2 * T), BF16),
            pltpu.VMEM((2, 2, T, T), F32),
            pltpu.VMEM((2, 2, 1, T), F32),
            pltpu.VMEM((2, C_DV + DIFF_ONES, T), F32),
            pltpu.VMEM((2, 1, T), F32),
        ],
        compiler_params=_cparams(("parallel", "parallel", "arbitrary")),
        name="diff_attn",
    )(lam, cfar, pt, p, pt, bias, gsub)


def _t5_bucket(rel):
    nb = T5_BUCKETS // 2
    max_exact = nb // 2
    ret = (rel > 0).astype(jnp.int32) * nb
    n = jnp.abs(rel)
    large = max_exact + (jnp.log(jnp.maximum(n, max_exact).astype(F32) / max_exact)
                         / math.log(T5_MAX_DIST / max_exact) * (nb - max_exact)).astype(jnp.int32)
    large = jnp.minimum(large, nb - 1)
    return ret + jnp.where(n < max_exact, n, large)


def _diff_bias(t5_table, nnear):
    T = DIFF_T
    e = np.arange(2 * T)
    amc = np.where(e < T, -e, 2 * T - e)
    rel = jnp.asarray(-np.arange(nnear)[:, None] * T + amc[None, :], jnp.int32)
    base = jnp.moveaxis(t5_table[_t5_bucket(rel)], -1, 0).astype(F32) * LOG2E
    tiles = _toeplitz(base, T, T)
    a = np.arange(T)[:, None]
    c = np.arange(T)[None, :]
    allowed = np.ones((nnear, T, T), bool)
    allowed[0] = (a // CHUNK) <= (c // CHUNK)
    far = t5_table[_t5_bucket(jnp.asarray(-T5_MAX_DIST, jnp.int32))].astype(F32) * LOG2E
    return jnp.where(allowed[None], tiles, NEG), far


def _merge_kernel(hat_ref, hbt_ref, hct_ref, g0_ref, g1_ref, g2_ref, x_ref, wb_ref, wo_ref,
                  n2_ref, wr_ref, br_ref, x1_ref, h2_ref, lg_ref):
    tn = (((0,), (0,)), ((), ()))
    y = g0_ref[...].astype(F32) * lax.dot_general(hat_ref[...], wb_ref[0], tn,
                                                  preferred_element_type=F32)
    y = y + g1_ref[...].astype(F32) * lax.dot_general(hbt_ref[...], wb_ref[1], tn,
                                                      preferred_element_type=F32)
    y = y + g2_ref[...].astype(F32) * lax.dot_general(hct_ref[...], wb_ref[2], tn,
                                                      preferred_element_type=F32)
    x1 = x_ref[...] + jnp.dot(y.astype(BF16), wo_ref[...], preferred_element_type=F32)
    x1_ref[...] = x1
    h2 = x1 * lax.rsqrt(jnp.mean(x1 * x1, axis=-1, keepdims=True) + EPS) * n2_ref[...]
    h2b = h2.astype(BF16)
    h2_ref[...] = h2b
    lg_ref[...] = jnp.dot(h2b, wr_ref[...], preferred_element_type=F32) + br_ref[...]


def _merge(ha, hbt, hct, p, x2, wb, wo, n2, wr, br, tm):
    n = x2.shape[0]
    gcol = SEG_GATES * SEG // D_MODEL

    def rows(width, col=0):
        return pl.BlockSpec((tm, width), lambda i: (i, col))

    def cols():
        return pl.BlockSpec((BRANCH_WIDTH, tm), lambda i: (0, i))

    def full(shape):
        return pl.BlockSpec(shape, lambda i: (0,) * len(shape))

    return pl.pallas_call(
        _merge_kernel,
        grid=(n // tm,),
        in_specs=[
            cols(), cols(), cols(),
            rows(D_MODEL, gcol), rows(D_MODEL, gcol + 1), rows(D_MODEL, gcol + 2),
            rows(D_MODEL),
            full((N_BRANCH, BRANCH_WIDTH, D_MODEL)), full((D_MODEL, D_MODEL)),
            full((1, D_MODEL)), full((D_MODEL, LANES)), full((1, LANES)),
        ],
        out_specs=[rows(D_MODEL), rows(D_MODEL), rows(LANES)],
        out_shape=[
            jax.ShapeDtypeStruct((n, D_MODEL), F32),
            jax.ShapeDtypeStruct((n, D_MODEL), BF16),
            jax.ShapeDtypeStruct((n, LANES), F32),
        ],
        compiler_params=_cparams(("parallel",)),
        name="merge",
    )(ha, hbt, hct, p, p, p, x2, wb, wo, n2, wr, br)


def _combine_weights(lg):
    lanef = lax.broadcasted_iota(jnp.int32, lg.shape, 1).astype(F32)
    big = 1e9
    is_g = (lanef >= N_EXPERTS) & (lanef < N_EXPERTS + N_GROUPS)
    gl = jnp.where(is_g, lg, -jnp.inf)
    gmax = jnp.max(gl, axis=-1, keepdims=True)
    g_idx = jnp.min(jnp.where(gl == gmax, lanef - N_EXPERTS, big), axis=-1, keepdims=True)
    p_g = 1.0 / jnp.sum(jnp.exp(gl - gmax), axis=-1, keepdims=True)
    in_grp = (lanef >= g_idx * EXPERTS_PER_GROUP) & (lanef < (g_idx + 1.0) * EXPERTS_PER_GROUP)
    el = jnp.where(in_grp, lg, -jnp.inf)
    ee = jnp.exp(el - jnp.max(el, axis=-1, keepdims=True))
    ep = ee / jnp.sum(ee, axis=-1, keepdims=True)
    ep = jnp.where(in_grp, ep, -1.0)
    v1 = jnp.max(ep, axis=-1, keepdims=True)
    i1 = jnp.min(jnp.where(ep == v1, lanef, big), axis=-1, keepdims=True)
    ep2 = jnp.where(lanef == i1, -1.0, ep)
    v2 = jnp.max(ep2, axis=-1, keepdims=True)
    i2 = jnp.min(jnp.where(ep2 == v2, lanef, big), axis=-1, keepdims=True)
    tot = v1 + v2
    comb = jnp.where(lanef == i1, p_g * (v1 / tot), 0.0) + jnp.where(lanef == i2, p_g * (v2 / tot), 0.0)
    return comb, g_idx


def _moe_kernel(x1_ref, h2_ref, lg_ref, tri_ref, wgu_ref, wd_ref, out_ref,
                pt_ref, xs_ref, ws_ref, ys_ref, meta_ref, *, rcap):
    C = MOE_C
    s = pl.program_id(1)
    tm = h2_ref.shape[0]

    @pl.when(s == 0)
    def _():
        comb, g_idx = _combine_weights(lg_ref[...])
        lanef = lax.broadcasted_iota(jnp.int32, comb.shape, 1).astype(F32)
        mine = lanef == g_idx
        onehot = jnp.where(mine, 1.0, 0.0)
        ranks = jnp.dot(tri_ref[...], onehot.astype(BF16), preferred_element_type=F32)
        dest = jnp.sum(jnp.where(mine, ranks, 0.0), axis=-1, keepdims=True)
        off = jnp.int32(0)
        for g in range(N_GROUPS):
            cnt = jnp.sum(onehot[:, g:g + 1]).astype(jnp.int32)
            nchunk = (cnt + (C - 1)) // C
            meta_ref[g] = off
            meta_ref[N_GROUPS + g] = nchunk
            dest = dest + jnp.where(g_idx == float(g), (off * C).astype(F32), 0.0)
            off = off + nchunk
        slot = lax.broadcasted_iota(jnp.int32, (tm, rcap), 1).astype(F32)
        pt = jnp.where(dest == slot, 1.0, 0.0).astype(BF16)
        pt_ref[...] = pt
        comb_hi = comb.astype(BF16)
        comb_lo = (comb - comb_hi.astype(F32)).astype(BF16)
        packed = jnp.concatenate([h2_ref[...], comb_hi, comb_lo], axis=1)
        srt = lax.dot_general(pt, packed, (((0,), (0,)), ((), ())), preferred_element_type=F32)
        xs_ref[...] = srt[:, 0:D_MODEL].astype(BF16)
        ws_ref[...] = srt[:, D_MODEL:D_MODEL + LANES] + srt[:, D_MODEL + LANES:D_MODEL + 2 * LANES]
        ys_ref[...] = jnp.zeros_like(ys_ref)

    def do_chunk(r0, m):
        r0 = pl.multiple_of(r0, C)
        xc = xs_ref[pl.ds(r0, m), :]
        wsc = ws_ref[pl.ds(r0, m), :]
        lane = lax.broadcasted_iota(jnp.int32, (m, LANES), 1)
        y = None
        for k in range(MOE_EPS):
            gu = jnp.dot(xc, wgu_ref[k], preferred_element_type=F32)
            gate = gu[:, 0:D_EXPERT]
            wk = jnp.sum(jnp.where(lane == s * MOE_EPS + k, wsc, 0.0), axis=-1, keepdims=True)
            he = (gate * jax.nn.sigmoid(gate) * gu[:, D_EXPERT:2 * D_EXPERT] * wk).astype(BF16)
            yk = jnp.dot(he, wd_ref[k], preferred_element_type=F32)
            y = yk if y is None else y + yk
        ys_ref[pl.ds(r0, m), :] = y.astype(BF16)

    g = s // (EXPERTS_PER_GROUP // MOE_EPS)
    start = meta_ref[g]
    nchunk = meta_ref[N_GROUPS + g]

    def pair(i, carry):
        do_chunk((start + 2 * i) * C, 2 * C)
        return carry

    lax.fori_loop(0, nchunk // 2, pair, 0)

    @pl.when(nchunk % 2 == 1)
    def _():
        do_chunk((start + nchunk - 1) * C, C)

    @pl.when(s == pl.num_programs(1) - 1)
    def _():
        out_ref[...] = x1_ref[...] + jnp.dot(pt_ref[...], ys_ref[...], preferred_element_type=F32)


def _moe(x1, h2, lg, wgu, wd, tm):
    n = x1.shape[0]
    rcap = tm + N_GROUPS * MOE_C
    idx = np.arange(tm)
    tri = jnp.asarray(idx[None, :] < idx[:, None], BF16)
    return pl.pallas_call(
        functools.partial(_moe_kernel, rcap=rcap),
        grid=(n // tm, N_EXPERTS // MOE_EPS),
        in_specs=[
            pl.BlockSpec((tm, D_MODEL), lambda i, s: (i, 0), pipeline_mode=pl.Buffered(1)),
            pl.BlockSpec((tm, D_MODEL), lambda i, s: (i, 0), pipeline_mode=pl.Buffered(1)),
            pl.BlockSpec((tm, LANES), lambda i, s: (i, 0), pipeline_mode=pl.Buffered(1)),
            pl.BlockSpec((tm, tm), lambda i, s: (0, 0), pipeline_mode=pl.Buffered(1)),
            pl.BlockSpec((MOE_EPS, D_MODEL, 2 * D_EXPERT), lambda i, s: (s, 0, 0)),
            pl.BlockSpec((MOE_EPS, D_EXPERT, D_MODEL), lambda i, s: (s, 0, 0)),
        ],
        out_specs=pl.BlockSpec((tm, D_MODEL), lambda i, s: (i, 0), pipeline_mode=pl.Buffered(1)),
        out_shape=jax.ShapeDtypeStruct((n, D_MODEL), F32),
        scratch_shapes=[
            pltpu.VMEM((tm, rcap), BF16),
            pltpu.VMEM((rcap, D_MODEL), BF16),
            pltpu.VMEM((rcap, LANES), F32),
            pltpu.VMEM((rcap, D_MODEL), BF16),
            pltpu.SMEM((2 * N_GROUPS,), jnp.int32),
        ],
        compiler_params=pltpu.CompilerParams(dimension_semantics=("parallel", "arbitrary"),
                                             vmem_limit_bytes=MOE_VMEM_LIMIT),
        name="moe",
    )(x1, h2, lg, tri, wgu, wd)


def _tile(n, pref):
    t = pref
    while n % t:
        t //= 2
    return t


def _mixer_params(layer, norm1_g, w_in, a_conv_w, a_conv_b, a_gate_bias, a_out_norm_g,
                  b_qk_norm_g, b_rel_bias, c_qk_norm_g, c_lambda, c_sub_norm_g, t5_bias,
                  w_branch, w_out, nq_diff):
    n_small = 2 * A_HEADS
    cut = 4 * A_WIDTH
    w_main = jnp.concatenate([w_in[:, :cut], w_in[:, cut + n_small:]], axis=1)
    w_main = w_main.reshape(D_MODEL, N_SEG, SEG)[:, np.asarray(SEG_PERM), :]
    w_main = w_main.reshape(D_MODEL, N_SEG * SEG).astype(BF16)
    w_if = jnp.pad(w_in[:, cut:cut + n_small], ((0, 0), (0, LANES - n_small))).astype(BF16)
    gain = jnp.ones((N_SEG, SEG), F32)
    gain = gain.at[N_ROW_SEG + TSEG_BQ].set(jnp.tile(b_qk_norm_g[0], B_HEADS) * (B_DH ** -0.5 * LOG2E))
    gain = gain.at[SEG_BK].set(jnp.tile(b_qk_norm_g[1], B_HEADS))
    gain = gain.at[N_ROW_SEG + TSEG_CQ].set(
        jnp.tile(c_qk_norm_g[0], 2 * C_HEADS) * (C_DQK ** -0.5 * LOG2E))
    gain = gain.at[SEG_CK].set(jnp.tile(c_qk_norm_g[1], 2 * C_HEADS))
    diff_bias, diff_far = _diff_bias(t5_bias, _diff_nnear(nq_diff))
    lam_init = 0.8 - 0.6 * math.exp(-0.3 * layer)
    lf32 = c_lambda.astype(F32)
    lam = jnp.exp(jnp.sum(lf32[0] * lf32[1])) - jnp.exp(jnp.sum(lf32[2] * lf32[3])) + lam_init
    return dict(
        g1=norm1_g.reshape(1, D_MODEL), w_main=w_main, w_if=w_if,
        gain=gain.reshape(N_SEG, 1, SEG),
        cw=a_conv_w, cb=a_conv_b.reshape(1, -1),
        gbr=jnp.pad(a_gate_bias, (0, LANES - n_small)).reshape(1, LANES),
        gbc=a_gate_bias.reshape(n_small, 1),
        ag=a_out_norm_g.reshape(1, A_WIDTH),
        band_bias=_band_bias(b_rel_bias),
        diff_bias=diff_bias, diff_far=diff_far,
        lam=lam.reshape(1).astype(F32),
        gsub=(c_sub_norm_g * (1.0 - lam_init)).reshape(C_DV, 1),
        wb=w_branch.astype(BF16), wo=w_out.astype(BF16),
    )


def _layer(x2, bsz, seq, mp, norm2_g, w_group, b_group, w_router, b_router, w_e_gate, w_e_up, w_e_down):
    n = bsz * seq
    p, pt, gif = _inproj(x2, mp["g1"], mp["w_main"], mp["w_if"], mp["gain"], _tile(n, 512))
    gift = jnp.transpose(gif[:, :2 * A_HEADS])
    ha = _mlstm(p, pt, gif, gift, mp["cw"], mp["cb"], mp["gbr"], mp["gbc"], mp["ag"], bsz, seq)
    hbt = _band(p, pt, mp["band_bias"], bsz, seq)
    hct = _diff(p, pt, mp["diff_bias"], mp["diff_far"], mp["lam"], mp["gsub"], bsz, seq)

    wr = jnp.concatenate([w_router, w_group], axis=1)
    wr = jnp.pad(wr, ((0, 0), (0, LANES - wr.shape[1]))).astype(BF16)
    br = jnp.pad(jnp.concatenate([b_router, b_group]), (0, LANES - N_EXPERTS - N_GROUPS)).reshape(1, LANES)
    x1, h2, lg = _merge(ha, hbt, hct, p, x2, mp["wb"], mp["wo"], norm2_g.reshape(1, D_MODEL), wr, br,
                        _tile(n, 512))
    wgu = jnp.concatenate([w_e_gate, w_e_up], axis=-1).astype(BF16)
    return _moe(x1, h2, lg, wgu, w_e_down.astype(BF16), _tile(n, MOE_TM))


def kernel(x, norm1_g, w_in, a_conv_w, a_conv_b, a_gate_bias, a_out_norm_g, b_qk_norm_g, b_rel_bias,
           c_qk_norm_g, c_lambda, c_sub_norm_g, t5_bias, w_branch, w_out, norm2_g, w_group, b_group,
           w_router, b_router, w_e_gate, w_e_up, w_e_down):
    bsz, seq, _ = x.shape
    assert seq % DIFF_T == 0 and seq % MLSTM_L == 0 and seq % BAND_TQ == 0
    x2 = x.reshape(bsz * seq, D_MODEL)
    for l in range(norm1_g.shape[0]):
        mp = _mixer_params(l, norm1_g[l], w_in[l], a_conv_w[l], a_conv_b[l], a_gate_bias[l],
                           a_out_norm_g[l], b_qk_norm_g[l], b_rel_bias[l], c_qk_norm_g[l], c_lambda[l],
                           c_sub_norm_g[l], t5_bias, w_branch[l], w_out[l], seq // DIFF_T)
        x2 = _layer(x2, bsz, seq, mp, norm2_g[l], w_group[l], b_group[l], w_router[l], b_router[l],
                    w_e_gate[l], w_e_up[l], w_e_down[l])
    return x2.reshape(bsz, seq, D_MODEL)
```

```python
import functools
import math

import numpy as np
import jax
import jax.numpy as jnp
from jax import lax
from jax.experimental import pallas as pl
from jax.experimental.pallas import tpu as pltpu

F32 = jnp.float32
BF16 = jnp.bfloat16

D_MODEL = 1024
CHUNK = 64
EPS = 1e-6
NEG = -1e30
LOG2E = math.log2(math.e)

A_HEADS = 4
A_DH = 128
A_WIDTH = A_HEADS * A_DH
CONV_W = 4
GATE_CAP = 15.0

B_HEADS = 8
B_DH = 64
B_WIDTH = B_HEADS * B_DH
B_LEFT_CHUNKS = 8
B_MAX_REL = 256

C_HEADS = 4
C_DQK = 64
C_DV = 2 * C_DQK
C_WIDTH = C_HEADS * C_DV

T5_BUCKETS = 32
T5_MAX_DIST = 1024

N_BRANCH = 3
BRANCH_WIDTH = 512

N_GROUPS = 4
EXPERTS_PER_GROUP = 8
N_EXPERTS = N_GROUPS * EXPERTS_PER_GROUP
D_EXPERT = D_MODEL // 4

LANES = 128
SEG = 512
N_SEG = 16
VMEM_LIMIT = 48 * 1024 * 1024

SEG_GATES, SEG_AQ, SEG_AK, SEG_BK, SEG_CK = 0, 6, 7, 8, 9
N_ROW_SEG = 10
TSEG_AV, TSEG_AO, TSEG_BQ, TSEG_BV, TSEG_CQ, TSEG_CV = 0, 1, 2, 3, 4, 5
N_T_SEG = N_SEG - N_ROW_SEG
SEG_PERM = (10, 11, 12, 13, 14, 15, 0, 1, 5, 8, 2, 3, 4, 6, 7, 9)

MLSTM_L = 128
BAND_TQ = 128
BAND_NKB = 1 + (B_LEFT_CHUNKS * CHUNK) // BAND_TQ
BAND_ONES = 16
DIFF_T = 512
DIFF_ONES = 16
MOE_TM = 1024
MOE_C = 128
MOE_EPS = EXPERTS_PER_GROUP
MOE_VMEM_LIMIT = 56 * 1024 * 1024

def _cparams(sem, flags=None):
    return pltpu.CompilerParams(dimension_semantics=sem, vmem_limit_bytes=VMEM_LIMIT, flags=flags)


NORM_SEGS = (SEG_BK, SEG_CK, N_ROW_SEG + TSEG_BQ, N_ROW_SEG + TSEG_CQ)
SIGMOID_SEGS = (N_ROW_SEG + TSEG_AO,) + tuple(range(SEG_GATES, SEG_GATES + 6))


def _head_norm_t(acc_t):
    rows, tm = acc_t.shape
    a3 = acc_t.reshape(rows // 64, 64, tm)
    ssq = jnp.sum(a3 * a3, axis=1, keepdims=True)
    return (a3 * lax.rsqrt(ssq * (1.0 / 64.0) + EPS)).reshape(rows, tm)


def _inproj_kernel(x_ref, g_ref, w_ref, wif_ref, gain_ref, gain_t_ref, p_ref, pt_ref, gif_ref):
    xf = x_ref[...]
    xn = (xf * lax.rsqrt(jnp.mean(xf * xf, axis=-1, keepdims=True) + EPS) * g_ref[...]).astype(BF16)
    gif_ref[...] = jnp.dot(xn, wif_ref[...], preferred_element_type=F32)
    for j in range(N_SEG):
        cols = slice(j * SEG, (j + 1) * SEG)
        acc = jnp.dot(xn, w_ref[:, cols], preferred_element_type=F32)
        if j in NORM_SEGS:
            acc_t = _head_norm_t(jnp.transpose(acc))
            if j < N_ROW_SEG:
                p_ref[:, cols] = (jnp.transpose(acc_t) * gain_ref[j]).astype(BF16)
            else:
                pt_ref[j - N_ROW_SEG] = (acc_t * gain_t_ref[NORM_SEGS.index(j) - 2]).astype(BF16)
            continue
        if j in SIGMOID_SEGS:
            acc = jax.nn.sigmoid(acc)
        if j < N_ROW_SEG:
            p_ref[:, cols] = acc.astype(BF16)
        else:
            pt_ref[j - N_ROW_SEG] = jnp.transpose(acc).astype(BF16)


def _inproj(x2, g, w, wif, gain, tm):
    n = x2.shape[0]
    tsegs = np.asarray([N_ROW_SEG + TSEG_BQ, N_ROW_SEG + TSEG_CQ])
    gain_t = jnp.broadcast_to(gain[tsegs, 0, :, None], (2, SEG, tm))

    def const(shape):
        return pl.BlockSpec(shape, lambda i: (0,) * len(shape), pipeline_mode=pl.Buffered(1))

    return pl.pallas_call(
        _inproj_kernel,
        grid=(n // tm,),
        in_specs=[
            pl.BlockSpec((tm, D_MODEL), lambda i: (i, 0)),
            const((1, D_MODEL)),
            const((D_MODEL, N_SEG * SEG)),
            const((D_MODEL, LANES)),
            const((N_SEG, 1, SEG)),
            const((2, SEG, tm)),
        ],
        out_specs=[
            pl.BlockSpec((tm, N_ROW_SEG * SEG), lambda i: (i, 0)),
            pl.BlockSpec((N_T_SEG, SEG, tm), lambda i: (0, 0, i)),
            pl.BlockSpec((tm, LANES), lambda i: (i, 0)),
        ],
        out_shape=[
            jax.ShapeDtypeStruct((n, N_ROW_SEG * SEG), BF16),
            jax.ShapeDtypeStruct((N_T_SEG, SEG, n), BF16),
            jax.ShapeDtypeStruct((n, LANES), F32),
        ],
        compiler_params=_cparams(("parallel",)),
        name="inproj",
    )(x2, g, w, wif, gain, gain_t)


def _log_sigmoid(z):
    return jnp.minimum(z, 0.0) - jnp.log(1.0 + jnp.exp(-jnp.abs(z)))


def _split3(a):
    hi = a.astype(BF16)
    r1 = a - hi.astype(F32)
    mid = r1.astype(BF16)
    lo = (r1 - mid.astype(F32)).astype(BF16)
    return hi, mid, lo


def _mlstm_kernel(aq_ref, ak_ref, vt_ref, aot_ref, gif_ref, gift_ref, cw_ref, cb_ref,
                  gbr_ref, gbc_ref, agt_ref, out_ref, ubuf, kq_ref, st_ref, ct_ref, n_ref, m_ref):
    L = MLSTM_L
    c = pl.program_id(1)

    @pl.when(c == 0)
    def _():
        ubuf[0:8, :] = jnp.zeros((8, 2 * A_WIDTH), F32)
        ct_ref[...] = jnp.zeros_like(ct_ref)
        n_ref[...] = jnp.zeros_like(n_ref)
        m_ref[...] = jnp.zeros_like(m_ref)

    @pl.when(c > 0)
    def _():
        ubuf[0:8, :] = ubuf[L:L + 8, :]

    ubuf[8:L + 8, 0:A_WIDTH] = aq_ref[...].astype(F32)
    ubuf[8:L + 8, A_WIDTH:2 * A_WIDTH] = ak_ref[...].astype(F32)
    y = cb_ref[...] + cw_ref[0:1, :] * ubuf[8:L + 8, :]
    for t in range(1, CONV_W):
        y = y + cw_ref[t:t + 1, :] * ubuf[8 - t:8 - t + L, :]
    qk = y * jax.nn.sigmoid(y)
    q_t = jnp.transpose(qk[:, 0:A_WIDTH]).astype(BF16)
    k_all = (qk[:, A_WIDTH:2 * A_WIDTH] * (A_DH ** -0.5)).astype(BF16)

    zc = gif_ref[...] + gbr_ref[...]
    ig_c = GATE_CAP * jnp.tanh(zc * (1.0 / GATE_CAP))
    lf_c = _log_sigmoid(zc)
    zr = gift_ref[...] + gbc_ref[...]
    ig_r = GATE_CAP * jnp.tanh(zr * (1.0 / GATE_CAP))
    lf_r = _log_sigmoid(zr)

    row = lax.broadcasted_iota(jnp.int32, (L, L), 0)
    col = lax.broadcasted_iota(jnp.int32, (L, L), 1)
    causal = col <= row
    tril = jnp.where(causal, 1.0, 0.0).astype(BF16)
    triu = jnp.where(row <= col, 1.0, 0.0).astype(BF16)
    b_c = sum(jnp.dot(tril, piece, preferred_element_type=F32) for piece in _split3(lf_c))
    b_r = sum(jnp.dot(piece, triu, preferred_element_type=F32) for piece in _split3(lf_r))

    sub8 = lax.broadcasted_iota(jnp.int32, (8, L), 0)
    heads = [slice(h * A_DH, (h + 1) * A_DH) for h in range(A_HEADS)]
    for h, rows in enumerate(heads):
        kq_ref[h] = jnp.dot(k_all[:, rows], q_t[rows, :], preferred_element_type=F32)

    stats = []
    for h, rows in enumerate(heads):
        bcol = b_c[:, A_HEADS + h:A_HEADS + h + 1]
        brow = b_r[A_HEADS + h:A_HEADS + h + 1, :]
        igcol = ig_c[:, h:h + 1]
        m_prev = m_ref[h][:, 0:1]
        dmat = jnp.where(row <= col, brow + (igcol - bcol), NEG)
        inter = brow + m_prev
        m_t = jnp.maximum(inter, jnp.max(dmat, axis=0, keepdims=True))
        st = kq_ref[h] * jnp.exp(dmat - m_t)
        st_ref[h] = st.astype(BF16)
        stats.append((brow, m_prev, m_t, jnp.exp(inter - m_t), jnp.sum(st, axis=0, keepdims=True)))

    for h, rows in enumerate(heads):
        brow, m_prev, m_t, w_inter, st_sum = stats[h]
        qt = q_t[rows, :]
        k = k_all[:, rows]
        vt = vt_ref[rows, :]
        igrow = ig_r[h:h + 1, :]
        b_last = brow[:, L - 1:L]
        ct = ct_ref[h]
        n8 = n_ref[h]

        num = w_inter * jnp.dot(ct.astype(BF16), qt, preferred_element_type=F32)
        num = num + jnp.dot(vt, st_ref[h], preferred_element_type=F32)
        nq = jnp.dot(n8.astype(BF16), qt, preferred_element_type=F32)[0:1, :]
        den = w_inter * nq + st_sum
        hh = num * (1.0 / jnp.maximum(jnp.abs(den), jnp.exp(-m_t)))

        g_end = b_last - brow + igrow
        m_new = jnp.maximum(b_last + m_prev, jnp.max(g_end, axis=-1, keepdims=True))
        dec = jnp.exp(b_last + m_prev - m_new)
        w_end = jnp.exp(g_end - m_new)
        vw = (vt.astype(F32) * w_end).astype(BF16)
        ct_ref[h] = dec * ct + jnp.dot(vw, k, preferred_element_type=F32)
        w8 = jnp.where(sub8 == 0, w_end, 0.0).astype(BF16)
        n_ref[h] = dec * n8 + jnp.dot(w8, k, preferred_element_type=F32)
        m_ref[h] = jnp.broadcast_to(m_new, (1, LANES))

        hn = hh * lax.rsqrt(jnp.mean(hh * hh, axis=0, keepdims=True) + EPS) * agt_ref[rows, :]
        out_ref[rows, :] = (hn * aot_ref[rows, :].astype(F32)).astype(BF16)


def _mlstm(p, pt, gif, gift, cw, cb, gbr, gbc, ag, bsz, seq):
    L = MLSTM_L
    nc = seq // L
    n = bsz * seq
    agt = jnp.broadcast_to(ag.reshape(A_WIDTH, 1), (A_WIDTH, L))

    def tseg(j):
        return pl.BlockSpec((None, SEG, L), lambda b, c: (j, 0, b * nc + c))

    def full(shape):
        return pl.BlockSpec(shape, lambda b, c: (0,) * len(shape))

    return pl.pallas_call(
        _mlstm_kernel,
        grid=(bsz, nc),
        in_specs=[
            pl.BlockSpec((L, SEG), lambda b, c: (b * nc + c, SEG_AQ)),
            pl.BlockSpec((L, SEG), lambda b, c: (b * nc + c, SEG_AK)),
            tseg(TSEG_AV), tseg(TSEG_AO),
            pl.BlockSpec((L, LANES), lambda b, c: (b * nc + c, 0)),
            pl.BlockSpec((8, L), lambda b, c: (0, b * nc + c)),
            full((CONV_W, 2 * A_WIDTH)), full((1, 2 * A_WIDTH)),
            full((1, LANES)), full((8, 1)), full((A_WIDTH, L)),
        ],
        out_specs=pl.BlockSpec((A_WIDTH, L), lambda b, c: (0, b * nc + c)),
        out_shape=jax.ShapeDtypeStruct((A_WIDTH, n), BF16),
        scratch_shapes=[
            pltpu.VMEM((L + 8, 2 * A_WIDTH), F32),
            pltpu.VMEM((A_HEADS, L, L), F32),
            pltpu.VMEM((A_HEADS, L, L), BF16),
            pltpu.VMEM((A_HEADS, A_DH, A_DH), F32),
            pltpu.VMEM((A_HEADS, 8, A_DH), F32),
            pltpu.VMEM((A_HEADS, 1, LANES), F32),
        ],
        compiler_params=_cparams(("parallel", "arbitrary")),
        name="mlstm",
    )(p, p, pt, pt, gif, gift, cw, cb, gbr, gbc, agt)


def _band_kernel(*refs):
    nkb = BAND_NKB
    qt_ref = refs[0]
    k_refs = refs[1:1 + nkb]
    vt_refs = refs[1 + nkb:1 + 2 * nkb]
    bias_ref = refs[1 + 2 * nkb]
    out_ref = refs[2 + 2 * nkb]
    s_ref, mx_ref = refs[3 + 2 * nkb:5 + 2 * nkb]
    tq = BAND_TQ
    nk = nkb * tq
    i = pl.program_id(1)

    def compute(mask_start):
        k_all = jnp.concatenate([r[...] for r in k_refs], axis=0)
        vt_all = jnp.concatenate([r[...] for r in vt_refs], axis=1)
        ones = jnp.ones((BAND_ONES, nk), BF16)
        row = lax.broadcasted_iota(jnp.int32, (LANES, tq), 0)
        lo = row < B_DH
        if mask_start:
            kidx = lax.broadcasted_iota(jnp.int32, (nk, 1), 0)
            valid = (kidx + (i - (nkb - 1)) * tq) >= 0

        def score(p):
            rows = slice(p * LANES, (p + 1) * LANES)
            qtp = qt_ref[rows, :]
            zero = jnp.zeros_like(qtp)
            qbd = jnp.concatenate([jnp.where(lo, qtp, zero), jnp.where(lo, zero, qtp)], axis=1)
            s = jnp.dot(k_all[:, rows], qbd, preferred_element_type=F32) + bias_ref[p]
            if mask_start:
                s = jnp.where(valid, s, NEG)
            s_ref[p & 1] = s
            mx_ref[p & 1] = jnp.max(s, axis=0, keepdims=True)

        def finish(p):
            rows = slice(p * LANES, (p + 1) * LANES)
            pr = jnp.exp2((s_ref[p & 1] - mx_ref[p & 1]).astype(BF16))
            o = jnp.dot(jnp.concatenate([vt_all[rows, :], ones], axis=0), pr,
                        preferred_element_type=F32)
            o = o[0:LANES, :] / o[LANES:LANES + 1, :]
            out_ref[rows, :] = jnp.where(lo, o[:, 0:tq], o[:, tq:2 * tq]).astype(BF16)

        score(0)
        for p in range(B_HEADS // 2):
            if p + 1 < B_HEADS // 2:
                score(p + 1)
            finish(p)

    @pl.when(i < nkb - 1)
    def _():
        compute(True)

    @pl.when(i >= nkb - 1)
    def _():
        compute(False)


def _band(p, pt, bias, bsz, seq):
    tq = BAND_TQ
    nkb = BAND_NKB
    nq = seq // tq
    n = bsz * seq

    def kblk(d):
        return pl.BlockSpec((tq, SEG), lambda b, i: (b * nq + jnp.maximum(i - d, 0), SEG_BK))

    def vblk(d):
        return pl.BlockSpec((None, SEG, tq), lambda b, i: (TSEG_BV, 0, b * nq + jnp.maximum(i - d, 0)))

    in_specs = [pl.BlockSpec((None, SEG, tq), lambda b, i: (TSEG_BQ, 0, b * nq + i))]
    in_specs += [kblk(d) for d in range(nkb - 1, -1, -1)]
    in_specs += [vblk(d) for d in range(nkb - 1, -1, -1)]
    in_specs += [pl.BlockSpec(bias.shape, lambda b, i: (0, 0, 0))]
    return pl.pallas_call(
        _band_kernel,
        grid=(bsz, nq),
        in_specs=in_specs,
        out_specs=pl.BlockSpec((B_WIDTH, tq), lambda b, i: (0, b * nq + i)),
        out_shape=jax.ShapeDtypeStruct((B_WIDTH, n), BF16),
        scratch_shapes=[
            pltpu.VMEM((2, nkb * tq, 2 * tq), F32),
            pltpu.VMEM((2, 1, 2 * tq), F32),
        ],
        compiler_params=_cparams(("parallel", "parallel")),
        name="band_attn",
    )(pt, *([p] * nkb), *([pt] * nkb), bias)


def _toeplitz(base, m, n):
    period = base.shape[-1]
    assert n <= period - 1
    reps = (1,) * (base.ndim - 1) + (m,)
    big = jnp.tile(base, reps)[..., :m * (period - 1)]
    return big.reshape(base.shape[:-1] + (m, period - 1))[..., :n]


def _band_bias(b_rel):
    tq = BAND_TQ
    nk = BAND_NKB * tq
    period = tq + nk
    e = np.arange(period)
    e = np.where(e < nk, e, e - period)
    rel = np.clip((nk - tq) - e, -B_MAX_REL, B_MAX_REL) + B_MAX_REL
    bias = _toeplitz(jnp.transpose(b_rel[rel]).astype(F32), tq, nk)
    qpos = np.arange(tq)
    kpos = np.arange(nk) - (nk - tq)
    qc = qpos[:, None] // CHUNK
    kc = np.floor_divide(kpos[None, :], CHUNK)
    allowed = (kc <= qc) & (kc >= qc - B_LEFT_CHUNKS)
    bias = jnp.where(allowed[None], bias * LOG2E, NEG)
    return jnp.swapaxes(bias.reshape(B_HEADS // 2, 2 * tq, nk), 1, 2)


def _diff_kernel(lam_ref, cfar_ref, qt_ref, k_ref, vt_ref, bias_ref, g_ref, out_ref,
                 qbd_ref, s_ref, mx_ref, acc_ref, m_ref, *, nnear):
    T = DIFF_T
    h = pl.program_id(1)
    qi = pl.program_id(2)
    cfar = cfar_ref[h]
    qt = qt_ref[...]
    row = lax.broadcasted_iota(jnp.int32, (2 * C_DQK, T), 0)
    zero = jnp.zeros_like(qt)
    qbd_ref[:, 0:T] = jnp.where(row < C_DQK, qt, zero)
    qbd_ref[:, T:2 * T] = jnp.where(row < C_DQK, zero, qt)
    m_ref[...] = jnp.full(m_ref.shape, NEG, F32)
    acc_ref[...] = jnp.zeros_like(acc_ref)

    def near_bias(t):
        return bias_ref[t] if t < nnear else None

    def stage_a(t, slot, bias, maps=(0, 1)):
        j = jnp.maximum(qi - t, 0)
        k = k_ref[pl.ds(pl.multiple_of(j * T, T), T), :]
        for mp in maps:
            sm = jnp.dot(k, qbd_ref[:, mp * T:(mp + 1) * T], preferred_element_type=F32)
            if bias is not None:
                sm = sm + bias
            s_ref[slot, mp] = sm
            mx = jnp.max(sm, axis=0, keepdims=True)
            mx_ref[slot, mp] = mx + cfar if bias is None else mx

    def stage_bc(t, slot, far, maps=(0, 1)):
        j = qi - t
        vt = jnp.concatenate([vt_ref[:, pl.ds(pl.multiple_of(j * T, T), T)],
                              jnp.ones((DIFF_ONES, T), BF16)], axis=0)
        for mp in maps:
            m_old = m_ref[mp]
            m_new = jnp.maximum(m_old, mx_ref[slot, mp])
            shift = m_new - cfar if far else m_new
            pr = jnp.exp2((s_ref[slot, mp] - shift).astype(BF16))
            m_ref[mp] = m_new
            acc_ref[mp] = jnp.exp2(m_old - m_new) * acc_ref[mp] + jnp.dot(
                vt, pr, preferred_element_type=F32)

    def full_step(t, slot, far, next_bias):
        for mp in range(2):
            stage_a(t + 1, 1 - slot, next_bias, (mp,))
            stage_bc(t, slot, far, (mp,))

    stage_a(0, 0, near_bias(0))
    for t in range(nnear):
        @pl.when(qi >= t)
        def _(t=t):
            full_step(t, t & 1, False, near_bias(t + 1))

    @pl.when(qi >= nnear)
    def _():
        n_full = qi - nnear
        s0 = nnear & 1

        def pair(i, carry):
            full_step(nnear + 2 * i, s0, True, None)
            full_step(nnear + 2 * i + 1, 1 - s0, True, None)
            return carry

        lax.fori_loop(0, n_full // 2, pair, 0)

        @pl.when(n_full % 2 == 1)
        def _():
            full_step(qi - 1, s0, True, None)

        stage_bc(qi, qi & 1, True)

    o1 = acc_ref[0, 0:C_DV, :] / acc_ref[0, C_DV:C_DV + 1, :]
    o2 = acc_ref[1, 0:C_DV, :] / acc_ref[1, C_DV:C_DV + 1, :]
    o = o1 - lam_ref[0] * o2
    on = o * lax.rsqrt(jnp.mean(o * o, axis=0, keepdims=True) + EPS) * g_ref[...]
    out_ref[...] = on.astype(BF16)


def _diff_nnear(nq):
    d_sat = -(-(T5_MAX_DIST - 1 + DIFF_T) // DIFF_T)
    return min(d_sat, nq)


def _diff(p, pt, bias, cfar, lam, gsub, bsz, seq):
    T = DIFF_T
    nq = seq // T
    nnear = bias.shape[1]
    kcol = SEG_CK * SEG // LANES
    n = bsz * seq
    return pl.pallas_call(
        functools.partial(_diff_kernel, nnear=nnear),
        grid=(bsz, C_HEADS, nq),
        in_specs=[
            pl.BlockSpec(memory_space=pltpu.SMEM),
            pl.BlockSpec(memory_space=pltpu.SMEM),
            pl.BlockSpec((None, 2 * C_DQK, T), lambda b, h, i: (TSEG_CQ, h, b * nq + i)),
            pl.BlockSpec((seq, LANES), lambda b, h, i: (b, kcol + h)),
            pl.BlockSpec((None, C_DV, seq), lambda b, h, i: (TSEG_CV, h, b)),
            pl.BlockSpec((None, nnear, T, T), lambda b, h, i: (h, 0, 0, 0)),
            pl.BlockSpec((C_DV, 1), lambda b, h, i: (0, 0)),
        ],
        out_specs=pl.BlockSpec((C_DV, T), lambda b, h, i: (h, b * nq + i)),
        out_shape=jax.ShapeDtypeStruct((C_WIDTH, n), BF16),
        scratch_shapes=[
            pltpu.VMEM((2 * C_DQK, 2 * T), BF16),
            pltpu.VMEM((2, 2, T, T), F32),
            pltpu.VMEM((2, 2, 1, T), F32),
            pltpu.VMEM((2, C_DV + DIFF_ONES, T), F32),
            pltpu.VMEM((2, 1, T), F32),
        ],
        compiler_params=_cparams(("parallel", "parallel", "arbitrary")),
        name="diff_attn",
    )(lam, cfar, pt, p, pt, bias, gsub)


def _t5_bucket(rel):
    nb = T5_BUCKETS // 2
    max_exact = nb // 2
    ret = (rel > 0).astype(jnp.int32) * nb
    n = jnp.abs(rel)
    large = max_exact + (jnp.log(jnp.maximum(n, max_exact).astype(F32) / max_exact)
                         / math.log(T5_MAX_DIST / max_exact) * (nb - max_exact)).astype(jnp.int32)
    large = jnp.minimum(large, nb - 1)
    return ret + jnp.where(n < max_exact, n, large)


def _diff_bias(t5_table, nnear):
    T = DIFF_T
    e = np.arange(2 * T)
    amc = np.where(e < T, -e, 2 * T - e)
    rel = jnp.asarray(-np.arange(nnear)[:, None] * T + amc[None, :], jnp.int32)
    base = jnp.moveaxis(t5_table[_t5_bucket(rel)], -1, 0).astype(F32) * LOG2E
    tiles = _toeplitz(base, T, T)
    a = np.arange(T)[:, None]
    c = np.arange(T)[None, :]
    allowed = np.ones((nnear, T, T), bool)
    allowed[0] = (a // CHUNK) <= (c // CHUNK)
    far = t5_table[_t5_bucket(jnp.asarray(-T5_MAX_DIST, jnp.int32))].astype(F32) * LOG2E
    return jnp.where(allowed[None], tiles, NEG), far


def _merge_kernel(hat_ref, hbt_ref, hct_ref, g0_ref, g1_ref, g2_ref, x_ref, wb_ref, wo_ref,
                  n2_ref, wr_ref, br_ref, x1_ref, h2_ref, lg_ref, y_ref):
    tn = (((0,), (0,)), ((), ()))
    tm = x_ref.shape[0]
    halves = [slice(i * (tm // 2), (i + 1) * (tm // 2)) for i in range(2)]
    for r in halves:
        y = g0_ref[r, :].astype(F32) * lax.dot_general(hat_ref[:, r], wb_ref[0], tn,
                                                       preferred_element_type=F32)
        y = y + g1_ref[r, :].astype(F32) * lax.dot_general(hbt_ref[:, r], wb_ref[1], tn,
                                                           preferred_element_type=F32)
        y = y + g2_ref[r, :].astype(F32) * lax.dot_general(hct_ref[:, r], wb_ref[2], tn,
                                                           preferred_element_type=F32)
        y_ref[r, :] = y.astype(BF16)
    for r in halves:
        x1 = x_ref[r, :] + jnp.dot(y_ref[r, :], wo_ref[...], preferred_element_type=F32)
        x1_ref[r, :] = x1
        h2 = x1 * lax.rsqrt(jnp.mean(x1 * x1, axis=-1, keepdims=True) + EPS) * n2_ref[...]
        h2_ref[r, :] = h2.astype(BF16)
    for r in halves:
        lg_ref[r, :] = jnp.dot(h2_ref[r, :], wr_ref[...], preferred_element_type=F32) + br_ref[...]


def _merge(ha, hbt, hct, p, x2, wb, wo, n2, wr, br, tm):
    n = x2.shape[0]
    gcol = SEG_GATES * SEG // D_MODEL

    def rows(width, col=0):
        return pl.BlockSpec((tm, width), lambda i: (i, col))

    def cols():
        return pl.BlockSpec((BRANCH_WIDTH, tm), lambda i: (0, i))

    def full(shape):
        return pl.BlockSpec(shape, lambda i: (0,) * len(shape))

    return pl.pallas_call(
        _merge_kernel,
        grid=(n // tm,),
        in_specs=[
            cols(), cols(), cols(),
            rows(D_MODEL, gcol), rows(D_MODEL, gcol + 1), rows(D_MODEL, gcol + 2),
            rows(D_MODEL),
            full((N_BRANCH, BRANCH_WIDTH, D_MODEL)), full((D_MODEL, D_MODEL)),
            full((1, D_MODEL)), full((D_MODEL, LANES)), full((1, LANES)),
        ],
        out_specs=[rows(D_MODEL), rows(D_MODEL), rows(LANES)],
        out_shape=[
            jax.ShapeDtypeStruct((n, D_MODEL), F32),
            jax.ShapeDtypeStruct((n, D_MODEL), BF16),
            jax.ShapeDtypeStruct((n, LANES), F32),
        ],
        scratch_shapes=[pltpu.VMEM((tm, D_MODEL), BF16)],
        compiler_params=_cparams(("parallel",)),
        name="merge",
    )(ha, hbt, hct, p, p, p, x2, wb, wo, n2, wr, br)


def _combine_weights(lg):
    lanef = lax.broadcasted_iota(jnp.int32, lg.shape, 1).astype(F32)
    big = 1e9
    is_g = (lanef >= N_EXPERTS) & (lanef < N_EXPERTS + N_GROUPS)
    gl = jnp.where(is_g, lg, -jnp.inf)
    gmax = jnp.max(gl, axis=-1, keepdims=True)
    g_idx = jnp.min(jnp.where(gl == gmax, lanef - N_EXPERTS, big), axis=-1, keepdims=True)
    p_g = 1.0 / jnp.sum(jnp.exp(gl - gmax), axis=-1, keepdims=True)
    in_grp = (lanef >= g_idx * EXPERTS_PER_GROUP) & (lanef < (g_idx + 1.0) * EXPERTS_PER_GROUP)
    el = jnp.where(in_grp, lg, -jnp.inf)
    ee = jnp.exp(el - jnp.max(el, axis=-1, keepdims=True))
    ep = ee / jnp.sum(ee, axis=-1, keepdims=True)
    ep = jnp.where(in_grp, ep, -1.0)
    v1 = jnp.max(ep, axis=-1, keepdims=True)
    i1 = jnp.min(jnp.where(ep == v1, lanef, big), axis=-1, keepdims=True)
    ep2 = jnp.where(lanef == i1, -1.0, ep)
    v2 = jnp.max(ep2, axis=-1, keepdims=True)
    i2 = jnp.min(jnp.where(ep2 == v2, lanef, big), axis=-1, keepdims=True)
    tot = v1 + v2
    comb = jnp.where(lanef == i1, p_g * (v1 / tot), 0.0) + jnp.where(lanef == i2, p_g * (v2 / tot), 0.0)
    return comb, g_idx


def _moe_kernel(x1_ref, h2_ref, lg_ref, tri_ref, wgu_ref, wd_ref, out_ref,
                pt_ref, xs_ref, ws_ref, ys_ref, meta_ref, *, rcap):
    C = MOE_C
    s = pl.program_id(1)
    tm = h2_ref.shape[0]

    @pl.when(s == 0)
    def _():
        comb, g_idx = _combine_weights(lg_ref[...])
        lanef = lax.broadcasted_iota(jnp.int32, comb.shape, 1).astype(F32)
        mine = lanef == g_idx
        onehot = jnp.where(mine, 1.0, 0.0)
        ranks = jnp.dot(tri_ref[...], onehot.astype(BF16), preferred_element_type=F32)
        dest = jnp.sum(jnp.where(mine, ranks, 0.0), axis=-1, keepdims=True)
        off = jnp.int32(0)
        for g in range(N_GROUPS):
            cnt = jnp.sum(onehot[:, g:g + 1]).astype(jnp.int32)
            nchunk = (cnt + (C - 1)) // C
            meta_ref[g] = off
            meta_ref[N_GROUPS + g] = nchunk
            dest = dest + jnp.where(g_idx == float(g), (off * C).astype(F32), 0.0)
            off = off + nchunk
        slot = lax.broadcasted_iota(jnp.int32, (tm, rcap), 1).astype(F32)
        pt = jnp.where(dest == slot, 1.0, 0.0).astype(BF16)
        pt_ref[...] = pt
        comb_hi = comb.astype(BF16)
        comb_lo = (comb - comb_hi.astype(F32)).astype(BF16)
        packed = jnp.concatenate([h2_ref[...], comb_hi, comb_lo], axis=1)
        srt = lax.dot_general(pt, packed, (((0,), (0,)), ((), ())), preferred_element_type=F32)
        xs_ref[...] = srt[:, 0:D_MODEL].astype(BF16)
        ws_ref[...] = srt[:, D_MODEL:D_MODEL + LANES] + srt[:, D_MODEL + LANES:D_MODEL + 2 * LANES]
        ys_ref[...] = jnp.zeros_like(ys_ref)

    def do_chunk(r0, m):
        r0 = pl.multiple_of(r0, C)
        xc = xs_ref[pl.ds(r0, m), :]
        wsc = ws_ref[pl.ds(r0, m), :]
        lane = lax.broadcasted_iota(jnp.int32, (m, LANES), 1)
        y = None
        for k in range(MOE_EPS):
            gu = jnp.dot(xc, wgu_ref[k], preferred_element_type=F32)
            gate = gu[:, 0:D_EXPERT]
            wk = jnp.sum(jnp.where(lane == s * MOE_EPS + k, wsc, 0.0), axis=-1, keepdims=True)
            he = (gate * jax.nn.sigmoid(gate) * gu[:, D_EXPERT:2 * D_EXPERT] * wk).astype(BF16)
            yk = jnp.dot(he, wd_ref[k], preferred_element_type=F32)
            y = yk if y is None else y + yk
        ys_ref[pl.ds(r0, m), :] = y.astype(BF16)

    g = s // (EXPERTS_PER_GROUP // MOE_EPS)
    start = meta_ref[g]
    nchunk = meta_ref[N_GROUPS + g]

    def pair(i, carry):
        do_chunk((start + 2 * i) * C, 2 * C)
        return carry

    lax.fori_loop(0, nchunk // 2, pair, 0)

    @pl.when(nchunk % 2 == 1)
    def _():
        do_chunk((start + nchunk - 1) * C, C)

    @pl.when(s == pl.num_programs(1) - 1)
    def _():
        out_ref[...] = x1_ref[...] + jnp.dot(pt_ref[...], ys_ref[...], preferred_element_type=F32)


def _moe(x1, h2, lg, wgu, wd, tm):
    n = x1.shape[0]
    rcap = tm + N_GROUPS * MOE_C
    idx = np.arange(tm)
    tri = jnp.asarray(idx[None, :] < idx[:, None], BF16)
    return pl.pallas_call(
        functools.partial(_moe_kernel, rcap=rcap),
        grid=(n // tm, N_EXPERTS // MOE_EPS),
        in_specs=[
            pl.BlockSpec((tm, D_MODEL), lambda i, s: (i, 0), pipeline_mode=pl.Buffered(1)),
            pl.BlockSpec((tm, D_MODEL), lambda i, s: (i, 0), pipeline_mode=pl.Buffered(1)),
            pl.BlockSpec((tm, LANES), lambda i, s: (i, 0), pipeline_mode=pl.Buffered(1)),
            pl.BlockSpec((tm, tm), lambda i, s: (0, 0), pipeline_mode=pl.Buffered(1)),
            pl.BlockSpec((MOE_EPS, D_MODEL, 2 * D_EXPERT), lambda i, s: (s, 0, 0)),
            pl.BlockSpec((MOE_EPS, D_EXPERT, D_MODEL), lambda i, s: (s, 0, 0)),
        ],
        out_specs=pl.BlockSpec((tm, D_MODEL), lambda i, s: (i, 0), pipeline_mode=pl.Buffered(1)),
        out_shape=jax.ShapeDtypeStruct((n, D_MODEL), F32),
        scratch_shapes=[
            pltpu.VMEM((tm, rcap), BF16),
            pltpu.VMEM((rcap, D_MODEL), BF16),
            pltpu.VMEM((rcap, LANES), F32),
            pltpu.VMEM((rcap, D_MODEL), BF16),
            pltpu.SMEM((2 * N_GROUPS,), jnp.int32),
        ],
        compiler_params=pltpu.CompilerParams(dimension_semantics=("parallel", "arbitrary"),
                                             vmem_limit_bytes=MOE_VMEM_LIMIT),
        name="moe",
    )(x1, h2, lg, tri, wgu, wd)


def _tile(n, pref):
    t = pref
    while n % t:
        t //= 2
    return t


def _mixer_params(layer, norm1_g, w_in, a_conv_w, a_conv_b, a_gate_bias, a_out_norm_g,
                  b_qk_norm_g, b_rel_bias, c_qk_norm_g, c_lambda, c_sub_norm_g, t5_bias,
                  w_branch, w_out, nq_diff):
    n_small = 2 * A_HEADS
    cut = 4 * A_WIDTH
    w_main = jnp.concatenate([w_in[:, :cut], w_in[:, cut + n_small:]], axis=1)
    w_main = w_main.reshape(D_MODEL, N_SEG, SEG)[:, np.asarray(SEG_PERM), :]
    w_main = w_main.reshape(D_MODEL, N_SEG * SEG).astype(BF16)
    w_if = jnp.pad(w_in[:, cut:cut + n_small], ((0, 0), (0, LANES - n_small))).astype(BF16)
    gain = jnp.ones((N_SEG, SEG), F32)
    gain = gain.at[N_ROW_SEG + TSEG_BQ].set(jnp.tile(b_qk_norm_g[0], B_HEADS) * (B_DH ** -0.5 * LOG2E))
    gain = gain.at[SEG_BK].set(jnp.tile(b_qk_norm_g[1], B_HEADS))
    gain = gain.at[N_ROW_SEG + TSEG_CQ].set(
        jnp.tile(c_qk_norm_g[0], 2 * C_HEADS) * (C_DQK ** -0.5 * LOG2E))
    gain = gain.at[SEG_CK].set(jnp.tile(c_qk_norm_g[1], 2 * C_HEADS))
    diff_bias, diff_far = _diff_bias(t5_bias, _diff_nnear(nq_diff))
    lam_init = 0.8 - 0.6 * math.exp(-0.3 * layer)
    lf32 = c_lambda.astype(F32)
    lam = jnp.exp(jnp.sum(lf32[0] * lf32[1])) - jnp.exp(jnp.sum(lf32[2] * lf32[3])) + lam_init
    return dict(
        g1=norm1_g.reshape(1, D_MODEL), w_main=w_main, w_if=w_if,
        gain=gain.reshape(N_SEG, 1, SEG),
        cw=a_conv_w, cb=a_conv_b.reshape(1, -1),
        gbr=jnp.pad(a_gate_bias, (0, LANES - n_small)).reshape(1, LANES),
        gbc=a_gate_bias.reshape(n_small, 1),
        ag=a_out_norm_g.reshape(1, A_WIDTH),
        band_bias=_band_bias(b_rel_bias),
        diff_bias=diff_bias, diff_far=diff_far,
        lam=lam.reshape(1).astype(F32),
        gsub=(c_sub_norm_g * (1.0 - lam_init)).reshape(C_DV, 1),
        wb=w_branch.astype(BF16), wo=w_out.astype(BF16),
    )


def _layer(x2, bsz, seq, mp, norm2_g, w_group, b_group, w_router, b_router, w_e_gate, w_e_up, w_e_down):
    n = bsz * seq
    p, pt, gif = _inproj(x2, mp["g1"], mp["w_main"], mp["w_if"], mp["gain"], _tile(n, 512))
    gift = jnp.transpose(gif[:, :2 * A_HEADS])
    ha = _mlstm(p, pt, gif, gift, mp["cw"], mp["cb"], mp["gbr"], mp["gbc"], mp["ag"], bsz, seq)
    hbt = _band(p, pt, mp["band_bias"], bsz, seq)
    hct = _diff(p, pt, mp["diff_bias"], mp["diff_far"], mp["lam"], mp["gsub"], bsz, seq)

    wr = jnp.concatenate([w_router, w_group], axis=1)
    wr = jnp.pad(wr, ((0, 0), (0, LANES - wr.shape[1]))).astype(BF16)
    br = jnp.pad(jnp.concatenate([b_router, b_group]), (0, LANES - N_EXPERTS - N_GROUPS)).reshape(1, LANES)
    x1, h2, lg = _merge(ha, hbt, hct, p, x2, mp["wb"], mp["wo"], norm2_g.reshape(1, D_MODEL), wr, br,
                        _tile(n, 512))
    wgu = jnp.concatenate([w_e_gate, w_e_up], axis=-1).astype(BF16)
    return _moe(x1, h2, lg, wgu, w_e_down.astype(BF16), _tile(n, MOE_TM))


def kernel(x, norm1_g, w_in, a_conv_w, a_conv_b, a_gate_bias, a_out_norm_g, b_qk_norm_g, b_rel_bias,
           c_qk_norm_g, c_lambda, c_sub_norm_g, t5_bias, w_branch, w_out, norm2_g, w_group, b_group,
           w_router, b_router, w_e_gate, w_e_up, w_e_down):
    bsz, seq, _ = x.shape
    assert seq % DIFF_T == 0 and seq % MLSTM_L == 0 and seq % BAND_TQ == 0
    x2 = x.reshape(bsz * seq, D_MODEL)
    for l in range(norm1_g.shape[0]):
        mp = _mixer_params(l, norm1_g[l], w_in[l], a_conv_w[l], a_conv_b[l], a_gate_bias[l],
                           a_out_norm_g[l], b_qk_norm_g[l], b_rel_bias[l], c_qk_norm_g[l], c_lambda[l],
                           c_sub_norm_g[l], t5_bias, w_branch[l], w_out[l], seq // DIFF_T)
        x2 = _layer(x2, bsz, seq, mp, norm2_g[l], w_group[l], b_group[l], w_router[l], b_router[l],
                    w_e_gate[l], w_e_up[l], w_e_down[l])
    return x2.reshape(bsz, seq, D_MODEL)
```

```python
import functools
import math

import numpy as np
import jax
import jax.numpy as jnp
from jax import lax
from jax.experimental import pallas as pl
from jax.experimental.pallas import tpu as pltpu

F32 = jnp.float32
BF16 = jnp.bfloat16

D_MODEL = 1024
CHUNK = 64
EPS = 1e-6
NEG = -1e30
LOG2E = math.log2(math.e)

A_HEADS = 4
A_DH = 128
A_WIDTH = A_HEADS * A_DH
CONV_W = 4
GATE_CAP = 15.0

B_HEADS = 8
B_DH = 64
B_WIDTH = B_HEADS * B_DH
B_LEFT_CHUNKS = 8
B_MAX_REL = 256

C_HEADS = 4
C_DQK = 64
C_DV = 2 * C_DQK
C_WIDTH = C_HEADS * C_DV

T5_BUCKETS = 32
T5_MAX_DIST = 1024

N_BRANCH = 3
BRANCH_WIDTH = 512

N_GROUPS = 4
EXPERTS_PER_GROUP = 8
N_EXPERTS = N_GROUPS * EXPERTS_PER_GROUP
D_EXPERT = D_MODEL // 4

LANES = 128
SEG = 512
N_SEG = 16
VMEM_LIMIT = 48 * 1024 * 1024

SEG_GATES, SEG_AQ, SEG_AK, SEG_BK, SEG_CK = 0, 6, 7, 8, 9
N_ROW_SEG = 10
TSEG_AV, TSEG_AO, TSEG_BQ, TSEG_BV, TSEG_CQ, TSEG_CV = 0, 1, 2, 3, 4, 5
N_T_SEG = N_SEG - N_ROW_SEG
SEG_PERM = (10, 11, 12, 13, 14, 15, 0, 1, 5, 8, 2, 3, 4, 6, 7, 9)

MLSTM_L = 128
BAND_TQ = 128
BAND_NKB = 1 + (B_LEFT_CHUNKS * CHUNK) // BAND_TQ
BAND_ONES = 16
DIFF_T = 512
DIFF_ONES = 16
MOE_TM = 1024
MOE_C = 64
MOE_BIG = 4
MOE_EPS = EXPERTS_PER_GROUP
MOE_VMEM_LIMIT = 56 * 1024 * 1024

def _cparams(sem, flags=None):
    return pltpu.CompilerParams(dimension_semantics=sem, vmem_limit_bytes=VMEM_LIMIT, flags=flags)


NORM_SEGS = (SEG_BK, SEG_CK, N_ROW_SEG + TSEG_BQ, N_ROW_SEG + TSEG_CQ)
SIGMOID_SEGS = (N_ROW_SEG + TSEG_AO,) + tuple(range(SEG_GATES, SEG_GATES + 6))


def _head_norm_t(acc_t):
    rows, tm = acc_t.shape
    a3 = acc_t.reshape(rows // 64, 64, tm)
    ssq = jnp.sum(a3 * a3, axis=1, keepdims=True)
    return (a3 * lax.rsqrt(ssq * (1.0 / 64.0) + EPS)).reshape(rows, tm)


def _inproj_kernel(x_ref, g_ref, w_ref, wif_ref, gain_ref, gain_t_ref, p_ref, pt_ref, gif_ref):
    xf = x_ref[...]
    xn = (xf * lax.rsqrt(jnp.mean(xf * xf, axis=-1, keepdims=True) + EPS) * g_ref[...]).astype(BF16)
    gif_ref[...] = jnp.dot(xn, wif_ref[...], preferred_element_type=F32)
    for j in range(N_SEG):
        cols = slice(j * SEG, (j + 1) * SEG)
        acc = jnp.dot(xn, w_ref[:, cols], preferred_element_type=F32)
        if j in NORM_SEGS:
            acc_t = _head_norm_t(jnp.transpose(acc))
            if j < N_ROW_SEG:
                p_ref[:, cols] = (jnp.transpose(acc_t) * gain_ref[j]).astype(BF16)
            else:
                pt_ref[j - N_ROW_SEG] = (acc_t * gain_t_ref[NORM_SEGS.index(j) - 2]).astype(BF16)
            continue
        if j in SIGMOID_SEGS:
            acc = jax.nn.sigmoid(acc)
        if j < N_ROW_SEG:
            p_ref[:, cols] = acc.astype(BF16)
        else:
            pt_ref[j - N_ROW_SEG] = jnp.transpose(acc).astype(BF16)


def _inproj(x2, g, w, wif, gain, tm):
    n = x2.shape[0]
    tsegs = np.asarray([N_ROW_SEG + TSEG_BQ, N_ROW_SEG + TSEG_CQ])
    gain_t = jnp.broadcast_to(gain[tsegs, 0, :, None], (2, SEG, tm))

    def const(shape):
        return pl.BlockSpec(shape, lambda i: (0,) * len(shape), pipeline_mode=pl.Buffered(1))

    return pl.pallas_call(
        _inproj_kernel,
        grid=(n // tm,),
        in_specs=[
            pl.BlockSpec((tm, D_MODEL), lambda i: (i, 0)),
            const((1, D_MODEL)),
            const((D_MODEL, N_SEG * SEG)),
            const((D_MODEL, LANES)),
            const((N_SEG, 1, SEG)),
            const((2, SEG, tm)),
        ],
        out_specs=[
            pl.BlockSpec((tm, N_ROW_SEG * SEG), lambda i: (i, 0)),
            pl.BlockSpec((N_T_SEG, SEG, tm), lambda i: (0, 0, i)),
            pl.BlockSpec((tm, LANES), lambda i: (i, 0)),
        ],
        out_shape=[
            jax.ShapeDtypeStruct((n, N_ROW_SEG * SEG), BF16),
            jax.ShapeDtypeStruct((N_T_SEG, SEG, n), BF16),
            jax.ShapeDtypeStruct((n, LANES), F32),
        ],
        compiler_params=_cparams(("parallel",)),
        name="inproj",
    )(x2, g, w, wif, gain, gain_t)


def _log_sigmoid(z):
    return jnp.minimum(z, 0.0) - jnp.log(1.0 + jnp.exp(-jnp.abs(z)))


def _split3(a):
    hi = a.astype(BF16)
    r1 = a - hi.astype(F32)
    mid = r1.astype(BF16)
    lo = (r1 - mid.astype(F32)).astype(BF16)
    return hi, mid, lo


def _mlstm_kernel(aq_ref, ak_ref, vt_ref, aot_ref, gif_ref, gift_ref, cw_ref, cb_ref,
                  gbr_ref, gbc_ref, agt_ref, out_ref, ubuf, kq_ref, st_ref, ct_ref, n_ref, m_ref):
    L = MLSTM_L
    c = pl.program_id(1)

    @pl.when(c == 0)
    def _():
        ubuf[0:8, :] = jnp.zeros((8, 2 * A_WIDTH), F32)
        ct_ref[...] = jnp.zeros_like(ct_ref)
        n_ref[...] = jnp.zeros_like(n_ref)
        m_ref[...] = jnp.zeros_like(m_ref)

    @pl.when(c > 0)
    def _():
        ubuf[0:8, :] = ubuf[L:L + 8, :]

    ubuf[8:L + 8, 0:A_WIDTH] = aq_ref[...].astype(F32)
    ubuf[8:L + 8, A_WIDTH:2 * A_WIDTH] = ak_ref[...].astype(F32)
    y = cb_ref[...] + cw_ref[0:1, :] * ubuf[8:L + 8, :]
    for t in range(1, CONV_W):
        y = y + cw_ref[t:t + 1, :] * ubuf[8 - t:8 - t + L, :]
    qk = y * jax.nn.sigmoid(y)
    q_t = jnp.transpose(qk[:, 0:A_WIDTH]).astype(BF16)
    k_all = (qk[:, A_WIDTH:2 * A_WIDTH] * (A_DH ** -0.5)).astype(BF16)

    zc = gif_ref[...] + gbr_ref[...]
    ig_c = GATE_CAP * jnp.tanh(zc * (1.0 / GATE_CAP))
    lf_c = _log_sigmoid(zc)
    zr = gift_ref[...] + gbc_ref[...]
    ig_r = GATE_CAP * jnp.tanh(zr * (1.0 / GATE_CAP))
    lf_r = _log_sigmoid(zr)

    row = lax.broadcasted_iota(jnp.int32, (L, L), 0)
    col = lax.broadcasted_iota(jnp.int32, (L, L), 1)
    causal = col <= row
    tril = jnp.where(causal, 1.0, 0.0).astype(BF16)
    triu = jnp.where(row <= col, 1.0, 0.0).astype(BF16)
    b_c = sum(jnp.dot(tril, piece, preferred_element_type=F32) for piece in _split3(lf_c))
    b_r = sum(jnp.dot(piece, triu, preferred_element_type=F32) for piece in _split3(lf_r))

    sub8 = lax.broadcasted_iota(jnp.int32, (8, L), 0)
    heads = [slice(h * A_DH, (h + 1) * A_DH) for h in range(A_HEADS)]
    for h, rows in enumerate(heads):
        kq_ref[h] = jnp.dot(k_all[:, rows], q_t[rows, :], preferred_element_type=F32)

    stats = []
    for h, rows in enumerate(heads):
        bcol = b_c[:, A_HEADS + h:A_HEADS + h + 1]
        brow = b_r[A_HEADS + h:A_HEADS + h + 1, :]
        igcol = ig_c[:, h:h + 1]
        m_prev = m_ref[h][:, 0:1]
        dmat = jnp.where(row <= col, brow + (igcol - bcol), NEG)
        inter = brow + m_prev
        m_t = jnp.maximum(inter, jnp.max(dmat, axis=0, keepdims=True))
        st = kq_ref[h] * jnp.exp(dmat - m_t)
        st_ref[h] = st.astype(BF16)
        stats.append((brow, m_prev, m_t, jnp.exp(inter - m_t), jnp.sum(st, axis=0, keepdims=True)))

    for h, rows in enumerate(heads):
        brow, m_prev, m_t, w_inter, st_sum = stats[h]
        qt = q_t[rows, :]
        k = k_all[:, rows]
        vt = vt_ref[rows, :]
        igrow = ig_r[h:h + 1, :]
        b_last = brow[:, L - 1:L]
        ct = ct_ref[h]
        n8 = n_ref[h]

        num = w_inter * jnp.dot(ct.astype(BF16), qt, preferred_element_type=F32)
        num = num + jnp.dot(vt, st_ref[h], preferred_element_type=F32)
        nq = jnp.dot(n8.astype(BF16), qt, preferred_element_type=F32)[0:1, :]
        den = w_inter * nq + st_sum
        hh = num * (1.0 / jnp.maximum(jnp.abs(den), jnp.exp(-m_t)))

        g_end = b_last - brow + igrow
        m_new = jnp.maximum(b_last + m_prev, jnp.max(g_end, axis=-1, keepdims=True))
        dec = jnp.exp(b_last + m_prev - m_new)
        w_end = jnp.exp(g_end - m_new)
        vw = (vt.astype(F32) * w_end).astype(BF16)
        ct_ref[h] = dec * ct + jnp.dot(vw, k, preferred_element_type=F32)
        w8 = jnp.where(sub8 == 0, w_end, 0.0).astype(BF16)
        n_ref[h] = dec * n8 + jnp.dot(w8, k, preferred_element_type=F32)
        m_ref[h] = jnp.broadcast_to(m_new, (1, LANES))

        hn = hh * lax.rsqrt(jnp.mean(hh * hh, axis=0, keepdims=True) + EPS) * agt_ref[rows, :]
        out_ref[rows, :] = (hn * aot_ref[rows, :].astype(F32)).astype(BF16)


def _mlstm(p, pt, gif, gift, cw, cb, gbr, gbc, ag, bsz, seq):
    L = MLSTM_L
    nc = seq // L
    n = bsz * seq
    agt = jnp.broadcast_to(ag.reshape(A_WIDTH, 1), (A_WIDTH, L))

    def tseg(j):
        return pl.BlockSpec((None, SEG, L), lambda b, c: (j, 0, b * nc + c))

    def full(shape):
        return pl.BlockSpec(shape, lambda b, c: (0,) * len(shape))

    return pl.pallas_call(
        _mlstm_kernel,
        grid=(bsz, nc),
        in_specs=[
            pl.BlockSpec((L, SEG), lambda b, c: (b * nc + c, SEG_AQ)),
            pl.BlockSpec((L, SEG), lambda b, c: (b * nc + c, SEG_AK)),
            tseg(TSEG_AV), tseg(TSEG_AO),
            pl.BlockSpec((L, LANES), lambda b, c: (b * nc + c, 0)),
            pl.BlockSpec((8, L), lambda b, c: (0, b * nc + c)),
            full((CONV_W, 2 * A_WIDTH)), full((1, 2 * A_WIDTH)),
            full((1, LANES)), full((8, 1)), full((A_WIDTH, L)),
        ],
        out_specs=pl.BlockSpec((A_WIDTH, L), lambda b, c: (0, b * nc + c)),
        out_shape=jax.ShapeDtypeStruct((A_WIDTH, n), BF16),
        scratch_shapes=[
            pltpu.VMEM((L + 8, 2 * A_WIDTH), F32),
            pltpu.VMEM((A_HEADS, L, L), F32),
            pltpu.VMEM((A_HEADS, L, L), BF16),
            pltpu.VMEM((A_HEADS, A_DH, A_DH), F32),
            pltpu.VMEM((A_HEADS, 8, A_DH), F32),
            pltpu.VMEM((A_HEADS, 1, LANES), F32),
        ],
        compiler_params=_cparams(("parallel", "arbitrary")),
        name="mlstm",
    )(p, p, pt, pt, gif, gift, cw, cb, gbr, gbc, agt)


def _band_kernel(*refs):
    nkb = BAND_NKB
    qt_ref = refs[0]
    k_refs = refs[1:1 + nkb]
    vt_refs = refs[1 + nkb:1 + 2 * nkb]
    bias_ref = refs[1 + 2 * nkb]
    out_ref = refs[2 + 2 * nkb]
    s_ref, mx_ref = refs[3 + 2 * nkb:5 + 2 * nkb]
    tq = BAND_TQ
    nk = nkb * tq
    i = pl.program_id(1)

    def compute(mask_start):
        k_all = jnp.concatenate([r[...] for r in k_refs], axis=0)
        vt_all = jnp.concatenate([r[...] for r in vt_refs], axis=1)
        ones = jnp.ones((BAND_ONES, nk), BF16)
        row = lax.broadcasted_iota(jnp.int32, (LANES, tq), 0)
        lo = row < B_DH
        if mask_start:
            kidx = lax.broadcasted_iota(jnp.int32, (nk, 1), 0)
            valid = (kidx + (i - (nkb - 1)) * tq) >= 0

        def score(p):
            rows = slice(p * LANES, (p + 1) * LANES)
            qtp = qt_ref[rows, :]
            zero = jnp.zeros_like(qtp)
            qbd = jnp.concatenate([jnp.where(lo, qtp, zero), jnp.where(lo, zero, qtp)], axis=1)
            s = jnp.dot(k_all[:, rows], qbd, preferred_element_type=F32) + bias_ref[p]
            if mask_start:
                s = jnp.where(valid, s, NEG)
            s_ref[p & 1] = s
            mx_ref[p & 1] = jnp.max(s, axis=0, keepdims=True)

        def finish(p):
            rows = slice(p * LANES, (p + 1) * LANES)
            pr = jnp.exp2((s_ref[p & 1] - mx_ref[p & 1]).astype(BF16))
            o = jnp.dot(jnp.concatenate([vt_all[rows, :], ones], axis=0), pr,
                        preferred_element_type=F32)
            o = o[0:LANES, :] / o[LANES:LANES + 1, :]
            out_ref[rows, :] = jnp.where(lo, o[:, 0:tq], o[:, tq:2 * tq]).astype(BF16)

        score(0)
        for p in range(B_HEADS // 2):
            if p + 1 < B_HEADS // 2:
                score(p + 1)
            finish(p)

    @pl.when(i < nkb - 1)
    def _():
        compute(True)

    @pl.when(i >= nkb - 1)
    def _():
        compute(False)


def _band(p, pt, bias, bsz, seq):
    tq = BAND_TQ
    nkb = BAND_NKB
    nq = seq // tq
    n = bsz * seq

    def kblk(d):
        return pl.BlockSpec((tq, SEG), lambda b, i: (b * nq + jnp.maximum(i - d, 0), SEG_BK))

    def vblk(d):
        return pl.BlockSpec((None, SEG, tq), lambda b, i: (TSEG_BV, 0, b * nq + jnp.maximum(i - d, 0)))

    in_specs = [pl.BlockSpec((None, SEG, tq), lambda b, i: (TSEG_BQ, 0, b * nq + i))]
    in_specs += [kblk(d) for d in range(nkb - 1, -1, -1)]
    in_specs += [vblk(d) for d in range(nkb - 1, -1, -1)]
    in_specs += [pl.BlockSpec(bias.shape, lambda b, i: (0, 0, 0))]
    return pl.pallas_call(
        _band_kernel,
        grid=(bsz, nq),
        in_specs=in_specs,
        out_specs=pl.BlockSpec((B_WIDTH, tq), lambda b, i: (0, b * nq + i)),
        out_shape=jax.ShapeDtypeStruct((B_WIDTH, n), BF16),
        scratch_shapes=[
            pltpu.VMEM((2, nkb * tq, 2 * tq), F32),
            pltpu.VMEM((2, 1, 2 * tq), F32),
        ],
        compiler_params=_cparams(("parallel", "parallel")),
        name="band_attn",
    )(pt, *([p] * nkb), *([pt] * nkb), bias)


def _toeplitz(base, m, n):
    period = base.shape[-1]
    assert n <= period - 1
    reps = (1,) * (base.ndim - 1) + (m,)
    big = jnp.tile(base, reps)[..., :m * (period - 1)]
    return big.reshape(base.shape[:-1] + (m, period - 1))[..., :n]


def _band_bias(b_rel):
    tq = BAND_TQ
    nk = BAND_NKB * tq
    period = tq + nk
    e = np.arange(period)
    e = np.where(e < nk, e, e - period)
    rel = np.clip((nk - tq) - e, -B_MAX_REL, B_MAX_REL) + B_MAX_REL
    bias = _toeplitz(jnp.transpose(b_rel[rel]).astype(F32), tq, nk)
    qpos = np.arange(tq)
    kpos = np.arange(nk) - (nk - tq)
    qc = qpos[:, None] // CHUNK
    kc = np.floor_divide(kpos[None, :], CHUNK)
    allowed = (kc <= qc) & (kc >= qc - B_LEFT_CHUNKS)
    bias = jnp.where(allowed[None], bias * LOG2E, NEG)
    return jnp.swapaxes(bias.reshape(B_HEADS // 2, 2 * tq, nk), 1, 2)


def _diff_kernel(lam_ref, cfar_ref, qt_ref, k_ref, vt_ref, bias_ref, g_ref, out_ref,
                 qbd_ref, s_ref, mx_ref, acc_ref, m_ref, *, nnear):
    T = DIFF_T
    h = pl.program_id(1)
    qi = pl.program_id(2)
    cfar = cfar_ref[h]
    qt = qt_ref[...]
    row = lax.broadcasted_iota(jnp.int32, (2 * C_DQK, T), 0)
    zero = jnp.zeros_like(qt)
    qbd_ref[:, 0:T] = jnp.where(row < C_DQK, qt, zero)
    qbd_ref[:, T:2 * T] = jnp.where(row < C_DQK, zero, qt)
    m_ref[...] = jnp.full(m_ref.shape, NEG, F32)
    acc_ref[...] = jnp.zeros_like(acc_ref)

    def near_bias(t):
        return bias_ref[t] if t < nnear else None

    def stage_a(t, slot, bias, maps=(0, 1)):
        j = jnp.maximum(qi - t, 0)
        k = k_ref[pl.ds(pl.multiple_of(j * T, T), T), :]
        for mp in maps:
            sm = jnp.dot(k, qbd_ref[:, mp * T:(mp + 1) * T], preferred_element_type=F32)
            if bias is not None:
                sm = sm + bias
            s_ref[slot, mp] = sm
            mx = jnp.max(sm, axis=0, keepdims=True)
            mx_ref[slot, mp] = mx + cfar if bias is None else mx

    def stage_bc(t, slot, far, maps=(0, 1)):
        j = qi - t
        vt = jnp.concatenate([vt_ref[:, pl.ds(pl.multiple_of(j * T, T), T)],
                              jnp.ones((DIFF_ONES, T), BF16)], axis=0)
        for mp in maps:
            m_old = m_ref[mp]
            m_new = jnp.maximum(m_old, mx_ref[slot, mp])
            shift = m_new - cfar if far else m_new
            pr = jnp.exp2((s_ref[slot, mp] - shift).astype(BF16))
            m_ref[mp] = m_new
            acc_ref[mp] = jnp.exp2(m_old - m_new) * acc_ref[mp] + jnp.dot(
                vt, pr, preferred_element_type=F32)

    def full_step(t, slot, far, next_bias):
        for mp in range(2):
            stage_a(t + 1, 1 - slot, next_bias, (mp,))
            stage_bc(t, slot, far, (mp,))

    stage_a(0, 0, near_bias(0))
    for t in range(nnear):
        @pl.when(qi >= t)
        def _(t=t):
            full_step(t, t & 1, False, near_bias(t + 1))

    @pl.when(qi >= nnear)
    def _():
        n_full = qi - nnear
        s0 = nnear & 1

        def pair(i, carry):
            full_step(nnear + 2 * i, s0, True, None)
            full_step(nnear + 2 * i + 1, 1 - s0, True, None)
            return carry

        lax.fori_loop(0, n_full // 2, pair, 0)

        @pl.when(n_full % 2 == 1)
        def _():
            full_step(qi - 1, s0, True, None)

        stage_bc(qi, qi & 1, True)

    o1 = acc_ref[0, 0:C_DV, :] / acc_ref[0, C_DV:C_DV + 1, :]
    o2 = acc_ref[1, 0:C_DV, :] / acc_ref[1, C_DV:C_DV + 1, :]
    o = o1 - lam_ref[0] * o2
    on = o * lax.rsqrt(jnp.mean(o * o, axis=0, keepdims=True) + EPS) * g_ref[...]
    out_ref[...] = on.astype(BF16)


def _diff_nnear(nq):
    d_sat = -(-(T5_MAX_DIST - 1 + DIFF_T) // DIFF_T)
    return min(d_sat, nq)


def _diff(p, pt, bias, cfar, lam, gsub, bsz, seq):
    T = DIFF_T
    nq = seq // T
    nnear = bias.shape[1]
    kcol = SEG_CK * SEG // LANES
    n = bsz * seq
    return pl.pallas_call(
        functools.partial(_diff_kernel, nnear=nnear),
        grid=(bsz, C_HEADS, nq),
        in_specs=[
            pl.BlockSpec(memory_space=pltpu.SMEM),
            pl.BlockSpec(memory_space=pltpu.SMEM),
            pl.BlockSpec((None, 2 * C_DQK, T), lambda b, h, i: (TSEG_CQ, h, b * nq + i)),
            pl.BlockSpec((seq, LANES), lambda b, h, i: (b, kcol + h)),
            pl.BlockSpec((None, C_DV, seq), lambda b, h, i: (TSEG_CV, h, b)),
            pl.BlockSpec((None, nnear, T, T), lambda b, h, i: (h, 0, 0, 0)),
            pl.BlockSpec((C_DV, 1), lambda b, h, i: (0, 0)),
        ],
        out_specs=pl.BlockSpec((C_DV, T), lambda b, h, i: (h, b * nq + i)),
        out_shape=jax.ShapeDtypeStruct((C_WIDTH, n), BF16),
        scratch_shapes=[
            pltpu.VMEM((2 * C_DQK, 2 * T), BF16),
            pltpu.VMEM((2, 2, T, T), F32),
            pltpu.VMEM((2, 2, 1, T), F32),
            pltpu.VMEM((2, C_DV + DIFF_ONES, T), F32),
            pltpu.VMEM((2, 1, T), F32),
        ],
        compiler_params=_cparams(("parallel", "parallel", "arbitrary")),
        name="diff_attn",
    )(lam, cfar, pt, p, pt, bias, gsub)


def _t5_bucket(rel):
    nb = T5_BUCKETS // 2
    max_exact = nb // 2
    ret = (rel > 0).astype(jnp.int32) * nb
    n = jnp.abs(rel)
    large = max_exact + (jnp.log(jnp.maximum(n, max_exact).astype(F32) / max_exact)
                         / math.log(T5_MAX_DIST / max_exact) * (nb - max_exact)).astype(jnp.int32)
    large = jnp.minimum(large, nb - 1)
    return ret + jnp.where(n < max_exact, n, large)


def _diff_bias(t5_table, nnear):
    T = DIFF_T
    e = np.arange(2 * T)
    amc = np.where(e < T, -e, 2 * T - e)
    rel = jnp.asarray(-np.arange(nnear)[:, None] * T + amc[None, :], jnp.int32)
    base = jnp.moveaxis(t5_table[_t5_bucket(rel)], -1, 0).astype(F32) * LOG2E
    tiles = _toeplitz(base, T, T)
    a = np.arange(T)[:, None]
    c = np.arange(T)[None, :]
    allowed = np.ones((nnear, T, T), bool)
    allowed[0] = (a // CHUNK) <= (c // CHUNK)
    far = t5_table[_t5_bucket(jnp.asarray(-T5_MAX_DIST, jnp.int32))].astype(F32) * LOG2E
    return jnp.where(allowed[None], tiles, NEG), far


def _merge_kernel(hat_ref, hbt_ref, hct_ref, g0_ref, g1_ref, g2_ref, x_ref, wb_ref, wo_ref,
                  n2_ref, wr_ref, br_ref, x1_ref, h2_ref, lg_ref, y_ref):
    tn = (((0,), (0,)), ((), ()))
    tm = x_ref.shape[0]
    halves = [slice(i * (tm // 2), (i + 1) * (tm // 2)) for i in range(2)]
    for r in halves:
        y = g0_ref[r, :].astype(F32) * lax.dot_general(hat_ref[:, r], wb_ref[0], tn,
                                                       preferred_element_type=F32)
        y = y + g1_ref[r, :].astype(F32) * lax.dot_general(hbt_ref[:, r], wb_ref[1], tn,
                                                           preferred_element_type=F32)
        y = y + g2_ref[r, :].astype(F32) * lax.dot_general(hct_ref[:, r], wb_ref[2], tn,
                                                           preferred_element_type=F32)
        y_ref[r, :] = y.astype(BF16)
    for r in halves:
        x1 = x_ref[r, :] + jnp.dot(y_ref[r, :], wo_ref[...], preferred_element_type=F32)
        x1_ref[r, :] = x1
        h2 = x1 * lax.rsqrt(jnp.mean(x1 * x1, axis=-1, keepdims=True) + EPS) * n2_ref[...]
        h2_ref[r, :] = h2.astype(BF16)
    for r in halves:
        lg_ref[r, :] = jnp.dot(h2_ref[r, :], wr_ref[...], preferred_element_type=F32) + br_ref[...]


def _merge(ha, hbt, hct, p, x2, wb, wo, n2, wr, br, tm):
    n = x2.shape[0]
    gcol = SEG_GATES * SEG // D_MODEL

    def rows(width, col=0):
        return pl.BlockSpec((tm, width), lambda i: (i, col))

    def cols():
        return pl.BlockSpec((BRANCH_WIDTH, tm), lambda i: (0, i))

    def full(shape):
        return pl.BlockSpec(shape, lambda i: (0,) * len(shape))

    return pl.pallas_call(
        _merge_kernel,
        grid=(n // tm,),
        in_specs=[
            cols(), cols(), cols(),
            rows(D_MODEL, gcol), rows(D_MODEL, gcol + 1), rows(D_MODEL, gcol + 2),
            rows(D_MODEL),
            full((N_BRANCH, BRANCH_WIDTH, D_MODEL)), full((D_MODEL, D_MODEL)),
            full((1, D_MODEL)), full((D_MODEL, LANES)), full((1, LANES)),
        ],
        out_specs=[rows(D_MODEL), rows(D_MODEL), rows(LANES)],
        out_shape=[
            jax.ShapeDtypeStruct((n, D_MODEL), F32),
            jax.ShapeDtypeStruct((n, D_MODEL), BF16),
            jax.ShapeDtypeStruct((n, LANES), F32),
        ],
        scratch_shapes=[pltpu.VMEM((tm, D_MODEL), BF16)],
        compiler_params=_cparams(("parallel",)),
        name="merge",
    )(ha, hbt, hct, p, p, p, x2, wb, wo, n2, wr, br)


def _combine_weights(lg):
    lanef = lax.broadcasted_iota(jnp.int32, lg.shape, 1).astype(F32)
    big = 1e9
    is_g = (lanef >= N_EXPERTS) & (lanef < N_EXPERTS + N_GROUPS)
    gl = jnp.where(is_g, lg, -jnp.inf)
    gmax = jnp.max(gl, axis=-1, keepdims=True)
    g_idx = jnp.min(jnp.where(gl == gmax, lanef - N_EXPERTS, big), axis=-1, keepdims=True)
    p_g = 1.0 / jnp.sum(jnp.exp(gl - gmax), axis=-1, keepdims=True)
    in_grp = (lanef >= g_idx * EXPERTS_PER_GROUP) & (lanef < (g_idx + 1.0) * EXPERTS_PER_GROUP)
    el = jnp.where(in_grp, lg, -jnp.inf)
    ee = jnp.exp(el - jnp.max(el, axis=-1, keepdims=True))
    ep = ee / jnp.sum(ee, axis=-1, keepdims=True)
    ep = jnp.where(in_grp, ep, -1.0)
    v1 = jnp.max(ep, axis=-1, keepdims=True)
    i1 = jnp.min(jnp.where(ep == v1, lanef, big), axis=-1, keepdims=True)
    ep2 = jnp.where(lanef == i1, -1.0, ep)
    v2 = jnp.max(ep2, axis=-1, keepdims=True)
    i2 = jnp.min(jnp.where(ep2 == v2, lanef, big), axis=-1, keepdims=True)
    tot = v1 + v2
    comb = jnp.where(lanef == i1, p_g * (v1 / tot), 0.0) + jnp.where(lanef == i2, p_g * (v2 / tot), 0.0)
    return comb, g_idx


def _moe_kernel(x1_ref, h2_ref, lg_ref, tri_ref, wgu_ref, wd_ref, out_ref,
                pt_ref, xs_ref, ws_ref, ys_ref, meta_ref, *, rcap):
    C = MOE_C
    s = pl.program_id(1)
    tm = h2_ref.shape[0]

    @pl.when(s == 0)
    def _():
        comb, g_idx = _combine_weights(lg_ref[...])
        lanef = lax.broadcasted_iota(jnp.int32, comb.shape, 1).astype(F32)
        mine = lanef == g_idx
        onehot = jnp.where(mine, 1.0, 0.0)
        ranks = jnp.dot(tri_ref[...], onehot.astype(BF16), preferred_element_type=F32)
        dest = jnp.sum(jnp.where(mine, ranks, 0.0), axis=-1, keepdims=True)
        off = jnp.int32(0)
        for g in range(N_GROUPS):
            cnt = jnp.sum(onehot[:, g:g + 1]).astype(jnp.int32)
            nchunk = (cnt + (C - 1)) // C
            meta_ref[g] = off
            meta_ref[N_GROUPS + g] = nchunk
            dest = dest + jnp.where(g_idx == float(g), (off * C).astype(F32), 0.0)
            off = off + nchunk
        slot = lax.broadcasted_iota(jnp.int32, (tm, rcap), 1).astype(F32)
        pt = jnp.where(dest == slot, 1.0, 0.0).astype(BF16)
        pt_ref[...] = pt
        comb_hi = comb.astype(BF16)
        comb_lo = (comb - comb_hi.astype(F32)).astype(BF16)
        packed = jnp.concatenate([h2_ref[...], comb_hi, comb_lo], axis=1)
        srt = lax.dot_general(pt, packed, (((0,), (0,)), ((), ())), preferred_element_type=F32)
        xs_ref[...] = srt[:, 0:D_MODEL].astype(BF16)
        ws_ref[...] = srt[:, D_MODEL:D_MODEL + LANES] + srt[:, D_MODEL + LANES:D_MODEL + 2 * LANES]
        ys_ref[...] = jnp.zeros_like(ys_ref)

    def do_chunk(r0, m):
        r0 = pl.multiple_of(r0, C)
        xc = xs_ref[pl.ds(r0, m), :]
        wsc = ws_ref[pl.ds(r0, m), :]
        lane = lax.broadcasted_iota(jnp.int32, (m, LANES), 1)
        y = None
        for k in range(MOE_EPS):
            gu = jnp.dot(xc, wgu_ref[k], preferred_element_type=F32)
            gate = gu[:, 0:D_EXPERT]
            wk = jnp.sum(jnp.where(lane == s * MOE_EPS + k, wsc, 0.0), axis=-1, keepdims=True)
            he = (gate * jax.nn.sigmoid(gate) * gu[:, D_EXPERT:2 * D_EXPERT] * wk).astype(BF16)
            yk = jnp.dot(he, wd_ref[k], preferred_element_type=F32)
            y = yk if y is None else y + yk
        ys_ref[pl.ds(r0, m), :] = y.astype(BF16)

    g = s // (EXPERTS_PER_GROUP // MOE_EPS)
    start = meta_ref[g]
    nchunk = meta_ref[N_GROUPS + g]

    def big(i, carry):
        do_chunk((start + MOE_BIG * i) * C, MOE_BIG * C)
        return carry

    common = nchunk == MOE_BIG + 1

    @pl.when(common)
    def _():
        do_chunk(start * C, (MOE_BIG + 1) * C)

    @pl.when(jnp.logical_not(common))
    def _():
        nbig = nchunk // MOE_BIG
        lax.fori_loop(0, nbig, big, 0)
        done = nbig * MOE_BIG
        size = MOE_BIG // 2
        while size >= 1:
            @pl.when((nchunk & size) != 0)
            def _(done=done, size=size):
                do_chunk((start + done) * C, size * C)
            done = done + (nchunk & size)
            size //= 2

    @pl.when(s == pl.num_programs(1) - 1)
    def _():
        out_ref[...] = x1_ref[...] + jnp.dot(pt_ref[...], ys_ref[...], preferred_element_type=F32)


def _moe(x1, h2, lg, wgu, wd, tm):
    n = x1.shape[0]
    rcap = (tm + N_GROUPS * (MOE_C - 1)) // MOE_C * MOE_C
    rcap = -(-rcap // LANES) * LANES
    idx = np.arange(tm)
    tri = jnp.asarray(idx[None, :] < idx[:, None], BF16)
    return pl.pallas_call(
        functools.partial(_moe_kernel, rcap=rcap),
        grid=(n // tm, N_EXPERTS // MOE_EPS),
        in_specs=[
            pl.BlockSpec((tm, D_MODEL), lambda i, s: (i, 0), pipeline_mode=pl.Buffered(1)),
            pl.BlockSpec((tm, D_MODEL), lambda i, s: (i, 0), pipeline_mode=pl.Buffered(1)),
            pl.BlockSpec((tm, LANES), lambda i, s: (i, 0), pipeline_mode=pl.Buffered(1)),
            pl.BlockSpec((tm, tm), lambda i, s: (0, 0), pipeline_mode=pl.Buffered(1)),
            pl.BlockSpec((MOE_EPS, D_MODEL, 2 * D_EXPERT), lambda i, s: (s, 0, 0)),
            pl.BlockSpec((MOE_EPS, D_EXPERT, D_MODEL), lambda i, s: (s, 0, 0)),
        ],
        out_specs=pl.BlockSpec((tm, D_MODEL), lambda i, s: (i, 0), pipeline_mode=pl.Buffered(1)),
        out_shape=jax.ShapeDtypeStruct((n, D_MODEL), F32),
        scratch_shapes=[
            pltpu.VMEM((tm, rcap), BF16),
            pltpu.VMEM((rcap, D_MODEL), BF16),
            pltpu.VMEM((rcap, LANES), F32),
            pltpu.VMEM((rcap, D_MODEL), BF16),
            pltpu.SMEM((2 * N_GROUPS,), jnp.int32),
        ],
        compiler_params=pltpu.CompilerParams(dimension_semantics=("parallel", "arbitrary"),
                                             vmem_limit_bytes=MOE_VMEM_LIMIT),
        name="moe",
    )(x1, h2, lg, tri, wgu, wd)


def _tile(n, pref):
    t = pref
    while n % t:
        t //= 2
    return t


def _mixer_params(layer, norm1_g, w_in, a_conv_w, a_conv_b, a_gate_bias, a_out_norm_g,
                  b_qk_norm_g, b_rel_bias, c_qk_norm_g, c_lambda, c_sub_norm_g, t5_bias,
                  w_branch, w_out, nq_diff):
    n_small = 2 * A_HEADS
    cut = 4 * A_WIDTH
    w_main = jnp.concatenate([w_in[:, :cut], w_in[:, cut + n_small:]], axis=1)
    w_main = w_main.reshape(D_MODEL, N_SEG, SEG)[:, np.asarray(SEG_PERM), :]
    w_main = w_main.reshape(D_MODEL, N_SEG * SEG).astype(BF16)
    w_if = jnp.pad(w_in[:, cut:cut + n_small], ((0, 0), (0, LANES - n_small))).astype(BF16)
    gain = jnp.ones((N_SEG, SEG), F32)
    gain = gain.at[N_ROW_SEG + TSEG_BQ].set(jnp.tile(b_qk_norm_g[0], B_HEADS) * (B_DH ** -0.5 * LOG2E))
    gain = gain.at[SEG_BK].set(jnp.tile(b_qk_norm_g[1], B_HEADS))
    gain = gain.at[N_ROW_SEG + TSEG_CQ].set(
        jnp.tile(c_qk_norm_g[0], 2 * C_HEADS) * (C_DQK ** -0.5 * LOG2E))
    gain = gain.at[SEG_CK].set(jnp.tile(c_qk_norm_g[1], 2 * C_HEADS))
    diff_bias, diff_far = _diff_bias(t5_bias, _diff_nnear(nq_diff))
    lam_init = 0.8 - 0.6 * math.exp(-0.3 * layer)
    lf32 = c_lambda.astype(F32)
    lam = jnp.exp(jnp.sum(lf32[0] * lf32[1])) - jnp.exp(jnp.sum(lf32[2] * lf32[3])) + lam_init
    return dict(
        g1=norm1_g.reshape(1, D_MODEL), w_main=w_main, w_if=w_if,
        gain=gain.reshape(N_SEG, 1, SEG),
        cw=a_conv_w, cb=a_conv_b.reshape(1, -1),
        gbr=jnp.pad(a_gate_bias, (0, LANES - n_small)).reshape(1, LANES),
        gbc=a_gate_bias.reshape(n_small, 1),
        ag=a_out_norm_g.reshape(1, A_WIDTH),
        band_bias=_band_bias(b_rel_bias),
        diff_bias=diff_bias, diff_far=diff_far,
        lam=lam.reshape(1).astype(F32),
        gsub=(c_sub_norm_g * (1.0 - lam_init)).reshape(C_DV, 1),
        wb=w_branch.astype(BF16), wo=w_out.astype(BF16),
    )


def _layer(x2, bsz, seq, mp, norm2_g, w_group, b_group, w_router, b_router, w_e_gate, w_e_up, w_e_down):
    n = bsz * seq
    p, pt, gif = _inproj(x2, mp["g1"], mp["w_main"], mp["w_if"], mp["gain"], _tile(n, 512))
    gift = jnp.transpose(gif[:, :2 * A_HEADS])
    ha = _mlstm(p, pt, gif, gift, mp["cw"], mp["cb"], mp["gbr"], mp["gbc"], mp["ag"], bsz, seq)
    hbt = _band(p, pt, mp["band_bias"], bsz, seq)
    hct = _diff(p, pt, mp["diff_bias"], mp["diff_far"], mp["lam"], mp["gsub"], bsz, seq)

    wr = jnp.concatenate([w_router, w_group], axis=1)
    wr = jnp.pad(wr, ((0, 0), (0, LANES - wr.shape[1]))).astype(BF16)
    br = jnp.pad(jnp.concatenate([b_router, b_group]), (0, LANES - N_EXPERTS - N_GROUPS)).reshape(1, LANES)
    x1, h2, lg = _merge(ha, hbt, hct, p, x2, mp["wb"], mp["wo"], norm2_g.reshape(1, D_MODEL), wr, br,
                        _tile(n, 512))
    wgu = jnp.concatenate([w_e_gate, w_e_up], axis=-1).astype(BF16)
    return _moe(x1, h2, lg, wgu, w_e_down.astype(BF16), _tile(n, MOE_TM))


def kernel(x, norm1_g, w_in, a_conv_w, a_conv_b, a_gate_bias, a_out_norm_g, b_qk_norm_g, b_rel_bias,
           c_qk_norm_g, c_lambda, c_sub_norm_g, t5_bias, w_branch, w_out, norm2_g, w_group, b_group,
           w_router, b_router, w_e_gate, w_e_up, w_e_down):
    bsz, seq, _ = x.shape
    assert seq % DIFF_T == 0 and seq % MLSTM_L == 0 and seq % BAND_TQ == 0
    x2 = x.reshape(bsz * seq, D_MODEL)
    for l in range(norm1_g.shape[0]):
        mp = _mixer_params(l, norm1_g[l], w_in[l], a_conv_w[l], a_conv_b[l], a_gate_bias[l],
                           a_out_norm_g[l], b_qk_norm_g[l], b_rel_bias[l], c_qk_norm_g[l], c_lambda[l],
                           c_sub_norm_g[l], t5_bias, w_branch[l], w_out[l], seq // DIFF_T)
        x2 = _layer(x2, bsz, seq, mp, norm2_g[l], w_group[l], b_group[l], w_router[l], b_router[l],
                    w_e_gate[l], w_e_up[l], w_e_down[l])
    return x2.reshape(bsz, seq, D_MODEL)
```

```python
import functools
import math

import numpy as np
import jax
import jax.numpy as jnp
from jax import lax
from jax.experimental import pallas as pl
from jax.experimental.pallas import tpu as pltpu

F32 = jnp.float32
BF16 = jnp.bfloat16

D_MODEL = 1024
CHUNK = 64
EPS = 1e-6
NEG = -1e30
LOG2E = math.log2(math.e)

A_HEADS = 4
A_DH = 128
A_WIDTH = A_HEADS * A_DH
CONV_W = 4
GATE_CAP = 15.0

B_HEADS = 8
B_DH = 64
B_WIDTH = B_HEADS * B_DH
B_LEFT_CHUNKS = 8
B_MAX_REL = 256

C_HEADS = 4
C_DQK = 64
C_DV = 2 * C_DQK
C_WIDTH = C_HEADS * C_DV

T5_BUCKETS = 32
T5_MAX_DIST = 1024

N_BRANCH = 3
BRANCH_WIDTH = 512

N_GROUPS = 4
EXPERTS_PER_GROUP = 8
N_EXPERTS = N_GROUPS * EXPERTS_PER_GROUP
D_EXPERT = D_MODEL // 4

LANES = 128
SEG = 512
N_SEG = 16
VMEM_LIMIT = 48 * 1024 * 1024

SEG_GATES, SEG_AQ, SEG_AK, SEG_BK, SEG_CK = 0, 6, 7, 8, 9
N_ROW_SEG = 10
TSEG_AV, TSEG_AO, TSEG_BQ, TSEG_BV, TSEG_CQ, TSEG_CV = 0, 1, 2, 3, 4, 5
N_T_SEG = N_SEG - N_ROW_SEG
SEG_PERM = (10, 11, 12, 13, 14, 15, 0, 1, 5, 8, 2, 3, 4, 6, 7, 9)

MLSTM_L = 128
BAND_TQ = 128
BAND_NKB = 1 + (B_LEFT_CHUNKS * CHUNK) // BAND_TQ
BAND_ONES = 16
DIFF_T = 512
DIFF_ONES = 16
MOE_TM = 1024
MOE_C = 64
MOE_BIG = 4
MOE_EPS = EXPERTS_PER_GROUP
MOE_VMEM_LIMIT = 56 * 1024 * 1024

def _cparams(sem, flags=None):
    return pltpu.CompilerParams(dimension_semantics=sem, vmem_limit_bytes=VMEM_LIMIT, flags=flags)


NORM_SEGS = (SEG_BK, SEG_CK, N_ROW_SEG + TSEG_BQ, N_ROW_SEG + TSEG_CQ)
SIGMOID_SEGS = (N_ROW_SEG + TSEG_AO,) + tuple(range(SEG_GATES, SEG_GATES + 6))


def _head_norm_t(acc_t):
    rows, tm = acc_t.shape
    a3 = acc_t.reshape(rows // 64, 64, tm)
    ssq = jnp.sum(a3 * a3, axis=1, keepdims=True)
    return (a3 * lax.rsqrt(ssq * (1.0 / 64.0) + EPS)).reshape(rows, tm)


def _inproj_kernel(x_ref, g_ref, w_ref, wif_ref, gain_ref, gain_t_ref, p_ref, pt_ref, gif_ref):
    xf = x_ref[...]
    xn = (xf * lax.rsqrt(jnp.mean(xf * xf, axis=-1, keepdims=True) + EPS) * g_ref[...]).astype(BF16)
    gif_ref[...] = jnp.dot(xn, wif_ref[...], preferred_element_type=F32)
    for j in range(N_SEG):
        cols = slice(j * SEG, (j + 1) * SEG)
        acc = jnp.dot(xn, w_ref[:, cols], preferred_element_type=F32)
        if j in NORM_SEGS:
            acc_t = _head_norm_t(jnp.transpose(acc))
            if j < N_ROW_SEG:
                p_ref[:, cols] = (jnp.transpose(acc_t) * gain_ref[j]).astype(BF16)
            else:
                pt_ref[j - N_ROW_SEG] = (acc_t * gain_t_ref[NORM_SEGS.index(j) - 2]).astype(BF16)
            continue
        if j in SIGMOID_SEGS:
            acc = jax.nn.sigmoid(acc)
        if j < N_ROW_SEG:
            p_ref[:, cols] = acc.astype(BF16)
        else:
            pt_ref[j - N_ROW_SEG] = jnp.transpose(acc).astype(BF16)


def _inproj(x2, g, w, wif, gain, tm):
    n = x2.shape[0]
    tsegs = np.asarray([N_ROW_SEG + TSEG_BQ, N_ROW_SEG + TSEG_CQ])
    gain_t = jnp.broadcast_to(gain[tsegs, 0, :, None], (2, SEG, tm))

    def const(shape):
        return pl.BlockSpec(shape, lambda i: (0,) * len(shape), pipeline_mode=pl.Buffered(1))

    return pl.pallas_call(
        _inproj_kernel,
        grid=(n // tm,),
        in_specs=[
            pl.BlockSpec((tm, D_MODEL), lambda i: (i, 0)),
            const((1, D_MODEL)),
            const((D_MODEL, N_SEG * SEG)),
            const((D_MODEL, LANES)),
            const((N_SEG, 1, SEG)),
            const((2, SEG, tm)),
        ],
        out_specs=[
            pl.BlockSpec((tm, N_ROW_SEG * SEG), lambda i: (i, 0)),
            pl.BlockSpec((N_T_SEG, SEG, tm), lambda i: (0, 0, i)),
            pl.BlockSpec((tm, LANES), lambda i: (i, 0)),
        ],
        out_shape=[
            jax.ShapeDtypeStruct((n, N_ROW_SEG * SEG), BF16),
            jax.ShapeDtypeStruct((N_T_SEG, SEG, n), BF16),
            jax.ShapeDtypeStruct((n, LANES), F32),
        ],
        compiler_params=_cparams(("parallel",)),
        name="inproj",
    )(x2, g, w, wif, gain, gain_t)


def _log_sigmoid(z):
    return jnp.minimum(z, 0.0) - jnp.log(1.0 + jnp.exp(-jnp.abs(z)))


def _split3(a):
    hi = a.astype(BF16)
    r1 = a - hi.astype(F32)
    mid = r1.astype(BF16)
    lo = (r1 - mid.astype(F32)).astype(BF16)
    return hi, mid, lo


def _mlstm_kernel(*refs):
    _mlstm_carry(*refs)
    for _ in _mlstm_chunk(*refs):
        pass


def _mlstm_carry(aq_ref, ak_ref, vt_ref, aot_ref, gif_ref, gift_ref, cw_ref, cb_ref,
                 gbr_ref, gbc_ref, agt_ref, out_ref, ubuf, kq_ref, st_ref, ct_ref, n_ref, m_ref):
    L = MLSTM_L
    c = pl.program_id(1)

    @pl.when(c == 0)
    def _():
        ubuf[0:8, :] = jnp.zeros((8, 2 * A_WIDTH), F32)
        ct_ref[...] = jnp.zeros_like(ct_ref)
        n_ref[...] = jnp.zeros_like(n_ref)
        m_ref[...] = jnp.zeros_like(m_ref)

    @pl.when(c > 0)
    def _():
        ubuf[0:8, :] = ubuf[L:L + 8, :]


def _mlstm_chunk(aq_ref, ak_ref, vt_ref, aot_ref, gif_ref, gift_ref, cw_ref, cb_ref,
                 gbr_ref, gbc_ref, agt_ref, out_ref, ubuf, kq_ref, st_ref, ct_ref, n_ref, m_ref):
    L = MLSTM_L

    ubuf[8:L + 8, 0:A_WIDTH] = aq_ref[...].astype(F32)
    ubuf[8:L + 8, A_WIDTH:2 * A_WIDTH] = ak_ref[...].astype(F32)
    y = cb_ref[...] + cw_ref[0:1, :] * ubuf[8:L + 8, :]
    for t in range(1, CONV_W):
        y = y + cw_ref[t:t + 1, :] * ubuf[8 - t:8 - t + L, :]
    qk = y * jax.nn.sigmoid(y)
    q_t = jnp.transpose(qk[:, 0:A_WIDTH]).astype(BF16)
    k_all = (qk[:, A_WIDTH:2 * A_WIDTH] * (A_DH ** -0.5)).astype(BF16)

    zc = gif_ref[...] + gbr_ref[...]
    ig_c = GATE_CAP * jnp.tanh(zc * (1.0 / GATE_CAP))
    lf_c = _log_sigmoid(zc)
    zr = gift_ref[...] + gbc_ref[...]
    ig_r = GATE_CAP * jnp.tanh(zr * (1.0 / GATE_CAP))
    lf_r = _log_sigmoid(zr)

    row = lax.broadcasted_iota(jnp.int32, (L, L), 0)
    col = lax.broadcasted_iota(jnp.int32, (L, L), 1)
    causal = col <= row
    tril = jnp.where(causal, 1.0, 0.0).astype(BF16)
    triu = jnp.where(row <= col, 1.0, 0.0).astype(BF16)
    b_c = sum(jnp.dot(tril, piece, preferred_element_type=F32) for piece in _split3(lf_c))
    b_r = sum(jnp.dot(piece, triu, preferred_element_type=F32) for piece in _split3(lf_r))

    sub8 = lax.broadcasted_iota(jnp.int32, (8, L), 0)
    heads = [slice(h * A_DH, (h + 1) * A_DH) for h in range(A_HEADS)]
    for h, rows in enumerate(heads):
        kq_ref[h] = jnp.dot(k_all[:, rows], q_t[rows, :], preferred_element_type=F32)
    yield

    stats = []
    for h, rows in enumerate(heads):
        bcol = b_c[:, A_HEADS + h:A_HEADS + h + 1]
        brow = b_r[A_HEADS + h:A_HEADS + h + 1, :]
        igcol = ig_c[:, h:h + 1]
        m_prev = m_ref[h][:, 0:1]
        dmat = jnp.where(row <= col, brow + (igcol - bcol), NEG)
        inter = brow + m_prev
        m_t = jnp.maximum(inter, jnp.max(dmat, axis=0, keepdims=True))
        st = kq_ref[h] * jnp.exp(dmat - m_t)
        st_ref[h] = st.astype(BF16)
        stats.append((brow, m_prev, m_t, jnp.exp(inter - m_t), jnp.sum(st, axis=0, keepdims=True)))
        if h % 2 == 1:
            yield

    for h, rows in enumerate(heads):
        brow, m_prev, m_t, w_inter, st_sum = stats[h]
        qt = q_t[rows, :]
        k = k_all[:, rows]
        vt = vt_ref[rows, :]
        igrow = ig_r[h:h + 1, :]
        b_last = brow[:, L - 1:L]
        ct = ct_ref[h]
        n8 = n_ref[h]

        num = w_inter * jnp.dot(ct.astype(BF16), qt, preferred_element_type=F32)
        num = num + jnp.dot(vt, st_ref[h], preferred_element_type=F32)
        nq = jnp.dot(n8.astype(BF16), qt, preferred_element_type=F32)[0:1, :]
        den = w_inter * nq + st_sum
        hh = num * (1.0 / jnp.maximum(jnp.abs(den), jnp.exp(-m_t)))

        g_end = b_last - brow + igrow
        m_new = jnp.maximum(b_last + m_prev, jnp.max(g_end, axis=-1, keepdims=True))
        dec = jnp.exp(b_last + m_prev - m_new)
        w_end = jnp.exp(g_end - m_new)
        vw = (vt.astype(F32) * w_end).astype(BF16)
        ct_ref[h] = dec * ct + jnp.dot(vw, k, preferred_element_type=F32)
        w8 = jnp.where(sub8 == 0, w_end, 0.0).astype(BF16)
        n_ref[h] = dec * n8 + jnp.dot(w8, k, preferred_element_type=F32)
        m_ref[h] = jnp.broadcast_to(m_new, (1, LANES))

        hn = hh * lax.rsqrt(jnp.mean(hh * hh, axis=0, keepdims=True) + EPS) * agt_ref[rows, :]
        out_ref[rows, :] = (hn * aot_ref[rows, :].astype(F32)).astype(BF16)
        if h % 2 == 1 and h + 1 < A_HEADS:
            yield


def _mlstm_call(p, pt, gif, gift, cw, cb, gbr, gbc, ag, bsz, seq):
    L = MLSTM_L
    nc = seq // L
    n = bsz * seq
    agt = jnp.broadcast_to(ag.reshape(A_WIDTH, 1), (A_WIDTH, L))

    def tseg(j):
        return pl.BlockSpec((None, SEG, L), lambda b, c: (j, 0, b * nc + c))

    def full(shape):
        return pl.BlockSpec(shape, lambda b, c: (0,) * len(shape))

    in_specs = [
        pl.BlockSpec((L, SEG), lambda b, c: (b * nc + c, SEG_AQ)),
        pl.BlockSpec((L, SEG), lambda b, c: (b * nc + c, SEG_AK)),
        tseg(TSEG_AV), tseg(TSEG_AO),
        pl.BlockSpec((L, LANES), lambda b, c: (b * nc + c, 0)),
        pl.BlockSpec((8, L), lambda b, c: (0, b * nc + c)),
        full((CONV_W, 2 * A_WIDTH)), full((1, 2 * A_WIDTH)),
        full((1, LANES)), full((8, 1)), full((A_WIDTH, L)),
    ]
    scratch = [
        pltpu.VMEM((L + 8, 2 * A_WIDTH), F32),
        pltpu.VMEM((A_HEADS, L, L), F32),
        pltpu.VMEM((A_HEADS, L, L), BF16),
        pltpu.VMEM((A_HEADS, A_DH, A_DH), F32),
        pltpu.VMEM((A_HEADS, 8, A_DH), F32),
        pltpu.VMEM((A_HEADS, 1, LANES), F32),
    ]
    return (in_specs, pl.BlockSpec((A_WIDTH, L), lambda b, c: (0, b * nc + c)),
            jax.ShapeDtypeStruct((A_WIDTH, n), BF16), scratch,
            (p, p, pt, pt, gif, gift, cw, cb, gbr, gbc, agt))


def _mlstm(p, pt, gif, gift, cw, cb, gbr, gbc, ag, bsz, seq):
    in_specs, out_spec, out_shape, scratch, operands = _mlstm_call(
        p, pt, gif, gift, cw, cb, gbr, gbc, ag, bsz, seq)
    return pl.pallas_call(
        _mlstm_kernel, grid=(bsz, seq // MLSTM_L), in_specs=in_specs, out_specs=out_spec,
        out_shape=out_shape, scratch_shapes=scratch,
        compiler_params=_cparams(("parallel", "arbitrary")), name="mlstm",
    )(*operands)


def _band_kernel(*refs, alongside=None):
    nkb = BAND_NKB
    qt_ref = refs[0]
    k_refs = refs[1:1 + nkb]
    vt_refs = refs[1 + nkb:1 + 2 * nkb]
    bias_ref = refs[1 + 2 * nkb]
    out_ref = refs[2 + 2 * nkb]
    s_ref, mx_ref = refs[3 + 2 * nkb:5 + 2 * nkb]
    tq = BAND_TQ
    nk = nkb * tq
    i = pl.program_id(1)

    def compute(mask_start):
        k_all = jnp.concatenate([r[...] for r in k_refs], axis=0)
        vt_all = jnp.concatenate([r[...] for r in vt_refs], axis=1)
        ones = jnp.ones((BAND_ONES, nk), BF16)
        row = lax.broadcasted_iota(jnp.int32, (LANES, tq), 0)
        lo = row < B_DH
        if mask_start:
            kidx = lax.broadcasted_iota(jnp.int32, (nk, 1), 0)
            valid = (kidx + (i - (nkb - 1)) * tq) >= 0

        def score(p):
            rows = slice(p * LANES, (p + 1) * LANES)
            qtp = qt_ref[rows, :]
            zero = jnp.zeros_like(qtp)
            qbd = jnp.concatenate([jnp.where(lo, qtp, zero), jnp.where(lo, zero, qtp)], axis=1)
            s = jnp.dot(k_all[:, rows], qbd, preferred_element_type=F32) + bias_ref[p]
            if mask_start:
                s = jnp.where(valid, s, NEG)
            s_ref[p & 1] = s
            mx_ref[p & 1] = jnp.max(s, axis=0, keepdims=True)

        def finish(p):
            rows = slice(p * LANES, (p + 1) * LANES)
            pr = jnp.exp2((s_ref[p & 1] - mx_ref[p & 1]).astype(BF16))
            o = jnp.dot(jnp.concatenate([vt_all[rows, :], ones], axis=0), pr,
                        preferred_element_type=F32)
            o = o[0:LANES, :] / o[LANES:LANES + 1, :]
            out_ref[rows, :] = jnp.where(lo, o[:, 0:tq], o[:, tq:2 * tq]).astype(BF16)

        other = alongside() if alongside is not None else iter(())
        score(0)
        next(other, None)
        for p in range(B_HEADS // 2):
            if p + 1 < B_HEADS // 2:
                score(p + 1)
            finish(p)
            next(other, None)
        for _ in other:
            pass

    @pl.when(i < nkb - 1)
    def _():
        compute(True)

    @pl.when(i >= nkb - 1)
    def _():
        compute(False)


def _band_call(p, pt, bias, bsz, seq):
    tq = BAND_TQ
    nkb = BAND_NKB
    nq = seq // tq
    n = bsz * seq

    def kblk(d):
        return pl.BlockSpec((tq, SEG), lambda b, i: (b * nq + jnp.maximum(i - d, 0), SEG_BK))

    def vblk(d):
        return pl.BlockSpec((None, SEG, tq), lambda b, i: (TSEG_BV, 0, b * nq + jnp.maximum(i - d, 0)))

    in_specs = [pl.BlockSpec((None, SEG, tq), lambda b, i: (TSEG_BQ, 0, b * nq + i))]
    in_specs += [kblk(d) for d in range(nkb - 1, -1, -1)]
    in_specs += [vblk(d) for d in range(nkb - 1, -1, -1)]
    in_specs += [pl.BlockSpec(bias.shape, lambda b, i: (0, 0, 0))]
    scratch = [
        pltpu.VMEM((2, nkb * tq, 2 * tq), F32),
        pltpu.VMEM((2, 1, 2 * tq), F32),
    ]
    return (in_specs, pl.BlockSpec((B_WIDTH, tq), lambda b, i: (0, b * nq + i)),
            jax.ShapeDtypeStruct((B_WIDTH, n), BF16), scratch,
            (pt, *([p] * nkb), *([pt] * nkb), bias))


def _band(p, pt, bias, bsz, seq):
    in_specs, out_spec, out_shape, scratch, operands = _band_call(p, pt, bias, bsz, seq)
    return pl.pallas_call(
        _band_kernel, grid=(bsz, seq // BAND_TQ), in_specs=in_specs, out_specs=out_spec,
        out_shape=out_shape, scratch_shapes=scratch,
        compiler_params=_cparams(("parallel", "parallel")), name="band_attn",
    )(*operands)


def _mlstm_band_kernel(*refs, n_mlstm_in, n_band_in):
    m_in = refs[:n_mlstm_in]
    b_in = refs[n_mlstm_in:n_mlstm_in + n_band_in]
    hat_ref, hbt_ref = refs[n_mlstm_in + n_band_in:n_mlstm_in + n_band_in + 2]
    scr = refs[n_mlstm_in + n_band_in + 2:]
    m_scr, b_scr = scr[:-2], scr[-2:]
    _mlstm_carry(*m_in, hat_ref, *m_scr)
    _band_kernel(*b_in, hbt_ref, *b_scr, alongside=lambda: _mlstm_chunk(*m_in, hat_ref, *m_scr))


def _mlstm_band(p, pt, gif, gift, cw, cb, gbr, gbc, ag, band_bias, bsz, seq):
    assert MLSTM_L == BAND_TQ
    m_specs, m_out, m_shape, m_scr, m_ops = _mlstm_call(p, pt, gif, gift, cw, cb, gbr, gbc, ag, bsz, seq)
    b_specs, b_out, b_shape, b_scr, b_ops = _band_call(p, pt, band_bias, bsz, seq)
    return pl.pallas_call(
        functools.partial(_mlstm_band_kernel, n_mlstm_in=len(m_specs), n_band_in=len(b_specs)),
        grid=(bsz, seq // MLSTM_L),
        in_specs=m_specs + b_specs,
        out_specs=[m_out, b_out],
        out_shape=[m_shape, b_shape],
        scratch_shapes=m_scr + b_scr,
        compiler_params=_cparams(("parallel", "arbitrary")),
        name="mlstm_band",
    )(*m_ops, *b_ops)


def _toeplitz(base, m, n):
    period = base.shape[-1]
    assert n <= period - 1
    reps = (1,) * (base.ndim - 1) + (m,)
    big = jnp.tile(base, reps)[..., :m * (period - 1)]
    return big.reshape(base.shape[:-1] + (m, period - 1))[..., :n]


def _band_bias(b_rel):
    tq = BAND_TQ
    nk = BAND_NKB * tq
    period = tq + nk
    e = np.arange(period)
    e = np.where(e < nk, e, e - period)
    rel = np.clip((nk - tq) - e, -B_MAX_REL, B_MAX_REL) + B_MAX_REL
    bias = _toeplitz(jnp.transpose(b_rel[rel]).astype(F32), tq, nk)
    qpos = np.arange(tq)
    kpos = np.arange(nk) - (nk - tq)
    qc = qpos[:, None] // CHUNK
    kc = np.floor_divide(kpos[None, :], CHUNK)
    allowed = (kc <= qc) & (kc >= qc - B_LEFT_CHUNKS)
    bias = jnp.where(allowed[None], bias * LOG2E, NEG)
    return jnp.swapaxes(bias.reshape(B_HEADS // 2, 2 * tq, nk), 1, 2)


def _diff_kernel(lam_ref, cfar_ref, qt_ref, k_ref, vt_ref, bias_ref, g_ref, out_ref,
                 qbd_ref, s_ref, mx_ref, acc_ref, m_ref, *, nnear):
    T = DIFF_T
    h = pl.program_id(1)
    qi = pl.program_id(2)
    cfar = cfar_ref[h]
    qt = qt_ref[...]
    row = lax.broadcasted_iota(jnp.int32, (2 * C_DQK, T), 0)
    zero = jnp.zeros_like(qt)
    qbd_ref[:, 0:T] = jnp.where(row < C_DQK, qt, zero)
    qbd_ref[:, T:2 * T] = jnp.where(row < C_DQK, zero, qt)
    m_ref[...] = jnp.full(m_ref.shape, NEG, F32)
    acc_ref[...] = jnp.zeros_like(acc_ref)

    def near_bias(t):
        return bias_ref[t] if t < nnear else None

    def stage_a(t, slot, bias, maps=(0, 1)):
        j = jnp.maximum(qi - t, 0)
        k = k_ref[pl.ds(pl.multiple_of(j * T, T), T), :]
        for mp in maps:
            sm = jnp.dot(k, qbd_ref[:, mp * T:(mp + 1) * T], preferred_element_type=F32)
            if bias is not None:
                sm = sm + bias
            s_ref[slot, mp] = sm
            mx = jnp.max(sm, axis=0, keepdims=True)
            mx_ref[slot, mp] = mx + cfar if bias is None else mx

    def stage_bc(t, slot, far, maps=(0, 1)):
        j = qi - t
        vt = jnp.concatenate([vt_ref[:, pl.ds(pl.multiple_of(j * T, T), T)],
                              jnp.ones((DIFF_ONES, T), BF16)], axis=0)
        for mp in maps:
            m_old = m_ref[mp]
            m_new = jnp.maximum(m_old, mx_ref[slot, mp])
            shift = m_new - cfar if far else m_new
            pr = jnp.exp2((s_ref[slot, mp] - shift).astype(BF16))
            m_ref[mp] = m_new
            acc_ref[mp] = jnp.exp2(m_old - m_new) * acc_ref[mp] + jnp.dot(
                vt, pr, preferred_element_type=F32)

    def full_step(t, slot, far, next_bias):
        for mp in range(2):
            stage_a(t + 1, 1 - slot, next_bias, (mp,))
            stage_bc(t, slot, far, (mp,))

    stage_a(0, 0, near_bias(0))
    for t in range(nnear):
        @pl.when(qi >= t)
        def _(t=t):
            full_step(t, t & 1, False, near_bias(t + 1))

    @pl.when(qi >= nnear)
    def _():
        n_full = qi - nnear
        s0 = nnear & 1

        def pair(i, carry):
            full_step(nnear + 2 * i, s0, True, None)
            full_step(nnear + 2 * i + 1, 1 - s0, True, None)
            return carry

        lax.fori_loop(0, n_full // 2, pair, 0)

        @pl.when(n_full % 2 == 1)
        def _():
            full_step(qi - 1, s0, True, None)

        stage_bc(qi, qi & 1, True)

    o1 = acc_ref[0, 0:C_DV, :] / acc_ref[0, C_DV:C_DV + 1, :]
    o2 = acc_ref[1, 0:C_DV, :] / acc_ref[1, C_DV:C_DV + 1, :]
    o = o1 - lam_ref[0] * o2
    on = o * lax.rsqrt(jnp.mean(o * o, axis=0, keepdims=True) + EPS) * g_ref[...]
    out_ref[...] = on.astype(BF16)


def _diff_nnear(nq):
    d_sat = -(-(T5_MAX_DIST - 1 + DIFF_T) // DIFF_T)
    return min(d_sat, nq)


def _diff(p, pt, bias, cfar, lam, gsub, bsz, seq):
    T = DIFF_T
    nq = seq // T
    nnear = bias.shape[1]
    kcol = SEG_CK * SEG // LANES
    n = bsz * seq
    return pl.pallas_call(
        functools.partial(_diff_kernel, nnear=nnear),
        grid=(bsz, C_HEADS, nq),
        in_specs=[
            pl.BlockSpec(memory_space=pltpu.SMEM),
            pl.BlockSpec(memory_space=pltpu.SMEM),
            pl.BlockSpec((None, 2 * C_DQK, T), lambda b, h, i: (TSEG_CQ, h, b * nq + i)),
            pl.BlockSpec((seq, LANES), lambda b, h, i: (b, kcol + h)),
            pl.BlockSpec((None, C_DV, seq), lambda b, h, i: (TSEG_CV, h, b)),
            pl.BlockSpec((None, nnear, T, T), lambda b, h, i: (h, 0, 0, 0)),
            pl.BlockSpec((C_DV, 1), lambda b, h, i: (0, 0)),
        ],
        out_specs=pl.BlockSpec((C_DV, T), lambda b, h, i: (h, b * nq + i)),
        out_shape=jax.ShapeDtypeStruct((C_WIDTH, n), BF16),
        scratch_shapes=[
            pltpu.VMEM((2 * C_DQK, 2 * T), BF16),
            pltpu.VMEM((2, 2, T, T), F32),
            pltpu.VMEM((2, 2, 1, T), F32),
            pltpu.VMEM((2, C_DV + DIFF_ONES, T), F32),
            pltpu.VMEM((2, 1, T), F32),
        ],
        compiler_params=_cparams(("parallel", "parallel", "arbitrary")),
        name="diff_attn",
    )(lam, cfar, pt, p, pt, bias, gsub)


def _t5_bucket(rel):
    nb = T5_BUCKETS // 2
    max_exact = nb // 2
    ret = (rel > 0).astype(jnp.int32) * nb
    n = jnp.abs(rel)
    large = max_exact + (jnp.log(jnp.maximum(n, max_exact).astype(F32) / max_exact)
                         / math.log(T5_MAX_DIST / max_exact) * (nb - max_exact)).astype(jnp.int32)
    large = jnp.minimum(large, nb - 1)
    return ret + jnp.where(n < max_exact, n, large)


def _diff_bias(t5_table, nnear):
    T = DIFF_T
    e = np.arange(2 * T)
    amc = np.where(e < T, -e, 2 * T - e)
    rel = jnp.asarray(-np.arange(nnear)[:, None] * T + amc[None, :], jnp.int32)
    base = jnp.moveaxis(t5_table[_t5_bucket(rel)], -1, 0).astype(F32) * LOG2E
    tiles = _toeplitz(base, T, T)
    a = np.arange(T)[:, None]
    c = np.arange(T)[None, :]
    allowed = np.ones((nnear, T, T), bool)
    allowed[0] = (a // CHUNK) <= (c // CHUNK)
    far = t5_table[_t5_bucket(jnp.asarray(-T5_MAX_DIST, jnp.int32))].astype(F32) * LOG2E
    return jnp.where(allowed[None], tiles, NEG), far


def _merge_kernel(hat_ref, hbt_ref, hct_ref, g0_ref, g1_ref, g2_ref, x_ref, wb_ref, wo_ref,
                  n2_ref, wr_ref, br_ref, x1_ref, h2_ref, lg_ref, y_ref):
    tn = (((0,), (0,)), ((), ()))
    tm = x_ref.shape[0]
    halves = [slice(i * (tm // 2), (i + 1) * (tm // 2)) for i in range(2)]
    for r in halves:
        y = g0_ref[r, :].astype(F32) * lax.dot_general(hat_ref[:, r], wb_ref[0], tn,
                                                       preferred_element_type=F32)
        y = y + g1_ref[r, :].astype(F32) * lax.dot_general(hbt_ref[:, r], wb_ref[1], tn,
                                                           preferred_element_type=F32)
        y = y + g2_ref[r, :].astype(F32) * lax.dot_general(hct_ref[:, r], wb_ref[2], tn,
                                                           preferred_element_type=F32)
        y_ref[r, :] = y.astype(BF16)
    for r in halves:
        x1 = x_ref[r, :] + jnp.dot(y_ref[r, :], wo_ref[...], preferred_element_type=F32)
        x1_ref[r, :] = x1
        h2 = x1 * lax.rsqrt(jnp.mean(x1 * x1, axis=-1, keepdims=True) + EPS) * n2_ref[...]
        h2_ref[r, :] = h2.astype(BF16)
    for r in halves:
        lg_ref[r, :] = jnp.dot(h2_ref[r, :], wr_ref[...], preferred_element_type=F32) + br_ref[...]


def _merge(ha, hbt, hct, p, x2, wb, wo, n2, wr, br, tm):
    n = x2.shape[0]
    gcol = SEG_GATES * SEG // D_MODEL

    def rows(width, col=0):
        return pl.BlockSpec((tm, width), lambda i: (i, col))

    def cols():
        return pl.BlockSpec((BRANCH_WIDTH, tm), lambda i: (0, i))

    def full(shape):
        return pl.BlockSpec(shape, lambda i: (0,) * len(shape))

    return pl.pallas_call(
        _merge_kernel,
        grid=(n // tm,),
        in_specs=[
            cols(), cols(), cols(),
            rows(D_MODEL, gcol), rows(D_MODEL, gcol + 1), rows(D_MODEL, gcol + 2),
            rows(D_MODEL),
            full((N_BRANCH, BRANCH_WIDTH, D_MODEL)), full((D_MODEL, D_MODEL)),
            full((1, D_MODEL)), full((D_MODEL, LANES)), full((1, LANES)),
        ],
        out_specs=[rows(D_MODEL), rows(D_MODEL), rows(LANES)],
        out_shape=[
            jax.ShapeDtypeStruct((n, D_MODEL), F32),
            jax.ShapeDtypeStruct((n, D_MODEL), BF16),
            jax.ShapeDtypeStruct((n, LANES), F32),
        ],
        scratch_shapes=[pltpu.VMEM((tm, D_MODEL), BF16)],
        compiler_params=_cparams(("parallel",)),
        name="merge",
    )(ha, hbt, hct, p, p, p, x2, wb, wo, n2, wr, br)


def _combine_weights(lg):
    lanef = lax.broadcasted_iota(jnp.int32, lg.shape, 1).astype(F32)
    big = 1e9
    is_g = (lanef >= N_EXPERTS) & (lanef < N_EXPERTS + N_GROUPS)
    gl = jnp.where(is_g, lg, -jnp.inf)
    gmax = jnp.max(gl, axis=-1, keepdims=True)
    g_idx = jnp.min(jnp.where(gl == gmax, lanef - N_EXPERTS, big), axis=-1, keepdims=True)
    p_g = 1.0 / jnp.sum(jnp.exp(gl - gmax), axis=-1, keepdims=True)
    in_grp = (lanef >= g_idx * EXPERTS_PER_GROUP) & (lanef < (g_idx + 1.0) * EXPERTS_PER_GROUP)
    el = jnp.where(in_grp, lg, -jnp.inf)
    ee = jnp.exp(el - jnp.max(el, axis=-1, keepdims=True))
    ep = ee / jnp.sum(ee, axis=-1, keepdims=True)
    ep = jnp.where(in_grp, ep, -1.0)
    v1 = jnp.max(ep, axis=-1, keepdims=True)
    i1 = jnp.min(jnp.where(ep == v1, lanef, big), axis=-1, keepdims=True)
    ep2 = jnp.where(lanef == i1, -1.0, ep)
    v2 = jnp.max(ep2, axis=-1, keepdims=True)
    i2 = jnp.min(jnp.where(ep2 == v2, lanef, big), axis=-1, keepdims=True)
    tot = v1 + v2
    comb = jnp.where(lanef == i1, p_g * (v1 / tot), 0.0) + jnp.where(lanef == i2, p_g * (v2 / tot), 0.0)
    return comb, g_idx


def _moe_kernel(x1_ref, h2_ref, lg_ref, tri_ref, wgu_ref, wd_ref, out_ref,
                pt_ref, xs_ref, ws_ref, ys_ref, meta_ref, *, rcap):
    C = MOE_C
    s = pl.program_id(1)
    tm = h2_ref.shape[0]

    @pl.when(s == 0)
    def _():
        comb, g_idx = _combine_weights(lg_ref[...])
        lanef = lax.broadcasted_iota(jnp.int32, comb.shape, 1).astype(F32)
        mine = lanef == g_idx
        onehot = jnp.where(mine, 1.0, 0.0)
        ranks = jnp.dot(tri_ref[...], onehot.astype(BF16), preferred_element_type=F32)
        dest = jnp.sum(jnp.where(mine, ranks, 0.0), axis=-1, keepdims=True)
        off = jnp.int32(0)
        for g in range(N_GROUPS):
            cnt = jnp.sum(onehot[:, g:g + 1]).astype(jnp.int32)
            nchunk = (cnt + (C - 1)) // C
            meta_ref[g] = off
            meta_ref[N_GROUPS + g] = nchunk
            dest = dest + jnp.where(g_idx == float(g), (off * C).astype(F32), 0.0)
            off = off + nchunk
        slot = lax.broadcasted_iota(jnp.int32, (tm, rcap), 1).astype(F32)
        pt = jnp.where(dest == slot, 1.0, 0.0).astype(BF16)
        pt_ref[...] = pt
        comb_hi = comb.astype(BF16)
        comb_lo = (comb - comb_hi.astype(F32)).astype(BF16)
        packed = jnp.concatenate([h2_ref[...], comb_hi, comb_lo], axis=1)
        srt = lax.dot_general(pt, packed, (((0,), (0,)), ((), ())), preferred_element_type=F32)
        xs_ref[...] = srt[:, 0:D_MODEL].astype(BF16)
        ws_ref[...] = srt[:, D_MODEL:D_MODEL + LANES] + srt[:, D_MODEL + LANES:D_MODEL + 2 * LANES]
        ys_ref[...] = jnp.zeros_like(ys_ref)

    def do_chunk(r0, m):
        r0 = pl.multiple_of(r0, C)
        xc = xs_ref[pl.ds(r0, m), :]
        wsc = ws_ref[pl.ds(r0, m), :]
        lane = lax.broadcasted_iota(jnp.int32, (m, LANES), 1)
        y = None
        for k in range(MOE_EPS):
            gu = jnp.dot(xc, wgu_ref[k], preferred_element_type=F32)
            gate = gu[:, 0:D_EXPERT]
            wk = jnp.sum(jnp.where(lane == s * MOE_EPS + k, wsc, 0.0), axis=-1, keepdims=True)
            he = (gate * jax.nn.sigmoid(gate) * gu[:, D_EXPERT:2 * D_EXPERT] * wk).astype(BF16)
            yk = jnp.dot(he, wd_ref[k], preferred_element_type=F32)
            y = yk if y is None else y + yk
        ys_ref[pl.ds(r0, m), :] = y.astype(BF16)

    g = s // (EXPERTS_PER_GROUP // MOE_EPS)
    start = meta_ref[g]
    nchunk = meta_ref[N_GROUPS + g]

    def big(i, carry):
        do_chunk((start + MOE_BIG * i) * C, MOE_BIG * C)
        return carry

    common = nchunk == MOE_BIG + 1

    @pl.when(common)
    def _():
        do_chunk(start * C, (MOE_BIG + 1) * C)

    @pl.when(jnp.logical_not(common))
    def _():
        nbig = nchunk // MOE_BIG
        lax.fori_loop(0, nbig, big, 0)
        done = nbig * MOE_BIG
        size = MOE_BIG // 2
        while size >= 1:
            @pl.when((nchunk & size) != 0)
            def _(done=done, size=size):
                do_chunk((start + done) * C, size * C)
            done = done + (nchunk & size)
            size //= 2

    @pl.when(s == pl.num_programs(1) - 1)
    def _():
        out_ref[...] = x1_ref[...] + jnp.dot(pt_ref[...], ys_ref[...], preferred_element_type=F32)


def _moe(x1, h2, lg, wgu, wd, tm):
    n = x1.shape[0]
    rcap = (tm + N_GROUPS * (MOE_C - 1)) // MOE_C * MOE_C
    rcap = -(-rcap // LANES) * LANES
    idx = np.arange(tm)
    tri = jnp.asarray(idx[None, :] < idx[:, None], BF16)
    return pl.pallas_call(
        functools.partial(_moe_kernel, rcap=rcap),
        grid=(n // tm, N_EXPERTS // MOE_EPS),
        in_specs=[
            pl.BlockSpec((tm, D_MODEL), lambda i, s: (i, 0), pipeline_mode=pl.Buffered(1)),
            pl.BlockSpec((tm, D_MODEL), lambda i, s: (i, 0), pipeline_mode=pl.Buffered(1)),
            pl.BlockSpec((tm, LANES), lambda i, s: (i, 0), pipeline_mode=pl.Buffered(1)),
            pl.BlockSpec((tm, tm), lambda i, s: (0, 0), pipeline_mode=pl.Buffered(1)),
            pl.BlockSpec((MOE_EPS, D_MODEL, 2 * D_EXPERT), lambda i, s: (s, 0, 0)),
            pl.BlockSpec((MOE_EPS, D_EXPERT, D_MODEL), lambda i, s: (s, 0, 0)),
        ],
        out_specs=pl.BlockSpec((tm, D_MODEL), lambda i, s: (i, 0), pipeline_mode=pl.Buffered(1)),
        out_shape=jax.ShapeDtypeStruct((n, D_MODEL), F32),
        scratch_shapes=[
            pltpu.VMEM((tm, rcap), BF16),
            pltpu.VMEM((rcap, D_MODEL), BF16),
            pltpu.VMEM((rcap, LANES), F32),
            pltpu.VMEM((rcap, D_MODEL), BF16),
            pltpu.SMEM((2 * N_GROUPS,), jnp.int32),
        ],
        compiler_params=pltpu.CompilerParams(dimension_semantics=("parallel", "arbitrary"),
                                             vmem_limit_bytes=MOE_VMEM_LIMIT),
        name="moe",
    )(x1, h2, lg, tri, wgu, wd)


def _tile(n, pref):
    t = pref
    while n % t:
        t //= 2
    return t


def _mixer_params(layer, norm1_g, w_in, a_conv_w, a_conv_b, a_gate_bias, a_out_norm_g,
                  b_qk_norm_g, b_rel_bias, c_qk_norm_g, c_lambda, c_sub_norm_g, t5_bias,
                  w_branch, w_out, nq_diff):
    n_small = 2 * A_HEADS
    cut = 4 * A_WIDTH
    w_main = jnp.concatenate([w_in[:, :cut], w_in[:, cut + n_small:]], axis=1)
    w_main = w_main.reshape(D_MODEL, N_SEG, SEG)[:, np.asarray(SEG_PERM), :]
    w_main = w_main.reshape(D_MODEL, N_SEG * SEG).astype(BF16)
    w_if = jnp.pad(w_in[:, cut:cut + n_small], ((0, 0), (0, LANES - n_small))).astype(BF16)
    gain = jnp.ones((N_SEG, SEG), F32)
    gain = gain.at[N_ROW_SEG + TSEG_BQ].set(jnp.tile(b_qk_norm_g[0], B_HEADS) * (B_DH ** -0.5 * LOG2E))
    gain = gain.at[SEG_BK].set(jnp.tile(b_qk_norm_g[1], B_HEADS))
    gain = gain.at[N_ROW_SEG + TSEG_CQ].set(
        jnp.tile(c_qk_norm_g[0], 2 * C_HEADS) * (C_DQK ** -0.5 * LOG2E))
    gain = gain.at[SEG_CK].set(jnp.tile(c_qk_norm_g[1], 2 * C_HEADS))
    diff_bias, diff_far = _diff_bias(t5_bias, _diff_nnear(nq_diff))
    lam_init = 0.8 - 0.6 * math.exp(-0.3 * layer)
    lf32 = c_lambda.astype(F32)
    lam = jnp.exp(jnp.sum(lf32[0] * lf32[1])) - jnp.exp(jnp.sum(lf32[2] * lf32[3])) + lam_init
    return dict(
        g1=norm1_g.reshape(1, D_MODEL), w_main=w_main, w_if=w_if,
        gain=gain.reshape(N_SEG, 1, SEG),
        cw=a_conv_w, cb=a_conv_b.reshape(1, -1),
        gbr=jnp.pad(a_gate_bias, (0, LANES - n_small)).reshape(1, LANES),
        gbc=a_gate_bias.reshape(n_small, 1),
        ag=a_out_norm_g.reshape(1, A_WIDTH),
        band_bias=_band_bias(b_rel_bias),
        diff_bias=diff_bias, diff_far=diff_far,
        lam=lam.reshape(1).astype(F32),
        gsub=(c_sub_norm_g * (1.0 - lam_init)).reshape(C_DV, 1),
        wb=w_branch.astype(BF16), wo=w_out.astype(BF16),
    )


def _layer(x2, bsz, seq, mp, norm2_g, w_group, b_group, w_router, b_router, w_e_gate, w_e_up, w_e_down):
    n = bsz * seq
    p, pt, gif = _inproj(x2, mp["g1"], mp["w_main"], mp["w_if"], mp["gain"], _tile(n, 512))
    gift = jnp.transpose(gif[:, :2 * A_HEADS])
    ha, hbt = _mlstm_band(p, pt, gif, gift, mp["cw"], mp["cb"], mp["gbr"], mp["gbc"], mp["ag"],
                          mp["band_bias"], bsz, seq)
    hct = _diff(p, pt, mp["diff_bias"], mp["diff_far"], mp["lam"], mp["gsub"], bsz, seq)

    wr = jnp.concatenate([w_router, w_group], axis=1)
    wr = jnp.pad(wr, ((0, 0), (0, LANES - wr.shape[1]))).astype(BF16)
    br = jnp.pad(jnp.concatenate([b_router, b_group]), (0, LANES - N_EXPERTS - N_GROUPS)).reshape(1, LANES)
    x1, h2, lg = _merge(ha, hbt, hct, p, x2, mp["wb"], mp["wo"], norm2_g.reshape(1, D_MODEL), wr, br,
                        _tile(n, 512))
    wgu = jnp.concatenate([w_e_gate, w_e_up], axis=-1).astype(BF16)
    return _moe(x1, h2, lg, wgu, w_e_down.astype(BF16), _tile(n, MOE_TM))


def kernel(x, norm1_g, w_in, a_conv_w, a_conv_b, a_gate_bias, a_out_norm_g, b_qk_norm_g, b_rel_bias,
           c_qk_norm_g, c_lambda, c_sub_norm_g, t5_bias, w_branch, w_out, norm2_g, w_group, b_group,
           w_router, b_router, w_e_gate, w_e_up, w_e_down):
    bsz, seq, _ = x.shape
    assert seq % DIFF_T == 0 and seq % MLSTM_L == 0 and seq % BAND_TQ == 0
    x2 = x.reshape(bsz * seq, D_MODEL)
    for l in range(norm1_g.shape[0]):
        mp = _mixer_params(l, norm1_g[l], w_in[l], a_conv_w[l], a_conv_b[l], a_gate_bias[l],
                           a_out_norm_g[l], b_qk_norm_g[l], b_rel_bias[l], c_qk_norm_g[l], c_lambda[l],
                           c_sub_norm_g[l], t5_bias, w_branch[l], w_out[l], seq // DIFF_T)
        x2 = _layer(x2, bsz, seq, mp, norm2_g[l], w_group[l], b_group[l], w_router[l], b_router[l],
                    w_e_gate[l], w_e_up[l], w_e_down[l])
    return x2.reshape(bsz, seq, D_MODEL)
```

```python
import functools
import math

import numpy as np
import jax
import jax.numpy as jnp
from jax import lax
from jax.experimental import pallas as pl
from jax.experimental.pallas import tpu as pltpu

F32 = jnp.float32
BF16 = jnp.bfloat16

D_MODEL = 1024
CHUNK = 64
EPS = 1e-6
NEG = -1e30
LOG2E = math.log2(math.e)

A_HEADS = 4
A_DH = 128
A_WIDTH = A_HEADS * A_DH
CONV_W = 4
GATE_CAP = 15.0

B_HEADS = 8
B_DH = 64
B_WIDTH = B_HEADS * B_DH
B_LEFT_CHUNKS = 8
B_MAX_REL = 256

C_HEADS = 4
C_DQK = 64
C_DV = 2 * C_DQK
C_WIDTH = C_HEADS * C_DV

T5_BUCKETS = 32
T5_MAX_DIST = 1024

N_BRANCH = 3
BRANCH_WIDTH = 512

N_GROUPS = 4
EXPERTS_PER_GROUP = 8
N_EXPERTS = N_GROUPS * EXPERTS_PER_GROUP
D_EXPERT = D_MODEL // 4

LANES = 128
SEG = 512
N_SEG = 16
VMEM_LIMIT = 48 * 1024 * 1024

SEG_GATES, SEG_AQ, SEG_AK, SEG_BK, SEG_CK = 0, 6, 7, 8, 9
N_ROW_SEG = 10
TSEG_AV, TSEG_AO, TSEG_BQ, TSEG_BV, TSEG_CQ, TSEG_CV = 0, 1, 2, 3, 4, 5
N_T_SEG = N_SEG - N_ROW_SEG
SEG_PERM = (10, 11, 12, 13, 14, 15, 0, 1, 5, 8, 2, 3, 4, 6, 7, 9)

MLSTM_L = 128
BAND_TQ = 128
BAND_NKB = 1 + (B_LEFT_CHUNKS * CHUNK) // BAND_TQ
BAND_ONES = 16
DIFF_T = 512
DIFF_ONES = 16
MOE_TM = 1024
MOE_C = 64
MOE_BIG = 4
MOE_EPS = EXPERTS_PER_GROUP
MOE_VMEM_LIMIT = 56 * 1024 * 1024

def _cparams(sem, flags=None):
    return pltpu.CompilerParams(dimension_semantics=sem, vmem_limit_bytes=VMEM_LIMIT, flags=flags)


NORM_SEGS = (SEG_BK, SEG_CK, N_ROW_SEG + TSEG_BQ, N_ROW_SEG + TSEG_CQ)
SIGMOID_SEGS = (N_ROW_SEG + TSEG_AO,) + tuple(range(SEG_GATES, SEG_GATES + 6))


def _head_norm_t(acc_t):
    rows, tm = acc_t.shape
    a3 = acc_t.reshape(rows // 64, 64, tm)
    ssq = jnp.sum(a3 * a3, axis=1, keepdims=True)
    return (a3 * lax.rsqrt(ssq * (1.0 / 64.0) + EPS)).reshape(rows, tm)


def _inproj_kernel(x_ref, g_ref, w_ref, wif_ref, gain_ref, gain_t_ref, p_ref, pt_ref, gif_ref):
    xf = x_ref[...]
    xn = (xf * lax.rsqrt(jnp.mean(xf * xf, axis=-1, keepdims=True) + EPS) * g_ref[...]).astype(BF16)
    gif_ref[...] = jnp.dot(xn, wif_ref[...], preferred_element_type=F32)
    for j in range(N_SEG):
        cols = slice(j * SEG, (j + 1) * SEG)
        acc = jnp.dot(xn, w_ref[:, cols], preferred_element_type=F32)
        if j in NORM_SEGS:
            acc_t = _head_norm_t(jnp.transpose(acc))
            if j < N_ROW_SEG:
                p_ref[:, cols] = (jnp.transpose(acc_t) * gain_ref[j]).astype(BF16)
            else:
                pt_ref[j - N_ROW_SEG] = (acc_t * gain_t_ref[NORM_SEGS.index(j) - 2]).astype(BF16)
            continue
        if j in SIGMOID_SEGS:
            acc = jax.nn.sigmoid(acc)
        if j < N_ROW_SEG:
            p_ref[:, cols] = acc.astype(BF16)
        else:
            pt_ref[j - N_ROW_SEG] = jnp.transpose(acc).astype(BF16)


def _inproj(x2, g, w, wif, gain, tm):
    n = x2.shape[0]
    tsegs = np.asarray([N_ROW_SEG + TSEG_BQ, N_ROW_SEG + TSEG_CQ])
    gain_t = jnp.broadcast_to(gain[tsegs, 0, :, None], (2, SEG, tm))

    def const(shape):
        return pl.BlockSpec(shape, lambda i: (0,) * len(shape), pipeline_mode=pl.Buffered(1))

    return pl.pallas_call(
        _inproj_kernel,
        grid=(n // tm,),
        in_specs=[
            pl.BlockSpec((tm, D_MODEL), lambda i: (i, 0)),
            const((1, D_MODEL)),
            const((D_MODEL, N_SEG * SEG)),
            const((D_MODEL, LANES)),
            const((N_SEG, 1, SEG)),
            const((2, SEG, tm)),
        ],
        out_specs=[
            pl.BlockSpec((tm, N_ROW_SEG * SEG), lambda i: (i, 0)),
            pl.BlockSpec((N_T_SEG, SEG, tm), lambda i: (0, 0, i)),
            pl.BlockSpec((tm, LANES), lambda i: (i, 0)),
        ],
        out_shape=[
            jax.ShapeDtypeStruct((n, N_ROW_SEG * SEG), BF16),
            jax.ShapeDtypeStruct((N_T_SEG, SEG, n), BF16),
            jax.ShapeDtypeStruct((n, LANES), F32),
        ],
        compiler_params=_cparams(("parallel",)),
        name="inproj",
    )(x2, g, w, wif, gain, gain_t)


def _log_sigmoid(z):
    return jnp.minimum(z, 0.0) - jnp.log(1.0 + jnp.exp(-jnp.abs(z)))


def _split3(a):
    hi = a.astype(BF16)
    r1 = a - hi.astype(F32)
    mid = r1.astype(BF16)
    lo = (r1 - mid.astype(F32)).astype(BF16)
    return hi, mid, lo


def _mlstm_kernel(*refs):
    _mlstm_carry(*refs)
    for _ in _mlstm_chunk(*refs):
        pass


def _mlstm_carry(aq_ref, ak_ref, vt_ref, aot_ref, gif_ref, gift_ref, cw_ref, cb_ref,
                 gbr_ref, gbc_ref, agt_ref, out_ref, ubuf, kq_ref, st_ref, ct_ref, n_ref, m_ref,
                 *, chunk=None):
    L = MLSTM_L
    c = pl.program_id(1) if chunk is None else chunk

    @pl.when(c == 0)
    def _():
        ubuf[0:8, :] = jnp.zeros((8, 2 * A_WIDTH), F32)
        ct_ref[...] = jnp.zeros_like(ct_ref)
        n_ref[...] = jnp.zeros_like(n_ref)
        m_ref[...] = jnp.zeros_like(m_ref)

    @pl.when(c > 0)
    def _():
        ubuf[0:8, :] = ubuf[L:L + 8, :]


def _mlstm_chunk(aq_ref, ak_ref, vt_ref, aot_ref, gif_ref, gift_ref, cw_ref, cb_ref,
                 gbr_ref, gbc_ref, agt_ref, out_ref, ubuf, kq_ref, st_ref, ct_ref, n_ref, m_ref):
    L = MLSTM_L

    ubuf[8:L + 8, 0:A_WIDTH] = aq_ref[...].astype(F32)
    ubuf[8:L + 8, A_WIDTH:2 * A_WIDTH] = ak_ref[...].astype(F32)
    y = cb_ref[...] + cw_ref[0:1, :] * ubuf[8:L + 8, :]
    for t in range(1, CONV_W):
        y = y + cw_ref[t:t + 1, :] * ubuf[8 - t:8 - t + L, :]
    qk = y * jax.nn.sigmoid(y)
    q_t = jnp.transpose(qk[:, 0:A_WIDTH]).astype(BF16)
    k_all = (qk[:, A_WIDTH:2 * A_WIDTH] * (A_DH ** -0.5)).astype(BF16)

    zc = gif_ref[...] + gbr_ref[...]
    ig_c = GATE_CAP * jnp.tanh(zc * (1.0 / GATE_CAP))
    lf_c = _log_sigmoid(zc)
    zr = gift_ref[...] + gbc_ref[...]
    ig_r = GATE_CAP * jnp.tanh(zr * (1.0 / GATE_CAP))
    lf_r = _log_sigmoid(zr)

    row = lax.broadcasted_iota(jnp.int32, (L, L), 0)
    col = lax.broadcasted_iota(jnp.int32, (L, L), 1)
    causal = col <= row
    tril = jnp.where(causal, 1.0, 0.0).astype(BF16)
    triu = jnp.where(row <= col, 1.0, 0.0).astype(BF16)
    b_c = sum(jnp.dot(tril, piece, preferred_element_type=F32) for piece in _split3(lf_c))
    b_r = sum(jnp.dot(piece, triu, preferred_element_type=F32) for piece in _split3(lf_r))

    sub8 = lax.broadcasted_iota(jnp.int32, (8, L), 0)
    heads = [slice(h * A_DH, (h + 1) * A_DH) for h in range(A_HEADS)]
    for h, rows in enumerate(heads):
        kq_ref[h] = jnp.dot(k_all[:, rows], q_t[rows, :], preferred_element_type=F32)
    yield

    stats = []
    for h, rows in enumerate(heads):
        bcol = b_c[:, A_HEADS + h:A_HEADS + h + 1]
        brow = b_r[A_HEADS + h:A_HEADS + h + 1, :]
        igcol = ig_c[:, h:h + 1]
        m_prev = m_ref[h][:, 0:1]
        dmat = jnp.where(row <= col, brow + (igcol - bcol), NEG)
        inter = brow + m_prev
        m_t = jnp.maximum(inter, jnp.max(dmat, axis=0, keepdims=True))
        st = kq_ref[h] * jnp.exp(dmat - m_t)
        st_ref[h] = st.astype(BF16)
        stats.append((brow, m_prev, m_t, jnp.exp(inter - m_t), jnp.sum(st, axis=0, keepdims=True)))
        if h % 2 == 1:
            yield

    for h, rows in enumerate(heads):
        brow, m_prev, m_t, w_inter, st_sum = stats[h]
        qt = q_t[rows, :]
        k = k_all[:, rows]
        vt = vt_ref[rows, :]
        igrow = ig_r[h:h + 1, :]
        b_last = brow[:, L - 1:L]
        ct = ct_ref[h]
        n8 = n_ref[h]

        num = w_inter * jnp.dot(ct.astype(BF16), qt, preferred_element_type=F32)
        num = num + jnp.dot(vt, st_ref[h], preferred_element_type=F32)
        nq = jnp.dot(n8.astype(BF16), qt, preferred_element_type=F32)[0:1, :]
        den = w_inter * nq + st_sum
        hh = num * (1.0 / jnp.maximum(jnp.abs(den), jnp.exp(-m_t)))

        g_end = b_last - brow + igrow
        m_new = jnp.maximum(b_last + m_prev, jnp.max(g_end, axis=-1, keepdims=True))
        dec = jnp.exp(b_last + m_prev - m_new)
        w_end = jnp.exp(g_end - m_new)
        vw = (vt.astype(F32) * w_end).astype(BF16)
        ct_ref[h] = dec * ct + jnp.dot(vw, k, preferred_element_type=F32)
        w8 = jnp.where(sub8 == 0, w_end, 0.0).astype(BF16)
        n_ref[h] = dec * n8 + jnp.dot(w8, k, preferred_element_type=F32)
        m_ref[h] = jnp.broadcast_to(m_new, (1, LANES))

        hn = hh * lax.rsqrt(jnp.mean(hh * hh, axis=0, keepdims=True) + EPS) * agt_ref[rows, :]
        out_ref[rows, :] = (hn * aot_ref[rows, :].astype(F32)).astype(BF16)
        if h % 2 == 1 and h + 1 < A_HEADS:
            yield


def _mlstm_call(p, pt, gif, gift, cw, cb, gbr, gbc, ag, bsz, seq):
    L = MLSTM_L
    nc = seq // L
    n = bsz * seq
    agt = jnp.broadcast_to(ag.reshape(A_WIDTH, 1), (A_WIDTH, L))

    def tseg(j):
        return pl.BlockSpec((None, SEG, L), lambda b, c: (j, 0, b * nc + c))

    def full(shape):
        return pl.BlockSpec(shape, lambda b, c: (0,) * len(shape))

    in_specs = [
        pl.BlockSpec((L, SEG), lambda b, c: (b * nc + c, SEG_AQ)),
        pl.BlockSpec((L, SEG), lambda b, c: (b * nc + c, SEG_AK)),
        tseg(TSEG_AV), tseg(TSEG_AO),
        pl.BlockSpec((L, LANES), lambda b, c: (b * nc + c, 0)),
        pl.BlockSpec((8, L), lambda b, c: (0, b * nc + c)),
        full((CONV_W, 2 * A_WIDTH)), full((1, 2 * A_WIDTH)),
        full((1, LANES)), full((8, 1)), full((A_WIDTH, L)),
    ]
    scratch = [
        pltpu.VMEM((L + 8, 2 * A_WIDTH), F32),
        pltpu.VMEM((A_HEADS, L, L), F32),
        pltpu.VMEM((A_HEADS, L, L), BF16),
        pltpu.VMEM((A_HEADS, A_DH, A_DH), F32),
        pltpu.VMEM((A_HEADS, 8, A_DH), F32),
        pltpu.VMEM((A_HEADS, 1, LANES), F32),
    ]
    return (in_specs, pl.BlockSpec((A_WIDTH, L), lambda b, c: (0, b * nc + c)),
            jax.ShapeDtypeStruct((A_WIDTH, n), BF16), scratch,
            (p, p, pt, pt, gif, gift, cw, cb, gbr, gbc, agt))


def _mlstm(p, pt, gif, gift, cw, cb, gbr, gbc, ag, bsz, seq):
    in_specs, out_spec, out_shape, scratch, operands = _mlstm_call(
        p, pt, gif, gift, cw, cb, gbr, gbc, ag, bsz, seq)
    return pl.pallas_call(
        _mlstm_kernel, grid=(bsz, seq // MLSTM_L), in_specs=in_specs, out_specs=out_spec,
        out_shape=out_shape, scratch_shapes=scratch,
        compiler_params=_cparams(("parallel", "arbitrary")), name="mlstm",
    )(*operands)


def _interleave(*gens):
    gens = list(gens)
    while gens:
        for g in list(gens):
            try:
                next(g)
            except StopIteration:
                gens.remove(g)
                continue
            yield


def _band_steps(refs, i, mask_start):
    nkb = BAND_NKB
    qt_ref = refs[0]
    k_refs = refs[1:1 + nkb]
    vt_refs = refs[1 + nkb:1 + 2 * nkb]
    bias_ref = refs[1 + 2 * nkb]
    out_ref = refs[2 + 2 * nkb]
    s_ref, mx_ref = refs[3 + 2 * nkb:5 + 2 * nkb]
    tq = BAND_TQ
    nk = nkb * tq

    if True:
        k_all = jnp.concatenate([r[...] for r in k_refs], axis=0)
        vt_all = jnp.concatenate([r[...] for r in vt_refs], axis=1)
        ones = jnp.ones((BAND_ONES, nk), BF16)
        row = lax.broadcasted_iota(jnp.int32, (LANES, tq), 0)
        lo = row < B_DH
        if mask_start:
            kidx = lax.broadcasted_iota(jnp.int32, (nk, 1), 0)
            valid = (kidx + (i - (nkb - 1)) * tq) >= 0

        def score(p):
            rows = slice(p * LANES, (p + 1) * LANES)
            qtp = qt_ref[rows, :]
            zero = jnp.zeros_like(qtp)
            qbd = jnp.concatenate([jnp.where(lo, qtp, zero), jnp.where(lo, zero, qtp)], axis=1)
            s = jnp.dot(k_all[:, rows], qbd, preferred_element_type=F32) + bias_ref[p]
            if mask_start:
                s = jnp.where(valid, s, NEG)
            s_ref[p & 1] = s
            mx_ref[p & 1] = jnp.max(s, axis=0, keepdims=True)

        def finish(p):
            rows = slice(p * LANES, (p + 1) * LANES)
            pr = jnp.exp2((s_ref[p & 1] - mx_ref[p & 1]).astype(BF16))
            o = jnp.dot(jnp.concatenate([vt_all[rows, :], ones], axis=0), pr,
                        preferred_element_type=F32)
            o = o[0:LANES, :] / o[LANES:LANES + 1, :]
            out_ref[rows, :] = jnp.where(lo, o[:, 0:tq], o[:, tq:2 * tq]).astype(BF16)

        score(0)
        yield
        for p in range(B_HEADS // 2):
            if p + 1 < B_HEADS // 2:
                score(p + 1)
            finish(p)
            if p + 1 < B_HEADS // 2:
                yield


def _band_variants(i, region):
    @pl.when(i < BAND_NKB - 1)
    def _():
        region(True)

    @pl.when(i >= BAND_NKB - 1)
    def _():
        region(False)


def _drain(gen):
    for _ in gen:
        pass


def _band_kernel(*refs):
    i = pl.program_id(1)
    _band_variants(i, lambda mask: _drain(_band_steps(refs, i, mask)))


def _band_call(p, pt, bias, bsz, seq):
    tq = BAND_TQ
    nkb = BAND_NKB
    nq = seq // tq
    n = bsz * seq

    def kblk(d):
        return pl.BlockSpec((tq, SEG), lambda b, i: (b * nq + jnp.maximum(i - d, 0), SEG_BK))

    def vblk(d):
        return pl.BlockSpec((None, SEG, tq), lambda b, i: (TSEG_BV, 0, b * nq + jnp.maximum(i - d, 0)))

    in_specs = [pl.BlockSpec((None, SEG, tq), lambda b, i: (TSEG_BQ, 0, b * nq + i))]
    in_specs += [kblk(d) for d in range(nkb - 1, -1, -1)]
    in_specs += [vblk(d) for d in range(nkb - 1, -1, -1)]
    in_specs += [pl.BlockSpec(bias.shape, lambda b, i: (0, 0, 0))]
    scratch = [
        pltpu.VMEM((2, nkb * tq, 2 * tq), F32),
        pltpu.VMEM((2, 1, 2 * tq), F32),
    ]
    return (in_specs, pl.BlockSpec((B_WIDTH, tq), lambda b, i: (0, b * nq + i)),
            jax.ShapeDtypeStruct((B_WIDTH, n), BF16), scratch,
            (pt, *([p] * nkb), *([pt] * nkb), bias))


def _band(p, pt, bias, bsz, seq):
    in_specs, out_spec, out_shape, scratch, operands = _band_call(p, pt, bias, bsz, seq)
    return pl.pallas_call(
        _band_kernel, grid=(bsz, seq // BAND_TQ), in_specs=in_specs, out_specs=out_spec,
        out_shape=out_shape, scratch_shapes=scratch,
        compiler_params=_cparams(("parallel", "parallel")), name="band_attn",
    )(*operands)


def _mlstm_band_kernel(*refs, n_mlstm_in, n_band_in):
    m_in = refs[:n_mlstm_in]
    b_in = refs[n_mlstm_in:n_mlstm_in + n_band_in]
    hat_ref, hbt_ref = refs[n_mlstm_in + n_band_in:n_mlstm_in + n_band_in + 2]
    scr = refs[n_mlstm_in + n_band_in + 2:]
    m_scr, b_scr = scr[:-2], scr[-2:]
    m_refs = (*m_in, hat_ref, *m_scr)
    b_refs = (*b_in, hbt_ref, *b_scr)
    c = pl.program_id(1)
    _mlstm_carry(*m_refs)
    _band_variants(c, lambda mask: _drain(_interleave(_band_steps(b_refs, c, mask),
                                                      _mlstm_chunk(*m_refs))))


def _mlstm_band(p, pt, gif, gift, cw, cb, gbr, gbc, ag, band_bias, bsz, seq):
    assert MLSTM_L == BAND_TQ
    m_specs, m_out, m_shape, m_scr, m_ops = _mlstm_call(p, pt, gif, gift, cw, cb, gbr, gbc, ag, bsz, seq)
    b_specs, b_out, b_shape, b_scr, b_ops = _band_call(p, pt, band_bias, bsz, seq)
    return pl.pallas_call(
        functools.partial(_mlstm_band_kernel, n_mlstm_in=len(m_specs), n_band_in=len(b_specs)),
        grid=(bsz, seq // MLSTM_L),
        in_specs=m_specs + b_specs,
        out_specs=[m_out, b_out],
        out_shape=[m_shape, b_shape],
        scratch_shapes=m_scr + b_scr,
        compiler_params=_cparams(("parallel", "arbitrary")),
        name="mlstm_band",
    )(*m_ops, *b_ops)


def _toeplitz(base, m, n):
    period = base.shape[-1]
    assert n <= period - 1
    reps = (1,) * (base.ndim - 1) + (m,)
    big = jnp.tile(base, reps)[..., :m * (period - 1)]
    return big.reshape(base.shape[:-1] + (m, period - 1))[..., :n]


def _band_bias(b_rel):
    tq = BAND_TQ
    nk = BAND_NKB * tq
    period = tq + nk
    e = np.arange(period)
    e = np.where(e < nk, e, e - period)
    rel = np.clip((nk - tq) - e, -B_MAX_REL, B_MAX_REL) + B_MAX_REL
    bias = _toeplitz(jnp.transpose(b_rel[rel]).astype(F32), tq, nk)
    qpos = np.arange(tq)
    kpos = np.arange(nk) - (nk - tq)
    qc = qpos[:, None] // CHUNK
    kc = np.floor_divide(kpos[None, :], CHUNK)
    allowed = (kc <= qc) & (kc >= qc - B_LEFT_CHUNKS)
    bias = jnp.where(allowed[None], bias * LOG2E, NEG)
    return jnp.swapaxes(bias.reshape(B_HEADS // 2, 2 * tq, nk), 1, 2)


def _diff_kernel(lam_ref, cfar_ref, qt_ref, k_ref, vt_ref, bias_ref, g_ref, out_ref,
                 qbd_ref, s_ref, mx_ref, acc_ref, m_ref, *, nnear, first_region=None):
    T = DIFF_T
    h = pl.program_id(1)
    qi = pl.program_id(2)
    cfar = cfar_ref[h]

    def prologue():
        qt = qt_ref[...]
        row = lax.broadcasted_iota(jnp.int32, (2 * C_DQK, T), 0)
        zero = jnp.zeros_like(qt)
        qbd_ref[:, 0:T] = jnp.where(row < C_DQK, qt, zero)
        qbd_ref[:, T:2 * T] = jnp.where(row < C_DQK, zero, qt)
        m_ref[...] = jnp.full(m_ref.shape, NEG, F32)
        acc_ref[...] = jnp.zeros_like(acc_ref)

    def near_bias(t):
        return bias_ref[t] if t < nnear else None

    def stage_a(t, slot, bias, maps=(0, 1)):
        j = jnp.maximum(qi - t, 0)
        k = k_ref[pl.ds(pl.multiple_of(j * T, T), T), :]
        for mp in maps:
            sm = jnp.dot(k, qbd_ref[:, mp * T:(mp + 1) * T], preferred_element_type=F32)
            if bias is not None:
                sm = sm + bias
            s_ref[slot, mp] = sm
            mx = jnp.max(sm, axis=0, keepdims=True)
            mx_ref[slot, mp] = mx + cfar if bias is None else mx

    def stage_bc(t, slot, far, maps=(0, 1)):
        j = qi - t
        vt = jnp.concatenate([vt_ref[:, pl.ds(pl.multiple_of(j * T, T), T)],
                              jnp.ones((DIFF_ONES, T), BF16)], axis=0)
        for mp in maps:
            m_old = m_ref[mp]
            m_new = jnp.maximum(m_old, mx_ref[slot, mp])
            shift = m_new - cfar if far else m_new
            pr = jnp.exp2((s_ref[slot, mp] - shift).astype(BF16))
            m_ref[mp] = m_new
            acc_ref[mp] = jnp.exp2(m_old - m_new) * acc_ref[mp] + jnp.dot(
                vt, pr, preferred_element_type=F32)

    def full_step(t, slot, far, next_bias):
        for mp in range(2):
            stage_a(t + 1, 1 - slot, next_bias, (mp,))
            stage_bc(t, slot, far, (mp,))

    def first_steps(other):
        prologue()
        next(other, None)
        stage_a(0, 0, near_bias(0))
        next(other, None)
        for mp in range(2):
            stage_a(1, 1, near_bias(1), (mp,))
            stage_bc(0, 0, False, (mp,))
            next(other, None)
        _drain(other)

    if first_region is None:
        first_steps(iter(()))
    else:
        first_region(first_steps)
    for t in range(1, nnear):
        @pl.when(qi >= t)
        def _(t=t):
            full_step(t, t & 1, False, near_bias(t + 1))

    @pl.when(qi >= nnear)
    def _():
        n_full = qi - nnear
        s0 = nnear & 1

        def pair(i, carry):
            full_step(nnear + 2 * i, s0, True, None)
            full_step(nnear + 2 * i + 1, 1 - s0, True, None)
            return carry

        lax.fori_loop(0, n_full // 2, pair, 0)

        @pl.when(n_full % 2 == 1)
        def _():
            full_step(qi - 1, s0, True, None)

        stage_bc(qi, qi & 1, True)

    o1 = acc_ref[0, 0:C_DV, :] / acc_ref[0, C_DV:C_DV + 1, :]
    o2 = acc_ref[1, 0:C_DV, :] / acc_ref[1, C_DV:C_DV + 1, :]
    o = o1 - lam_ref[0] * o2
    on = o * lax.rsqrt(jnp.mean(o * o, axis=0, keepdims=True) + EPS) * g_ref[...]
    out_ref[...] = on.astype(BF16)


def _diff_nnear(nq):
    d_sat = -(-(T5_MAX_DIST - 1 + DIFF_T) // DIFF_T)
    return min(d_sat, nq)


def _diff_call(p, pt, bias, cfar, lam, gsub, bsz, seq):
    T = DIFF_T
    nq = seq // T
    nnear = bias.shape[1]
    kcol = SEG_CK * SEG // LANES
    n = bsz * seq
    in_specs = [
        pl.BlockSpec(memory_space=pltpu.SMEM),
        pl.BlockSpec(memory_space=pltpu.SMEM),
        pl.BlockSpec((None, 2 * C_DQK, T), lambda b, h, i: (TSEG_CQ, h, b * nq + i)),
        pl.BlockSpec((seq, LANES), lambda b, h, i: (b, kcol + h)),
        pl.BlockSpec((None, C_DV, seq), lambda b, h, i: (TSEG_CV, h, b)),
        pl.BlockSpec((None, nnear, T, T), lambda b, h, i: (h, 0, 0, 0)),
        pl.BlockSpec((C_DV, 1), lambda b, h, i: (0, 0)),
    ]
    scratch = [
        pltpu.VMEM((2 * C_DQK, 2 * T), BF16),
        pltpu.VMEM((2, 2, T, T), F32),
        pltpu.VMEM((2, 2, 1, T), F32),
        pltpu.VMEM((2, C_DV + DIFF_ONES, T), F32),
        pltpu.VMEM((2, 1, T), F32),
    ]
    return (in_specs, pl.BlockSpec((C_DV, T), lambda b, h, i: (h, b * nq + i)),
            jax.ShapeDtypeStruct((C_WIDTH, n), BF16), scratch, (lam, cfar, pt, p, pt, bias, gsub))


def _diff(p, pt, bias, cfar, lam, gsub, bsz, seq):
    in_specs, out_spec, out_shape, scratch, operands = _diff_call(p, pt, bias, cfar, lam, gsub, bsz, seq)
    return pl.pallas_call(
        functools.partial(_diff_kernel, nnear=bias.shape[1]),
        grid=(bsz, C_HEADS, seq // DIFF_T), in_specs=in_specs, out_specs=out_spec,
        out_shape=out_shape, scratch_shapes=scratch,
        compiler_params=_cparams(("parallel", "parallel", "arbitrary")), name="diff_attn",
    )(*operands)


def _mixers_kernel(*refs, n_in, n_scr, nnear):
    d_in, m_in, b_in = (refs[sum(n_in[:k]):sum(n_in[:k + 1])] for k in range(3))
    hct_ref, hat_ref, hbt_ref = refs[sum(n_in):sum(n_in) + 3]
    scr = refs[sum(n_in) + 3:]
    d_scr, m_scr, b_scr = (scr[sum(n_scr[:k]):sum(n_scr[:k + 1])] for k in range(3))
    m_refs = (*m_in, hat_ref, *m_scr)
    b_refs = (*b_in, hbt_ref, *b_scr)
    chunk = pl.program_id(1) * pl.num_programs(2) + pl.program_id(2)
    _mlstm_carry(*m_refs, chunk=chunk)

    def first_region(trace):
        _band_variants(chunk, lambda mask: trace(_interleave(_band_steps(b_refs, chunk, mask),
                                                             _mlstm_chunk(*m_refs))))

    _diff_kernel(*d_in, hct_ref, *d_scr, nnear=nnear, first_region=first_region)


def _mixers(p, pt, gif, gift, mp, bsz, seq):
    nq = seq // DIFF_T
    assert C_HEADS * nq == seq // MLSTM_L and MLSTM_L == BAND_TQ

    def on_diff_grid(spec):
        if spec.index_map is None:
            return spec
        return pl.BlockSpec(spec.block_shape, lambda b, h, i, f=spec.index_map: f(b, h * nq + i))

    d_specs, d_out, d_shape, d_scr, d_ops = _diff_call(p, pt, mp["diff_bias"], mp["diff_far"], mp["lam"],
                                                       mp["gsub"], bsz, seq)
    m_specs, m_out, m_shape, m_scr, m_ops = _mlstm_call(p, pt, gif, gift, mp["cw"], mp["cb"], mp["gbr"],
                                                        mp["gbc"], mp["ag"], bsz, seq)
    b_specs, b_out, b_shape, b_scr, b_ops = _band_call(p, pt, mp["band_bias"], bsz, seq)
    hct, hat, hbt = pl.pallas_call(
        functools.partial(_mixers_kernel, n_in=(len(d_specs), len(m_specs), len(b_specs)),
                          n_scr=(len(d_scr), len(m_scr), len(b_scr)), nnear=mp["diff_bias"].shape[1]),
        grid=(bsz, C_HEADS, nq),
        in_specs=d_specs + [on_diff_grid(s) for s in m_specs + b_specs],
        out_specs=[d_out, on_diff_grid(m_out), on_diff_grid(b_out)],
        out_shape=[d_shape, m_shape, b_shape],
        scratch_shapes=d_scr + m_scr + b_scr,
        compiler_params=_cparams(("parallel", "arbitrary", "arbitrary")),
        name="mixers",
    )(*d_ops, *m_ops, *b_ops)
    return hat, hbt, hct


def _t5_bucket(rel):
    nb = T5_BUCKETS // 2
    max_exact = nb // 2
    ret = (rel > 0).astype(jnp.int32) * nb
    n = jnp.abs(rel)
    large = max_exact + (jnp.log(jnp.maximum(n, max_exact).astype(F32) / max_exact)
                         / math.log(T5_MAX_DIST / max_exact) * (nb - max_exact)).astype(jnp.int32)
    large = jnp.minimum(large, nb - 1)
    return ret + jnp.where(n < max_exact, n, large)


def _diff_bias(t5_table, nnear):
    T = DIFF_T
    e = np.arange(2 * T)
    amc = np.where(e < T, -e, 2 * T - e)
    rel = jnp.asarray(-np.arange(nnear)[:, None] * T + amc[None, :], jnp.int32)
    base = jnp.moveaxis(t5_table[_t5_bucket(rel)], -1, 0).astype(F32) * LOG2E
    tiles = _toeplitz(base, T, T)
    a = np.arange(T)[:, None]
    c = np.arange(T)[None, :]
    allowed = np.ones((nnear, T, T), bool)
    allowed[0] = (a // CHUNK) <= (c // CHUNK)
    far = t5_table[_t5_bucket(jnp.asarray(-T5_MAX_DIST, jnp.int32))].astype(F32) * LOG2E
    return jnp.where(allowed[None], tiles, NEG), far


def _merge_kernel(hat_ref, hbt_ref, hct_ref, g0_ref, g1_ref, g2_ref, x_ref, wb_ref, wo_ref,
                  n2_ref, wr_ref, br_ref, x1_ref, h2_ref, lg_ref, y_ref):
    tn = (((0,), (0,)), ((), ()))
    tm = x_ref.shape[0]
    halves = [slice(i * (tm // 2), (i + 1) * (tm // 2)) for i in range(2)]
    for r in halves:
        y = g0_ref[r, :].astype(F32) * lax.dot_general(hat_ref[:, r], wb_ref[0], tn,
                                                       preferred_element_type=F32)
        y = y + g1_ref[r, :].astype(F32) * lax.dot_general(hbt_ref[:, r], wb_ref[1], tn,
                                                           preferred_element_type=F32)
        y = y + g2_ref[r, :].astype(F32) * lax.dot_general(hct_ref[:, r], wb_ref[2], tn,
                                                           preferred_element_type=F32)
        y_ref[r, :] = y.astype(BF16)
    for r in halves:
        x1 = x_ref[r, :] + jnp.dot(y_ref[r, :], wo_ref[...], preferred_element_type=F32)
        x1_ref[r, :] = x1
        h2 = x1 * lax.rsqrt(jnp.mean(x1 * x1, axis=-1, keepdims=True) + EPS) * n2_ref[...]
        h2_ref[r, :] = h2.astype(BF16)
    for r in halves:
        lg_ref[r, :] = jnp.dot(h2_ref[r, :], wr_ref[...], preferred_element_type=F32) + br_ref[...]


def _merge(ha, hbt, hct, p, x2, wb, wo, n2, wr, br, tm):
    n = x2.shape[0]
    gcol = SEG_GATES * SEG // D_MODEL

    def rows(width, col=0):
        return pl.BlockSpec((tm, width), lambda i: (i, col))

    def cols():
        return pl.BlockSpec((BRANCH_WIDTH, tm), lambda i: (0, i))

    def full(shape):
        return pl.BlockSpec(shape, lambda i: (0,) * len(shape))

    return pl.pallas_call(
        _merge_kernel,
        grid=(n // tm,),
        in_specs=[
            cols(), cols(), cols(),
            rows(D_MODEL, gcol), rows(D_MODEL, gcol + 1), rows(D_MODEL, gcol + 2),
            rows(D_MODEL),
            full((N_BRANCH, BRANCH_WIDTH, D_MODEL)), full((D_MODEL, D_MODEL)),
            full((1, D_MODEL)), full((D_MODEL, LANES)), full((1, LANES)),
        ],
        out_specs=[rows(D_MODEL), rows(D_MODEL), rows(LANES)],
        out_shape=[
            jax.ShapeDtypeStruct((n, D_MODEL), F32),
            jax.ShapeDtypeStruct((n, D_MODEL), BF16),
            jax.ShapeDtypeStruct((n, LANES), F32),
        ],
        scratch_shapes=[pltpu.VMEM((tm, D_MODEL), BF16)],
        compiler_params=_cparams(("parallel",)),
        name="merge",
    )(ha, hbt, hct, p, p, p, x2, wb, wo, n2, wr, br)


def _combine_weights(lg):
    lanef = lax.broadcasted_iota(jnp.int32, lg.shape, 1).astype(F32)
    big = 1e9
    is_g = (lanef >= N_EXPERTS) & (lanef < N_EXPERTS + N_GROUPS)
    gl = jnp.where(is_g, lg, -jnp.inf)
    gmax = jnp.max(gl, axis=-1, keepdims=True)
    g_idx = jnp.min(jnp.where(gl == gmax, lanef - N_EXPERTS, big), axis=-1, keepdims=True)
    p_g = 1.0 / jnp.sum(jnp.exp(gl - gmax), axis=-1, keepdims=True)
    in_grp = (lanef >= g_idx * EXPERTS_PER_GROUP) & (lanef < (g_idx + 1.0) * EXPERTS_PER_GROUP)
    el = jnp.where(in_grp, lg, -jnp.inf)
    ee = jnp.exp(el - jnp.max(el, axis=-1, keepdims=True))
    ep = ee / jnp.sum(ee, axis=-1, keepdims=True)
    ep = jnp.where(in_grp, ep, -1.0)
    v1 = jnp.max(ep, axis=-1, keepdims=True)
    i1 = jnp.min(jnp.where(ep == v1, lanef, big), axis=-1, keepdims=True)
    ep2 = jnp.where(lanef == i1, -1.0, ep)
    v2 = jnp.max(ep2, axis=-1, keepdims=True)
    i2 = jnp.min(jnp.where(ep2 == v2, lanef, big), axis=-1, keepdims=True)
    tot = v1 + v2
    comb = jnp.where(lanef == i1, p_g * (v1 / tot), 0.0) + jnp.where(lanef == i2, p_g * (v2 / tot), 0.0)
    return comb, g_idx


def _moe_kernel(x1_ref, h2_ref, lg_ref, tri_ref, wgu_ref, wd_ref, out_ref,
                pt_ref, xs_ref, ws_ref, ys_ref, meta_ref, *, rcap):
    C = MOE_C
    s = pl.program_id(1)
    tm = h2_ref.shape[0]

    @pl.when(s == 0)
    def _():
        comb, g_idx = _combine_weights(lg_ref[...])
        lanef = lax.broadcasted_iota(jnp.int32, comb.shape, 1).astype(F32)
        mine = lanef == g_idx
        onehot = jnp.where(mine, 1.0, 0.0)
        ranks = jnp.dot(tri_ref[...], onehot.astype(BF16), preferred_element_type=F32)
        dest = jnp.sum(jnp.where(mine, ranks, 0.0), axis=-1, keepdims=True)
        off = jnp.int32(0)
        for g in range(N_GROUPS):
            cnt = jnp.sum(onehot[:, g:g + 1]).astype(jnp.int32)
            nchunk = (cnt + (C - 1)) // C
            meta_ref[g] = off
            meta_ref[N_GROUPS + g] = nchunk
            dest = dest + jnp.where(g_idx == float(g), (off * C).astype(F32), 0.0)
            off = off + nchunk
        slot = lax.broadcasted_iota(jnp.int32, (tm, rcap), 1).astype(F32)
        pt = jnp.where(dest == slot, 1.0, 0.0).astype(BF16)
        pt_ref[...] = pt
        comb_hi = comb.astype(BF16)
        comb_lo = (comb - comb_hi.astype(F32)).astype(BF16)
        packed = jnp.concatenate([h2_ref[...], comb_hi, comb_lo], axis=1)
        srt = lax.dot_general(pt, packed, (((0,), (0,)), ((), ())), preferred_element_type=F32)
        xs_ref[...] = srt[:, 0:D_MODEL].astype(BF16)
        ws_ref[...] = srt[:, D_MODEL:D_MODEL + LANES] + srt[:, D_MODEL + LANES:D_MODEL + 2 * LANES]
        ys_ref[...] = jnp.zeros_like(ys_ref)

    def do_chunk(r0, m):
        r0 = pl.multiple_of(r0, C)
        xc = xs_ref[pl.ds(r0, m), :]
        wsc = ws_ref[pl.ds(r0, m), :]
        lane = lax.broadcasted_iota(jnp.int32, (m, LANES), 1)
        y = None
        for k in range(MOE_EPS):
            gu = jnp.dot(xc, wgu_ref[k], preferred_element_type=F32)
            gate = gu[:, 0:D_EXPERT]
            wk = jnp.sum(jnp.where(lane == s * MOE_EPS + k, wsc, 0.0), axis=-1, keepdims=True)
            he = (gate * jax.nn.sigmoid(gate) * gu[:, D_EXPERT:2 * D_EXPERT] * wk).astype(BF16)
            yk = jnp.dot(he, wd_ref[k], preferred_element_type=F32)
            y = yk if y is None else y + yk
        ys_ref[pl.ds(r0, m), :] = y.astype(BF16)

    g = s // (EXPERTS_PER_GROUP // MOE_EPS)
    start = meta_ref[g]
    nchunk = meta_ref[N_GROUPS + g]

    def big(i, carry):
        do_chunk((start + MOE_BIG * i) * C, MOE_BIG * C)
        return carry

    common = nchunk == MOE_BIG + 1

    @pl.when(common)
    def _():
        do_chunk(start * C, (MOE_BIG + 1) * C)

    @pl.when(jnp.logical_not(common))
    def _():
        nbig = nchunk // MOE_BIG
        lax.fori_loop(0, nbig, big, 0)
        done = nbig * MOE_BIG
        size = MOE_BIG // 2
        while size >= 1:
            @pl.when((nchunk & size) != 0)
            def _(done=done, size=size):
                do_chunk((start + done) * C, size * C)
            done = done + (nchunk & size)
            size //= 2

    @pl.when(s == pl.num_programs(1) - 1)
    def _():
        out_ref[...] = x1_ref[...] + jnp.dot(pt_ref[...], ys_ref[...], preferred_element_type=F32)


def _moe(x1, h2, lg, wgu, wd, tm):
    n = x1.shape[0]
    rcap = (tm + N_GROUPS * (MOE_C - 1)) // MOE_C * MOE_C
    rcap = -(-rcap // LANES) * LANES
    idx = np.arange(tm)
    tri = jnp.asarray(idx[None, :] < idx[:, None], BF16)
    return pl.pallas_call(
        functools.partial(_moe_kernel, rcap=rcap),
        grid=(n // tm, N_EXPERTS // MOE_EPS),
        in_specs=[
            pl.BlockSpec((tm, D_MODEL), lambda i, s: (i, 0), pipeline_mode=pl.Buffered(1)),
            pl.BlockSpec((tm, D_MODEL), lambda i, s: (i, 0), pipeline_mode=pl.Buffered(1)),
            pl.BlockSpec((tm, LANES), lambda i, s: (i, 0), pipeline_mode=pl.Buffered(1)),
            pl.BlockSpec((tm, tm), lambda i, s: (0, 0), pipeline_mode=pl.Buffered(1)),
            pl.BlockSpec((MOE_EPS, D_MODEL, 2 * D_EXPERT), lambda i, s: (s, 0, 0)),
            pl.BlockSpec((MOE_EPS, D_EXPERT, D_MODEL), lambda i, s: (s, 0, 0)),
        ],
        out_specs=pl.BlockSpec((tm, D_MODEL), lambda i, s: (i, 0), pipeline_mode=pl.Buffered(1)),
        out_shape=jax.ShapeDtypeStruct((n, D_MODEL), F32),
        scratch_shapes=[
            pltpu.VMEM((tm, rcap), BF16),
            pltpu.VMEM((rcap, D_MODEL), BF16),
            pltpu.VMEM((rcap, LANES), F32),
            pltpu.VMEM((rcap, D_MODEL), BF16),
            pltpu.SMEM((2 * N_GROUPS,), jnp.int32),
        ],
        compiler_params=pltpu.CompilerParams(dimension_semantics=("parallel", "arbitrary"),
                                             vmem_limit_bytes=MOE_VMEM_LIMIT),
        name="moe",
    )(x1, h2, lg, tri, wgu, wd)


def _tile(n, pref):
    t = pref
    while n % t:
        t //= 2
    return t


def _mixer_params(layer, norm1_g, w_in, a_conv_w, a_conv_b, a_gate_bias, a_out_norm_g,
                  b_qk_norm_g, b_rel_bias, c_qk_norm_g, c_lambda, c_sub_norm_g, t5_bias,
                  w_branch, w_out, nq_diff):
    n_small = 2 * A_HEADS
    cut = 4 * A_WIDTH
    w_main = jnp.concatenate([w_in[:, :cut], w_in[:, cut + n_small:]], axis=1)
    w_main = w_main.reshape(D_MODEL, N_SEG, SEG)[:, np.asarray(SEG_PERM), :]
    w_main = w_main.reshape(D_MODEL, N_SEG * SEG).astype(BF16)
    w_if = jnp.pad(w_in[:, cut:cut + n_small], ((0, 0), (0, LANES - n_small))).astype(BF16)
    gain = jnp.ones((N_SEG, SEG), F32)
    gain = gain.at[N_ROW_SEG + TSEG_BQ].set(jnp.tile(b_qk_norm_g[0], B_HEADS) * (B_DH ** -0.5 * LOG2E))
    gain = gain.at[SEG_BK].set(jnp.tile(b_qk_norm_g[1], B_HEADS))
    gain = gain.at[N_ROW_SEG + TSEG_CQ].set(
        jnp.tile(c_qk_norm_g[0], 2 * C_HEADS) * (C_DQK ** -0.5 * LOG2E))
    gain = gain.at[SEG_CK].set(jnp.tile(c_qk_norm_g[1], 2 * C_HEADS))
    diff_bias, diff_far = _diff_bias(t5_bias, _diff_nnear(nq_diff))
    lam_init = 0.8 - 0.6 * math.exp(-0.3 * layer)
    lf32 = c_lambda.astype(F32)
    lam = jnp.exp(jnp.sum(lf32[0] * lf32[1])) - jnp.exp(jnp.sum(lf32[2] * lf32[3])) + lam_init
    return dict(
        g1=norm1_g.reshape(1, D_MODEL), w_main=w_main, w_if=w_if,
        gain=gain.reshape(N_SEG, 1, SEG),
        cw=a_conv_w, cb=a_conv_b.reshape(1, -1),
        gbr=jnp.pad(a_gate_bias, (0, LANES - n_small)).reshape(1, LANES),
        gbc=a_gate_bias.reshape(n_small, 1),
        ag=a_out_norm_g.reshape(1, A_WIDTH),
        band_bias=_band_bias(b_rel_bias),
        diff_bias=diff_bias, diff_far=diff_far,
        lam=lam.reshape(1).astype(F32),
        gsub=(c_sub_norm_g * (1.0 - lam_init)).reshape(C_DV, 1),
        wb=w_branch.astype(BF16), wo=w_out.astype(BF16),
    )


def _layer(x2, bsz, seq, mp, norm2_g, w_group, b_group, w_router, b_router, w_e_gate, w_e_up, w_e_down):
    n = bsz * seq
    p, pt, gif = _inproj(x2, mp["g1"], mp["w_main"], mp["w_if"], mp["gain"], _tile(n, 512))
    gift = jnp.transpose(gif[:, :2 * A_HEADS])
    ha, hbt, hct = _mixers(p, pt, gif, gift, mp, bsz, seq)

    wr = jnp.concatenate([w_router, w_group], axis=1)
    wr = jnp.pad(wr, ((0, 0), (0, LANES - wr.shape[1]))).astype(BF16)
    br = jnp.pad(jnp.concatenate([b_router, b_group]), (0, LANES - N_EXPERTS - N_GROUPS)).reshape(1, LANES)
    x1, h2, lg = _merge(ha, hbt, hct, p, x2, mp["wb"], mp["wo"], norm2_g.reshape(1, D_MODEL), wr, br,
                        _tile(n, 512))
    wgu = jnp.concatenate([w_e_gate, w_e_up], axis=-1).astype(BF16)
    return _moe(x1, h2, lg, wgu, w_e_down.astype(BF16), _tile(n, MOE_TM))


def kernel(x, norm1_g, w_in, a_conv_w, a_conv_b, a_gate_bias, a_out_norm_g, b_qk_norm_g, b_rel_bias,
           c_qk_norm_g, c_lambda, c_sub_norm_g, t5_bias, w_branch, w_out, norm2_g, w_group, b_group,
           w_router, b_router, w_e_gate, w_e_up, w_e_down):
    bsz, seq, _ = x.shape
    assert seq % DIFF_T == 0 and seq % MLSTM_L == 0 and seq % BAND_TQ == 0
    x2 = x.reshape(bsz * seq, D_MODEL)
    for l in range(norm1_g.shape[0]):
        mp = _mixer_params(l, norm1_g[l], w_in[l], a_conv_w[l], a_conv_b[l], a_gate_bias[l],
                           a_out_norm_g[l], b_qk_norm_g[l], b_rel_bias[l], c_qk_norm_g[l], c_lambda[l],
                           c_sub_norm_g[l], t5_bias, w_branch[l], w_out[l], seq // DIFF_T)
        x2 = _layer(x2, bsz, seq, mp, norm2_g[l], w_group[l], b_group[l], w_router[l], b_router[l],
                    w_e_gate[l], w_e_up[l], w_e_down[l])
    return x2.reshape(bsz, seq, D_MODEL)
```

```python
import functools
import math

import numpy as np
import jax
import jax.numpy as jnp
from jax import lax
from jax.experimental import pallas as pl
from jax.experimental.pallas import tpu as pltpu

F32 = jnp.float32
BF16 = jnp.bfloat16

D_MODEL = 1024
CHUNK = 64
EPS = 1e-6
NEG = -1e30
LOG2E = math.log2(math.e)

A_HEADS = 4
A_DH = 128
A_WIDTH = A_HEADS * A_DH
CONV_W = 4
GATE_CAP = 15.0

B_HEADS = 8
B_DH = 64
B_WIDTH = B_HEADS * B_DH
B_LEFT_CHUNKS = 8
B_MAX_REL = 256

C_HEADS = 4
C_DQK = 64
C_DV = 2 * C_DQK
C_WIDTH = C_HEADS * C_DV

T5_BUCKETS = 32
T5_MAX_DIST = 1024

N_BRANCH = 3
BRANCH_WIDTH = 512

N_GROUPS = 4
EXPERTS_PER_GROUP = 8
N_EXPERTS = N_GROUPS * EXPERTS_PER_GROUP
D_EXPERT = D_MODEL // 4

LANES = 128
SEG = 512
N_SEG = 16
VMEM_LIMIT = 48 * 1024 * 1024

SEG_GATES, SEG_AQ, SEG_AK, SEG_BK, SEG_CK = 0, 6, 7, 8, 9
N_ROW_SEG = 10
TSEG_AV, TSEG_AO, TSEG_BQ, TSEG_BV, TSEG_CQ, TSEG_CV = 0, 1, 2, 3, 4, 5
N_T_SEG = N_SEG - N_ROW_SEG
SEG_PERM = (10, 11, 12, 13, 14, 15, 0, 1, 5, 8, 2, 3, 4, 6, 7, 9)

MLSTM_L = 128
CONV_HIST = 8
BAND_TQ = 128
BAND_NKB = 1 + (B_LEFT_CHUNKS * CHUNK) // BAND_TQ
BAND_ONES = 16
DIFF_T = 512
DIFF_ONES = 16
MOE_TM = 1024
MOE_C = 64
MOE_BIG = 4
MOE_EPS = EXPERTS_PER_GROUP
MOE_VMEM_LIMIT = 56 * 1024 * 1024

def _cparams(sem, flags=None):
    return pltpu.CompilerParams(dimension_semantics=sem, vmem_limit_bytes=VMEM_LIMIT, flags=flags)


NORM_SEGS = (SEG_BK, SEG_CK, N_ROW_SEG + TSEG_BQ, N_ROW_SEG + TSEG_CQ)
SIGMOID_SEGS = (N_ROW_SEG + TSEG_AO,) + tuple(range(SEG_GATES, SEG_GATES + 6))


def _head_norm_t(acc_t):
    rows, tm = acc_t.shape
    a3 = acc_t.reshape(rows // 64, 64, tm)
    ssq = jnp.sum(a3 * a3, axis=1, keepdims=True)
    return (a3 * lax.rsqrt(ssq * (1.0 / 64.0) + EPS)).reshape(rows, tm)


def _inproj_kernel(x_ref, g_ref, w_ref, wif_ref, gain_ref, gain_t_ref, p_ref, pt_ref, gif_ref):
    xf = x_ref[...]
    xn = (xf * lax.rsqrt(jnp.mean(xf * xf, axis=-1, keepdims=True) + EPS) * g_ref[...]).astype(BF16)
    gif_ref[...] = jnp.dot(xn, wif_ref[...], preferred_element_type=F32)
    for j in range(N_SEG):
        cols = slice(j * SEG, (j + 1) * SEG)
        acc = jnp.dot(xn, w_ref[:, cols], preferred_element_type=F32)
        if j in NORM_SEGS:
            acc_t = _head_norm_t(jnp.transpose(acc))
            if j < N_ROW_SEG:
                p_ref[:, cols] = (jnp.transpose(acc_t) * gain_ref[j]).astype(BF16)
            else:
                pt_ref[j - N_ROW_SEG] = (acc_t * gain_t_ref[NORM_SEGS.index(j) - 2]).astype(BF16)
            continue
        if j in SIGMOID_SEGS:
            acc = jax.nn.sigmoid(acc)
        if j < N_ROW_SEG:
            p_ref[:, cols] = acc.astype(BF16)
        else:
            pt_ref[j - N_ROW_SEG] = jnp.transpose(acc).astype(BF16)


def _inproj(x2, g, w, wif, gain, tm):
    n = x2.shape[0]
    tsegs = np.asarray([N_ROW_SEG + TSEG_BQ, N_ROW_SEG + TSEG_CQ])
    gain_t = jnp.broadcast_to(gain[tsegs, 0, :, None], (2, SEG, tm))

    def const(shape):
        return pl.BlockSpec(shape, lambda i: (0,) * len(shape), pipeline_mode=pl.Buffered(1))

    return pl.pallas_call(
        _inproj_kernel,
        grid=(n // tm,),
        in_specs=[
            pl.BlockSpec((tm, D_MODEL), lambda i: (i, 0)),
            const((1, D_MODEL)),
            const((D_MODEL, N_SEG * SEG)),
            const((D_MODEL, LANES)),
            const((N_SEG, 1, SEG)),
            const((2, SEG, tm)),
        ],
        out_specs=[
            pl.BlockSpec((tm, N_ROW_SEG * SEG), lambda i: (i, 0)),
            pl.BlockSpec((N_T_SEG, SEG, tm), lambda i: (0, 0, i)),
            pl.BlockSpec((tm, LANES), lambda i: (i, 0)),
        ],
        out_shape=[
            jax.ShapeDtypeStruct((n, N_ROW_SEG * SEG), BF16),
            jax.ShapeDtypeStruct((N_T_SEG, SEG, n), BF16),
            jax.ShapeDtypeStruct((n, LANES), F32),
        ],
        compiler_params=_cparams(("parallel",)),
        name="inproj",
    )(x2, g, w, wif, gain, gain_t)


def _log_sigmoid(z):
    return jnp.minimum(z, 0.0) - jnp.log(1.0 + jnp.exp(-jnp.abs(z)))


def _split3(a):
    hi = a.astype(BF16)
    r1 = a - hi.astype(F32)
    mid = r1.astype(BF16)
    lo = (r1 - mid.astype(F32)).astype(BF16)
    return hi, mid, lo


def _mlstm_kernel(*refs):
    _mlstm_carry(*refs)
    for _ in _mlstm_chunk(*refs):
        pass


def _mlstm_carry(aq_ref, ak_ref, vt_ref, aot_ref, gif_ref, gift_ref, cw_ref, cb_ref,
                 gbr_ref, gbc_ref, agt_ref, out_ref, ubuf, kq_ref, st_ref, ct_ref, n_ref, m_ref,
                 *, chunk=None):
    L = MLSTM_L
    c = pl.program_id(1) if chunk is None else chunk

    @pl.when(c == 0)
    def _():
        ubuf[0:CONV_HIST, :] = jnp.zeros((CONV_HIST, 2 * A_WIDTH), F32)
        ct_ref[...] = jnp.zeros_like(ct_ref)
        n_ref[...] = jnp.zeros_like(n_ref)
        m_ref[...] = jnp.zeros_like(m_ref)

    @pl.when(c > 0)
    def _():
        ubuf[0:CONV_HIST, :] = ubuf[L:L + CONV_HIST, :]


def _mlstm_chunk(aq_ref, ak_ref, vt_ref, aot_ref, gif_ref, gift_ref, cw_ref, cb_ref,
                 gbr_ref, gbc_ref, agt_ref, out_ref, ubuf, kq_ref, st_ref, ct_ref, n_ref, m_ref):
    L = MLSTM_L

    H = CONV_HIST
    ubuf[H:L + H, 0:A_WIDTH] = aq_ref[...].astype(F32)
    ubuf[H:L + H, A_WIDTH:2 * A_WIDTH] = ak_ref[...].astype(F32)
    y = cb_ref[...] + cw_ref[0:1, :] * ubuf[H:L + H, :]
    for t in range(1, CONV_W):
        y = y + cw_ref[t:t + 1, :] * ubuf[H - t:H - t + L, :]
    qk = y * jax.nn.sigmoid(y)
    q_t = jnp.transpose(qk[:, 0:A_WIDTH]).astype(BF16)
    k_all = (qk[:, A_WIDTH:2 * A_WIDTH] * (A_DH ** -0.5)).astype(BF16)

    zc = gif_ref[...] + gbr_ref[...]
    ig_c = GATE_CAP * jnp.tanh(zc * (1.0 / GATE_CAP))
    lf_c = _log_sigmoid(zc)
    zr = gift_ref[...] + gbc_ref[...]
    ig_r = GATE_CAP * jnp.tanh(zr * (1.0 / GATE_CAP))
    lf_r = _log_sigmoid(zr)

    row = lax.broadcasted_iota(jnp.int32, (L, L), 0)
    col = lax.broadcasted_iota(jnp.int32, (L, L), 1)
    causal = col <= row
    tril = jnp.where(causal, 1.0, 0.0).astype(BF16)
    triu = jnp.where(row <= col, 1.0, 0.0).astype(BF16)
    b_c = sum(jnp.dot(tril, piece, preferred_element_type=F32) for piece in _split3(lf_c))
    b_r = sum(jnp.dot(piece, triu, preferred_element_type=F32) for piece in _split3(lf_r))

    sub8 = lax.broadcasted_iota(jnp.int32, (8, L), 0)
    heads = [slice(h * A_DH, (h + 1) * A_DH) for h in range(A_HEADS)]
    for h, rows in enumerate(heads):
        kq_ref[h] = jnp.dot(k_all[:, rows], q_t[rows, :], preferred_element_type=F32)
    yield

    stats = []
    for h, rows in enumerate(heads):
        bcol = b_c[:, A_HEADS + h:A_HEADS + h + 1]
        brow = b_r[A_HEADS + h:A_HEADS + h + 1, :]
        igcol = ig_c[:, h:h + 1]
        m_prev = m_ref[h][:, 0:1]
        dmat = jnp.where(row <= col, brow + (igcol - bcol), NEG)
        inter = brow + m_prev
        m_t = jnp.maximum(inter, jnp.max(dmat, axis=0, keepdims=True))
        st = kq_ref[h] * jnp.exp(dmat - m_t)
        st_ref[h] = st.astype(BF16)
        stats.append((brow, m_prev, m_t, jnp.exp(inter - m_t), jnp.sum(st, axis=0, keepdims=True)))
        if h % 2 == 1:
            yield

    for h, rows in enumerate(heads):
        brow, m_prev, m_t, w_inter, st_sum = stats[h]
        qt = q_t[rows, :]
        k = k_all[:, rows]
        vt = vt_ref[rows, :]
        igrow = ig_r[h:h + 1, :]
        b_last = brow[:, L - 1:L]
        ct = ct_ref[h]
        n8 = n_ref[h]

        num = w_inter * jnp.dot(ct.astype(BF16), qt, preferred_element_type=F32)
        num = num + jnp.dot(vt, st_ref[h], preferred_element_type=F32)
        nq = jnp.dot(n8.astype(BF16), qt, preferred_element_type=F32)[0:1, :]
        den = w_inter * nq + st_sum
        hh = num * (1.0 / jnp.maximum(jnp.abs(den), jnp.exp(-m_t)))

        g_end = b_last - brow + igrow
        m_new = jnp.maximum(b_last + m_prev, jnp.max(g_end, axis=-1, keepdims=True))
        dec = jnp.exp(b_last + m_prev - m_new)
        w_end = jnp.exp(g_end - m_new)
        vw = (vt.astype(F32) * w_end).astype(BF16)
        ct_ref[h] = dec * ct + jnp.dot(vw, k, preferred_element_type=F32)
        w8 = jnp.where(sub8 == 0, w_end, 0.0).astype(BF16)
        n_ref[h] = dec * n8 + jnp.dot(w8, k, preferred_element_type=F32)
        m_ref[h] = jnp.broadcast_to(m_new, (1, LANES))

        hn = hh * lax.rsqrt(jnp.mean(hh * hh, axis=0, keepdims=True) + EPS) * agt_ref[rows, :]
        out_ref[rows, :] = (hn * aot_ref[rows, :].astype(F32)).astype(BF16)
        if h % 2 == 1 and h + 1 < A_HEADS:
            yield


def _mlstm_call(p, pt, gif, gift, cw, cb, gbr, gbc, ag, bsz, seq):
    L = MLSTM_L
    nc = seq // L
    n = bsz * seq
    agt = jnp.broadcast_to(ag.reshape(A_WIDTH, 1), (A_WIDTH, L))

    def tseg(j):
        return pl.BlockSpec((None, SEG, L), lambda b, c: (j, 0, b * nc + c))

    def full(shape):
        return pl.BlockSpec(shape, lambda b, c: (0,) * len(shape))

    in_specs = [
        pl.BlockSpec((L, SEG), lambda b, c: (b * nc + c, SEG_AQ)),
        pl.BlockSpec((L, SEG), lambda b, c: (b * nc + c, SEG_AK)),
        tseg(TSEG_AV), tseg(TSEG_AO),
        pl.BlockSpec((L, LANES), lambda b, c: (b * nc + c, 0)),
        pl.BlockSpec((8, L), lambda b, c: (0, b * nc + c)),
        full((CONV_W, 2 * A_WIDTH)), full((1, 2 * A_WIDTH)),
        full((1, LANES)), full((8, 1)), full((A_WIDTH, L)),
    ]
    scratch = [
        pltpu.VMEM((L + CONV_HIST, 2 * A_WIDTH), F32),
        pltpu.VMEM((A_HEADS, L, L), F32),
        pltpu.VMEM((A_HEADS, L, L), BF16),
        pltpu.VMEM((A_HEADS, A_DH, A_DH), F32),
        pltpu.VMEM((A_HEADS, 8, A_DH), F32),
        pltpu.VMEM((A_HEADS, 1, LANES), F32),
    ]
    return (in_specs, pl.BlockSpec((A_WIDTH, L), lambda b, c: (0, b * nc + c)),
            jax.ShapeDtypeStruct((A_WIDTH, n), BF16), scratch,
            (p, p, pt, pt, gif, gift, cw, cb, gbr, gbc, agt))


def _mlstm(p, pt, gif, gift, cw, cb, gbr, gbc, ag, bsz, seq):
    in_specs, out_spec, out_shape, scratch, operands = _mlstm_call(
        p, pt, gif, gift, cw, cb, gbr, gbc, ag, bsz, seq)
    return pl.pallas_call(
        _mlstm_kernel, grid=(bsz, seq // MLSTM_L), in_specs=in_specs, out_specs=out_spec,
        out_shape=out_shape, scratch_shapes=scratch,
        compiler_params=_cparams(("parallel", "arbitrary")), name="mlstm",
    )(*operands)


def _interleave(*gens):
    gens = list(gens)
    while gens:
        for g in list(gens):
            try:
                next(g)
            except StopIteration:
                gens.remove(g)
                continue
            yield


def _band_steps(refs, i, mask_start):
    nkb = BAND_NKB
    qt_ref = refs[0]
    k_refs = refs[1:1 + nkb]
    vt_refs = refs[1 + nkb:1 + 2 * nkb]
    bias_ref = refs[1 + 2 * nkb]
    out_ref = refs[2 + 2 * nkb]
    s_ref, mx_ref = refs[3 + 2 * nkb:5 + 2 * nkb]
    tq = BAND_TQ
    nk = nkb * tq

    k_all = jnp.concatenate([r[...] for r in k_refs], axis=0)
    vt_all = jnp.concatenate([r[...] for r in vt_refs], axis=1)
    ones = jnp.ones((BAND_ONES, nk), BF16)
    row = lax.broadcasted_iota(jnp.int32, (LANES, tq), 0)
    lo = row < B_DH
    if mask_start:
        kidx = lax.broadcasted_iota(jnp.int32, (nk, 1), 0)
        valid = (kidx + (i - (nkb - 1)) * tq) >= 0

    def score(p):
        rows = slice(p * LANES, (p + 1) * LANES)
        qtp = qt_ref[rows, :]
        zero = jnp.zeros_like(qtp)
        qbd = jnp.concatenate([jnp.where(lo, qtp, zero), jnp.where(lo, zero, qtp)], axis=1)
        s = jnp.dot(k_all[:, rows], qbd, preferred_element_type=F32) + bias_ref[p]
        if mask_start:
            s = jnp.where(valid, s, NEG)
        s_ref[p & 1] = s
        mx_ref[p & 1] = jnp.max(s, axis=0, keepdims=True)

    def finish(p):
        rows = slice(p * LANES, (p + 1) * LANES)
        pr = jnp.exp2((s_ref[p & 1] - mx_ref[p & 1]).astype(BF16))
        o = jnp.dot(jnp.concatenate([vt_all[rows, :], ones], axis=0), pr,
                    preferred_element_type=F32)
        o = o[0:LANES, :] / o[LANES:LANES + 1, :]
        out_ref[rows, :] = jnp.where(lo, o[:, 0:tq], o[:, tq:2 * tq]).astype(BF16)

    score(0)
    yield
    for p in range(B_HEADS // 2):
        if p + 1 < B_HEADS // 2:
            score(p + 1)
        finish(p)
        if p + 1 < B_HEADS // 2:
            yield


def _band_variants(i, region):
    @pl.when(i < BAND_NKB - 1)
    def _():
        region(True)

    @pl.when(i >= BAND_NKB - 1)
    def _():
        region(False)


def _drain(gen):
    for _ in gen:
        pass


def _band_kernel(*refs):
    i = pl.program_id(1)
    _band_variants(i, lambda mask: _drain(_band_steps(refs, i, mask)))


def _band_call(p, pt, bias, bsz, seq):
    tq = BAND_TQ
    nkb = BAND_NKB
    nq = seq // tq
    n = bsz * seq

    def kblk(d):
        return pl.BlockSpec((tq, SEG), lambda b, i: (b * nq + jnp.maximum(i - d, 0), SEG_BK))

    def vblk(d):
        return pl.BlockSpec((None, SEG, tq), lambda b, i: (TSEG_BV, 0, b * nq + jnp.maximum(i - d, 0)))

    in_specs = [pl.BlockSpec((None, SEG, tq), lambda b, i: (TSEG_BQ, 0, b * nq + i))]
    in_specs += [kblk(d) for d in range(nkb - 1, -1, -1)]
    in_specs += [vblk(d) for d in range(nkb - 1, -1, -1)]
    in_specs += [pl.BlockSpec(bias.shape, lambda b, i: (0, 0, 0))]
    scratch = [
        pltpu.VMEM((2, nkb * tq, 2 * tq), F32),
        pltpu.VMEM((2, 1, 2 * tq), F32),
    ]
    return (in_specs, pl.BlockSpec((B_WIDTH, tq), lambda b, i: (0, b * nq + i)),
            jax.ShapeDtypeStruct((B_WIDTH, n), BF16), scratch,
            (pt, *([p] * nkb), *([pt] * nkb), bias))


def _band(p, pt, bias, bsz, seq):
    in_specs, out_spec, out_shape, scratch, operands = _band_call(p, pt, bias, bsz, seq)
    return pl.pallas_call(
        _band_kernel, grid=(bsz, seq // BAND_TQ), in_specs=in_specs, out_specs=out_spec,
        out_shape=out_shape, scratch_shapes=scratch,
        compiler_params=_cparams(("parallel", "parallel")), name="band_attn",
    )(*operands)


def _toeplitz(base, m, n):
    period = base.shape[-1]
    assert n <= period - 1
    reps = (1,) * (base.ndim - 1) + (m,)
    big = jnp.tile(base, reps)[..., :m * (period - 1)]
    return big.reshape(base.shape[:-1] + (m, period - 1))[..., :n]


def _band_bias(b_rel):
    tq = BAND_TQ
    nk = BAND_NKB * tq
    period = tq + nk
    e = np.arange(period)
    e = np.where(e < nk, e, e - period)
    rel = np.clip((nk - tq) - e, -B_MAX_REL, B_MAX_REL) + B_MAX_REL
    bias = _toeplitz(jnp.transpose(b_rel[rel]).astype(F32), tq, nk)
    qpos = np.arange(tq)
    kpos = np.arange(nk) - (nk - tq)
    qc = qpos[:, None] // CHUNK
    kc = np.floor_divide(kpos[None, :], CHUNK)
    allowed = (kc <= qc) & (kc >= qc - B_LEFT_CHUNKS)
    bias = jnp.where(allowed[None], bias * LOG2E, NEG)
    return jnp.swapaxes(bias.reshape(B_HEADS // 2, 2 * tq, nk), 1, 2)


def _diff_kernel(lam_ref, cfar_ref, qt_ref, k_ref, vt_ref, bias_ref, g_ref, out_ref,
                 qbd_ref, s_ref, mx_ref, acc_ref, m_ref, *, nnear, first_region=None):
    T = DIFF_T
    h = pl.program_id(1)
    qi = pl.program_id(2)
    cfar = cfar_ref[h]

    def prologue():
        qt = qt_ref[...]
        row = lax.broadcasted_iota(jnp.int32, (2 * C_DQK, T), 0)
        zero = jnp.zeros_like(qt)
        qbd_ref[:, 0:T] = jnp.where(row < C_DQK, qt, zero)
        qbd_ref[:, T:2 * T] = jnp.where(row < C_DQK, zero, qt)
        m_ref[...] = jnp.full(m_ref.shape, NEG, F32)
        acc_ref[...] = jnp.zeros_like(acc_ref)

    def near_bias(t):
        return bias_ref[t] if t < nnear else None

    def stage_a(t, slot, bias, maps=(0, 1)):
        j = jnp.maximum(qi - t, 0)
        k = k_ref[pl.ds(pl.multiple_of(j * T, T), T), :]
        for mp in maps:
            sm = jnp.dot(k, qbd_ref[:, mp * T:(mp + 1) * T], preferred_element_type=F32)
            if bias is not None:
                sm = sm + bias
            s_ref[slot, mp] = sm
            mx = jnp.max(sm, axis=0, keepdims=True)
            mx_ref[slot, mp] = mx + cfar if bias is None else mx

    def stage_bc(t, slot, far, maps=(0, 1)):
        j = qi - t
        vt = jnp.concatenate([vt_ref[:, pl.ds(pl.multiple_of(j * T, T), T)],
                              jnp.ones((DIFF_ONES, T), BF16)], axis=0)
        for mp in maps:
            m_old = m_ref[mp]
            m_new = jnp.maximum(m_old, mx_ref[slot, mp])
            shift = m_new - cfar if far else m_new
            pr = jnp.exp2((s_ref[slot, mp] - shift).astype(BF16))
            m_ref[mp] = m_new
            acc_ref[mp] = jnp.exp2(m_old - m_new) * acc_ref[mp] + jnp.dot(
                vt, pr, preferred_element_type=F32)

    def full_step(t, slot, far, next_bias):
        for mp in range(2):
            stage_a(t + 1, 1 - slot, next_bias, (mp,))
            stage_bc(t, slot, far, (mp,))

    def first_steps(other):
        prologue()
        next(other, None)
        stage_a(0, 0, near_bias(0))
        next(other, None)
        for mp in range(2):
            stage_a(1, 1, near_bias(1), (mp,))
            stage_bc(0, 0, False, (mp,))
            next(other, None)
        _drain(other)

    if first_region is None:
        first_steps(iter(()))
    else:
        first_region(first_steps)
    for t in range(1, nnear):
        @pl.when(qi >= t)
        def _(t=t):
            full_step(t, t & 1, False, near_bias(t + 1))

    @pl.when(qi >= nnear)
    def _():
        n_full = qi - nnear
        s0 = nnear & 1

        def pair(i, carry):
            full_step(nnear + 2 * i, s0, True, None)
            full_step(nnear + 2 * i + 1, 1 - s0, True, None)
            return carry

        lax.fori_loop(0, n_full // 2, pair, 0)

        @pl.when(n_full % 2 == 1)
        def _():
            full_step(qi - 1, s0, True, None)

        stage_bc(qi, qi & 1, True)

    o1 = acc_ref[0, 0:C_DV, :] / acc_ref[0, C_DV:C_DV + 1, :]
    o2 = acc_ref[1, 0:C_DV, :] / acc_ref[1, C_DV:C_DV + 1, :]
    o = o1 - lam_ref[0] * o2
    on = o * lax.rsqrt(jnp.mean(o * o, axis=0, keepdims=True) + EPS) * g_ref[...]
    out_ref[...] = on.astype(BF16)


def _diff_nnear(nq):
    d_sat = -(-(T5_MAX_DIST - 1 + DIFF_T) // DIFF_T)
    return min(d_sat, nq)


def _diff_call(p, pt, bias, cfar, lam, gsub, bsz, seq):
    T = DIFF_T
    nq = seq // T
    nnear = bias.shape[1]
    kcol = SEG_CK * SEG // LANES
    n = bsz * seq
    in_specs = [
        pl.BlockSpec(memory_space=pltpu.SMEM),
        pl.BlockSpec(memory_space=pltpu.SMEM),
        pl.BlockSpec((None, 2 * C_DQK, T), lambda b, h, i: (TSEG_CQ, h, b * nq + i)),
        pl.BlockSpec((seq, LANES), lambda b, h, i: (b, kcol + h)),
        pl.BlockSpec((None, C_DV, seq), lambda b, h, i: (TSEG_CV, h, b)),
        pl.BlockSpec((None, nnear, T, T), lambda b, h, i: (h, 0, 0, 0)),
        pl.BlockSpec((C_DV, 1), lambda b, h, i: (0, 0)),
    ]
    scratch = [
        pltpu.VMEM((2 * C_DQK, 2 * T), BF16),
        pltpu.VMEM((2, 2, T, T), F32),
        pltpu.VMEM((2, 2, 1, T), F32),
        pltpu.VMEM((2, C_DV + DIFF_ONES, T), F32),
        pltpu.VMEM((2, 1, T), F32),
    ]
    return (in_specs, pl.BlockSpec((C_DV, T), lambda b, h, i: (h, b * nq + i)),
            jax.ShapeDtypeStruct((C_WIDTH, n), BF16), scratch, (lam, cfar, pt, p, pt, bias, gsub))


def _diff(p, pt, bias, cfar, lam, gsub, bsz, seq):
    in_specs, out_spec, out_shape, scratch, operands = _diff_call(p, pt, bias, cfar, lam, gsub, bsz, seq)
    return pl.pallas_call(
        functools.partial(_diff_kernel, nnear=bias.shape[1]),
        grid=(bsz, C_HEADS, seq // DIFF_T), in_specs=in_specs, out_specs=out_spec,
        out_shape=out_shape, scratch_shapes=scratch,
        compiler_params=_cparams(("parallel", "parallel", "arbitrary")), name="diff_attn",
    )(*operands)


def _mixers_kernel(*refs, n_in, n_scr, nnear):
    d_in, m_in, b_in = (refs[sum(n_in[:k]):sum(n_in[:k + 1])] for k in range(3))
    hct_ref, hat_ref, hbt_ref = refs[sum(n_in):sum(n_in) + 3]
    scr = refs[sum(n_in) + 3:]
    d_scr, m_scr, b_scr = (scr[sum(n_scr[:k]):sum(n_scr[:k + 1])] for k in range(3))
    m_refs = (*m_in, hat_ref, *m_scr)
    b_refs = (*b_in, hbt_ref, *b_scr)
    chunk = pl.program_id(1) * pl.num_programs(2) + pl.program_id(2)
    _mlstm_carry(*m_refs, chunk=chunk)

    def first_region(trace):
        _band_variants(chunk, lambda mask: trace(_interleave(_band_steps(b_refs, chunk, mask),
                                                             _mlstm_chunk(*m_refs))))

    _diff_kernel(*d_in, hct_ref, *d_scr, nnear=nnear, first_region=first_region)


def _mixers(p, pt, gif, gift, mp, bsz, seq):
    nq = seq // DIFF_T
    assert C_HEADS * nq == seq // MLSTM_L and MLSTM_L == BAND_TQ

    def on_diff_grid(spec):
        if spec.index_map is None:
            return spec
        return pl.BlockSpec(spec.block_shape, lambda b, h, i, f=spec.index_map: f(b, h * nq + i))

    d_specs, d_out, d_shape, d_scr, d_ops = _diff_call(p, pt, mp["diff_bias"], mp["diff_far"], mp["lam"],
                                                       mp["gsub"], bsz, seq)
    m_specs, m_out, m_shape, m_scr, m_ops = _mlstm_call(p, pt, gif, gift, mp["cw"], mp["cb"], mp["gbr"],
                                                        mp["gbc"], mp["ag"], bsz, seq)
    b_specs, b_out, b_shape, b_scr, b_ops = _band_call(p, pt, mp["band_bias"], bsz, seq)
    hct, hat, hbt = pl.pallas_call(
        functools.partial(_mixers_kernel, n_in=(len(d_specs), len(m_specs), len(b_specs)),
                          n_scr=(len(d_scr), len(m_scr), len(b_scr)), nnear=mp["diff_bias"].shape[1]),
        grid=(bsz, C_HEADS, nq),
        in_specs=d_specs + [on_diff_grid(s) for s in m_specs + b_specs],
        out_specs=[d_out, on_diff_grid(m_out), on_diff_grid(b_out)],
        out_shape=[d_shape, m_shape, b_shape],
        scratch_shapes=d_scr + m_scr + b_scr,
        compiler_params=_cparams(("parallel", "arbitrary", "arbitrary")),
        name="mixers",
    )(*d_ops, *m_ops, *b_ops)
    return hat, hbt, hct


def _t5_bucket(rel):
    nb = T5_BUCKETS // 2
    max_exact = nb // 2
    ret = (rel > 0).astype(jnp.int32) * nb
    n = jnp.abs(rel)
    large = max_exact + (jnp.log(jnp.maximum(n, max_exact).astype(F32) / max_exact)
                         / math.log(T5_MAX_DIST / max_exact) * (nb - max_exact)).astype(jnp.int32)
    large = jnp.minimum(large, nb - 1)
    return ret + jnp.where(n < max_exact, n, large)


def _diff_bias(t5_table, nnear):
    T = DIFF_T
    e = np.arange(2 * T)
    amc = np.where(e < T, -e, 2 * T - e)
    rel = jnp.asarray(-np.arange(nnear)[:, None] * T + amc[None, :], jnp.int32)
    base = jnp.moveaxis(t5_table[_t5_bucket(rel)], -1, 0).astype(F32) * LOG2E
    tiles = _toeplitz(base, T, T)
    a = np.arange(T)[:, None]
    c = np.arange(T)[None, :]
    allowed = np.ones((nnear, T, T), bool)
    allowed[0] = (a // CHUNK) <= (c // CHUNK)
    far = t5_table[_t5_bucket(jnp.asarray(-T5_MAX_DIST, jnp.int32))].astype(F32) * LOG2E
    return jnp.where(allowed[None], tiles, NEG), far


def _merge_kernel(hat_ref, hbt_ref, hct_ref, g0_ref, g1_ref, g2_ref, x_ref, wb_ref, wo_ref,
                  n2_ref, wr_ref, br_ref, x1_ref, h2_ref, lg_ref, y_ref):
    tn = (((0,), (0,)), ((), ()))
    tm = x_ref.shape[0]
    halves = [slice(i * (tm // 2), (i + 1) * (tm // 2)) for i in range(2)]
    for r in halves:
        y = g0_ref[r, :].astype(F32) * lax.dot_general(hat_ref[:, r], wb_ref[0], tn,
                                                       preferred_element_type=F32)
        y = y + g1_ref[r, :].astype(F32) * lax.dot_general(hbt_ref[:, r], wb_ref[1], tn,
                                                           preferred_element_type=F32)
        y = y + g2_ref[r, :].astype(F32) * lax.dot_general(hct_ref[:, r], wb_ref[2], tn,
                                                           preferred_element_type=F32)
        y_ref[r, :] = y.astype(BF16)
    for r in halves:
        x1 = x_ref[r, :] + jnp.dot(y_ref[r, :], wo_ref[...], preferred_element_type=F32)
        x1_ref[r, :] = x1
        h2 = x1 * lax.rsqrt(jnp.mean(x1 * x1, axis=-1, keepdims=True) + EPS) * n2_ref[...]
        h2_ref[r, :] = h2.astype(BF16)
    for r in halves:
        lg_ref[r, :] = jnp.dot(h2_ref[r, :], wr_ref[...], preferred_element_type=F32) + br_ref[...]


def _merge(ha, hbt, hct, p, x2, wb, wo, n2, wr, br, tm):
    n = x2.shape[0]
    gcol = SEG_GATES * SEG // D_MODEL

    def rows(width, col=0):
        return pl.BlockSpec((tm, width), lambda i: (i, col))

    def cols():
        return pl.BlockSpec((BRANCH_WIDTH, tm), lambda i: (0, i))

    def full(shape):
        return pl.BlockSpec(shape, lambda i: (0,) * len(shape))

    return pl.pallas_call(
        _merge_kernel,
        grid=(n // tm,),
        in_specs=[
            cols(), cols(), cols(),
            rows(D_MODEL, gcol), rows(D_MODEL, gcol + 1), rows(D_MODEL, gcol + 2),
            rows(D_MODEL),
            full((N_BRANCH, BRANCH_WIDTH, D_MODEL)), full((D_MODEL, D_MODEL)),
            full((1, D_MODEL)), full((D_MODEL, LANES)), full((1, LANES)),
        ],
        out_specs=[rows(D_MODEL), rows(D_MODEL), rows(LANES)],
        out_shape=[
            jax.ShapeDtypeStruct((n, D_MODEL), F32),
            jax.ShapeDtypeStruct((n, D_MODEL), BF16),
            jax.ShapeDtypeStruct((n, LANES), F32),
        ],
        scratch_shapes=[pltpu.VMEM((tm, D_MODEL), BF16)],
        compiler_params=_cparams(("parallel",)),
        name="merge",
    )(ha, hbt, hct, p, p, p, x2, wb, wo, n2, wr, br)


def _combine_weights(lg):
    lanef = lax.broadcasted_iota(jnp.int32, lg.shape, 1).astype(F32)
    big = 1e9
    is_g = (lanef >= N_EXPERTS) & (lanef < N_EXPERTS + N_GROUPS)
    gl = jnp.where(is_g, lg, -jnp.inf)
    gmax = jnp.max(gl, axis=-1, keepdims=True)
    g_idx = jnp.min(jnp.where(gl == gmax, lanef - N_EXPERTS, big), axis=-1, keepdims=True)
    p_g = 1.0 / jnp.sum(jnp.exp(gl - gmax), axis=-1, keepdims=True)
    in_grp = (lanef >= g_idx * EXPERTS_PER_GROUP) & (lanef < (g_idx + 1.0) * EXPERTS_PER_GROUP)
    el = jnp.where(in_grp, lg, -jnp.inf)
    ee = jnp.exp(el - jnp.max(el, axis=-1, keepdims=True))
    ep = ee / jnp.sum(ee, axis=-1, keepdims=True)
    ep = jnp.where(in_grp, ep, -1.0)
    v1 = jnp.max(ep, axis=-1, keepdims=True)
    i1 = jnp.min(jnp.where(ep == v1, lanef, big), axis=-1, keepdims=True)
    ep2 = jnp.where(lanef == i1, -1.0, ep)
    v2 = jnp.max(ep2, axis=-1, keepdims=True)
    i2 = jnp.min(jnp.where(ep2 == v2, lanef, big), axis=-1, keepdims=True)
    tot = v1 + v2
    comb = jnp.where(lanef == i1, p_g * (v1 / tot), 0.0) + jnp.where(lanef == i2, p_g * (v2 / tot), 0.0)
    return comb, g_idx


def _moe_kernel(x1_ref, h2_ref, lg_ref, tri_ref, wgu_ref, wd_ref, out_ref,
                pt_ref, xs_ref, ws_ref, ys_ref, meta_ref, *, rcap):
    C = MOE_C
    s = pl.program_id(1)
    tm = h2_ref.shape[0]

    @pl.when(s == 0)
    def _():
        comb, g_idx = _combine_weights(lg_ref[...])
        lanef = lax.broadcasted_iota(jnp.int32, comb.shape, 1).astype(F32)
        mine = lanef == g_idx
        onehot = jnp.where(mine, 1.0, 0.0)
        ranks = jnp.dot(tri_ref[...], onehot.astype(BF16), preferred_element_type=F32)
        dest = jnp.sum(jnp.where(mine, ranks, 0.0), axis=-1, keepdims=True)
        off = jnp.int32(0)
        for g in range(N_GROUPS):
            cnt = jnp.sum(onehot[:, g:g + 1]).astype(jnp.int32)
            nchunk = (cnt + (C - 1)) // C
            meta_ref[g] = off
            meta_ref[N_GROUPS + g] = nchunk
            dest = dest + jnp.where(g_idx == float(g), (off * C).astype(F32), 0.0)
            off = off + nchunk
        slot = lax.broadcasted_iota(jnp.int32, (tm, rcap), 1).astype(F32)
        pt = jnp.where(dest == slot, 1.0, 0.0).astype(BF16)
        pt_ref[...] = pt
        comb_hi = comb.astype(BF16)
        comb_lo = (comb - comb_hi.astype(F32)).astype(BF16)
        packed = jnp.concatenate([h2_ref[...], comb_hi, comb_lo], axis=1)
        srt = lax.dot_general(pt, packed, (((0,), (0,)), ((), ())), preferred_element_type=F32)
        xs_ref[...] = srt[:, 0:D_MODEL].astype(BF16)
        ws_ref[...] = srt[:, D_MODEL:D_MODEL + LANES] + srt[:, D_MODEL + LANES:D_MODEL + 2 * LANES]
        ys_ref[...] = jnp.zeros_like(ys_ref)

    def do_chunk(r0, m):
        r0 = pl.multiple_of(r0, C)
        xc = xs_ref[pl.ds(r0, m), :]
        wsc = ws_ref[pl.ds(r0, m), :]
        lane = lax.broadcasted_iota(jnp.int32, (m, LANES), 1)
        y = None
        for k in range(MOE_EPS):
            gu = jnp.dot(xc, wgu_ref[k], preferred_element_type=F32)
            gate = gu[:, 0:D_EXPERT]
            wk = jnp.sum(jnp.where(lane == s * MOE_EPS + k, wsc, 0.0), axis=-1, keepdims=True)
            he = (gate * jax.nn.sigmoid(gate) * gu[:, D_EXPERT:2 * D_EXPERT] * wk).astype(BF16)
            yk = jnp.dot(he, wd_ref[k], preferred_element_type=F32)
            y = yk if y is None else y + yk
        ys_ref[pl.ds(r0, m), :] = y.astype(BF16)

    g = s // (EXPERTS_PER_GROUP // MOE_EPS)
    start = meta_ref[g]
    nchunk = meta_ref[N_GROUPS + g]

    def big(i, carry):
        do_chunk((start + MOE_BIG * i) * C, MOE_BIG * C)
        return carry

    common = nchunk == MOE_BIG + 1

    @pl.when(common)
    def _():
        do_chunk(start * C, (MOE_BIG + 1) * C)

    @pl.when(jnp.logical_not(common))
    def _():
        nbig = nchunk // MOE_BIG
        lax.fori_loop(0, nbig, big, 0)
        done = nbig * MOE_BIG
        size = MOE_BIG // 2
        while size >= 1:
            @pl.when((nchunk & size) != 0)
            def _(done=done, size=size):
                do_chunk((start + done) * C, size * C)
            done = done + (nchunk & size)
            size //= 2

    @pl.when(s == pl.num_programs(1) - 1)
    def _():
        out_ref[...] = x1_ref[...] + jnp.dot(pt_ref[...], ys_ref[...], preferred_element_type=F32)


def _moe(x1, h2, lg, wgu, wd, tm):
    n = x1.shape[0]
    rcap = (tm + N_GROUPS * (MOE_C - 1)) // MOE_C * MOE_C
    rcap = -(-rcap // LANES) * LANES
    idx = np.arange(tm)
    tri = jnp.asarray(idx[None, :] < idx[:, None], BF16)
    return pl.pallas_call(
        functools.partial(_moe_kernel, rcap=rcap),
        grid=(n // tm, N_EXPERTS // MOE_EPS),
        in_specs=[
            pl.BlockSpec((tm, D_MODEL), lambda i, s: (i, 0), pipeline_mode=pl.Buffered(1)),
            pl.BlockSpec((tm, D_MODEL), lambda i, s: (i, 0), pipeline_mode=pl.Buffered(1)),
            pl.BlockSpec((tm, LANES), lambda i, s: (i, 0), pipeline_mode=pl.Buffered(1)),
            pl.BlockSpec((tm, tm), lambda i, s: (0, 0), pipeline_mode=pl.Buffered(1)),
            pl.BlockSpec((MOE_EPS, D_MODEL, 2 * D_EXPERT), lambda i, s: (s, 0, 0)),
            pl.BlockSpec((MOE_EPS, D_EXPERT, D_MODEL), lambda i, s: (s, 0, 0)),
        ],
        out_specs=pl.BlockSpec((tm, D_MODEL), lambda i, s: (i, 0), pipeline_mode=pl.Buffered(1)),
        out_shape=jax.ShapeDtypeStruct((n, D_MODEL), F32),
        scratch_shapes=[
            pltpu.VMEM((tm, rcap), BF16),
            pltpu.VMEM((rcap, D_MODEL), BF16),
            pltpu.VMEM((rcap, LANES), F32),
            pltpu.VMEM((rcap, D_MODEL), BF16),
            pltpu.SMEM((2 * N_GROUPS,), jnp.int32),
        ],
        compiler_params=pltpu.CompilerParams(dimension_semantics=("parallel", "arbitrary"),
                                             vmem_limit_bytes=MOE_VMEM_LIMIT),
        name="moe",
    )(x1, h2, lg, tri, wgu, wd)


def _tile(n, pref):
    t = pref
    while n % t:
        t //= 2
    return t


def _mixer_params(layer, norm1_g, w_in, a_conv_w, a_conv_b, a_gate_bias, a_out_norm_g,
                  b_qk_norm_g, b_rel_bias, c_qk_norm_g, c_lambda, c_sub_norm_g, t5_bias,
                  w_branch, w_out, nq_diff):
    n_small = 2 * A_HEADS
    cut = 4 * A_WIDTH
    w_main = jnp.concatenate([w_in[:, :cut], w_in[:, cut + n_small:]], axis=1)
    w_main = w_main.reshape(D_MODEL, N_SEG, SEG)[:, np.asarray(SEG_PERM), :]
    w_main = w_main.reshape(D_MODEL, N_SEG * SEG).astype(BF16)
    w_if = jnp.pad(w_in[:, cut:cut + n_small], ((0, 0), (0, LANES - n_small))).astype(BF16)
    gain = jnp.ones((N_SEG, SEG), F32)
    gain = gain.at[N_ROW_SEG + TSEG_BQ].set(jnp.tile(b_qk_norm_g[0], B_HEADS) * (B_DH ** -0.5 * LOG2E))
    gain = gain.at[SEG_BK].set(jnp.tile(b_qk_norm_g[1], B_HEADS))
    gain = gain.at[N_ROW_SEG + TSEG_CQ].set(
        jnp.tile(c_qk_norm_g[0], 2 * C_HEADS) * (C_DQK ** -0.5 * LOG2E))
    gain = gain.at[SEG_CK].set(jnp.tile(c_qk_norm_g[1], 2 * C_HEADS))
    diff_bias, diff_far = _diff_bias(t5_bias, _diff_nnear(nq_diff))
    lam_init = 0.8 - 0.6 * math.exp(-0.3 * layer)
    lf32 = c_lambda.astype(F32)
    lam = jnp.exp(jnp.sum(lf32[0] * lf32[1])) - jnp.exp(jnp.sum(lf32[2] * lf32[3])) + lam_init
    return dict(
        g1=norm1_g.reshape(1, D_MODEL), w_main=w_main, w_if=w_if,
        gain=gain.reshape(N_SEG, 1, SEG),
        cw=a_conv_w, cb=a_conv_b.reshape(1, -1),
        gbr=jnp.pad(a_gate_bias, (0, LANES - n_small)).reshape(1, LANES),
        gbc=a_gate_bias.reshape(n_small, 1),
        ag=a_out_norm_g.reshape(1, A_WIDTH),
        band_bias=_band_bias(b_rel_bias),
        diff_bias=diff_bias, diff_far=diff_far,
        lam=lam.reshape(1).astype(F32),
        gsub=(c_sub_norm_g * (1.0 - lam_init)).reshape(C_DV, 1),
        wb=w_branch.astype(BF16), wo=w_out.astype(BF16),
    )


def _layer(x2, bsz, seq, mp, norm2_g, w_group, b_group, w_router, b_router, w_e_gate, w_e_up, w_e_down):
    n = bsz * seq
    p, pt, gif = _inproj(x2, mp["g1"], mp["w_main"], mp["w_if"], mp["gain"], _tile(n, 512))
    gift = jnp.transpose(gif[:, :2 * A_HEADS])
    ha, hbt, hct = _mixers(p, pt, gif, gift, mp, bsz, seq)

    wr = jnp.concatenate([w_router, w_group], axis=1)
    wr = jnp.pad(wr, ((0, 0), (0, LANES - wr.shape[1]))).astype(BF16)
    br = jnp.pad(jnp.concatenate([b_router, b_group]), (0, LANES - N_EXPERTS - N_GROUPS)).reshape(1, LANES)
    x1, h2, lg = _merge(ha, hbt, hct, p, x2, mp["wb"], mp["wo"], norm2_g.reshape(1, D_MODEL), wr, br,
                        _tile(n, 512))
    wgu = jnp.concatenate([w_e_gate, w_e_up], axis=-1).astype(BF16)
    return _moe(x1, h2, lg, wgu, w_e_down.astype(BF16), _tile(n, MOE_TM))


def kernel(x, norm1_g, w_in, a_conv_w, a_conv_b, a_gate_bias, a_out_norm_g, b_qk_norm_g, b_rel_bias,
           c_qk_norm_g, c_lambda, c_sub_norm_g, t5_bias, w_branch, w_out, norm2_g, w_group, b_group,
           w_router, b_router, w_e_gate, w_e_up, w_e_down):
    bsz, seq, _ = x.shape
    assert seq % DIFF_T == 0 and seq % MLSTM_L == 0 and seq % BAND_TQ == 0
    x2 = x.reshape(bsz * seq, D_MODEL)
    for l in range(norm1_g.shape[0]):
        mp = _mixer_params(l, norm1_g[l], w_in[l], a_conv_w[l], a_conv_b[l], a_gate_bias[l],
                           a_out_norm_g[l], b_qk_norm_g[l], b_rel_bias[l], c_qk_norm_g[l], c_lambda[l],
                           c_sub_norm_g[l], t5_bias, w_branch[l], w_out[l], seq // DIFF_T)
        x2 = _layer(x2, bsz, seq, mp, norm2_g[l], w_group[l], b_group[l], w_router[l], b_router[l],
                    w_e_gate[l], w_e_up[l], w_e_down[l])
    return x2.reshape(bsz, seq, D_MODEL)
```

```python
import functools
import math

import numpy as np
import jax
import jax.numpy as jnp
from jax import lax
from jax.experimental import pallas as pl
from jax.experimental.pallas import tpu as pltpu

F32 = jnp.float32
BF16 = jnp.bfloat16

D_MODEL = 1024
CHUNK = 64
EPS = 1e-6
NEG = -1e30
LOG2E = math.log2(math.e)

A_HEADS = 4
A_DH = 128
A_WIDTH = A_HEADS * A_DH
CONV_W = 4
GATE_CAP = 15.0

B_HEADS = 8
B_DH = 64
B_WIDTH = B_HEADS * B_DH
B_LEFT_CHUNKS = 8
B_MAX_REL = 256

C_HEADS = 4
C_DQK = 64
C_DV = 2 * C_DQK
C_WIDTH = C_HEADS * C_DV

T5_BUCKETS = 32
T5_MAX_DIST = 1024

N_BRANCH = 3
BRANCH_WIDTH = 512

N_GROUPS = 4
EXPERTS_PER_GROUP = 8
N_EXPERTS = N_GROUPS * EXPERTS_PER_GROUP
D_EXPERT = D_MODEL // 4

LANES = 128
SEG = 512
N_SEG = 16
VMEM_LIMIT = 48 * 1024 * 1024

SEG_GATES, SEG_AQ, SEG_AK, SEG_BK, SEG_CK = 0, 6, 7, 8, 9
N_ROW_SEG = 10
TSEG_AV, TSEG_AO, TSEG_BQ, TSEG_BV, TSEG_CQ, TSEG_CV = 0, 1, 2, 3, 4, 5
N_T_SEG = N_SEG - N_ROW_SEG
SEG_PERM = (10, 11, 12, 13, 14, 15, 0, 1, 5, 8, 2, 3, 4, 6, 7, 9)

MLSTM_L = 128
CONV_HIST = 8
BAND_TQ = 128
BAND_NKB = 1 + (B_LEFT_CHUNKS * CHUNK) // BAND_TQ
BAND_ONES = 16
DIFF_T = 512
DIFF_ONES = 16
MOE_TM = 1024
MOE_C = 64
MOE_BIG = 4
MOE_EPS = EXPERTS_PER_GROUP
MOE_VMEM_LIMIT = 56 * 1024 * 1024

def _cparams(sem, flags=None):
    return pltpu.CompilerParams(dimension_semantics=sem, vmem_limit_bytes=VMEM_LIMIT, flags=flags)


NORM_SEGS = (SEG_BK, SEG_CK, N_ROW_SEG + TSEG_BQ, N_ROW_SEG + TSEG_CQ)
SIGMOID_SEGS = (N_ROW_SEG + TSEG_AO,) + tuple(range(SEG_GATES, SEG_GATES + 6))


def _head_norm_t(acc_t):
    rows, tm = acc_t.shape
    a3 = acc_t.reshape(rows // 64, 64, tm)
    ssq = jnp.sum(a3 * a3, axis=1, keepdims=True)
    return (a3 * lax.rsqrt(ssq * (1.0 / 64.0) + EPS)).reshape(rows, tm)


def _inproj_kernel(x_ref, g_ref, w_ref, wif_ref, gain_ref, gain_t_ref, p_ref, pt_ref, gif_ref):
    xf = x_ref[...]
    xn = (xf * lax.rsqrt(jnp.mean(xf * xf, axis=-1, keepdims=True) + EPS) * g_ref[...]).astype(BF16)
    gif_ref[...] = jnp.dot(xn, wif_ref[...], preferred_element_type=F32)
    for j in range(N_SEG):
        cols = slice(j * SEG, (j + 1) * SEG)
        acc = jnp.dot(xn, w_ref[:, cols], preferred_element_type=F32)
        if j in NORM_SEGS:
            acc_t = _head_norm_t(jnp.transpose(acc))
            if j < N_ROW_SEG:
                p_ref[:, cols] = (jnp.transpose(acc_t) * gain_ref[j]).astype(BF16)
            else:
                pt_ref[j - N_ROW_SEG] = (acc_t * gain_t_ref[NORM_SEGS.index(j) - 2]).astype(BF16)
            continue
        if j in SIGMOID_SEGS:
            acc = jax.nn.sigmoid(acc)
        if j < N_ROW_SEG:
            p_ref[:, cols] = acc.astype(BF16)
        else:
            pt_ref[j - N_ROW_SEG] = jnp.transpose(acc).astype(BF16)


def _inproj(x2, g, w, wif, gain, tm):
    n = x2.shape[0]
    tsegs = np.asarray([N_ROW_SEG + TSEG_BQ, N_ROW_SEG + TSEG_CQ])
    gain_t = jnp.broadcast_to(gain[tsegs, 0, :, None], (2, SEG, tm))

    def const(shape):
        return pl.BlockSpec(shape, lambda i: (0,) * len(shape), pipeline_mode=pl.Buffered(1))

    return pl.pallas_call(
        _inproj_kernel,
        grid=(n // tm,),
        in_specs=[
            pl.BlockSpec((tm, D_MODEL), lambda i: (i, 0)),
            const((1, D_MODEL)),
            const((D_MODEL, N_SEG * SEG)),
            const((D_MODEL, LANES)),
            const((N_SEG, 1, SEG)),
            const((2, SEG, tm)),
        ],
        out_specs=[
            pl.BlockSpec((tm, N_ROW_SEG * SEG), lambda i: (i, 0)),
            pl.BlockSpec((N_T_SEG, SEG, tm), lambda i: (0, 0, i)),
            pl.BlockSpec((tm, LANES), lambda i: (i, 0)),
        ],
        out_shape=[
            jax.ShapeDtypeStruct((n, N_ROW_SEG * SEG), BF16),
            jax.ShapeDtypeStruct((N_T_SEG, SEG, n), BF16),
            jax.ShapeDtypeStruct((n, LANES), F32),
        ],
        compiler_params=_cparams(("parallel",)),
        name="inproj",
    )(x2, g, w, wif, gain, gain_t)


def _log_sigmoid(z):
    return jnp.minimum(z, 0.0) - jnp.log(1.0 + jnp.exp(-jnp.abs(z)))


def _split3(a):
    hi = a.astype(BF16)
    r1 = a - hi.astype(F32)
    mid = r1.astype(BF16)
    lo = (r1 - mid.astype(F32)).astype(BF16)
    return hi, mid, lo


def _mlstm_kernel(*refs):
    _mlstm_carry(*refs)
    for _ in _mlstm_chunk(*refs):
        pass


def _mlstm_carry(aq_ref, ak_ref, vt_ref, aot_ref, gif_ref, gift_ref, cw_ref, cb_ref,
                 gbr_ref, gbc_ref, agt_ref, out_ref, ubuf, kq_ref, st_ref, ct_ref, n_ref, m_ref,
                 *, chunk=None):
    L = MLSTM_L
    c = pl.program_id(1) if chunk is None else chunk

    @pl.when(c == 0)
    def _():
        ubuf[0:CONV_HIST, :] = jnp.zeros((CONV_HIST, 2 * A_WIDTH), F32)
        ct_ref[...] = jnp.zeros_like(ct_ref)
        n_ref[...] = jnp.zeros_like(n_ref)
        m_ref[...] = jnp.zeros_like(m_ref)

    @pl.when(c > 0)
    def _():
        ubuf[0:CONV_HIST, :] = ubuf[L:L + CONV_HIST, :]


def _mlstm_chunk(aq_ref, ak_ref, vt_ref, aot_ref, gif_ref, gift_ref, cw_ref, cb_ref,
                 gbr_ref, gbc_ref, agt_ref, out_ref, ubuf, kq_ref, st_ref, ct_ref, n_ref, m_ref):
    L = MLSTM_L

    H = CONV_HIST
    ubuf[H:L + H, 0:A_WIDTH] = aq_ref[...].astype(F32)
    ubuf[H:L + H, A_WIDTH:2 * A_WIDTH] = ak_ref[...].astype(F32)
    y = cb_ref[...] + cw_ref[0:1, :] * ubuf[H:L + H, :]
    for t in range(1, CONV_W):
        y = y + cw_ref[t:t + 1, :] * ubuf[H - t:H - t + L, :]
    qk = y * jax.nn.sigmoid(y)
    q_t = jnp.transpose(qk[:, 0:A_WIDTH]).astype(BF16)
    k_all = (qk[:, A_WIDTH:2 * A_WIDTH] * (A_DH ** -0.5)).astype(BF16)

    zc = gif_ref[...] + gbr_ref[...]
    ig_c = GATE_CAP * jnp.tanh(zc * (1.0 / GATE_CAP))
    lf_c = _log_sigmoid(zc)
    zr = gift_ref[...] + gbc_ref[...]
    ig_r = GATE_CAP * jnp.tanh(zr * (1.0 / GATE_CAP))
    lf_r = _log_sigmoid(zr)

    row = lax.broadcasted_iota(jnp.int32, (L, L), 0)
    col = lax.broadcasted_iota(jnp.int32, (L, L), 1)
    causal = col <= row
    tril = jnp.where(causal, 1.0, 0.0).astype(BF16)
    triu = jnp.where(row <= col, 1.0, 0.0).astype(BF16)
    b_c = sum(jnp.dot(tril, piece, preferred_element_type=F32) for piece in _split3(lf_c))
    b_r = sum(jnp.dot(piece, triu, preferred_element_type=F32) for piece in _split3(lf_r))

    sub8 = lax.broadcasted_iota(jnp.int32, (8, L), 0)
    heads = [slice(h * A_DH, (h + 1) * A_DH) for h in range(A_HEADS)]
    for h, rows in enumerate(heads):
        kq_ref[h] = jnp.dot(k_all[:, rows], q_t[rows, :], preferred_element_type=F32)
    yield

    stats = []
    for h, rows in enumerate(heads):
        bcol = b_c[:, A_HEADS + h:A_HEADS + h + 1]
        brow = b_r[A_HEADS + h:A_HEADS + h + 1, :]
        igcol = ig_c[:, h:h + 1]
        m_prev = m_ref[h][:, 0:1]
        dmat = jnp.where(row <= col, brow + (igcol - bcol), NEG)
        inter = brow + m_prev
        m_t = jnp.maximum(inter, jnp.max(dmat, axis=0, keepdims=True))
        st = kq_ref[h] * jnp.exp(dmat - m_t)
        st_ref[h] = st.astype(BF16)
        stats.append((brow, m_prev, m_t, jnp.exp(inter - m_t), jnp.sum(st, axis=0, keepdims=True)))
        if h % 2 == 1:
            yield

    for h, rows in enumerate(heads):
        brow, m_prev, m_t, w_inter, st_sum = stats[h]
        qt = q_t[rows, :]
        k = k_all[:, rows]
        vt = vt_ref[rows, :]
        igrow = ig_r[h:h + 1, :]
        b_last = brow[:, L - 1:L]
        ct = ct_ref[h]
        n8 = n_ref[h]

        num = w_inter * jnp.dot(ct.astype(BF16), qt, preferred_element_type=F32)
        num = num + jnp.dot(vt, st_ref[h], preferred_element_type=F32)
        nq = jnp.dot(n8.astype(BF16), qt, preferred_element_type=F32)[0:1, :]
        den = w_inter * nq + st_sum
        hh = num * (1.0 / jnp.maximum(jnp.abs(den), jnp.exp(-m_t)))

        g_end = b_last - brow + igrow
        m_new = jnp.maximum(b_last + m_prev, jnp.max(g_end, axis=-1, keepdims=True))
        dec = jnp.exp(b_last + m_prev - m_new)
        w_end = jnp.exp(g_end - m_new)
        vw = (vt.astype(F32) * w_end).astype(BF16)
        ct_ref[h] = dec * ct + jnp.dot(vw, k, preferred_element_type=F32)
        w8 = jnp.where(sub8 == 0, w_end, 0.0).astype(BF16)
        n_ref[h] = dec * n8 + jnp.dot(w8, k, preferred_element_type=F32)
        m_ref[h] = jnp.broadcast_to(m_new, (1, LANES))

        hn = hh * lax.rsqrt(jnp.mean(hh * hh, axis=0, keepdims=True) + EPS) * agt_ref[rows, :]
        out_ref[rows, :] = (hn * aot_ref[rows, :].astype(F32)).astype(BF16)
        if h % 2 == 1 and h + 1 < A_HEADS:
            yield


def _mlstm_call(p, pt, gif, gift, cw, cb, gbr, gbc, ag, bsz, seq):
    L = MLSTM_L
    nc = seq // L
    n = bsz * seq
    agt = jnp.broadcast_to(ag.reshape(A_WIDTH, 1), (A_WIDTH, L))

    def tseg(j):
        return pl.BlockSpec((None, SEG, L), lambda b, c: (j, 0, b * nc + c))

    def full(shape):
        return pl.BlockSpec(shape, lambda b, c: (0,) * len(shape))

    in_specs = [
        pl.BlockSpec((L, SEG), lambda b, c: (b * nc + c, SEG_AQ)),
        pl.BlockSpec((L, SEG), lambda b, c: (b * nc + c, SEG_AK)),
        tseg(TSEG_AV), tseg(TSEG_AO),
        pl.BlockSpec((L, LANES), lambda b, c: (b * nc + c, 0)),
        pl.BlockSpec((8, L), lambda b, c: (0, b * nc + c)),
        full((CONV_W, 2 * A_WIDTH)), full((1, 2 * A_WIDTH)),
        full((1, LANES)), full((8, 1)), full((A_WIDTH, L)),
    ]
    scratch = [
        pltpu.VMEM((L + CONV_HIST, 2 * A_WIDTH), F32),
        pltpu.VMEM((A_HEADS, L, L), F32),
        pltpu.VMEM((A_HEADS, L, L), BF16),
        pltpu.VMEM((A_HEADS, A_DH, A_DH), F32),
        pltpu.VMEM((A_HEADS, 8, A_DH), F32),
        pltpu.VMEM((A_HEADS, 1, LANES), F32),
    ]
    return (in_specs, pl.BlockSpec((A_WIDTH, L), lambda b, c: (0, b * nc + c)),
            jax.ShapeDtypeStruct((A_WIDTH, n), BF16), scratch,
            (p, p, pt, pt, gif, gift, cw, cb, gbr, gbc, agt))


def _mlstm(p, pt, gif, gift, cw, cb, gbr, gbc, ag, bsz, seq):
    in_specs, out_spec, out_shape, scratch, operands = _mlstm_call(
        p, pt, gif, gift, cw, cb, gbr, gbc, ag, bsz, seq)
    return pl.pallas_call(
        _mlstm_kernel, grid=(bsz, seq // MLSTM_L), in_specs=in_specs, out_specs=out_spec,
        out_shape=out_shape, scratch_shapes=scratch,
        compiler_params=_cparams(("parallel", "arbitrary")), name="mlstm",
    )(*operands)


def _interleave(*gens):
    gens = list(gens)
    while gens:
        for g in list(gens):
            try:
                next(g)
            except StopIteration:
                gens.remove(g)
                continue
            yield


def _band_steps(refs, i, mask_start):
    nkb = BAND_NKB
    qt_ref = refs[0]
    k_refs = refs[1:1 + nkb]
    vt_refs = refs[1 + nkb:1 + 2 * nkb]
    bias_ref = refs[1 + 2 * nkb]
    out_ref = refs[2 + 2 * nkb]
    s_ref, mx_ref = refs[3 + 2 * nkb:5 + 2 * nkb]
    tq = BAND_TQ
    nk = nkb * tq

    k_all = jnp.concatenate([r[...] for r in k_refs], axis=0)
    vt_all = jnp.concatenate([r[...] for r in vt_refs], axis=1)
    ones = jnp.ones((BAND_ONES, nk), BF16)
    row = lax.broadcasted_iota(jnp.int32, (LANES, tq), 0)
    lo = row < B_DH
    if mask_start:
        kidx = lax.broadcasted_iota(jnp.int32, (nk, 1), 0)
        valid = (kidx + (i - (nkb - 1)) * tq) >= 0

    def score(p):
        rows = slice(p * LANES, (p + 1) * LANES)
        qtp = qt_ref[rows, :]
        zero = jnp.zeros_like(qtp)
        qbd = jnp.concatenate([jnp.where(lo, qtp, zero), jnp.where(lo, zero, qtp)], axis=1)
        s = jnp.dot(k_all[:, rows], qbd, preferred_element_type=F32) + bias_ref[p]
        if mask_start:
            s = jnp.where(valid, s, NEG)
        s_ref[p & 1] = s
        mx_ref[p & 1] = jnp.max(s, axis=0, keepdims=True)

    def finish(p):
        rows = slice(p * LANES, (p + 1) * LANES)
        pr = jnp.exp2((s_ref[p & 1] - mx_ref[p & 1]).astype(BF16))
        o = jnp.dot(jnp.concatenate([vt_all[rows, :], ones], axis=0), pr,
                    preferred_element_type=F32)
        o = o[0:LANES, :] / o[LANES:LANES + 1, :]
        out_ref[rows, :] = jnp.where(lo, o[:, 0:tq], o[:, tq:2 * tq]).astype(BF16)

    score(0)
    yield
    for p in range(B_HEADS // 2):
        if p + 1 < B_HEADS // 2:
            score(p + 1)
        finish(p)
        if p + 1 < B_HEADS // 2:
            yield


def _band_variants(i, region):
    @pl.when(i < BAND_NKB - 1)
    def _():
        region(True)

    @pl.when(i >= BAND_NKB - 1)
    def _():
        region(False)


def _drain(gen):
    for _ in gen:
        pass


def _band_kernel(*refs):
    i = pl.program_id(1)
    _band_variants(i, lambda mask: _drain(_band_steps(refs, i, mask)))


def _band_call(p, pt, bias, bsz, seq):
    tq = BAND_TQ
    nkb = BAND_NKB
    nq = seq // tq
    n = bsz * seq

    def kblk(d):
        return pl.BlockSpec((tq, SEG), lambda b, i: (b * nq + jnp.maximum(i - d, 0), SEG_BK))

    def vblk(d):
        return pl.BlockSpec((None, SEG, tq), lambda b, i: (TSEG_BV, 0, b * nq + jnp.maximum(i - d, 0)))

    in_specs = [pl.BlockSpec((None, SEG, tq), lambda b, i: (TSEG_BQ, 0, b * nq + i))]
    in_specs += [kblk(d) for d in range(nkb - 1, -1, -1)]
    in_specs += [vblk(d) for d in range(nkb - 1, -1, -1)]
    in_specs += [pl.BlockSpec(bias.shape, lambda b, i: (0, 0, 0))]
    scratch = [
        pltpu.VMEM((2, nkb * tq, 2 * tq), F32),
        pltpu.VMEM((2, 1, 2 * tq), F32),
    ]
    return (in_specs, pl.BlockSpec((B_WIDTH, tq), lambda b, i: (0, b * nq + i)),
            jax.ShapeDtypeStruct((B_WIDTH, n), BF16), scratch,
            (pt, *([p] * nkb), *([pt] * nkb), bias))


def _band(p, pt, bias, bsz, seq):
    in_specs, out_spec, out_shape, scratch, operands = _band_call(p, pt, bias, bsz, seq)
    return pl.pallas_call(
        _band_kernel, grid=(bsz, seq // BAND_TQ), in_specs=in_specs, out_specs=out_spec,
        out_shape=out_shape, scratch_shapes=scratch,
        compiler_params=_cparams(("parallel", "parallel")), name="band_attn",
    )(*operands)


def _toeplitz(base, m, n):
    period = base.shape[-1]
    assert n <= period - 1
    reps = (1,) * (base.ndim - 1) + (m,)
    big = jnp.tile(base, reps)[..., :m * (period - 1)]
    return big.reshape(base.shape[:-1] + (m, period - 1))[..., :n]


def _band_bias(b_rel):
    tq = BAND_TQ
    nk = BAND_NKB * tq
    period = tq + nk
    e = np.arange(period)
    e = np.where(e < nk, e, e - period)
    rel = np.clip((nk - tq) - e, -B_MAX_REL, B_MAX_REL) + B_MAX_REL
    bias = _toeplitz(jnp.transpose(b_rel[rel]).astype(F32), tq, nk)
    qpos = np.arange(tq)
    kpos = np.arange(nk) - (nk - tq)
    qc = qpos[:, None] // CHUNK
    kc = np.floor_divide(kpos[None, :], CHUNK)
    allowed = (kc <= qc) & (kc >= qc - B_LEFT_CHUNKS)
    bias = jnp.where(allowed[None], bias * LOG2E, NEG)
    return jnp.swapaxes(bias.reshape(B_HEADS // 2, 2 * tq, nk), 1, 2)


def _diff_kernel(lam_ref, cfar_ref, qt_ref, k_ref, vt_ref, bias_ref, g_ref, out_ref,
                 qbd_ref, s_ref, mx_ref, acc_ref, m_ref, *, nnear, first_region=None):
    T = DIFF_T
    h = pl.program_id(1)
    qi = pl.program_id(2)
    cfar = cfar_ref[h]

    def prologue():
        qt = qt_ref[...]
        row = lax.broadcasted_iota(jnp.int32, (2 * C_DQK, T), 0)
        zero = jnp.zeros_like(qt)
        qbd_ref[:, 0:T] = jnp.where(row < C_DQK, qt, zero)
        qbd_ref[:, T:2 * T] = jnp.where(row < C_DQK, zero, qt)
        m_ref[...] = jnp.full(m_ref.shape, NEG, F32)
        acc_ref[...] = jnp.zeros_like(acc_ref)

    def near_bias(t):
        return bias_ref[t] if t < nnear else None

    def stage_a(t, slot, bias, maps=(0, 1)):
        j = jnp.maximum(qi - t, 0)
        k = k_ref[pl.ds(pl.multiple_of(j * T, T), T), :]
        for mp in maps:
            sm = jnp.dot(k, qbd_ref[:, mp * T:(mp + 1) * T], preferred_element_type=F32)
            if bias is not None:
                sm = sm + bias
            s_ref[slot, mp] = sm
            mx = jnp.max(sm, axis=0, keepdims=True)
            mx_ref[slot, mp] = mx + cfar if bias is None else mx

    def stage_bc(t, slot, far, maps=(0, 1)):
        j = qi - t
        vt = jnp.concatenate([vt_ref[:, pl.ds(pl.multiple_of(j * T, T), T)],
                              jnp.ones((DIFF_ONES, T), BF16)], axis=0)
        for mp in maps:
            m_old = m_ref[mp]
            m_new = jnp.maximum(m_old, mx_ref[slot, mp])
            shift = m_new - cfar if far else m_new
            pr = jnp.exp2((s_ref[slot, mp] - shift).astype(BF16))
            m_ref[mp] = m_new
            acc_ref[mp] = jnp.exp2(m_old - m_new) * acc_ref[mp] + jnp.dot(
                vt, pr, preferred_element_type=F32)

    def full_step(t, slot, far, next_bias):
        for mp in range(2):
            stage_a(t + 1, 1 - slot, next_bias, (mp,))
            stage_bc(t, slot, far, (mp,))

    def first_steps(other, n_steps=1):
        prologue()
        next(other, None)
        stage_a(0, 0, near_bias(0))
        next(other, None)
        for t in range(n_steps):
            for mp in range(2):
                stage_a(t + 1, 1 - (t & 1), near_bias(t + 1), (mp,))
                stage_bc(t, t & 1, False, (mp,))
                next(other, None)
        _drain(other)

    all_near = qi >= nnear - 1
    if first_region is None:
        first_steps(iter(()))
        near_done = False
    else:
        near_done = first_region(first_steps, all_near, nnear)
    for t in range(1, nnear):
        @pl.when((qi >= t) & jnp.logical_not(near_done))
        def _(t=t):
            full_step(t, t & 1, False, near_bias(t + 1))

    @pl.when(qi >= nnear)
    def _():
        n_full = qi - nnear
        s0 = nnear & 1

        def pair(i, carry):
            full_step(nnear + 2 * i, s0, True, None)
            full_step(nnear + 2 * i + 1, 1 - s0, True, None)
            return carry

        lax.fori_loop(0, n_full // 2, pair, 0)

        @pl.when(n_full % 2 == 1)
        def _():
            full_step(qi - 1, s0, True, None)

        stage_bc(qi, qi & 1, True)

    o1 = acc_ref[0, 0:C_DV, :] / acc_ref[0, C_DV:C_DV + 1, :]
    o2 = acc_ref[1, 0:C_DV, :] / acc_ref[1, C_DV:C_DV + 1, :]
    o = o1 - lam_ref[0] * o2
    on = o * lax.rsqrt(jnp.mean(o * o, axis=0, keepdims=True) + EPS) * g_ref[...]
    out_ref[...] = on.astype(BF16)


def _diff_nnear(nq):
    d_sat = -(-(T5_MAX_DIST - 1 + DIFF_T) // DIFF_T)
    return min(d_sat, nq)


def _diff_call(p, pt, bias, cfar, lam, gsub, bsz, seq):
    T = DIFF_T
    nq = seq // T
    nnear = bias.shape[1]
    kcol = SEG_CK * SEG // LANES
    n = bsz * seq
    in_specs = [
        pl.BlockSpec(memory_space=pltpu.SMEM),
        pl.BlockSpec(memory_space=pltpu.SMEM),
        pl.BlockSpec((None, 2 * C_DQK, T), lambda b, h, i: (TSEG_CQ, h, b * nq + i)),
        pl.BlockSpec((seq, LANES), lambda b, h, i: (b, kcol + h)),
        pl.BlockSpec((None, C_DV, seq), lambda b, h, i: (TSEG_CV, h, b)),
        pl.BlockSpec((None, nnear, T, T), lambda b, h, i: (h, 0, 0, 0)),
        pl.BlockSpec((C_DV, 1), lambda b, h, i: (0, 0)),
    ]
    scratch = [
        pltpu.VMEM((2 * C_DQK, 2 * T), BF16),
        pltpu.VMEM((2, 2, T, T), F32),
        pltpu.VMEM((2, 2, 1, T), F32),
        pltpu.VMEM((2, C_DV + DIFF_ONES, T), F32),
        pltpu.VMEM((2, 1, T), F32),
    ]
    return (in_specs, pl.BlockSpec((C_DV, T), lambda b, h, i: (h, b * nq + i)),
            jax.ShapeDtypeStruct((C_WIDTH, n), BF16), scratch, (lam, cfar, pt, p, pt, bias, gsub))


def _diff(p, pt, bias, cfar, lam, gsub, bsz, seq):
    in_specs, out_spec, out_shape, scratch, operands = _diff_call(p, pt, bias, cfar, lam, gsub, bsz, seq)
    return pl.pallas_call(
        functools.partial(_diff_kernel, nnear=bias.shape[1]),
        grid=(bsz, C_HEADS, seq // DIFF_T), in_specs=in_specs, out_specs=out_spec,
        out_shape=out_shape, scratch_shapes=scratch,
        compiler_params=_cparams(("parallel", "parallel", "arbitrary")), name="diff_attn",
    )(*operands)


def _mixers_kernel(*refs, n_in, n_scr, nnear):
    d_in, m_in, b_in = (refs[sum(n_in[:k]):sum(n_in[:k + 1])] for k in range(3))
    hct_ref, hat_ref, hbt_ref = refs[sum(n_in):sum(n_in) + 3]
    scr = refs[sum(n_in) + 3:]
    d_scr, m_scr, b_scr = (scr[sum(n_scr[:k]):sum(n_scr[:k + 1])] for k in range(3))
    m_refs = (*m_in, hat_ref, *m_scr)
    b_refs = (*b_in, hbt_ref, *b_scr)
    chunk = pl.program_id(1) * pl.num_programs(2) + pl.program_id(2)
    _mlstm_carry(*m_refs, chunk=chunk)

    def first_region(trace, all_near, n_near):
        def other(mask):
            return _interleave(_band_steps(b_refs, chunk, mask), _mlstm_chunk(*m_refs))

        masked = chunk < BAND_NKB - 1
        wide = jnp.logical_not(masked) & all_near

        @pl.when(masked)
        def _():
            trace(other(True), 1)

        @pl.when(wide)
        def _():
            trace(other(False), n_near)

        @pl.when(jnp.logical_not(masked) & jnp.logical_not(all_near))
        def _():
            trace(other(False), 1)

        return wide

    _diff_kernel(*d_in, hct_ref, *d_scr, nnear=nnear, first_region=first_region)


def _mixers(p, pt, gif, gift, mp, bsz, seq):
    nq = seq // DIFF_T
    assert C_HEADS * nq == seq // MLSTM_L and MLSTM_L == BAND_TQ

    def on_diff_grid(spec):
        if spec.index_map is None:
            return spec
        return pl.BlockSpec(spec.block_shape, lambda b, h, i, f=spec.index_map: f(b, h * nq + i))

    d_specs, d_out, d_shape, d_scr, d_ops = _diff_call(p, pt, mp["diff_bias"], mp["diff_far"], mp["lam"],
                                                       mp["gsub"], bsz, seq)
    m_specs, m_out, m_shape, m_scr, m_ops = _mlstm_call(p, pt, gif, gift, mp["cw"], mp["cb"], mp["gbr"],
                                                        mp["gbc"], mp["ag"], bsz, seq)
    b_specs, b_out, b_shape, b_scr, b_ops = _band_call(p, pt, mp["band_bias"], bsz, seq)
    hct, hat, hbt = pl.pallas_call(
        functools.partial(_mixers_kernel, n_in=(len(d_specs), len(m_specs), len(b_specs)),
                          n_scr=(len(d_scr), len(m_scr), len(b_scr)), nnear=mp["diff_bias"].shape[1]),
        grid=(bsz, C_HEADS, nq),
        in_specs=d_specs + [on_diff_grid(s) for s in m_specs + b_specs],
        out_specs=[d_out, on_diff_grid(m_out), on_diff_grid(b_out)],
        out_shape=[d_shape, m_shape, b_shape],
        scratch_shapes=d_scr + m_scr + b_scr,
        compiler_params=_cparams(("parallel", "arbitrary", "arbitrary")),
        name="mixers",
    )(*d_ops, *m_ops, *b_ops)
    return hat, hbt, hct


def _t5_bucket(rel):
    nb = T5_BUCKETS // 2
    max_exact = nb // 2
    ret = (rel > 0).astype(jnp.int32) * nb
    n = jnp.abs(rel)
    large = max_exact + (jnp.log(jnp.maximum(n, max_exact).astype(F32) / max_exact)
                         / math.log(T5_MAX_DIST / max_exact) * (nb - max_exact)).astype(jnp.int32)
    large = jnp.minimum(large, nb - 1)
    return ret + jnp.where(n < max_exact, n, large)


def _diff_bias(t5_table, nnear):
    T = DIFF_T
    e = np.arange(2 * T)
    amc = np.where(e < T, -e, 2 * T - e)
    rel = jnp.asarray(-np.arange(nnear)[:, None] * T + amc[None, :], jnp.int32)
    base = jnp.moveaxis(t5_table[_t5_bucket(rel)], -1, 0).astype(F32) * LOG2E
    tiles = _toeplitz(base, T, T)
    a = np.arange(T)[:, None]
    c = np.arange(T)[None, :]
    allowed = np.ones((nnear, T, T), bool)
    allowed[0] = (a // CHUNK) <= (c // CHUNK)
    far = t5_table[_t5_bucket(jnp.asarray(-T5_MAX_DIST, jnp.int32))].astype(F32) * LOG2E
    return jnp.where(allowed[None], tiles, NEG), far


def _merge_kernel(hat_ref, hbt_ref, hct_ref, g0_ref, g1_ref, g2_ref, x_ref, wb_ref, wo_ref,
                  n2_ref, wr_ref, br_ref, x1_ref, h2_ref, lg_ref, y_ref):
    tn = (((0,), (0,)), ((), ()))
    tm = x_ref.shape[0]
    halves = [slice(i * (tm // 2), (i + 1) * (tm // 2)) for i in range(2)]
    for r in halves:
        y = g0_ref[r, :].astype(F32) * lax.dot_general(hat_ref[:, r], wb_ref[0], tn,
                                                       preferred_element_type=F32)
        y = y + g1_ref[r, :].astype(F32) * lax.dot_general(hbt_ref[:, r], wb_ref[1], tn,
                                                           preferred_element_type=F32)
        y = y + g2_ref[r, :].astype(F32) * lax.dot_general(hct_ref[:, r], wb_ref[2], tn,
                                                           preferred_element_type=F32)
        y_ref[r, :] = y.astype(BF16)
    for r in halves:
        x1 = x_ref[r, :] + jnp.dot(y_ref[r, :], wo_ref[...], preferred_element_type=F32)
        x1_ref[r, :] = x1
        h2 = x1 * lax.rsqrt(jnp.mean(x1 * x1, axis=-1, keepdims=True) + EPS) * n2_ref[...]
        h2_ref[r, :] = h2.astype(BF16)
    for r in halves:
        lg_ref[r, :] = jnp.dot(h2_ref[r, :], wr_ref[...], preferred_element_type=F32) + br_ref[...]


def _merge(ha, hbt, hct, p, x2, wb, wo, n2, wr, br, tm):
    n = x2.shape[0]
    gcol = SEG_GATES * SEG // D_MODEL

    def rows(width, col=0):
        return pl.BlockSpec((tm, width), lambda i: (i, col))

    def cols():
        return pl.BlockSpec((BRANCH_WIDTH, tm), lambda i: (0, i))

    def full(shape):
        return pl.BlockSpec(shape, lambda i: (0,) * len(shape))

    return pl.pallas_call(
        _merge_kernel,
        grid=(n // tm,),
        in_specs=[
            cols(), cols(), cols(),
            rows(D_MODEL, gcol), rows(D_MODEL, gcol + 1), rows(D_MODEL, gcol + 2),
            rows(D_MODEL),
            full((N_BRANCH, BRANCH_WIDTH, D_MODEL)), full((D_MODEL, D_MODEL)),
            full((1, D_MODEL)), full((D_MODEL, LANES)), full((1, LANES)),
        ],
        out_specs=[rows(D_MODEL), rows(D_MODEL), rows(LANES)],
        out_shape=[
            jax.ShapeDtypeStruct((n, D_MODEL), F32),
            jax.ShapeDtypeStruct((n, D_MODEL), BF16),
            jax.ShapeDtypeStruct((n, LANES), F32),
        ],
        scratch_shapes=[pltpu.VMEM((tm, D_MODEL), BF16)],
        compiler_params=_cparams(("parallel",)),
        name="merge",
    )(ha, hbt, hct, p, p, p, x2, wb, wo, n2, wr, br)


def _combine_weights(lg):
    lanef = lax.broadcasted_iota(jnp.int32, lg.shape, 1).astype(F32)
    big = 1e9
    is_g = (lanef >= N_EXPERTS) & (lanef < N_EXPERTS + N_GROUPS)
    gl = jnp.where(is_g, lg, -jnp.inf)
    gmax = jnp.max(gl, axis=-1, keepdims=True)
    g_idx = jnp.min(jnp.where(gl == gmax, lanef - N_EXPERTS, big), axis=-1, keepdims=True)
    p_g = 1.0 / jnp.sum(jnp.exp(gl - gmax), axis=-1, keepdims=True)
    in_grp = (lanef >= g_idx * EXPERTS_PER_GROUP) & (lanef < (g_idx + 1.0) * EXPERTS_PER_GROUP)
    el = jnp.where(in_grp, lg, -jnp.inf)
    ee = jnp.exp(el - jnp.max(el, axis=-1, keepdims=True))
    ep = ee / jnp.sum(ee, axis=-1, keepdims=True)
    ep = jnp.where(in_grp, ep, -1.0)
    v1 = jnp.max(ep, axis=-1, keepdims=True)
    i1 = jnp.min(jnp.where(ep == v1, lanef, big), axis=-1, keepdims=True)
    ep2 = jnp.where(lanef == i1, -1.0, ep)
    v2 = jnp.max(ep2, axis=-1, keepdims=True)
    i2 = jnp.min(jnp.where(ep2 == v2, lanef, big), axis=-1, keepdims=True)
    tot = v1 + v2
    comb = jnp.where(lanef == i1, p_g * (v1 / tot), 0.0) + jnp.where(lanef == i2, p_g * (v2 / tot), 0.0)
    return comb, g_idx


def _moe_kernel(x1_ref, h2_ref, lg_ref, tri_ref, wgu_ref, wd_ref, out_ref,
                pt_ref, xs_ref, ws_ref, ys_ref, meta_ref, *, rcap):
    C = MOE_C
    s = pl.program_id(1)
    tm = h2_ref.shape[0]

    @pl.when(s == 0)
    def _():
        comb, g_idx = _combine_weights(lg_ref[...])
        lanef = lax.broadcasted_iota(jnp.int32, comb.shape, 1).astype(F32)
        mine = lanef == g_idx
        onehot = jnp.where(mine, 1.0, 0.0)
        ranks = jnp.dot(tri_ref[...], onehot.astype(BF16), preferred_element_type=F32)
        dest = jnp.sum(jnp.where(mine, ranks, 0.0), axis=-1, keepdims=True)
        off = jnp.int32(0)
        for g in range(N_GROUPS):
            cnt = jnp.sum(onehot[:, g:g + 1]).astype(jnp.int32)
            nchunk = (cnt + (C - 1)) // C
            meta_ref[g] = off
            meta_ref[N_GROUPS + g] = nchunk
            dest = dest + jnp.where(g_idx == float(g), (off * C).astype(F32), 0.0)
            off = off + nchunk
        slot = lax.broadcasted_iota(jnp.int32, (tm, rcap), 1).astype(F32)
        pt = jnp.where(dest == slot, 1.0, 0.0).astype(BF16)
        pt_ref[...] = pt
        comb_hi = comb.astype(BF16)
        comb_lo = (comb - comb_hi.astype(F32)).astype(BF16)
        packed = jnp.concatenate([h2_ref[...], comb_hi, comb_lo], axis=1)
        srt = lax.dot_general(pt, packed, (((0,), (0,)), ((), ())), preferred_element_type=F32)
        xs_ref[...] = srt[:, 0:D_MODEL].astype(BF16)
        ws_ref[...] = srt[:, D_MODEL:D_MODEL + LANES] + srt[:, D_MODEL + LANES:D_MODEL + 2 * LANES]
        ys_ref[...] = jnp.zeros_like(ys_ref)

    def do_chunk(r0, m):
        r0 = pl.multiple_of(r0, C)
        xc = xs_ref[pl.ds(r0, m), :]
        wsc = ws_ref[pl.ds(r0, m), :]
        lane = lax.broadcasted_iota(jnp.int32, (m, LANES), 1)
        y = None
        for k in range(MOE_EPS):
            gu = jnp.dot(xc, wgu_ref[k], preferred_element_type=F32)
            gate = gu[:, 0:D_EXPERT]
            wk = jnp.sum(jnp.where(lane == s * MOE_EPS + k, wsc, 0.0), axis=-1, keepdims=True)
            he = (gate * jax.nn.sigmoid(gate) * gu[:, D_EXPERT:2 * D_EXPERT] * wk).astype(BF16)
            yk = jnp.dot(he, wd_ref[k], preferred_element_type=F32)
            y = yk if y is None else y + yk
        ys_ref[pl.ds(r0, m), :] = y.astype(BF16)

    g = s // (EXPERTS_PER_GROUP // MOE_EPS)
    start = meta_ref[g]
    nchunk = meta_ref[N_GROUPS + g]

    def big(i, carry):
        do_chunk((start + MOE_BIG * i) * C, MOE_BIG * C)
        return carry

    common = nchunk == MOE_BIG + 1

    @pl.when(common)
    def _():
        do_chunk(start * C, (MOE_BIG + 1) * C)

    @pl.when(jnp.logical_not(common))
    def _():
        nbig = nchunk // MOE_BIG
        lax.fori_loop(0, nbig, big, 0)
        done = nbig * MOE_BIG
        size = MOE_BIG // 2
        while size >= 1:
            @pl.when((nchunk & size) != 0)
            def _(done=done, size=size):
                do_chunk((start + done) * C, size * C)
            done = done + (nchunk & size)
            size //= 2

    @pl.when(s == pl.num_programs(1) - 1)
    def _():
        out_ref[...] = x1_ref[...] + jnp.dot(pt_ref[...], ys_ref[...], preferred_element_type=F32)


def _moe(x1, h2, lg, wgu, wd, tm):
    n = x1.shape[0]
    rcap = (tm + N_GROUPS * (MOE_C - 1)) // MOE_C * MOE_C
    rcap = -(-rcap // LANES) * LANES
    idx = np.arange(tm)
    tri = jnp.asarray(idx[None, :] < idx[:, None], BF16)
    return pl.pallas_call(
        functools.partial(_moe_kernel, rcap=rcap),
        grid=(n // tm, N_EXPERTS // MOE_EPS),
        in_specs=[
            pl.BlockSpec((tm, D_MODEL), lambda i, s: (i, 0), pipeline_mode=pl.Buffered(1)),
            pl.BlockSpec((tm, D_MODEL), lambda i, s: (i, 0), pipeline_mode=pl.Buffered(1)),
            pl.BlockSpec((tm, LANES), lambda i, s: (i, 0), pipeline_mode=pl.Buffered(1)),
            pl.BlockSpec((tm, tm), lambda i, s: (0, 0), pipeline_mode=pl.Buffered(1)),
            pl.BlockSpec((MOE_EPS, D_MODEL, 2 * D_EXPERT), lambda i, s: (s, 0, 0)),
            pl.BlockSpec((MOE_EPS, D_EXPERT, D_MODEL), lambda i, s: (s, 0, 0)),
        ],
        out_specs=pl.BlockSpec((tm, D_MODEL), lambda i, s: (i, 0), pipeline_mode=pl.Buffered(1)),
        out_shape=jax.ShapeDtypeStruct((n, D_MODEL), F32),
        scratch_shapes=[
            pltpu.VMEM((tm, rcap), BF16),
            pltpu.VMEM((rcap, D_MODEL), BF16),
            pltpu.VMEM((rcap, LANES), F32),
            pltpu.VMEM((rcap, D_MODEL), BF16),
            pltpu.SMEM((2 * N_GROUPS,), jnp.int32),
        ],
        compiler_params=pltpu.CompilerParams(dimension_semantics=("parallel", "arbitrary"),
                                             vmem_limit_bytes=MOE_VMEM_LIMIT),
        name="moe",
    )(x1, h2, lg, tri, wgu, wd)


def _tile(n, pref):
    t = pref
    while n % t:
        t //= 2
    return t


def _mixer_params(layer, norm1_g, w_in, a_conv_w, a_conv_b, a_gate_bias, a_out_norm_g,
                  b_qk_norm_g, b_rel_bias, c_qk_norm_g, c_lambda, c_sub_norm_g, t5_bias,
                  w_branch, w_out, nq_diff):
    n_small = 2 * A_HEADS
    cut = 4 * A_WIDTH
    w_main = jnp.concatenate([w_in[:, :cut], w_in[:, cut + n_small:]], axis=1)
    w_main = w_main.reshape(D_MODEL, N_SEG, SEG)[:, np.asarray(SEG_PERM), :]
    w_main = w_main.reshape(D_MODEL, N_SEG * SEG).astype(BF16)
    w_if = jnp.pad(w_in[:, cut:cut + n_small], ((0, 0), (0, LANES - n_small))).astype(BF16)
    gain = jnp.ones((N_SEG, SEG), F32)
    gain = gain.at[N_ROW_SEG + TSEG_BQ].set(jnp.tile(b_qk_norm_g[0], B_HEADS) * (B_DH ** -0.5 * LOG2E))
    gain = gain.at[SEG_BK].set(jnp.tile(b_qk_norm_g[1], B_HEADS))
    gain = gain.at[N_ROW_SEG + TSEG_CQ].set(
        jnp.tile(c_qk_norm_g[0], 2 * C_HEADS) * (C_DQK ** -0.5 * LOG2E))
    gain = gain.at[SEG_CK].set(jnp.tile(c_qk_norm_g[1], 2 * C_HEADS))
    diff_bias, diff_far = _diff_bias(t5_bias, _diff_nnear(nq_diff))
    lam_init = 0.8 - 0.6 * math.exp(-0.3 * layer)
    lf32 = c_lambda.astype(F32)
    lam = jnp.exp(jnp.sum(lf32[0] * lf32[1])) - jnp.exp(jnp.sum(lf32[2] * lf32[3])) + lam_init
    return dict(
        g1=norm1_g.reshape(1, D_MODEL), w_main=w_main, w_if=w_if,
        gain=gain.reshape(N_SEG, 1, SEG),
        cw=a_conv_w, cb=a_conv_b.reshape(1, -1),
        gbr=jnp.pad(a_gate_bias, (0, LANES - n_small)).reshape(1, LANES),
        gbc=a_gate_bias.reshape(n_small, 1),
        ag=a_out_norm_g.reshape(1, A_WIDTH),
        band_bias=_band_bias(b_rel_bias),
        diff_bias=diff_bias, diff_far=diff_far,
        lam=lam.reshape(1).astype(F32),
        gsub=(c_sub_norm_g * (1.0 - lam_init)).reshape(C_DV, 1),
        wb=w_branch.astype(BF16), wo=w_out.astype(BF16),
    )


def _layer(x2, bsz, seq, mp, norm2_g, w_group, b_group, w_router, b_router, w_e_gate, w_e_up, w_e_down):
    n = bsz * seq
    p, pt, gif = _inproj(x2, mp["g1"], mp["w_main"], mp["w_if"], mp["gain"], _tile(n, 512))
    gift = jnp.transpose(gif[:, :2 * A_HEADS])
    ha, hbt, hct = _mixers(p, pt, gif, gift, mp, bsz, seq)

    wr = jnp.concatenate([w_router, w_group], axis=1)
    wr = jnp.pad(wr, ((0, 0), (0, LANES - wr.shape[1]))).astype(BF16)
    br = jnp.pad(jnp.concatenate([b_router, b_group]), (0, LANES - N_EXPERTS - N_GROUPS)).reshape(1, LANES)
    x1, h2, lg = _merge(ha, hbt, hct, p, x2, mp["wb"], mp["wo"], norm2_g.reshape(1, D_MODEL), wr, br,
                        _tile(n, 512))
    wgu = jnp.concatenate([w_e_gate, w_e_up], axis=-1).astype(BF16)
    return _moe(x1, h2, lg, wgu, w_e_down.astype(BF16), _tile(n, MOE_TM))


def kernel(x, norm1_g, w_in, a_conv_w, a_conv_b, a_gate_bias, a_out_norm_g, b_qk_norm_g, b_rel_bias,
           c_qk_norm_g, c_lambda, c_sub_norm_g, t5_bias, w_branch, w_out, norm2_g, w_group, b_group,
           w_router, b_router, w_e_gate, w_e_up, w_e_down):
    bsz, seq, _ = x.shape
    assert seq % DIFF_T == 0 and seq % MLSTM_L == 0 and seq % BAND_TQ == 0
    x2 = x.reshape(bsz * seq, D_MODEL)
    for l in range(norm1_g.shape[0]):
        mp = _mixer_params(l, norm1_g[l], w_in[l], a_conv_w[l], a_conv_b[l], a_gate_bias[l],
                           a_out_norm_g[l], b_qk_norm_g[l], b_rel_bias[l], c_qk_norm_g[l], c_lambda[l],
                           c_sub_norm_g[l], t5_bias, w_branch[l], w_out[l], seq // DIFF_T)
        x2 = _layer(x2, bsz, seq, mp, norm2_g[l], w_group[l], b_group[l], w_router[l], b_router[l],
                    w_e_gate[l], w_e_up[l], w_e_down[l])
    return x2.reshape(bsz, seq, D_MODEL)
```

```python
import functools
import math

import numpy as np
import jax
import jax.numpy as jnp
from jax import lax
from jax.experimental import pallas as pl
from jax.experimental.pallas import tpu as pltpu

F32 = jnp.float32
BF16 = jnp.bfloat16

D_MODEL = 1024
CHUNK = 64
EPS = 1e-6
NEG = -1e30
LOG2E = math.log2(math.e)

A_HEADS = 4
A_DH = 128
A_WIDTH = A_HEADS * A_DH
CONV_W = 4
GATE_CAP = 15.0

B_HEADS = 8
B_DH = 64
B_WIDTH = B_HEADS * B_DH
B_LEFT_CHUNKS = 8
B_MAX_REL = 256

C_HEADS = 4
C_DQK = 64
C_DV = 2 * C_DQK
C_WIDTH = C_HEADS * C_DV

T5_BUCKETS = 32
T5_MAX_DIST = 1024

N_BRANCH = 3
BRANCH_WIDTH = 512

N_GROUPS = 4
EXPERTS_PER_GROUP = 8
N_EXPERTS = N_GROUPS * EXPERTS_PER_GROUP
D_EXPERT = D_MODEL // 4

LANES = 128
SEG = 512
N_SEG = 10
VMEM_LIMIT = 48 * 1024 * 1024

SEG_AQ, SEG_AK, SEG_BK, SEG_CK = 0, 1, 2, 3
N_ROW_SEG = 4
TSEG_AV, TSEG_AO, TSEG_BQ, TSEG_BV, TSEG_CQ, TSEG_CV = 0, 1, 2, 3, 4, 5
N_T_SEG = N_SEG - N_ROW_SEG
SEG_PERM = (0, 1, 5, 8, 2, 3, 4, 6, 7, 9)

MLSTM_L = 128
CONV_HIST = 8
BAND_TQ = 128
BAND_NKB = 1 + (B_LEFT_CHUNKS * CHUNK) // BAND_TQ
BAND_ONES = 16
DIFF_T = 512
DIFF_ONES = 16
MOE_TM = 1024
MOE_C = 64
MOE_BIG = 4
MOE_EPS = EXPERTS_PER_GROUP
MOE_VMEM_LIMIT = 56 * 1024 * 1024

def _cparams(sem, flags=None):
    return pltpu.CompilerParams(dimension_semantics=sem, vmem_limit_bytes=VMEM_LIMIT, flags=flags)


NORM_SEGS = (SEG_BK, SEG_CK, N_ROW_SEG + TSEG_BQ, N_ROW_SEG + TSEG_CQ)
SIGMOID_SEGS = (N_ROW_SEG + TSEG_AO,)


def _head_norm_t(acc_t):
    rows, tm = acc_t.shape
    a3 = acc_t.reshape(rows // 64, 64, tm)
    ssq = jnp.sum(a3 * a3, axis=1, keepdims=True)
    return (a3 * lax.rsqrt(ssq * (1.0 / 64.0) + EPS)).reshape(rows, tm)


def _inproj_kernel(x_ref, g_ref, w_ref, wif_ref, gain_ref, gain_t_ref, p_ref, pt_ref, gif_ref):
    xf = x_ref[...]
    xn = (xf * lax.rsqrt(jnp.mean(xf * xf, axis=-1, keepdims=True) + EPS) * g_ref[...]).astype(BF16)
    gif_ref[...] = jnp.dot(xn, wif_ref[...], preferred_element_type=F32)
    for j in range(N_SEG):
        cols = slice(j * SEG, (j + 1) * SEG)
        acc = jnp.dot(xn, w_ref[:, cols], preferred_element_type=F32)
        if j in NORM_SEGS:
            acc_t = _head_norm_t(jnp.transpose(acc))
            if j < N_ROW_SEG:
                p_ref[:, cols] = (jnp.transpose(acc_t) * gain_ref[j]).astype(BF16)
            else:
                pt_ref[j - N_ROW_SEG] = (acc_t * gain_t_ref[NORM_SEGS.index(j) - 2]).astype(BF16)
            continue
        if j in SIGMOID_SEGS:
            acc = jax.nn.sigmoid(acc)
        if j < N_ROW_SEG:
            p_ref[:, cols] = acc.astype(BF16)
        else:
            pt_ref[j - N_ROW_SEG] = jnp.transpose(acc).astype(BF16)


def _inproj(x2, g, w, wif, gain, tm):
    n = x2.shape[0]
    tsegs = np.asarray([N_ROW_SEG + TSEG_BQ, N_ROW_SEG + TSEG_CQ])
    gain_t = jnp.broadcast_to(gain[tsegs, 0, :, None], (2, SEG, tm))

    def const(shape):
        return pl.BlockSpec(shape, lambda i: (0,) * len(shape), pipeline_mode=pl.Buffered(1))

    return pl.pallas_call(
        _inproj_kernel,
        grid=(n // tm,),
        in_specs=[
            pl.BlockSpec((tm, D_MODEL), lambda i: (i, 0)),
            const((1, D_MODEL)),
            const((D_MODEL, N_SEG * SEG)),
            const((D_MODEL, LANES)),
            const((N_SEG, 1, SEG)),
            const((2, SEG, tm)),
        ],
        out_specs=[
            pl.BlockSpec((tm, N_ROW_SEG * SEG), lambda i: (i, 0)),
            pl.BlockSpec((N_T_SEG, SEG, tm), lambda i: (0, 0, i)),
            pl.BlockSpec((tm, LANES), lambda i: (i, 0)),
        ],
        out_shape=[
            jax.ShapeDtypeStruct((n, N_ROW_SEG * SEG), BF16),
            jax.ShapeDtypeStruct((N_T_SEG, SEG, n), BF16),
            jax.ShapeDtypeStruct((n, LANES), F32),
        ],
        compiler_params=_cparams(("parallel",)),
        name="inproj",
    )(x2, g, w, wif, gain, gain_t)


def _log_sigmoid(z):
    return jnp.minimum(z, 0.0) - jnp.log(1.0 + jnp.exp(-jnp.abs(z)))


def _split3(a):
    hi = a.astype(BF16)
    r1 = a - hi.astype(F32)
    mid = r1.astype(BF16)
    lo = (r1 - mid.astype(F32)).astype(BF16)
    return hi, mid, lo


def _mlstm_kernel(*refs):
    _mlstm_carry(*refs)
    for _ in _mlstm_chunk(*refs):
        pass


def _mlstm_carry(aq_ref, ak_ref, vt_ref, aot_ref, gif_ref, gift_ref, cw_ref, cb_ref,
                 gbr_ref, gbc_ref, agt_ref, out_ref, ubuf, kq_ref, st_ref, ct_ref, n_ref, m_ref,
                 *, chunk=None):
    L = MLSTM_L
    c = pl.program_id(1) if chunk is None else chunk

    @pl.when(c == 0)
    def _():
        ubuf[0:CONV_HIST, :] = jnp.zeros((CONV_HIST, 2 * A_WIDTH), F32)
        ct_ref[...] = jnp.zeros_like(ct_ref)
        n_ref[...] = jnp.zeros_like(n_ref)
        m_ref[...] = jnp.zeros_like(m_ref)

    @pl.when(c > 0)
    def _():
        ubuf[0:CONV_HIST, :] = ubuf[L:L + CONV_HIST, :]


def _mlstm_chunk(aq_ref, ak_ref, vt_ref, aot_ref, gif_ref, gift_ref, cw_ref, cb_ref,
                 gbr_ref, gbc_ref, agt_ref, out_ref, ubuf, kq_ref, st_ref, ct_ref, n_ref, m_ref):
    L = MLSTM_L

    H = CONV_HIST
    ubuf[H:L + H, 0:A_WIDTH] = aq_ref[...].astype(F32)
    ubuf[H:L + H, A_WIDTH:2 * A_WIDTH] = ak_ref[...].astype(F32)
    y = cb_ref[...] + cw_ref[0:1, :] * ubuf[H:L + H, :]
    for t in range(1, CONV_W):
        y = y + cw_ref[t:t + 1, :] * ubuf[H - t:H - t + L, :]
    qk = y * jax.nn.sigmoid(y)
    q_t = jnp.transpose(qk[:, 0:A_WIDTH]).astype(BF16)
    k_all = (qk[:, A_WIDTH:2 * A_WIDTH] * (A_DH ** -0.5)).astype(BF16)

    zc = gif_ref[...] + gbr_ref[...]
    ig_c = GATE_CAP * jnp.tanh(zc * (1.0 / GATE_CAP))
    lf_c = _log_sigmoid(zc)
    zr = gift_ref[...] + gbc_ref[...]
    ig_r = GATE_CAP * jnp.tanh(zr * (1.0 / GATE_CAP))
    lf_r = _log_sigmoid(zr)

    row = lax.broadcasted_iota(jnp.int32, (L, L), 0)
    col = lax.broadcasted_iota(jnp.int32, (L, L), 1)
    causal = col <= row
    tril = jnp.where(causal, 1.0, 0.0).astype(BF16)
    triu = jnp.where(row <= col, 1.0, 0.0).astype(BF16)
    b_c = sum(jnp.dot(tril, piece, preferred_element_type=F32) for piece in _split3(lf_c))
    b_r = sum(jnp.dot(piece, triu, preferred_element_type=F32) for piece in _split3(lf_r))

    sub8 = lax.broadcasted_iota(jnp.int32, (8, L), 0)
    heads = [slice(h * A_DH, (h + 1) * A_DH) for h in range(A_HEADS)]
    for h, rows in enumerate(heads):
        kq_ref[h] = jnp.dot(k_all[:, rows], q_t[rows, :], preferred_element_type=F32)
    yield

    stats = []
    for h, rows in enumerate(heads):
        bcol = b_c[:, A_HEADS + h:A_HEADS + h + 1]
        brow = b_r[A_HEADS + h:A_HEADS + h + 1, :]
        igcol = ig_c[:, h:h + 1]
        m_prev = m_ref[h][:, 0:1]
        dmat = jnp.where(row <= col, brow + (igcol - bcol), NEG)
        inter = brow + m_prev
        m_t = jnp.maximum(inter, jnp.max(dmat, axis=0, keepdims=True))
        st = kq_ref[h] * jnp.exp(dmat - m_t)
        st_ref[h] = st.astype(BF16)
        stats.append((brow, m_prev, m_t, jnp.exp(inter - m_t), jnp.sum(st, axis=0, keepdims=True)))
        if h % 2 == 1:
            yield

    for h, rows in enumerate(heads):
        brow, m_prev, m_t, w_inter, st_sum = stats[h]
        qt = q_t[rows, :]
        k = k_all[:, rows]
        vt = vt_ref[rows, :]
        igrow = ig_r[h:h + 1, :]
        b_last = brow[:, L - 1:L]
        ct = ct_ref[h]
        n8 = n_ref[h]

        num = w_inter * jnp.dot(ct.astype(BF16), qt, preferred_element_type=F32)
        num = num + jnp.dot(vt, st_ref[h], preferred_element_type=F32)
        nq = jnp.dot(n8.astype(BF16), qt, preferred_element_type=F32)[0:1, :]
        den = w_inter * nq + st_sum
        hh = num * (1.0 / jnp.maximum(jnp.abs(den), jnp.exp(-m_t)))

        g_end = b_last - brow + igrow
        m_new = jnp.maximum(b_last + m_prev, jnp.max(g_end, axis=-1, keepdims=True))
        dec = jnp.exp(b_last + m_prev - m_new)
        w_end = jnp.exp(g_end - m_new)
        vw = (vt.astype(F32) * w_end).astype(BF16)
        ct_ref[h] = dec * ct + jnp.dot(vw, k, preferred_element_type=F32)
        w8 = jnp.where(sub8 == 0, w_end, 0.0).astype(BF16)
        n_ref[h] = dec * n8 + jnp.dot(w8, k, preferred_element_type=F32)
        m_ref[h] = jnp.broadcast_to(m_new, (1, LANES))

        hn = hh * lax.rsqrt(jnp.mean(hh * hh, axis=0, keepdims=True) + EPS) * agt_ref[rows, :]
        out_ref[rows, :] = (hn * aot_ref[rows, :].astype(F32)).astype(BF16)
        if h % 2 == 1 and h + 1 < A_HEADS:
            yield


def _mlstm_call(p, pt, gif, gift, cw, cb, gbr, gbc, ag, bsz, seq):
    L = MLSTM_L
    nc = seq // L
    n = bsz * seq
    agt = jnp.broadcast_to(ag.reshape(A_WIDTH, 1), (A_WIDTH, L))

    def tseg(j):
        return pl.BlockSpec((None, SEG, L), lambda b, c: (j, 0, b * nc + c))

    def full(shape):
        return pl.BlockSpec(shape, lambda b, c: (0,) * len(shape))

    in_specs = [
        pl.BlockSpec((L, SEG), lambda b, c: (b * nc + c, SEG_AQ)),
        pl.BlockSpec((L, SEG), lambda b, c: (b * nc + c, SEG_AK)),
        tseg(TSEG_AV), tseg(TSEG_AO),
        pl.BlockSpec((L, LANES), lambda b, c: (b * nc + c, 0)),
        pl.BlockSpec((8, L), lambda b, c: (0, b * nc + c)),
        full((CONV_W, 2 * A_WIDTH)), full((1, 2 * A_WIDTH)),
        full((1, LANES)), full((8, 1)), full((A_WIDTH, L)),
    ]
    scratch = [
        pltpu.VMEM((L + CONV_HIST, 2 * A_WIDTH), F32),
        pltpu.VMEM((A_HEADS, L, L), F32),
        pltpu.VMEM((A_HEADS, L, L), BF16),
        pltpu.VMEM((A_HEADS, A_DH, A_DH), F32),
        pltpu.VMEM((A_HEADS, 8, A_DH), F32),
        pltpu.VMEM((A_HEADS, 1, LANES), F32),
    ]
    return (in_specs, pl.BlockSpec((A_WIDTH, L), lambda b, c: (0, b * nc + c)),
            jax.ShapeDtypeStruct((A_WIDTH, n), BF16), scratch,
            (p, p, pt, pt, gif, gift, cw, cb, gbr, gbc, agt))


def _mlstm(p, pt, gif, gift, cw, cb, gbr, gbc, ag, bsz, seq):
    in_specs, out_spec, out_shape, scratch, operands = _mlstm_call(
        p, pt, gif, gift, cw, cb, gbr, gbc, ag, bsz, seq)
    return pl.pallas_call(
        _mlstm_kernel, grid=(bsz, seq // MLSTM_L), in_specs=in_specs, out_specs=out_spec,
        out_shape=out_shape, scratch_shapes=scratch,
        compiler_params=_cparams(("parallel", "arbitrary")), name="mlstm",
    )(*operands)


def _interleave(*gens):
    gens = list(gens)
    while gens:
        for g in list(gens):
            try:
                next(g)
            except StopIteration:
                gens.remove(g)
                continue
            yield


def _band_steps(refs, i, mask_start):
    nkb = BAND_NKB
    qt_ref = refs[0]
    k_refs = refs[1:1 + nkb]
    vt_refs = refs[1 + nkb:1 + 2 * nkb]
    bias_ref = refs[1 + 2 * nkb]
    out_ref = refs[2 + 2 * nkb]
    s_ref, mx_ref = refs[3 + 2 * nkb:5 + 2 * nkb]
    tq = BAND_TQ
    nk = nkb * tq

    k_all = jnp.concatenate([r[...] for r in k_refs], axis=0)
    vt_all = jnp.concatenate([r[...] for r in vt_refs], axis=1)
    ones = jnp.ones((BAND_ONES, nk), BF16)
    row = lax.broadcasted_iota(jnp.int32, (LANES, tq), 0)
    lo = row < B_DH
    if mask_start:
        kidx = lax.broadcasted_iota(jnp.int32, (nk, 1), 0)
        valid = (kidx + (i - (nkb - 1)) * tq) >= 0

    def score(p):
        rows = slice(p * LANES, (p + 1) * LANES)
        qtp = qt_ref[rows, :]
        zero = jnp.zeros_like(qtp)
        qbd = jnp.concatenate([jnp.where(lo, qtp, zero), jnp.where(lo, zero, qtp)], axis=1)
        s = jnp.dot(k_all[:, rows], qbd, preferred_element_type=F32) + bias_ref[p]
        if mask_start:
            s = jnp.where(valid, s, NEG)
        s_ref[p & 1] = s
        mx_ref[p & 1] = jnp.max(s, axis=0, keepdims=True)

    def finish(p):
        rows = slice(p * LANES, (p + 1) * LANES)
        pr = jnp.exp2((s_ref[p & 1] - mx_ref[p & 1]).astype(BF16))
        o = jnp.dot(jnp.concatenate([vt_all[rows, :], ones], axis=0), pr,
                    preferred_element_type=F32)
        o = o[0:LANES, :] / o[LANES:LANES + 1, :]
        out_ref[rows, :] = jnp.where(lo, o[:, 0:tq], o[:, tq:2 * tq]).astype(BF16)

    score(0)
    yield
    for p in range(B_HEADS // 2):
        if p + 1 < B_HEADS // 2:
            score(p + 1)
        finish(p)
        if p + 1 < B_HEADS // 2:
            yield


def _band_variants(i, region):
    @pl.when(i < BAND_NKB - 1)
    def _():
        region(True)

    @pl.when(i >= BAND_NKB - 1)
    def _():
        region(False)


def _drain(gen):
    for _ in gen:
        pass


def _band_kernel(*refs):
    i = pl.program_id(1)
    _band_variants(i, lambda mask: _drain(_band_steps(refs, i, mask)))


def _band_call(p, pt, bias, bsz, seq):
    tq = BAND_TQ
    nkb = BAND_NKB
    nq = seq // tq
    n = bsz * seq

    def kblk(d):
        return pl.BlockSpec((tq, SEG), lambda b, i: (b * nq + jnp.maximum(i - d, 0), SEG_BK))

    def vblk(d):
        return pl.BlockSpec((None, SEG, tq), lambda b, i: (TSEG_BV, 0, b * nq + jnp.maximum(i - d, 0)))

    in_specs = [pl.BlockSpec((None, SEG, tq), lambda b, i: (TSEG_BQ, 0, b * nq + i))]
    in_specs += [kblk(d) for d in range(nkb - 1, -1, -1)]
    in_specs += [vblk(d) for d in range(nkb - 1, -1, -1)]
    in_specs += [pl.BlockSpec(bias.shape, lambda b, i: (0, 0, 0))]
    scratch = [
        pltpu.VMEM((2, nkb * tq, 2 * tq), F32),
        pltpu.VMEM((2, 1, 2 * tq), F32),
    ]
    return (in_specs, pl.BlockSpec((B_WIDTH, tq), lambda b, i: (0, b * nq + i)),
            jax.ShapeDtypeStruct((B_WIDTH, n), BF16), scratch,
            (pt, *([p] * nkb), *([pt] * nkb), bias))


def _band(p, pt, bias, bsz, seq):
    in_specs, out_spec, out_shape, scratch, operands = _band_call(p, pt, bias, bsz, seq)
    return pl.pallas_call(
        _band_kernel, grid=(bsz, seq // BAND_TQ), in_specs=in_specs, out_specs=out_spec,
        out_shape=out_shape, scratch_shapes=scratch,
        compiler_params=_cparams(("parallel", "parallel")), name="band_attn",
    )(*operands)


def _toeplitz(base, m, n):
    period = base.shape[-1]
    assert n <= period - 1
    reps = (1,) * (base.ndim - 1) + (m,)
    big = jnp.tile(base, reps)[..., :m * (period - 1)]
    return big.reshape(base.shape[:-1] + (m, period - 1))[..., :n]


def _band_bias(b_rel):
    tq = BAND_TQ
    nk = BAND_NKB * tq
    period = tq + nk
    e = np.arange(period)
    e = np.where(e < nk, e, e - period)
    rel = np.clip((nk - tq) - e, -B_MAX_REL, B_MAX_REL) + B_MAX_REL
    bias = _toeplitz(jnp.transpose(b_rel[rel]).astype(F32), tq, nk)
    qpos = np.arange(tq)
    kpos = np.arange(nk) - (nk - tq)
    qc = qpos[:, None] // CHUNK
    kc = np.floor_divide(kpos[None, :], CHUNK)
    allowed = (kc <= qc) & (kc >= qc - B_LEFT_CHUNKS)
    bias = jnp.where(allowed[None], bias * LOG2E, NEG)
    return jnp.swapaxes(bias.reshape(B_HEADS // 2, 2 * tq, nk), 1, 2)


def _diff_kernel(lam_ref, cfar_ref, qt_ref, k_ref, vt_ref, bias_ref, g_ref, out_ref,
                 qbd_ref, s_ref, mx_ref, acc_ref, m_ref, *, nnear, first_region=None):
    T = DIFF_T
    h = pl.program_id(1)
    qi = pl.program_id(2)
    cfar = cfar_ref[h]

    def prologue():
        qt = qt_ref[...]
        row = lax.broadcasted_iota(jnp.int32, (2 * C_DQK, T), 0)
        zero = jnp.zeros_like(qt)
        qbd_ref[:, 0:T] = jnp.where(row < C_DQK, qt, zero)
        qbd_ref[:, T:2 * T] = jnp.where(row < C_DQK, zero, qt)
        m_ref[...] = jnp.full(m_ref.shape, NEG, F32)
        acc_ref[...] = jnp.zeros_like(acc_ref)

    def near_bias(t):
        return bias_ref[t] if t < nnear else None

    def stage_a(t, slot, bias, maps=(0, 1)):
        j = jnp.maximum(qi - t, 0)
        k = k_ref[pl.ds(pl.multiple_of(j * T, T), T), :]
        for mp in maps:
            sm = jnp.dot(k, qbd_ref[:, mp * T:(mp + 1) * T], preferred_element_type=F32)
            if bias is not None:
                sm = sm + bias
            s_ref[slot, mp] = sm
            mx = jnp.max(sm, axis=0, keepdims=True)
            mx_ref[slot, mp] = mx + cfar if bias is None else mx

    def stage_bc(t, slot, far, maps=(0, 1)):
        j = qi - t
        vt = jnp.concatenate([vt_ref[:, pl.ds(pl.multiple_of(j * T, T), T)],
                              jnp.ones((DIFF_ONES, T), BF16)], axis=0)
        for mp in maps:
            m_old = m_ref[mp]
            m_new = jnp.maximum(m_old, mx_ref[slot, mp])
            shift = m_new - cfar if far else m_new
            pr = jnp.exp2((s_ref[slot, mp] - shift).astype(BF16))
            m_ref[mp] = m_new
            acc_ref[mp] = jnp.exp2(m_old - m_new) * acc_ref[mp] + jnp.dot(
                vt, pr, preferred_element_type=F32)

    def full_step(t, slot, far, next_bias):
        for mp in range(2):
            stage_a(t + 1, 1 - slot, next_bias, (mp,))
            stage_bc(t, slot, far, (mp,))

    def first_steps(other, n_steps=1):
        prologue()
        next(other, None)
        stage_a(0, 0, near_bias(0))
        next(other, None)
        for t in range(n_steps):
            for mp in range(2):
                stage_a(t + 1, 1 - (t & 1), near_bias(t + 1), (mp,))
                stage_bc(t, t & 1, False, (mp,))
                next(other, None)
        _drain(other)

    all_near = qi >= nnear - 1
    if first_region is None:
        first_steps(iter(()))
        near_done = False
    else:
        near_done = first_region(first_steps, all_near, nnear)
    for t in range(1, nnear):
        @pl.when((qi >= t) & jnp.logical_not(near_done))
        def _(t=t):
            full_step(t, t & 1, False, near_bias(t + 1))

    @pl.when(qi >= nnear)
    def _():
        n_full = qi - nnear
        s0 = nnear & 1

        def pair(i, carry):
            full_step(nnear + 2 * i, s0, True, None)
            full_step(nnear + 2 * i + 1, 1 - s0, True, None)
            return carry

        lax.fori_loop(0, n_full // 2, pair, 0)

        @pl.when(n_full % 2 == 1)
        def _():
            full_step(qi - 1, s0, True, None)

        stage_bc(qi, qi & 1, True)

    o1 = acc_ref[0, 0:C_DV, :] / acc_ref[0, C_DV:C_DV + 1, :]
    o2 = acc_ref[1, 0:C_DV, :] / acc_ref[1, C_DV:C_DV + 1, :]
    o = o1 - lam_ref[0] * o2
    on = o * lax.rsqrt(jnp.mean(o * o, axis=0, keepdims=True) + EPS) * g_ref[...]
    out_ref[...] = on.astype(BF16)


def _diff_nnear(nq):
    d_sat = -(-(T5_MAX_DIST - 1 + DIFF_T) // DIFF_T)
    return min(d_sat, nq)


def _diff_call(p, pt, bias, cfar, lam, gsub, bsz, seq):
    T = DIFF_T
    nq = seq // T
    nnear = bias.shape[1]
    kcol = SEG_CK * SEG // LANES
    n = bsz * seq
    in_specs = [
        pl.BlockSpec(memory_space=pltpu.SMEM),
        pl.BlockSpec(memory_space=pltpu.SMEM),
        pl.BlockSpec((None, 2 * C_DQK, T), lambda b, h, i: (TSEG_CQ, h, b * nq + i)),
        pl.BlockSpec((seq, LANES), lambda b, h, i: (b, kcol + h)),
        pl.BlockSpec((None, C_DV, seq), lambda b, h, i: (TSEG_CV, h, b)),
        pl.BlockSpec((None, nnear, T, T), lambda b, h, i: (h, 0, 0, 0)),
        pl.BlockSpec((C_DV, 1), lambda b, h, i: (0, 0)),
    ]
    scratch = [
        pltpu.VMEM((2 * C_DQK, 2 * T), BF16),
        pltpu.VMEM((2, 2, T, T), F32),
        pltpu.VMEM((2, 2, 1, T), F32),
        pltpu.VMEM((2, C_DV + DIFF_ONES, T), F32),
        pltpu.VMEM((2, 1, T), F32),
    ]
    return (in_specs, pl.BlockSpec((C_DV, T), lambda b, h, i: (h, b * nq + i)),
            jax.ShapeDtypeStruct((C_WIDTH, n), BF16), scratch, (lam, cfar, pt, p, pt, bias, gsub))


def _diff(p, pt, bias, cfar, lam, gsub, bsz, seq):
    in_specs, out_spec, out_shape, scratch, operands = _diff_call(p, pt, bias, cfar, lam, gsub, bsz, seq)
    return pl.pallas_call(
        functools.partial(_diff_kernel, nnear=bias.shape[1]),
        grid=(bsz, C_HEADS, seq // DIFF_T), in_specs=in_specs, out_specs=out_spec,
        out_shape=out_shape, scratch_shapes=scratch,
        compiler_params=_cparams(("parallel", "parallel", "arbitrary")), name="diff_attn",
    )(*operands)


def _mixers_kernel(*refs, n_in, n_scr, nnear):
    d_in, m_in, b_in = (refs[sum(n_in[:k]):sum(n_in[:k + 1])] for k in range(3))
    hct_ref, hat_ref, hbt_ref = refs[sum(n_in):sum(n_in) + 3]
    scr = refs[sum(n_in) + 3:]
    d_scr, m_scr, b_scr = (scr[sum(n_scr[:k]):sum(n_scr[:k + 1])] for k in range(3))
    m_refs = (*m_in, hat_ref, *m_scr)
    b_refs = (*b_in, hbt_ref, *b_scr)
    chunk = pl.program_id(1) * pl.num_programs(2) + pl.program_id(2)
    _mlstm_carry(*m_refs, chunk=chunk)

    def first_region(trace, all_near, n_near):
        def other(mask):
            return _interleave(_band_steps(b_refs, chunk, mask), _mlstm_chunk(*m_refs))

        masked = chunk < BAND_NKB - 1
        wide = jnp.logical_not(masked) & all_near

        @pl.when(masked)
        def _():
            trace(other(True), 1)

        @pl.when(wide)
        def _():
            trace(other(False), n_near)

        @pl.when(jnp.logical_not(masked) & jnp.logical_not(all_near))
        def _():
            trace(other(False), 1)

        return wide

    _diff_kernel(*d_in, hct_ref, *d_scr, nnear=nnear, first_region=first_region)


def _mixers(p, pt, gif, gift, mp, bsz, seq):
    nq = seq // DIFF_T
    assert C_HEADS * nq == seq // MLSTM_L and MLSTM_L == BAND_TQ

    def on_diff_grid(spec):
        if spec.index_map is None:
            return spec
        return pl.BlockSpec(spec.block_shape, lambda b, h, i, f=spec.index_map: f(b, h * nq + i))

    d_specs, d_out, d_shape, d_scr, d_ops = _diff_call(p, pt, mp["diff_bias"], mp["diff_far"], mp["lam"],
                                                       mp["gsub"], bsz, seq)
    m_specs, m_out, m_shape, m_scr, m_ops = _mlstm_call(p, pt, gif, gift, mp["cw"], mp["cb"], mp["gbr"],
                                                        mp["gbc"], mp["ag"], bsz, seq)
    b_specs, b_out, b_shape, b_scr, b_ops = _band_call(p, pt, mp["band_bias"], bsz, seq)
    hct, hat, hbt = pl.pallas_call(
        functools.partial(_mixers_kernel, n_in=(len(d_specs), len(m_specs), len(b_specs)),
                          n_scr=(len(d_scr), len(m_scr), len(b_scr)), nnear=mp["diff_bias"].shape[1]),
        grid=(bsz, C_HEADS, nq),
        in_specs=d_specs + [on_diff_grid(s) for s in m_specs + b_specs],
        out_specs=[d_out, on_diff_grid(m_out), on_diff_grid(b_out)],
        out_shape=[d_shape, m_shape, b_shape],
        scratch_shapes=d_scr + m_scr + b_scr,
        compiler_params=_cparams(("parallel", "arbitrary", "arbitrary")),
        name="mixers",
    )(*d_ops, *m_ops, *b_ops)
    return hat, hbt, hct


def _t5_bucket(rel):
    nb = T5_BUCKETS // 2
    max_exact = nb // 2
    ret = (rel > 0).astype(jnp.int32) * nb
    n = jnp.abs(rel)
    large = max_exact + (jnp.log(jnp.maximum(n, max_exact).astype(F32) / max_exact)
                         / math.log(T5_MAX_DIST / max_exact) * (nb - max_exact)).astype(jnp.int32)
    large = jnp.minimum(large, nb - 1)
    return ret + jnp.where(n < max_exact, n, large)


def _diff_bias(t5_table, nnear):
    T = DIFF_T
    e = np.arange(2 * T)
    amc = np.where(e < T, -e, 2 * T - e)
    rel = jnp.asarray(-np.arange(nnear)[:, None] * T + amc[None, :], jnp.int32)
    base = jnp.moveaxis(t5_table[_t5_bucket(rel)], -1, 0).astype(F32) * LOG2E
    tiles = _toeplitz(base, T, T)
    a = np.arange(T)[:, None]
    c = np.arange(T)[None, :]
    allowed = np.ones((nnear, T, T), bool)
    allowed[0] = (a // CHUNK) <= (c // CHUNK)
    far = t5_table[_t5_bucket(jnp.asarray(-T5_MAX_DIST, jnp.int32))].astype(F32) * LOG2E
    return jnp.where(allowed[None], tiles, NEG), far


def _merge_kernel(hat_ref, hbt_ref, hct_ref, x_ref, n1_ref, wg_ref, wb_ref, wo_ref,
                  n2_ref, wr_ref, br_ref, x1_ref, h2_ref, lg_ref, y_ref):
    tn = (((0,), (0,)), ((), ()))
    tm = x_ref.shape[0]
    halves = [slice(i * (tm // 2), (i + 1) * (tm // 2)) for i in range(2)]
    for r in halves:
        xf = x_ref[r, :]
        xn = (xf * lax.rsqrt(jnp.mean(xf * xf, axis=-1, keepdims=True) + EPS) * n1_ref[...]).astype(BF16)
        y = None
        for b, ht_ref in enumerate((hat_ref, hbt_ref, hct_ref)):
            gate = jax.nn.sigmoid(jnp.dot(xn, wg_ref[:, b * D_MODEL:(b + 1) * D_MODEL],
                                          preferred_element_type=F32))
            yb = gate * lax.dot_general(ht_ref[:, r], wb_ref[b], tn, preferred_element_type=F32)
            y = yb if y is None else y + yb
        y_ref[r, :] = y.astype(BF16)
    for r in halves:
        x1 = x_ref[r, :] + jnp.dot(y_ref[r, :], wo_ref[...], preferred_element_type=F32)
        x1_ref[r, :] = x1
        h2 = x1 * lax.rsqrt(jnp.mean(x1 * x1, axis=-1, keepdims=True) + EPS) * n2_ref[...]
        h2_ref[r, :] = h2.astype(BF16)
    for r in halves:
        lg_ref[r, :] = jnp.dot(h2_ref[r, :], wr_ref[...], preferred_element_type=F32) + br_ref[...]


def _merge(ha, hbt, hct, x2, n1, wg, wb, wo, n2, wr, br, tm):
    n = x2.shape[0]

    def rows(width, col=0):
        return pl.BlockSpec((tm, width), lambda i: (i, col))

    def cols():
        return pl.BlockSpec((BRANCH_WIDTH, tm), lambda i: (0, i))

    def full(shape):
        return pl.BlockSpec(shape, lambda i: (0,) * len(shape), pipeline_mode=pl.Buffered(1))

    return pl.pallas_call(
        _merge_kernel,
        grid=(n // tm,),
        in_specs=[
            cols(), cols(), cols(),
            rows(D_MODEL),
            full((1, D_MODEL)), full((D_MODEL, N_BRANCH * D_MODEL)),
            full((N_BRANCH, BRANCH_WIDTH, D_MODEL)), full((D_MODEL, D_MODEL)),
            full((1, D_MODEL)), full((D_MODEL, LANES)), full((1, LANES)),
        ],
        out_specs=[rows(D_MODEL), rows(D_MODEL), rows(LANES)],
        out_shape=[
            jax.ShapeDtypeStruct((n, D_MODEL), F32),
            jax.ShapeDtypeStruct((n, D_MODEL), BF16),
            jax.ShapeDtypeStruct((n, LANES), F32),
        ],
        scratch_shapes=[pltpu.VMEM((tm, D_MODEL), BF16)],
        compiler_params=_cparams(("parallel",)),
        name="merge",
    )(ha, hbt, hct, x2, n1, wg, wb, wo, n2, wr, br)


def _combine_weights(lg):
    lanef = lax.broadcasted_iota(jnp.int32, lg.shape, 1).astype(F32)
    big = 1e9
    is_g = (lanef >= N_EXPERTS) & (lanef < N_EXPERTS + N_GROUPS)
    gl = jnp.where(is_g, lg, -jnp.inf)
    gmax = jnp.max(gl, axis=-1, keepdims=True)
    g_idx = jnp.min(jnp.where(gl == gmax, lanef - N_EXPERTS, big), axis=-1, keepdims=True)
    p_g = 1.0 / jnp.sum(jnp.exp(gl - gmax), axis=-1, keepdims=True)
    in_grp = (lanef >= g_idx * EXPERTS_PER_GROUP) & (lanef < (g_idx + 1.0) * EXPERTS_PER_GROUP)
    el = jnp.where(in_grp, lg, -jnp.inf)
    ee = jnp.exp(el - jnp.max(el, axis=-1, keepdims=True))
    ep = ee / jnp.sum(ee, axis=-1, keepdims=True)
    ep = jnp.where(in_grp, ep, -1.0)
    v1 = jnp.max(ep, axis=-1, keepdims=True)
    i1 = jnp.min(jnp.where(ep == v1, lanef, big), axis=-1, keepdims=True)
    ep2 = jnp.where(lanef == i1, -1.0, ep)
    v2 = jnp.max(ep2, axis=-1, keepdims=True)
    i2 = jnp.min(jnp.where(ep2 == v2, lanef, big), axis=-1, keepdims=True)
    tot = v1 + v2
    comb = jnp.where(lanef == i1, p_g * (v1 / tot), 0.0) + jnp.where(lanef == i2, p_g * (v2 / tot), 0.0)
    return comb, g_idx


def _moe_kernel(x1_ref, h2_ref, lg_ref, tri_ref, wgu_ref, wd_ref, out_ref,
                pt_ref, xs_ref, ws_ref, ys_ref, meta_ref, *, rcap):
    C = MOE_C
    s = pl.program_id(1)
    tm = h2_ref.shape[0]

    @pl.when(s == 0)
    def _():
        comb, g_idx = _combine_weights(lg_ref[...])
        lanef = lax.broadcasted_iota(jnp.int32, comb.shape, 1).astype(F32)
        mine = lanef == g_idx
        onehot = jnp.where(mine, 1.0, 0.0)
        ranks = jnp.dot(tri_ref[...], onehot.astype(BF16), preferred_element_type=F32)
        dest = jnp.sum(jnp.where(mine, ranks, 0.0), axis=-1, keepdims=True)
        off = jnp.int32(0)
        for g in range(N_GROUPS):
            cnt = jnp.sum(onehot[:, g:g + 1]).astype(jnp.int32)
            nchunk = (cnt + (C - 1)) // C
            meta_ref[g] = off
            meta_ref[N_GROUPS + g] = nchunk
            dest = dest + jnp.where(g_idx == float(g), (off * C).astype(F32), 0.0)
            off = off + nchunk
        slot = lax.broadcasted_iota(jnp.int32, (tm, rcap), 1).astype(F32)
        pt = jnp.where(dest == slot, 1.0, 0.0).astype(BF16)
        pt_ref[...] = pt
        comb_hi = comb.astype(BF16)
        comb_lo = (comb - comb_hi.astype(F32)).astype(BF16)
        packed = jnp.concatenate([h2_ref[...], comb_hi, comb_lo], axis=1)
        srt = lax.dot_general(pt, packed, (((0,), (0,)), ((), ())), preferred_element_type=F32)
        xs_ref[...] = srt[:, 0:D_MODEL].astype(BF16)
        ws_ref[...] = srt[:, D_MODEL:D_MODEL + LANES] + srt[:, D_MODEL + LANES:D_MODEL + 2 * LANES]
        ys_ref[...] = jnp.zeros_like(ys_ref)

    def do_chunk(r0, m):
        r0 = pl.multiple_of(r0, C)
        xc = xs_ref[pl.ds(r0, m), :]
        wsc = ws_ref[pl.ds(r0, m), :]
        lane = lax.broadcasted_iota(jnp.int32, (m, LANES), 1)
        y = None
        for k in range(MOE_EPS):
            gu = jnp.dot(xc, wgu_ref[k], preferred_element_type=F32)
            gate = gu[:, 0:D_EXPERT]
            wk = jnp.sum(jnp.where(lane == s * MOE_EPS + k, wsc, 0.0), axis=-1, keepdims=True)
            he = (gate * jax.nn.sigmoid(gate) * gu[:, D_EXPERT:2 * D_EXPERT] * wk).astype(BF16)
            yk = jnp.dot(he, wd_ref[k], preferred_element_type=F32)
            y = yk if y is None else y + yk
        ys_ref[pl.ds(r0, m), :] = y.astype(BF16)

    g = s // (EXPERTS_PER_GROUP // MOE_EPS)
    start = meta_ref[g]
    nchunk = meta_ref[N_GROUPS + g]

    def big(i, carry):
        do_chunk((start + MOE_BIG * i) * C, MOE_BIG * C)
        return carry

    common = nchunk == MOE_BIG + 1

    @pl.when(common)
    def _():
        do_chunk(start * C, (MOE_BIG + 1) * C)

    @pl.when(jnp.logical_not(common))
    def _():
        nbig = nchunk // MOE_BIG
        lax.fori_loop(0, nbig, big, 0)
        done = nbig * MOE_BIG
        size = MOE_BIG // 2
        while size >= 1:
            @pl.when((nchunk & size) != 0)
            def _(done=done, size=size):
                do_chunk((start + done) * C, size * C)
            done = done + (nchunk & size)
            size //= 2

    @pl.when(s == pl.num_programs(1) - 1)
    def _():
        out_ref[...] = x1_ref[...] + jnp.dot(pt_ref[...], ys_ref[...], preferred_element_type=F32)


def _moe(x1, h2, lg, wgu, wd, tm):
    n = x1.shape[0]
    rcap = (tm + N_GROUPS * (MOE_C - 1)) // MOE_C * MOE_C
    rcap = -(-rcap // LANES) * LANES
    idx = np.arange(tm)
    tri = jnp.asarray(idx[None, :] < idx[:, None], BF16)
    return pl.pallas_call(
        functools.partial(_moe_kernel, rcap=rcap),
        grid=(n // tm, N_EXPERTS // MOE_EPS),
        in_specs=[
            pl.BlockSpec((tm, D_MODEL), lambda i, s: (i, 0), pipeline_mode=pl.Buffered(1)),
            pl.BlockSpec((tm, D_MODEL), lambda i, s: (i, 0), pipeline_mode=pl.Buffered(1)),
            pl.BlockSpec((tm, LANES), lambda i, s: (i, 0), pipeline_mode=pl.Buffered(1)),
            pl.BlockSpec((tm, tm), lambda i, s: (0, 0), pipeline_mode=pl.Buffered(1)),
            pl.BlockSpec((MOE_EPS, D_MODEL, 2 * D_EXPERT), lambda i, s: (s, 0, 0)),
            pl.BlockSpec((MOE_EPS, D_EXPERT, D_MODEL), lambda i, s: (s, 0, 0)),
        ],
        out_specs=pl.BlockSpec((tm, D_MODEL), lambda i, s: (i, 0), pipeline_mode=pl.Buffered(1)),
        out_shape=jax.ShapeDtypeStruct((n, D_MODEL), F32),
        scratch_shapes=[
            pltpu.VMEM((tm, rcap), BF16),
            pltpu.VMEM((rcap, D_MODEL), BF16),
            pltpu.VMEM((rcap, LANES), F32),
            pltpu.VMEM((rcap, D_MODEL), BF16),
            pltpu.SMEM((2 * N_GROUPS,), jnp.int32),
        ],
        compiler_params=pltpu.CompilerParams(dimension_semantics=("parallel", "arbitrary"),
                                             vmem_limit_bytes=MOE_VMEM_LIMIT),
        name="moe",
    )(x1, h2, lg, tri, wgu, wd)


def _tile(n, pref):
    t = pref
    while n % t:
        t //= 2
    return t


def _mixer_params(layer, norm1_g, w_in, a_conv_w, a_conv_b, a_gate_bias, a_out_norm_g,
                  b_qk_norm_g, b_rel_bias, c_qk_norm_g, c_lambda, c_sub_norm_g, t5_bias,
                  w_branch, w_out, nq_diff):
    n_small = 2 * A_HEADS
    cut = 4 * A_WIDTH
    n_proj = N_SEG * SEG
    w_main = jnp.concatenate([w_in[:, :cut], w_in[:, cut + n_small:n_small + n_proj]], axis=1)
    w_main = w_main.reshape(D_MODEL, N_SEG, SEG)[:, np.asarray(SEG_PERM), :]
    w_main = w_main.reshape(D_MODEL, n_proj).astype(BF16)
    w_gates = w_in[:, n_small + n_proj:].astype(BF16)
    w_if = jnp.pad(w_in[:, cut:cut + n_small], ((0, 0), (0, LANES - n_small))).astype(BF16)
    gain = jnp.ones((N_SEG, SEG), F32)
    gain = gain.at[N_ROW_SEG + TSEG_BQ].set(jnp.tile(b_qk_norm_g[0], B_HEADS) * (B_DH ** -0.5 * LOG2E))
    gain = gain.at[SEG_BK].set(jnp.tile(b_qk_norm_g[1], B_HEADS))
    gain = gain.at[N_ROW_SEG + TSEG_CQ].set(
        jnp.tile(c_qk_norm_g[0], 2 * C_HEADS) * (C_DQK ** -0.5 * LOG2E))
    gain = gain.at[SEG_CK].set(jnp.tile(c_qk_norm_g[1], 2 * C_HEADS))
    diff_bias, diff_far = _diff_bias(t5_bias, _diff_nnear(nq_diff))
    lam_init = 0.8 - 0.6 * math.exp(-0.3 * layer)
    lf32 = c_lambda.astype(F32)
    lam = jnp.exp(jnp.sum(lf32[0] * lf32[1])) - jnp.exp(jnp.sum(lf32[2] * lf32[3])) + lam_init
    return dict(
        g1=norm1_g.reshape(1, D_MODEL), w_main=w_main, w_if=w_if, w_gates=w_gates,
        gain=gain.reshape(N_SEG, 1, SEG),
        cw=a_conv_w, cb=a_conv_b.reshape(1, -1),
        gbr=jnp.pad(a_gate_bias, (0, LANES - n_small)).reshape(1, LANES),
        gbc=a_gate_bias.reshape(n_small, 1),
        ag=a_out_norm_g.reshape(1, A_WIDTH),
        band_bias=_band_bias(b_rel_bias),
        diff_bias=diff_bias, diff_far=diff_far,
        lam=lam.reshape(1).astype(F32),
        gsub=(c_sub_norm_g * (1.0 - lam_init)).reshape(C_DV, 1),
        wb=w_branch.astype(BF16), wo=w_out.astype(BF16),
    )


def _layer(x2, bsz, seq, mp, norm2_g, w_group, b_group, w_router, b_router, w_e_gate, w_e_up, w_e_down):
    n = bsz * seq
    p, pt, gif = _inproj(x2, mp["g1"], mp["w_main"], mp["w_if"], mp["gain"], _tile(n, 512))
    gift = jnp.transpose(gif[:, :2 * A_HEADS])
    ha, hbt, hct = _mixers(p, pt, gif, gift, mp, bsz, seq)

    wr = jnp.concatenate([w_router, w_group], axis=1)
    wr = jnp.pad(wr, ((0, 0), (0, LANES - wr.shape[1]))).astype(BF16)
    br = jnp.pad(jnp.concatenate([b_router, b_group]), (0, LANES - N_EXPERTS - N_GROUPS)).reshape(1, LANES)
    x1, h2, lg = _merge(ha, hbt, hct, x2, mp["g1"], mp["w_gates"], mp["wb"], mp["wo"],
                        norm2_g.reshape(1, D_MODEL), wr, br, _tile(n, 512))
    wgu = jnp.concatenate([w_e_gate, w_e_up], axis=-1).astype(BF16)
    return _moe(x1, h2, lg, wgu, w_e_down.astype(BF16), _tile(n, MOE_TM))


def kernel(x, norm1_g, w_in, a_conv_w, a_conv_b, a_gate_bias, a_out_norm_g, b_qk_norm_g, b_rel_bias,
           c_qk_norm_g, c_lambda, c_sub_norm_g, t5_bias, w_branch, w_out, norm2_g, w_group, b_group,
           w_router, b_router, w_e_gate, w_e_up, w_e_down):
    bsz, seq, _ = x.shape
    assert seq % DIFF_T == 0 and seq % MLSTM_L == 0 and seq % BAND_TQ == 0
    x2 = x.reshape(bsz * seq, D_MODEL)
    for l in range(norm1_g.shape[0]):
        mp = _mixer_params(l, norm1_g[l], w_in[l], a_conv_w[l], a_conv_b[l], a_gate_bias[l],
                           a_out_norm_g[l], b_qk_norm_g[l], b_rel_bias[l], c_qk_norm_g[l], c_lambda[l],
                           c_sub_norm_g[l], t5_bias, w_branch[l], w_out[l], seq // DIFF_T)
        x2 = _layer(x2, bsz, seq, mp, norm2_g[l], w_group[l], b_group[l], w_router[l], b_router[l],
                    w_e_gate[l], w_e_up[l], w_e_down[l])
    return x2.reshape(bsz, seq, D_MODEL)
```

```python
import functools
import math

import numpy as np
import jax
import jax.numpy as jnp
from jax import lax
from jax.experimental import pallas as pl
from jax.experimental.pallas import tpu as pltpu

F32 = jnp.float32
BF16 = jnp.bfloat16

D_MODEL = 1024
CHUNK = 64
EPS = 1e-6
NEG = -1e30
LOG2E = math.log2(math.e)

A_HEADS = 4
A_DH = 128
A_WIDTH = A_HEADS * A_DH
CONV_W = 4
GATE_CAP = 15.0

B_HEADS = 8
B_DH = 64
B_WIDTH = B_HEADS * B_DH
B_LEFT_CHUNKS = 8
B_MAX_REL = 256

C_HEADS = 4
C_DQK = 64
C_DV = 2 * C_DQK
C_WIDTH = C_HEADS * C_DV

T5_BUCKETS = 32
T5_MAX_DIST = 1024

N_BRANCH = 3
BRANCH_WIDTH = 512

N_GROUPS = 4
EXPERTS_PER_GROUP = 8
N_EXPERTS = N_GROUPS * EXPERTS_PER_GROUP
D_EXPERT = D_MODEL // 4

LANES = 128
SEG = 512
N_SEG = 10
VMEM_LIMIT = 48 * 1024 * 1024

SEG_AQ, SEG_AK, SEG_BK, SEG_CK = 0, 1, 2, 3
N_ROW_SEG = 4
TSEG_AV, TSEG_AO, TSEG_BQ, TSEG_BV, TSEG_CQ, TSEG_CV = 0, 1, 2, 3, 4, 5
N_T_SEG = N_SEG - N_ROW_SEG
SEG_PERM = (0, 1, 5, 8, 2, 3, 4, 6, 7, 9)

MLSTM_L = 128
CONV_HIST = 8
BAND_TQ = 128
BAND_NKB = 1 + (B_LEFT_CHUNKS * CHUNK) // BAND_TQ
BAND_ONES = 16
DIFF_T = 512
DIFF_ONES = 16
MOE_TM = 1024
MOE_C = 64
MOE_BIG = 4
MOE_EPS = EXPERTS_PER_GROUP
MOE_VMEM_LIMIT = 56 * 1024 * 1024

def _cparams(sem, flags=None):
    return pltpu.CompilerParams(dimension_semantics=sem, vmem_limit_bytes=VMEM_LIMIT, flags=flags)


NORM_SEGS = (SEG_BK, SEG_CK, N_ROW_SEG + TSEG_BQ, N_ROW_SEG + TSEG_CQ)
SIGMOID_SEGS = (N_ROW_SEG + TSEG_AO,)


def _head_norm_t(acc_t):
    rows, tm = acc_t.shape
    a3 = acc_t.reshape(rows // 64, 64, tm)
    ssq = jnp.sum(a3 * a3, axis=1, keepdims=True)
    return (a3 * lax.rsqrt(ssq * (1.0 / 64.0) + EPS)).reshape(rows, tm)


def _inproj_kernel(x_ref, g_ref, w_ref, wif_ref, gain_ref, gain_t_ref, p_ref, pt_ref, gif_ref):
    xf = x_ref[...]
    xn = (xf * lax.rsqrt(jnp.mean(xf * xf, axis=-1, keepdims=True) + EPS) * g_ref[...]).astype(BF16)
    gif_ref[...] = jnp.dot(xn, wif_ref[...], preferred_element_type=F32)
    for j in range(N_SEG):
        cols = slice(j * SEG, (j + 1) * SEG)
        acc = jnp.dot(xn, w_ref[:, cols], preferred_element_type=F32)
        if j in NORM_SEGS:
            acc_t = _head_norm_t(jnp.transpose(acc))
            if j < N_ROW_SEG:
                p_ref[:, cols] = (jnp.transpose(acc_t) * gain_ref[j]).astype(BF16)
            else:
                pt_ref[j - N_ROW_SEG] = (acc_t * gain_t_ref[NORM_SEGS.index(j) - 2]).astype(BF16)
            continue
        if j in SIGMOID_SEGS:
            acc = jax.nn.sigmoid(acc)
        if j < N_ROW_SEG:
            p_ref[:, cols] = acc.astype(BF16)
        else:
            pt_ref[j - N_ROW_SEG] = jnp.transpose(acc).astype(BF16)


def _inproj(x2, g, w, wif, gain, tm):
    n = x2.shape[0]
    tsegs = np.asarray([N_ROW_SEG + TSEG_BQ, N_ROW_SEG + TSEG_CQ])
    gain_t = jnp.broadcast_to(gain[tsegs, 0, :, None], (2, SEG, tm))

    def const(shape):
        return pl.BlockSpec(shape, lambda i: (0,) * len(shape), pipeline_mode=pl.Buffered(1))

    return pl.pallas_call(
        _inproj_kernel,
        grid=(n // tm,),
        in_specs=[
            pl.BlockSpec((tm, D_MODEL), lambda i: (i, 0)),
            const((1, D_MODEL)),
            const((D_MODEL, N_SEG * SEG)),
            const((D_MODEL, LANES)),
            const((N_SEG, 1, SEG)),
            const((2, SEG, tm)),
        ],
        out_specs=[
            pl.BlockSpec((tm, N_ROW_SEG * SEG), lambda i: (i, 0)),
            pl.BlockSpec((N_T_SEG, SEG, tm), lambda i: (0, 0, i)),
            pl.BlockSpec((tm, LANES), lambda i: (i, 0)),
        ],
        out_shape=[
            jax.ShapeDtypeStruct((n, N_ROW_SEG * SEG), BF16),
            jax.ShapeDtypeStruct((N_T_SEG, SEG, n), BF16),
            jax.ShapeDtypeStruct((n, LANES), F32),
        ],
        compiler_params=_cparams(("parallel",)),
        name="inproj",
    )(x2, g, w, wif, gain, gain_t)


def _log_sigmoid(z):
    return jnp.minimum(z, 0.0) - jnp.log(1.0 + jnp.exp(-jnp.abs(z)))


def _split3(a):
    hi = a.astype(BF16)
    r1 = a - hi.astype(F32)
    mid = r1.astype(BF16)
    lo = (r1 - mid.astype(F32)).astype(BF16)
    return hi, mid, lo


def _mlstm_kernel(*refs):
    _mlstm_carry(*refs)
    for _ in _mlstm_chunk(*refs):
        pass


def _mlstm_carry(aq_ref, ak_ref, vt_ref, aot_ref, gif_ref, gift_ref, cw_ref, cb_ref,
                 gbr_ref, gbc_ref, agt_ref, out_ref, ubuf, kq_ref, st_ref, ct_ref, n_ref, m_ref,
                 *, chunk=None):
    L = MLSTM_L
    c = pl.program_id(1) if chunk is None else chunk

    @pl.when(c == 0)
    def _():
        ubuf[0:CONV_HIST, :] = jnp.zeros((CONV_HIST, 2 * A_WIDTH), F32)
        ct_ref[...] = jnp.zeros_like(ct_ref)
        n_ref[...] = jnp.zeros_like(n_ref)
        m_ref[...] = jnp.zeros_like(m_ref)

    @pl.when(c > 0)
    def _():
        ubuf[0:CONV_HIST, :] = ubuf[L:L + CONV_HIST, :]


def _mlstm_chunk(aq_ref, ak_ref, vt_ref, aot_ref, gif_ref, gift_ref, cw_ref, cb_ref,
                 gbr_ref, gbc_ref, agt_ref, out_ref, ubuf, kq_ref, st_ref, ct_ref, n_ref, m_ref):
    L = MLSTM_L

    H = CONV_HIST
    ubuf[H:L + H, 0:A_WIDTH] = aq_ref[...].astype(F32)
    ubuf[H:L + H, A_WIDTH:2 * A_WIDTH] = ak_ref[...].astype(F32)
    y = cb_ref[...] + cw_ref[0:1, :] * ubuf[H:L + H, :]
    for t in range(1, CONV_W):
        y = y + cw_ref[t:t + 1, :] * ubuf[H - t:H - t + L, :]
    qk = y * jax.nn.sigmoid(y)
    q_t = jnp.transpose(qk[:, 0:A_WIDTH]).astype(BF16)
    k_all = (qk[:, A_WIDTH:2 * A_WIDTH] * (A_DH ** -0.5)).astype(BF16)

    zc = gif_ref[...] + gbr_ref[...]
    ig_c = GATE_CAP * jnp.tanh(zc * (1.0 / GATE_CAP))
    lf_c = _log_sigmoid(zc)
    zr = gift_ref[...] + gbc_ref[...]
    ig_r = GATE_CAP * jnp.tanh(zr * (1.0 / GATE_CAP))
    lf_r = _log_sigmoid(zr)

    row = lax.broadcasted_iota(jnp.int32, (L, L), 0)
    col = lax.broadcasted_iota(jnp.int32, (L, L), 1)
    causal = col <= row
    tril = jnp.where(causal, 1.0, 0.0).astype(BF16)
    triu = jnp.where(row <= col, 1.0, 0.0).astype(BF16)
    b_c = sum(jnp.dot(tril, piece, preferred_element_type=F32) for piece in _split3(lf_c))
    b_r = sum(jnp.dot(piece, triu, preferred_element_type=F32) for piece in _split3(lf_r))

    sub8 = lax.broadcasted_iota(jnp.int32, (8, L), 0)
    heads = [slice(h * A_DH, (h + 1) * A_DH) for h in range(A_HEADS)]
    for h, rows in enumerate(heads):
        kq_ref[h] = jnp.dot(k_all[:, rows], q_t[rows, :], preferred_element_type=F32)
    yield

    stats = []
    for h, rows in enumerate(heads):
        bcol = b_c[:, A_HEADS + h:A_HEADS + h + 1]
        brow = b_r[A_HEADS + h:A_HEADS + h + 1, :]
        igcol = ig_c[:, h:h + 1]
        m_prev = m_ref[h][:, 0:1]
        dmat = jnp.where(row <= col, brow + (igcol - bcol), NEG)
        inter = brow + m_prev
        m_t = jnp.maximum(inter, jnp.max(dmat, axis=0, keepdims=True))
        st = kq_ref[h] * jnp.exp(dmat - m_t)
        st_ref[h] = st.astype(BF16)
        stats.append((brow, m_prev, m_t, jnp.exp(inter - m_t), jnp.sum(st, axis=0, keepdims=True)))
        if h % 2 == 1:
            yield

    for h, rows in enumerate(heads):
        brow, m_prev, m_t, w_inter, st_sum = stats[h]
        qt = q_t[rows, :]
        k = k_all[:, rows]
        vt = vt_ref[rows, :]
        igrow = ig_r[h:h + 1, :]
        b_last = brow[:, L - 1:L]
        ct = ct_ref[h]
        n8 = n_ref[h]

        num = w_inter * jnp.dot(ct.astype(BF16), qt, preferred_element_type=F32)
        num = num + jnp.dot(vt, st_ref[h], preferred_element_type=F32)
        nq = jnp.dot(n8.astype(BF16), qt, preferred_element_type=F32)[0:1, :]
        den = w_inter * nq + st_sum
        hh = num * (1.0 / jnp.maximum(jnp.abs(den), jnp.exp(-m_t)))

        g_end = b_last - brow + igrow
        m_new = jnp.maximum(b_last + m_prev, jnp.max(g_end, axis=-1, keepdims=True))
        dec = jnp.exp(b_last + m_prev - m_new)
        w_end = jnp.exp(g_end - m_new)
        vw = (vt.astype(F32) * w_end).astype(BF16)
        ct_ref[h] = dec * ct + jnp.dot(vw, k, preferred_element_type=F32)
        w8 = jnp.where(sub8 == 0, w_end, 0.0).astype(BF16)
        n_ref[h] = dec * n8 + jnp.dot(w8, k, preferred_element_type=F32)
        m_ref[h] = jnp.broadcast_to(m_new, (1, LANES))

        hn = hh * lax.rsqrt(jnp.mean(hh * hh, axis=0, keepdims=True) + EPS) * agt_ref[rows, :]
        out_ref[rows, :] = (hn * aot_ref[rows, :].astype(F32)).astype(BF16)
        if h % 2 == 1 and h + 1 < A_HEADS:
            yield


def _mlstm_call(p, pt, gif, gift, cw, cb, gbr, gbc, ag, bsz, seq):
    L = MLSTM_L
    nc = seq // L
    n = bsz * seq
    agt = jnp.broadcast_to(ag.reshape(A_WIDTH, 1), (A_WIDTH, L))

    def tseg(j):
        return pl.BlockSpec((None, SEG, L), lambda b, c: (j, 0, b * nc + c))

    def full(shape):
        return pl.BlockSpec(shape, lambda b, c: (0,) * len(shape))

    in_specs = [
        pl.BlockSpec((L, SEG), lambda b, c: (b * nc + c, SEG_AQ)),
        pl.BlockSpec((L, SEG), lambda b, c: (b * nc + c, SEG_AK)),
        tseg(TSEG_AV), tseg(TSEG_AO),
        pl.BlockSpec((L, LANES), lambda b, c: (b * nc + c, 0)),
        pl.BlockSpec((8, L), lambda b, c: (0, b * nc + c)),
        full((CONV_W, 2 * A_WIDTH)), full((1, 2 * A_WIDTH)),
        full((1, LANES)), full((8, 1)), full((A_WIDTH, L)),
    ]
    scratch = [
        pltpu.VMEM((L + CONV_HIST, 2 * A_WIDTH), F32),
        pltpu.VMEM((A_HEADS, L, L), F32),
        pltpu.VMEM((A_HEADS, L, L), BF16),
        pltpu.VMEM((A_HEADS, A_DH, A_DH), F32),
        pltpu.VMEM((A_HEADS, 8, A_DH), F32),
        pltpu.VMEM((A_HEADS, 1, LANES), F32),
    ]
    return (in_specs, pl.BlockSpec((A_WIDTH, L), lambda b, c: (0, b * nc + c)),
            jax.ShapeDtypeStruct((A_WIDTH, n), BF16), scratch,
            (p, p, pt, pt, gif, gift, cw, cb, gbr, gbc, agt))


def _mlstm(p, pt, gif, gift, cw, cb, gbr, gbc, ag, bsz, seq):
    in_specs, out_spec, out_shape, scratch, operands = _mlstm_call(
        p, pt, gif, gift, cw, cb, gbr, gbc, ag, bsz, seq)
    return pl.pallas_call(
        _mlstm_kernel, grid=(bsz, seq // MLSTM_L), in_specs=in_specs, out_specs=out_spec,
        out_shape=out_shape, scratch_shapes=scratch,
        compiler_params=_cparams(("parallel", "arbitrary")), name="mlstm",
    )(*operands)


def _interleave(*gens):
    gens = list(gens)
    while gens:
        for g in list(gens):
            try:
                next(g)
            except StopIteration:
                gens.remove(g)
                continue
            yield


def _band_steps(refs, i, mask_start):
    nkb = BAND_NKB
    qt_ref = refs[0]
    k_refs = refs[1:1 + nkb]
    vt_refs = refs[1 + nkb:1 + 2 * nkb]
    bias_ref = refs[1 + 2 * nkb]
    out_ref = refs[2 + 2 * nkb]
    s_ref, mx_ref = refs[3 + 2 * nkb:5 + 2 * nkb]
    tq = BAND_TQ
    nk = nkb * tq

    k_all = jnp.concatenate([r[...] for r in k_refs], axis=0)
    vt_all = jnp.concatenate([r[...] for r in vt_refs], axis=1)
    ones = jnp.ones((BAND_ONES, nk), BF16)
    row = lax.broadcasted_iota(jnp.int32, (LANES, tq), 0)
    lo = row < B_DH
    if mask_start:
        kidx = lax.broadcasted_iota(jnp.int32, (nk, 1), 0)
        valid = (kidx + (i - (nkb - 1)) * tq) >= 0

    def score(p):
        rows = slice(p * LANES, (p + 1) * LANES)
        qtp = qt_ref[rows, :]
        zero = jnp.zeros_like(qtp)
        qbd = jnp.concatenate([jnp.where(lo, qtp, zero), jnp.where(lo, zero, qtp)], axis=1)
        s = jnp.dot(k_all[:, rows], qbd, preferred_element_type=F32) + bias_ref[p]
        if mask_start:
            s = jnp.where(valid, s, NEG)
        s_ref[p & 1] = s
        mx_ref[p & 1] = jnp.max(s, axis=0, keepdims=True)

    def finish(p):
        rows = slice(p * LANES, (p + 1) * LANES)
        pr = jnp.exp2((s_ref[p & 1] - mx_ref[p & 1]).astype(BF16))
        o = jnp.dot(jnp.concatenate([vt_all[rows, :], ones], axis=0), pr,
                    preferred_element_type=F32)
        o = o[0:LANES, :] / o[LANES:LANES + 1, :]
        out_ref[rows, :] = jnp.where(lo, o[:, 0:tq], o[:, tq:2 * tq]).astype(BF16)

    score(0)
    yield
    for p in range(B_HEADS // 2):
        if p + 1 < B_HEADS // 2:
            score(p + 1)
        finish(p)
        if p + 1 < B_HEADS // 2:
            yield


def _band_variants(i, region):
    @pl.when(i < BAND_NKB - 1)
    def _():
        region(True)

    @pl.when(i >= BAND_NKB - 1)
    def _():
        region(False)


def _drain(gen):
    for _ in gen:
        pass


def _band_kernel(*refs):
    i = pl.program_id(1)
    _band_variants(i, lambda mask: _drain(_band_steps(refs, i, mask)))


def _band_call(p, pt, bias, bsz, seq):
    tq = BAND_TQ
    nkb = BAND_NKB
    nq = seq // tq
    n = bsz * seq

    def kblk(d):
        return pl.BlockSpec((tq, SEG), lambda b, i: (b * nq + jnp.maximum(i - d, 0), SEG_BK))

    def vblk(d):
        return pl.BlockSpec((None, SEG, tq), lambda b, i: (TSEG_BV, 0, b * nq + jnp.maximum(i - d, 0)))

    in_specs = [pl.BlockSpec((None, SEG, tq), lambda b, i: (TSEG_BQ, 0, b * nq + i))]
    in_specs += [kblk(d) for d in range(nkb - 1, -1, -1)]
    in_specs += [vblk(d) for d in range(nkb - 1, -1, -1)]
    in_specs += [pl.BlockSpec(bias.shape, lambda b, i: (0, 0, 0))]
    scratch = [
        pltpu.VMEM((2, nkb * tq, 2 * tq), F32),
        pltpu.VMEM((2, 1, 2 * tq), F32),
    ]
    return (in_specs, pl.BlockSpec((B_WIDTH, tq), lambda b, i: (0, b * nq + i)),
            jax.ShapeDtypeStruct((B_WIDTH, n), BF16), scratch,
            (pt, *([p] * nkb), *([pt] * nkb), bias))


def _band(p, pt, bias, bsz, seq):
    in_specs, out_spec, out_shape, scratch, operands = _band_call(p, pt, bias, bsz, seq)
    return pl.pallas_call(
        _band_kernel, grid=(bsz, seq // BAND_TQ), in_specs=in_specs, out_specs=out_spec,
        out_shape=out_shape, scratch_shapes=scratch,
        compiler_params=_cparams(("parallel", "parallel")), name="band_attn",
    )(*operands)


def _toeplitz(base, m, n):
    period = base.shape[-1]
    assert n <= period - 1
    reps = (1,) * (base.ndim - 1) + (m,)
    big = jnp.tile(base, reps)[..., :m * (period - 1)]
    return big.reshape(base.shape[:-1] + (m, period - 1))[..., :n]


def _band_bias(b_rel):
    tq = BAND_TQ
    nk = BAND_NKB * tq
    period = tq + nk
    e = np.arange(period)
    e = np.where(e < nk, e, e - period)
    rel = np.clip((nk - tq) - e, -B_MAX_REL, B_MAX_REL) + B_MAX_REL
    bias = _toeplitz(jnp.transpose(b_rel[rel]).astype(F32), tq, nk)
    qpos = np.arange(tq)
    kpos = np.arange(nk) - (nk - tq)
    qc = qpos[:, None] // CHUNK
    kc = np.floor_divide(kpos[None, :], CHUNK)
    allowed = (kc <= qc) & (kc >= qc - B_LEFT_CHUNKS)
    bias = jnp.where(allowed[None], bias * LOG2E, NEG)
    return jnp.swapaxes(bias.reshape(B_HEADS // 2, 2 * tq, nk), 1, 2)


def _diff_kernel(lam_ref, cfar_ref, qt_ref, k_ref, vt_ref, bias_ref, g_ref, out_ref,
                 qbd_ref, s_ref, mx_ref, acc_ref, m_ref, *, nnear, first_region=None):
    T = DIFF_T
    h = pl.program_id(1)
    qi = pl.program_id(2)
    cfar = cfar_ref[h]

    def prologue():
        qt = qt_ref[...]
        row = lax.broadcasted_iota(jnp.int32, (2 * C_DQK, T), 0)
        zero = jnp.zeros_like(qt)
        qbd_ref[:, 0:T] = jnp.where(row < C_DQK, qt, zero)
        qbd_ref[:, T:2 * T] = jnp.where(row < C_DQK, zero, qt)
        m_ref[...] = jnp.full(m_ref.shape, NEG, F32)
        acc_ref[...] = jnp.zeros_like(acc_ref)

    def near_bias(t):
        return bias_ref[t] if t < nnear else None

    def stage_a(t, slot, bias, maps=(0, 1)):
        j = jnp.maximum(qi - t, 0)
        k = k_ref[pl.ds(pl.multiple_of(j * T, T), T), :]
        for mp in maps:
            sm = jnp.dot(k, qbd_ref[:, mp * T:(mp + 1) * T], preferred_element_type=F32)
            if bias is not None:
                sm = sm + bias
            s_ref[slot, mp] = sm
            mx = jnp.max(sm, axis=0, keepdims=True)
            mx_ref[slot, mp] = mx + cfar if bias is None else mx

    def stage_bc(t, slot, far, maps=(0, 1)):
        j = qi - t
        vt = jnp.concatenate([vt_ref[:, pl.ds(pl.multiple_of(j * T, T), T)],
                              jnp.ones((DIFF_ONES, T), BF16)], axis=0)
        for mp in maps:
            m_old = m_ref[mp]
            m_new = jnp.maximum(m_old, mx_ref[slot, mp])
            shift = m_new - cfar if far else m_new
            pr = jnp.exp2((s_ref[slot, mp] - shift).astype(BF16))
            m_ref[mp] = m_new
            acc_ref[mp] = jnp.exp2(m_old - m_new) * acc_ref[mp] + jnp.dot(
                vt, pr, preferred_element_type=F32)

    def full_step(t, slot, far, next_bias):
        for mp in range(2):
            stage_a(t + 1, 1 - slot, next_bias, (mp,))
            stage_bc(t, slot, far, (mp,))

    def first_steps(other, n_steps=1):
        prologue()
        next(other, None)
        stage_a(0, 0, near_bias(0))
        next(other, None)
        for t in range(n_steps):
            for mp in range(2):
                stage_a(t + 1, 1 - (t & 1), near_bias(t + 1), (mp,))
                stage_bc(t, t & 1, False, (mp,))
                next(other, None)
        _drain(other)

    all_near = qi >= nnear - 1
    if first_region is None:
        first_steps(iter(()))
        near_done = False
    else:
        near_done = first_region(first_steps, all_near, nnear)
    for t in range(1, nnear):
        @pl.when((qi >= t) & jnp.logical_not(near_done))
        def _(t=t):
            full_step(t, t & 1, False, near_bias(t + 1))

    @pl.when(qi >= nnear)
    def _():
        n_full = qi - nnear
        s0 = nnear & 1

        def pair(i, carry):
            full_step(nnear + 2 * i, s0, True, None)
            full_step(nnear + 2 * i + 1, 1 - s0, True, None)
            return carry

        lax.fori_loop(0, n_full // 2, pair, 0)

        @pl.when(n_full % 2 == 1)
        def _():
            full_step(qi - 1, s0, True, None)

        stage_bc(qi, qi & 1, True)

    o1 = acc_ref[0, 0:C_DV, :] / acc_ref[0, C_DV:C_DV + 1, :]
    o2 = acc_ref[1, 0:C_DV, :] / acc_ref[1, C_DV:C_DV + 1, :]
    o = o1 - lam_ref[0] * o2
    on = o * lax.rsqrt(jnp.mean(o * o, axis=0, keepdims=True) + EPS) * g_ref[...]
    out_ref[...] = on.astype(BF16)


def _diff_nnear(nq):
    d_sat = -(-(T5_MAX_DIST - 1 + DIFF_T) // DIFF_T)
    return min(d_sat, nq)


def _diff_call(p, pt, bias, cfar, lam, gsub, bsz, seq):
    T = DIFF_T
    nq = seq // T
    nnear = bias.shape[1]
    kcol = SEG_CK * SEG // LANES
    n = bsz * seq
    in_specs = [
        pl.BlockSpec(memory_space=pltpu.SMEM),
        pl.BlockSpec(memory_space=pltpu.SMEM),
        pl.BlockSpec((None, 2 * C_DQK, T), lambda b, h, i: (TSEG_CQ, h, b * nq + i)),
        pl.BlockSpec((seq, LANES), lambda b, h, i: (b, kcol + h)),
        pl.BlockSpec((None, C_DV, seq), lambda b, h, i: (TSEG_CV, h, b)),
        pl.BlockSpec((None, nnear, T, T), lambda b, h, i: (h, 0, 0, 0)),
        pl.BlockSpec((C_DV, 1), lambda b, h, i: (0, 0)),
    ]
    scratch = [
        pltpu.VMEM((2 * C_DQK, 2 * T), BF16),
        pltpu.VMEM((2, 2, T, T), F32),
        pltpu.VMEM((2, 2, 1, T), F32),
        pltpu.VMEM((2, C_DV + DIFF_ONES, T), F32),
        pltpu.VMEM((2, 1, T), F32),
    ]
    return (in_specs, pl.BlockSpec((C_DV, T), lambda b, h, i: (h, b * nq + i)),
            jax.ShapeDtypeStruct((C_WIDTH, n), BF16), scratch, (lam, cfar, pt, p, pt, bias, gsub))


def _diff(p, pt, bias, cfar, lam, gsub, bsz, seq):
    in_specs, out_spec, out_shape, scratch, operands = _diff_call(p, pt, bias, cfar, lam, gsub, bsz, seq)
    return pl.pallas_call(
        functools.partial(_diff_kernel, nnear=bias.shape[1]),
        grid=(bsz, C_HEADS, seq // DIFF_T), in_specs=in_specs, out_specs=out_spec,
        out_shape=out_shape, scratch_shapes=scratch,
        compiler_params=_cparams(("parallel", "parallel", "arbitrary")), name="diff_attn",
    )(*operands)


def _mixers_kernel(*refs, n_in, n_scr, nnear):
    d_in, m_in, b_in = (refs[sum(n_in[:k]):sum(n_in[:k + 1])] for k in range(3))
    hct_ref, hat_ref, hbt_ref = refs[sum(n_in):sum(n_in) + 3]
    scr = refs[sum(n_in) + 3:]
    d_scr, m_scr, b_scr = (scr[sum(n_scr[:k]):sum(n_scr[:k + 1])] for k in range(3))
    m_refs = (*m_in, hat_ref, *m_scr)
    b_refs = (*b_in, hbt_ref, *b_scr)
    chunk = pl.program_id(1) * pl.num_programs(2) + pl.program_id(2)
    _mlstm_carry(*m_refs, chunk=chunk)

    def first_region(trace, all_near, n_near):
        def other(mask):
            return _interleave(_band_steps(b_refs, chunk, mask), _mlstm_chunk(*m_refs))

        masked = chunk < BAND_NKB - 1
        wide = jnp.logical_not(masked) & all_near

        @pl.when(masked)
        def _():
            trace(other(True), 1)

        @pl.when(wide)
        def _():
            trace(other(False), n_near)

        @pl.when(jnp.logical_not(masked) & jnp.logical_not(all_near))
        def _():
            trace(other(False), 1)

        return wide

    _diff_kernel(*d_in, hct_ref, *d_scr, nnear=nnear, first_region=first_region)


def _mixers(p, pt, gif, gift, mp, bsz, seq):
    nq = seq // DIFF_T
    assert C_HEADS * nq == seq // MLSTM_L and MLSTM_L == BAND_TQ

    def on_diff_grid(spec):
        if spec.index_map is None:
            return spec
        return pl.BlockSpec(spec.block_shape, lambda b, h, i, f=spec.index_map: f(b, h * nq + i))

    d_specs, d_out, d_shape, d_scr, d_ops = _diff_call(p, pt, mp["diff_bias"], mp["diff_far"], mp["lam"],
                                                       mp["gsub"], bsz, seq)
    m_specs, m_out, m_shape, m_scr, m_ops = _mlstm_call(p, pt, gif, gift, mp["cw"], mp["cb"], mp["gbr"],
                                                        mp["gbc"], mp["ag"], bsz, seq)
    b_specs, b_out, b_shape, b_scr, b_ops = _band_call(p, pt, mp["band_bias"], bsz, seq)
    hct, hat, hbt = pl.pallas_call(
        functools.partial(_mixers_kernel, n_in=(len(d_specs), len(m_specs), len(b_specs)),
                          n_scr=(len(d_scr), len(m_scr), len(b_scr)), nnear=mp["diff_bias"].shape[1]),
        grid=(bsz, C_HEADS, nq),
        in_specs=d_specs + [on_diff_grid(s) for s in m_specs + b_specs],
        out_specs=[d_out, on_diff_grid(m_out), on_diff_grid(b_out)],
        out_shape=[d_shape, m_shape, b_shape],
        scratch_shapes=d_scr + m_scr + b_scr,
        compiler_params=_cparams(("parallel", "arbitrary", "arbitrary")),
        name="mixers",
    )(*d_ops, *m_ops, *b_ops)
    return hat, hbt, hct


def _t5_bucket(rel):
    nb = T5_BUCKETS // 2
    max_exact = nb // 2
    ret = (rel > 0).astype(jnp.int32) * nb
    n = jnp.abs(rel)
    large = max_exact + (jnp.log(jnp.maximum(n, max_exact).astype(F32) / max_exact)
                         / math.log(T5_MAX_DIST / max_exact) * (nb - max_exact)).astype(jnp.int32)
    large = jnp.minimum(large, nb - 1)
    return ret + jnp.where(n < max_exact, n, large)


def _diff_bias(t5_table, nnear):
    T = DIFF_T
    e = np.arange(2 * T)
    amc = np.where(e < T, -e, 2 * T - e)
    rel = jnp.asarray(-np.arange(nnear)[:, None] * T + amc[None, :], jnp.int32)
    base = jnp.moveaxis(t5_table[_t5_bucket(rel)], -1, 0).astype(F32) * LOG2E
    tiles = _toeplitz(base, T, T)
    a = np.arange(T)[:, None]
    c = np.arange(T)[None, :]
    allowed = np.ones((nnear, T, T), bool)
    allowed[0] = (a // CHUNK) <= (c // CHUNK)
    far = t5_table[_t5_bucket(jnp.asarray(-T5_MAX_DIST, jnp.int32))].astype(F32) * LOG2E
    return jnp.where(allowed[None], tiles, NEG), far


def _merge_kernel(hat_ref, hbt_ref, hct_ref, x_ref, n1_ref, wg_ref, wb_ref, wo_ref,
                  n2_ref, wr_ref, br_ref, x1_ref, h2_ref, lg_ref, y_ref):
    tn = (((0,), (0,)), ((), ()))
    tm = x_ref.shape[0]
    halves = [slice(i * (tm // 2), (i + 1) * (tm // 2)) for i in range(2)]
    for r in halves:
        xf = x_ref[r, :]
        xn = (xf * lax.rsqrt(jnp.mean(xf * xf, axis=-1, keepdims=True) + EPS) * n1_ref[...]).astype(BF16)
        y = None
        for b, ht_ref in enumerate((hat_ref, hbt_ref, hct_ref)):
            gate = jax.nn.sigmoid(jnp.dot(xn, wg_ref[:, b * D_MODEL:(b + 1) * D_MODEL],
                                          preferred_element_type=F32))
            yb = gate * lax.dot_general(ht_ref[:, r], wb_ref[b], tn, preferred_element_type=F32)
            y = yb if y is None else y + yb
        y_ref[r, :] = y.astype(BF16)
    for r in halves:
        x1 = x_ref[r, :] + jnp.dot(y_ref[r, :], wo_ref[...], preferred_element_type=F32)
        x1_ref[r, :] = x1
        h2 = x1 * lax.rsqrt(jnp.mean(x1 * x1, axis=-1, keepdims=True) + EPS) * n2_ref[...]
        h2_ref[r, :] = h2.astype(BF16)
    for r in halves:
        lg_ref[r, :] = jnp.dot(h2_ref[r, :], wr_ref[...], preferred_element_type=F32) + br_ref[...]


def _merge(ha, hbt, hct, x2, n1, wg, wb, wo, n2, wr, br, tm):
    n = x2.shape[0]

    def rows(width, col=0):
        return pl.BlockSpec((tm, width), lambda i: (i, col))

    def cols():
        return pl.BlockSpec((BRANCH_WIDTH, tm), lambda i: (0, i))

    def full(shape):
        return pl.BlockSpec(shape, lambda i: (0,) * len(shape), pipeline_mode=pl.Buffered(1))

    return pl.pallas_call(
        _merge_kernel,
        grid=(n // tm,),
        in_specs=[
            cols(), cols(), cols(),
            rows(D_MODEL),
            full((1, D_MODEL)), full((D_MODEL, N_BRANCH * D_MODEL)),
            full((N_BRANCH, BRANCH_WIDTH, D_MODEL)), full((D_MODEL, D_MODEL)),
            full((1, D_MODEL)), full((D_MODEL, LANES)), full((1, LANES)),
        ],
        out_specs=[rows(D_MODEL), rows(D_MODEL), rows(LANES)],
        out_shape=[
            jax.ShapeDtypeStruct((n, D_MODEL), F32),
            jax.ShapeDtypeStruct((n, D_MODEL), BF16),
            jax.ShapeDtypeStruct((n, LANES), F32),
        ],
        scratch_shapes=[pltpu.VMEM((tm, D_MODEL), BF16)],
        compiler_params=_cparams(("parallel",)),
        name="merge",
    )(ha, hbt, hct, x2, n1, wg, wb, wo, n2, wr, br)


def _combine_weights(lg):
    lanef = lax.broadcasted_iota(jnp.int32, lg.shape, 1).astype(F32)
    big = 1e9
    is_g = (lanef >= N_EXPERTS) & (lanef < N_EXPERTS + N_GROUPS)
    gl = jnp.where(is_g, lg, -jnp.inf)
    gmax = jnp.max(gl, axis=-1, keepdims=True)
    g_idx = jnp.min(jnp.where(gl == gmax, lanef - N_EXPERTS, big), axis=-1, keepdims=True)
    p_g = 1.0 / jnp.sum(jnp.exp(gl - gmax), axis=-1, keepdims=True)
    in_grp = (lanef >= g_idx * EXPERTS_PER_GROUP) & (lanef < (g_idx + 1.0) * EXPERTS_PER_GROUP)
    el = jnp.where(in_grp, lg, -jnp.inf)
    ee = jnp.exp(el - jnp.max(el, axis=-1, keepdims=True))
    ep = ee / jnp.sum(ee, axis=-1, keepdims=True)
    ep = jnp.where(in_grp, ep, -1.0)
    v1 = jnp.max(ep, axis=-1, keepdims=True)
    i1 = jnp.min(jnp.where(ep == v1, lanef, big), axis=-1, keepdims=True)
    ep2 = jnp.where(lanef == i1, -1.0, ep)
    v2 = jnp.max(ep2, axis=-1, keepdims=True)
    i2 = jnp.min(jnp.where(ep2 == v2, lanef, big), axis=-1, keepdims=True)
    tot = v1 + v2
    comb = jnp.where(lanef == i1, p_g * (v1 / tot), 0.0) + jnp.where(lanef == i2, p_g * (v2 / tot), 0.0)
    return comb, g_idx


def _moe_kernel(x1_hbm, h2_ref, lg_ref, tri_ref, wgu_ref, wd_ref, out_ref,
                pt_ref, xs_ref, ws_ref, ys_ref, meta_ref, x1_buf, x1_sem, *, rcap):
    C = MOE_C
    s = pl.program_id(1)
    tm = h2_ref.shape[0]

    def x1_copy():
        rows = pl.ds(pl.multiple_of(pl.program_id(0) * tm, tm), tm)
        return pltpu.make_async_copy(x1_hbm.at[rows, :], x1_buf, x1_sem)

    @pl.when(s == 0)
    def _():
        x1_copy().start()

    @pl.when(s == 0)
    def _():
        comb, g_idx = _combine_weights(lg_ref[...])
        lanef = lax.broadcasted_iota(jnp.int32, comb.shape, 1).astype(F32)
        mine = lanef == g_idx
        onehot = jnp.where(mine, 1.0, 0.0)
        ranks = jnp.dot(tri_ref[...], onehot.astype(BF16), preferred_element_type=F32)
        dest = jnp.sum(jnp.where(mine, ranks, 0.0), axis=-1, keepdims=True)
        off = jnp.int32(0)
        for g in range(N_GROUPS):
            cnt = jnp.sum(onehot[:, g:g + 1]).astype(jnp.int32)
            nchunk = (cnt + (C - 1)) // C
            meta_ref[g] = off
            meta_ref[N_GROUPS + g] = nchunk
            dest = dest + jnp.where(g_idx == float(g), (off * C).astype(F32), 0.0)
            off = off + nchunk
        slot = lax.broadcasted_iota(jnp.int32, (tm, rcap), 1).astype(F32)
        pt = jnp.where(dest == slot, 1.0, 0.0).astype(BF16)
        pt_ref[...] = pt
        comb_hi = comb.astype(BF16)
        comb_lo = (comb - comb_hi.astype(F32)).astype(BF16)
        packed = jnp.concatenate([h2_ref[...], comb_hi, comb_lo], axis=1)
        srt = lax.dot_general(pt, packed, (((0,), (0,)), ((), ())), preferred_element_type=F32)
        xs_ref[...] = srt[:, 0:D_MODEL].astype(BF16)
        ws_ref[...] = srt[:, D_MODEL:D_MODEL + LANES] + srt[:, D_MODEL + LANES:D_MODEL + 2 * LANES]
        ys_ref[...] = jnp.zeros_like(ys_ref)

    def do_chunk(r0, m):
        r0 = pl.multiple_of(r0, C)
        xc = xs_ref[pl.ds(r0, m), :]
        wsc = ws_ref[pl.ds(r0, m), :]
        lane = lax.broadcasted_iota(jnp.int32, (m, LANES), 1)
        y = None
        for k in range(MOE_EPS):
            gu = jnp.dot(xc, wgu_ref[k], preferred_element_type=F32)
            gate = gu[:, 0:D_EXPERT]
            wk = jnp.sum(jnp.where(lane == s * MOE_EPS + k, wsc, 0.0), axis=-1, keepdims=True)
            he = (gate * jax.nn.sigmoid(gate) * gu[:, D_EXPERT:2 * D_EXPERT] * wk).astype(BF16)
            yk = jnp.dot(he, wd_ref[k], preferred_element_type=F32)
            y = yk if y is None else y + yk
        ys_ref[pl.ds(r0, m), :] = y.astype(BF16)

    g = s // (EXPERTS_PER_GROUP // MOE_EPS)
    start = meta_ref[g]
    nchunk = meta_ref[N_GROUPS + g]

    def big(i, carry):
        do_chunk((start + MOE_BIG * i) * C, MOE_BIG * C)
        return carry

    common = nchunk == MOE_BIG + 1

    @pl.when(common)
    def _():
        do_chunk(start * C, (MOE_BIG + 1) * C)

    @pl.when(jnp.logical_not(common))
    def _():
        nbig = nchunk // MOE_BIG
        lax.fori_loop(0, nbig, big, 0)
        done = nbig * MOE_BIG
        size = MOE_BIG // 2
        while size >= 1:
            @pl.when((nchunk & size) != 0)
            def _(done=done, size=size):
                do_chunk((start + done) * C, size * C)
            done = done + (nchunk & size)
            size //= 2

    @pl.when(s == pl.num_programs(1) - 1)
    def _():
        x1_copy().wait()
        out_ref[...] = x1_buf[...] + jnp.dot(pt_ref[...], ys_ref[...], preferred_element_type=F32)


def _moe(x1, h2, lg, wgu, wd, tm):
    n = x1.shape[0]
    rcap = (tm + N_GROUPS * (MOE_C - 1)) // MOE_C * MOE_C
    rcap = -(-rcap // LANES) * LANES
    idx = np.arange(tm)
    tri = jnp.asarray(idx[None, :] < idx[:, None], BF16)
    return pl.pallas_call(
        functools.partial(_moe_kernel, rcap=rcap),
        grid=(n // tm, N_EXPERTS // MOE_EPS),
        in_specs=[
            pl.BlockSpec(memory_space=pl.ANY),
            pl.BlockSpec((tm, D_MODEL), lambda i, s: (i, 0), pipeline_mode=pl.Buffered(1)),
            pl.BlockSpec((tm, LANES), lambda i, s: (i, 0), pipeline_mode=pl.Buffered(1)),
            pl.BlockSpec((tm, tm), lambda i, s: (0, 0), pipeline_mode=pl.Buffered(1)),
            pl.BlockSpec((MOE_EPS, D_MODEL, 2 * D_EXPERT), lambda i, s: (s, 0, 0)),
            pl.BlockSpec((MOE_EPS, D_EXPERT, D_MODEL), lambda i, s: (s, 0, 0)),
        ],
        out_specs=pl.BlockSpec((tm, D_MODEL), lambda i, s: (i, 0), pipeline_mode=pl.Buffered(1)),
        out_shape=jax.ShapeDtypeStruct((n, D_MODEL), F32),
        scratch_shapes=[
            pltpu.VMEM((tm, rcap), BF16),
            pltpu.VMEM((rcap, D_MODEL), BF16),
            pltpu.VMEM((rcap, LANES), F32),
            pltpu.VMEM((rcap, D_MODEL), BF16),
            pltpu.SMEM((2 * N_GROUPS,), jnp.int32),
            pltpu.VMEM((tm, D_MODEL), F32),
            pltpu.SemaphoreType.DMA,
        ],
        compiler_params=pltpu.CompilerParams(dimension_semantics=("parallel", "arbitrary"),
                                             vmem_limit_bytes=MOE_VMEM_LIMIT),
        name="moe",
    )(x1, h2, lg, tri, wgu, wd)


def _tile(n, pref):
    t = pref
    while n % t:
        t //= 2
    return t


def _mixer_params(layer, norm1_g, w_in, a_conv_w, a_conv_b, a_gate_bias, a_out_norm_g,
                  b_qk_norm_g, b_rel_bias, c_qk_norm_g, c_lambda, c_sub_norm_g, t5_bias,
                  w_branch, w_out, nq_diff):
    n_small = 2 * A_HEADS
    cut = 4 * A_WIDTH
    n_proj = N_SEG * SEG
    w_main = jnp.concatenate([w_in[:, :cut], w_in[:, cut + n_small:n_small + n_proj]], axis=1)
    w_main = w_main.reshape(D_MODEL, N_SEG, SEG)[:, np.asarray(SEG_PERM), :]
    w_main = w_main.reshape(D_MODEL, n_proj).astype(BF16)
    w_gates = w_in[:, n_small + n_proj:].astype(BF16)
    w_if = jnp.pad(w_in[:, cut:cut + n_small], ((0, 0), (0, LANES - n_small))).astype(BF16)
    gain = jnp.ones((N_SEG, SEG), F32)
    gain = gain.at[N_ROW_SEG + TSEG_BQ].set(jnp.tile(b_qk_norm_g[0], B_HEADS) * (B_DH ** -0.5 * LOG2E))
    gain = gain.at[SEG_BK].set(jnp.tile(b_qk_norm_g[1], B_HEADS))
    gain = gain.at[N_ROW_SEG + TSEG_CQ].set(
        jnp.tile(c_qk_norm_g[0], 2 * C_HEADS) * (C_DQK ** -0.5 * LOG2E))
    gain = gain.at[SEG_CK].set(jnp.tile(c_qk_norm_g[1], 2 * C_HEADS))
    diff_bias, diff_far = _diff_bias(t5_bias, _diff_nnear(nq_diff))
    lam_init = 0.8 - 0.6 * math.exp(-0.3 * layer)
    lf32 = c_lambda.astype(F32)
    lam = jnp.exp(jnp.sum(lf32[0] * lf32[1])) - jnp.exp(jnp.sum(lf32[2] * lf32[3])) + lam_init
    return dict(
        g1=norm1_g.reshape(1, D_MODEL), w_main=w_main, w_if=w_if, w_gates=w_gates,
        gain=gain.reshape(N_SEG, 1, SEG),
        cw=a_conv_w, cb=a_conv_b.reshape(1, -1),
        gbr=jnp.pad(a_gate_bias, (0, LANES - n_small)).reshape(1, LANES),
        gbc=a_gate_bias.reshape(n_small, 1),
        ag=a_out_norm_g.reshape(1, A_WIDTH),
        band_bias=_band_bias(b_rel_bias),
        diff_bias=diff_bias, diff_far=diff_far,
        lam=lam.reshape(1).astype(F32),
        gsub=(c_sub_norm_g * (1.0 - lam_init)).reshape(C_DV, 1),
        wb=w_branch.astype(BF16), wo=w_out.astype(BF16),
    )


def _layer(x2, bsz, seq, mp, norm2_g, w_group, b_group, w_router, b_router, w_e_gate, w_e_up, w_e_down):
    n = bsz * seq
    p, pt, gif = _inproj(x2, mp["g1"], mp["w_main"], mp["w_if"], mp["gain"], _tile(n, 512))
    gift = jnp.transpose(gif[:, :2 * A_HEADS])
    ha, hbt, hct = _mixers(p, pt, gif, gift, mp, bsz, seq)

    wr = jnp.concatenate([w_router, w_group], axis=1)
    wr = jnp.pad(wr, ((0, 0), (0, LANES - wr.shape[1]))).astype(BF16)
    br = jnp.pad(jnp.concatenate([b_router, b_group]), (0, LANES - N_EXPERTS - N_GROUPS)).reshape(1, LANES)
    x1, h2, lg = _merge(ha, hbt, hct, x2, mp["g1"], mp["w_gates"], mp["wb"], mp["wo"],
                        norm2_g.reshape(1, D_MODEL), wr, br, _tile(n, 512))
    wgu = jnp.concatenate([w_e_gate, w_e_up], axis=-1).astype(BF16)
    return _moe(x1, h2, lg, wgu, w_e_down.astype(BF16), _tile(n, MOE_TM))


def kernel(x, norm1_g, w_in, a_conv_w, a_conv_b, a_gate_bias, a_out_norm_g, b_qk_norm_g, b_rel_bias,
           c_qk_norm_g, c_lambda, c_sub_norm_g, t5_bias, w_branch, w_out, norm2_g, w_group, b_group,
           w_router, b_router, w_e_gate, w_e_up, w_e_down):
    bsz, seq, _ = x.shape
    assert seq % DIFF_T == 0 and seq % MLSTM_L == 0 and seq % BAND_TQ == 0
    x2 = x.reshape(bsz * seq, D_MODEL)
    for l in range(norm1_g.shape[0]):
        mp = _mixer_params(l, norm1_g[l], w_in[l], a_conv_w[l], a_conv_b[l], a_gate_bias[l],
                           a_out_norm_g[l], b_qk_norm_g[l], b_rel_bias[l], c_qk_norm_g[l], c_lambda[l],
                           c_sub_norm_g[l], t5_bias, w_branch[l], w_out[l], seq // DIFF_T)
        x2 = _layer(x2, bsz, seq, mp, norm2_g[l], w_group[l], b_group[l], w_router[l], b_router[l],
                    w_e_gate[l], w_e_up[l], w_e_down[l])
    return x2.reshape(bsz, seq, D_MODEL)
```

```python
import functools
import math

import numpy as np
import jax
import jax.numpy as jnp
from jax import lax
from jax.experimental import pallas as pl
from jax.experimental.pallas import tpu as pltpu

F32 = jnp.float32
BF16 = jnp.bfloat16

D_MODEL = 1024
CHUNK = 64
EPS = 1e-6
NEG = -1e30
LOG2E = math.log2(math.e)

A_HEADS = 4
A_DH = 128
A_WIDTH = A_HEADS * A_DH
CONV_W = 4
GATE_CAP = 15.0

B_HEADS = 8
B_DH = 64
B_WIDTH = B_HEADS * B_DH
B_LEFT_CHUNKS = 8
B_MAX_REL = 256

C_HEADS = 4
C_DQK = 64
C_DV = 2 * C_DQK
C_WIDTH = C_HEADS * C_DV

T5_BUCKETS = 32
T5_MAX_DIST = 1024

N_BRANCH = 3
BRANCH_WIDTH = 512

N_GROUPS = 4
EXPERTS_PER_GROUP = 8
N_EXPERTS = N_GROUPS * EXPERTS_PER_GROUP
D_EXPERT = D_MODEL // 4

LANES = 128
SEG = 512
N_SEG = 10
VMEM_LIMIT = 48 * 1024 * 1024

SEG_AQ, SEG_AK, SEG_BK, SEG_CK = 0, 1, 2, 3
N_ROW_SEG = 4
TSEG_AV, TSEG_AO, TSEG_BQ, TSEG_BV, TSEG_CQ, TSEG_CV = 0, 1, 2, 3, 4, 5
N_T_SEG = N_SEG - N_ROW_SEG
SEG_PERM = (0, 1, 5, 8, 2, 3, 4, 6, 7, 9)

MLSTM_L = 128
CONV_HIST = 8
BAND_TQ = 128
BAND_NKB = 1 + (B_LEFT_CHUNKS * CHUNK) // BAND_TQ
BAND_ONES = 16
DIFF_T = 512
DIFF_ONES = 16
MOE_TM = 1024
MOE_C = 64
MOE_BIG = 4
MOE_EPS = EXPERTS_PER_GROUP
MOE_VMEM_LIMIT = 56 * 1024 * 1024

def _cparams(sem, flags=None):
    return pltpu.CompilerParams(dimension_semantics=sem, vmem_limit_bytes=VMEM_LIMIT, flags=flags)


NORM_SEGS = (SEG_BK, SEG_CK, N_ROW_SEG + TSEG_BQ, N_ROW_SEG + TSEG_CQ)
SIGMOID_SEGS = (N_ROW_SEG + TSEG_AO,)


def _head_norm_t(acc_t):
    rows, tm = acc_t.shape
    a3 = acc_t.reshape(rows // 64, 64, tm)
    ssq = jnp.sum(a3 * a3, axis=1, keepdims=True)
    return (a3 * lax.rsqrt(ssq * (1.0 / 64.0) + EPS)).reshape(rows, tm)


def _inproj_kernel(x_ref, g_ref, w_ref, wif_ref, gain_ref, gain_t_ref, p_ref, pt_ref, gif_ref):
    xf = x_ref[...]
    xn = (xf * lax.rsqrt(jnp.mean(xf * xf, axis=-1, keepdims=True) + EPS) * g_ref[...]).astype(BF16)
    gif_ref[...] = jnp.dot(xn, wif_ref[...], preferred_element_type=F32)
    for j in range(N_SEG):
        cols = slice(j * SEG, (j + 1) * SEG)
        acc = jnp.dot(xn, w_ref[:, cols], preferred_element_type=F32)
        if j in NORM_SEGS:
            acc_t = _head_norm_t(jnp.transpose(acc))
            if j < N_ROW_SEG:
                p_ref[:, cols] = (jnp.transpose(acc_t) * gain_ref[j]).astype(BF16)
            else:
                pt_ref[j - N_ROW_SEG] = (acc_t * gain_t_ref[NORM_SEGS.index(j) - 2]).astype(BF16)
            continue
        if j in SIGMOID_SEGS:
            acc = jax.nn.sigmoid(acc)
        if j < N_ROW_SEG:
            p_ref[:, cols] = acc.astype(BF16)
        else:
            pt_ref[j - N_ROW_SEG] = jnp.transpose(acc).astype(BF16)


def _inproj(x2, g, w, wif, gain, tm):
    n = x2.shape[0]
    tsegs = np.asarray([N_ROW_SEG + TSEG_BQ, N_ROW_SEG + TSEG_CQ])
    gain_t = jnp.broadcast_to(gain[tsegs, 0, :, None], (2, SEG, tm))

    def const(shape):
        return pl.BlockSpec(shape, lambda i: (0,) * len(shape), pipeline_mode=pl.Buffered(1))

    return pl.pallas_call(
        _inproj_kernel,
        grid=(n // tm,),
        in_specs=[
            pl.BlockSpec((tm, D_MODEL), lambda i: (i, 0)),
            const((1, D_MODEL)),
            const((D_MODEL, N_SEG * SEG)),
            const((D_MODEL, LANES)),
            const((N_SEG, 1, SEG)),
            const((2, SEG, tm)),
        ],
        out_specs=[
            pl.BlockSpec((tm, N_ROW_SEG * SEG), lambda i: (i, 0)),
            pl.BlockSpec((N_T_SEG, SEG, tm), lambda i: (0, 0, i)),
            pl.BlockSpec((tm, LANES), lambda i: (i, 0)),
        ],
        out_shape=[
            jax.ShapeDtypeStruct((n, N_ROW_SEG * SEG), BF16),
            jax.ShapeDtypeStruct((N_T_SEG, SEG, n), BF16),
            jax.ShapeDtypeStruct((n, LANES), F32),
        ],
        compiler_params=_cparams(("parallel",)),
        name="inproj",
    )(x2, g, w, wif, gain, gain_t)


def _log_sigmoid(z):
    return jnp.minimum(z, 0.0) - jnp.log(1.0 + jnp.exp(-jnp.abs(z)))


def _split3(a):
    hi = a.astype(BF16)
    r1 = a - hi.astype(F32)
    mid = r1.astype(BF16)
    lo = (r1 - mid.astype(F32)).astype(BF16)
    return hi, mid, lo


def _mlstm_kernel(*refs):
    _mlstm_carry(*refs)
    for _ in _mlstm_chunk(*refs):
        pass


def _mlstm_carry(aq_ref, ak_ref, vt_ref, aot_ref, gif_ref, gift_ref, cw_ref, cb_ref,
                 gbr_ref, gbc_ref, agt_ref, out_ref, ubuf, kq_ref, st_ref, ct_ref, n_ref, m_ref,
                 *, chunk=None):
    L = MLSTM_L
    c = pl.program_id(1) if chunk is None else chunk

    @pl.when(c == 0)
    def _():
        ubuf[0:CONV_HIST, :] = jnp.zeros((CONV_HIST, 2 * A_WIDTH), F32)
        ct_ref[...] = jnp.zeros_like(ct_ref)
        n_ref[...] = jnp.zeros_like(n_ref)
        m_ref[...] = jnp.zeros_like(m_ref)

    @pl.when(c > 0)
    def _():
        ubuf[0:CONV_HIST, :] = ubuf[L:L + CONV_HIST, :]


def _mlstm_chunk(aq_ref, ak_ref, vt_ref, aot_ref, gif_ref, gift_ref, cw_ref, cb_ref,
                 gbr_ref, gbc_ref, agt_ref, out_ref, ubuf, kq_ref, st_ref, ct_ref, n_ref, m_ref):
    L = MLSTM_L

    H = CONV_HIST
    ubuf[H:L + H, 0:A_WIDTH] = aq_ref[...].astype(F32)
    ubuf[H:L + H, A_WIDTH:2 * A_WIDTH] = ak_ref[...].astype(F32)
    y = cb_ref[...] + cw_ref[0:1, :] * ubuf[H:L + H, :]
    for t in range(1, CONV_W):
        y = y + cw_ref[t:t + 1, :] * ubuf[H - t:H - t + L, :]
    qk = y * jax.nn.sigmoid(y)
    q_t = jnp.transpose(qk[:, 0:A_WIDTH]).astype(BF16)
    k_all = (qk[:, A_WIDTH:2 * A_WIDTH] * (A_DH ** -0.5)).astype(BF16)

    zc = gif_ref[...] + gbr_ref[...]
    ig_c = GATE_CAP * jnp.tanh(zc * (1.0 / GATE_CAP))
    lf_c = _log_sigmoid(zc)
    zr = gift_ref[...] + gbc_ref[...]
    ig_r = GATE_CAP * jnp.tanh(zr * (1.0 / GATE_CAP))
    lf_r = _log_sigmoid(zr)

    row = lax.broadcasted_iota(jnp.int32, (L, L), 0)
    col = lax.broadcasted_iota(jnp.int32, (L, L), 1)
    causal = col <= row
    tril = jnp.where(causal, 1.0, 0.0).astype(BF16)
    triu = jnp.where(row <= col, 1.0, 0.0).astype(BF16)
    b_c = sum(jnp.dot(tril, piece, preferred_element_type=F32) for piece in _split3(lf_c))
    b_r = sum(jnp.dot(piece, triu, preferred_element_type=F32) for piece in _split3(lf_r))

    sub8 = lax.broadcasted_iota(jnp.int32, (8, L), 0)
    heads = [slice(h * A_DH, (h + 1) * A_DH) for h in range(A_HEADS)]
    for h, rows in enumerate(heads):
        kq_ref[h] = jnp.dot(k_all[:, rows], q_t[rows, :], preferred_element_type=F32)
    yield

    stats = []
    for h, rows in enumerate(heads):
        bcol = b_c[:, A_HEADS + h:A_HEADS + h + 1]
        brow = b_r[A_HEADS + h:A_HEADS + h + 1, :]
        igcol = ig_c[:, h:h + 1]
        m_prev = m_ref[h][:, 0:1]
        dmat = jnp.where(row <= col, brow + (igcol - bcol), NEG)
        inter = brow + m_prev
        m_t = jnp.maximum(inter, jnp.max(dmat, axis=0, keepdims=True))
        st = kq_ref[h] * jnp.exp(dmat - m_t)
        st_ref[h] = st.astype(BF16)
        stats.append((brow, m_prev, m_t, jnp.exp(inter - m_t), jnp.sum(st, axis=0, keepdims=True)))
        if h % 2 == 1:
            yield

    for h, rows in enumerate(heads):
        brow, m_prev, m_t, w_inter, st_sum = stats[h]
        qt = q_t[rows, :]
        k = k_all[:, rows]
        vt = vt_ref[rows, :]
        igrow = ig_r[h:h + 1, :]
        b_last = brow[:, L - 1:L]
        ct = ct_ref[h]
        n8 = n_ref[h]

        num = w_inter * jnp.dot(ct.astype(BF16), qt, preferred_element_type=F32)
        num = num + jnp.dot(vt, st_ref[h], preferred_element_type=F32)
        nq = jnp.dot(n8.astype(BF16), qt, preferred_element_type=F32)[0:1, :]
        den = w_inter * nq + st_sum
        hh = num * (1.0 / jnp.maximum(jnp.abs(den), jnp.exp(-m_t)))

        g_end = b_last - brow + igrow
        m_new = jnp.maximum(b_last + m_prev, jnp.max(g_end, axis=-1, keepdims=True))
        dec = jnp.exp(b_last + m_prev - m_new)
        w_end = jnp.exp(g_end - m_new)
        vw = (vt.astype(F32) * w_end).astype(BF16)
        ct_ref[h] = dec * ct + jnp.dot(vw, k, preferred_element_type=F32)
        w8 = jnp.where(sub8 == 0, w_end, 0.0).astype(BF16)
        n_ref[h] = dec * n8 + jnp.dot(w8, k, preferred_element_type=F32)
        m_ref[h] = jnp.broadcast_to(m_new, (1, LANES))

        hn = hh * lax.rsqrt(jnp.mean(hh * hh, axis=0, keepdims=True) + EPS) * agt_ref[rows, :]
        out_ref[rows, :] = (hn * aot_ref[rows, :].astype(F32)).astype(BF16)
        if h % 2 == 1 and h + 1 < A_HEADS:
            yield


def _mlstm_call(p, pt, gif, gift, cw, cb, gbr, gbc, ag, bsz, seq):
    L = MLSTM_L
    nc = seq // L
    n = bsz * seq
    agt = jnp.broadcast_to(ag.reshape(A_WIDTH, 1), (A_WIDTH, L))

    def tseg(j):
        return pl.BlockSpec((None, SEG, L), lambda b, c: (j, 0, b * nc + c))

    def full(shape):
        return pl.BlockSpec(shape, lambda b, c: (0,) * len(shape))

    in_specs = [
        pl.BlockSpec((L, SEG), lambda b, c: (b * nc + c, SEG_AQ)),
        pl.BlockSpec((L, SEG), lambda b, c: (b * nc + c, SEG_AK)),
        tseg(TSEG_AV), tseg(TSEG_AO),
        pl.BlockSpec((L, LANES), lambda b, c: (b * nc + c, 0)),
        pl.BlockSpec((8, L), lambda b, c: (0, b * nc + c)),
        full((CONV_W, 2 * A_WIDTH)), full((1, 2 * A_WIDTH)),
        full((1, LANES)), full((8, 1)), full((A_WIDTH, L)),
    ]
    scratch = [
        pltpu.VMEM((L + CONV_HIST, 2 * A_WIDTH), F32),
        pltpu.VMEM((A_HEADS, L, L), F32),
        pltpu.VMEM((A_HEADS, L, L), BF16),
        pltpu.VMEM((A_HEADS, A_DH, A_DH), F32),
        pltpu.VMEM((A_HEADS, 8, A_DH), F32),
        pltpu.VMEM((A_HEADS, 1, LANES), F32),
    ]
    return (in_specs, pl.BlockSpec((A_WIDTH, L), lambda b, c: (0, b * nc + c)),
            jax.ShapeDtypeStruct((A_WIDTH, n), BF16), scratch,
            (p, p, pt, pt, gif, gift, cw, cb, gbr, gbc, agt))


def _mlstm(p, pt, gif, gift, cw, cb, gbr, gbc, ag, bsz, seq):
    in_specs, out_spec, out_shape, scratch, operands = _mlstm_call(
        p, pt, gif, gift, cw, cb, gbr, gbc, ag, bsz, seq)
    return pl.pallas_call(
        _mlstm_kernel, grid=(bsz, seq // MLSTM_L), in_specs=in_specs, out_specs=out_spec,
        out_shape=out_shape, scratch_shapes=scratch,
        compiler_params=_cparams(("parallel", "arbitrary")), name="mlstm",
    )(*operands)


def _interleave(*gens):
    gens = list(gens)
    while gens:
        for g in list(gens):
            try:
                next(g)
            except StopIteration:
                gens.remove(g)
                continue
            yield


def _band_steps(refs, i, mask_start):
    nkb = BAND_NKB
    qt_ref = refs[0]
    k_refs = refs[1:1 + nkb]
    vt_refs = refs[1 + nkb:1 + 2 * nkb]
    bias_ref = refs[1 + 2 * nkb]
    out_ref = refs[2 + 2 * nkb]
    s_ref, mx_ref = refs[3 + 2 * nkb:5 + 2 * nkb]
    tq = BAND_TQ
    nk = nkb * tq

    k_all = jnp.concatenate([r[...] for r in k_refs], axis=0)
    vt_all = jnp.concatenate([r[...] for r in vt_refs], axis=1)
    ones = jnp.ones((BAND_ONES, nk), BF16)
    row = lax.broadcasted_iota(jnp.int32, (LANES, tq), 0)
    lo = row < B_DH
    if mask_start:
        kidx = lax.broadcasted_iota(jnp.int32, (nk, 1), 0)
        valid = (kidx + (i - (nkb - 1)) * tq) >= 0

    def score(p):
        rows = slice(p * LANES, (p + 1) * LANES)
        qtp = qt_ref[rows, :]
        zero = jnp.zeros_like(qtp)
        qbd = jnp.concatenate([jnp.where(lo, qtp, zero), jnp.where(lo, zero, qtp)], axis=1)
        s = jnp.dot(k_all[:, rows], qbd, preferred_element_type=F32) + bias_ref[p]
        if mask_start:
            s = jnp.where(valid, s, NEG)
        s_ref[p & 1] = s
        mx_ref[p & 1] = jnp.max(s, axis=0, keepdims=True)

    def finish(p):
        rows = slice(p * LANES, (p + 1) * LANES)
        pr = jnp.exp2((s_ref[p & 1] - mx_ref[p & 1]).astype(BF16))
        o = jnp.dot(jnp.concatenate([vt_all[rows, :], ones], axis=0), pr,
                    preferred_element_type=F32)
        o = o[0:LANES, :] / o[LANES:LANES + 1, :]
        out_ref[rows, :] = jnp.where(lo, o[:, 0:tq], o[:, tq:2 * tq]).astype(BF16)

    score(0)
    yield
    for p in range(B_HEADS // 2):
        if p + 1 < B_HEADS // 2:
            score(p + 1)
        finish(p)
        if p + 1 < B_HEADS // 2:
            yield


def _band_variants(i, region):
    @pl.when(i < BAND_NKB - 1)
    def _():
        region(True)

    @pl.when(i >= BAND_NKB - 1)
    def _():
        region(False)


def _drain(gen):
    for _ in gen:
        pass


def _band_kernel(*refs):
    i = pl.program_id(1)
    _band_variants(i, lambda mask: _drain(_band_steps(refs, i, mask)))


def _band_call(p, pt, bias, bsz, seq):
    tq = BAND_TQ
    nkb = BAND_NKB
    nq = seq // tq
    n = bsz * seq

    def kblk(d):
        return pl.BlockSpec((tq, SEG), lambda b, i: (b * nq + jnp.maximum(i - d, 0), SEG_BK))

    def vblk(d):
        return pl.BlockSpec((None, SEG, tq), lambda b, i: (TSEG_BV, 0, b * nq + jnp.maximum(i - d, 0)))

    in_specs = [pl.BlockSpec((None, SEG, tq), lambda b, i: (TSEG_BQ, 0, b * nq + i))]
    in_specs += [kblk(d) for d in range(nkb - 1, -1, -1)]
    in_specs += [vblk(d) for d in range(nkb - 1, -1, -1)]
    in_specs += [pl.BlockSpec(bias.shape, lambda b, i: (0, 0, 0))]
    scratch = [
        pltpu.VMEM((2, nkb * tq, 2 * tq), F32),
        pltpu.VMEM((2, 1, 2 * tq), F32),
    ]
    return (in_specs, pl.BlockSpec((B_WIDTH, tq), lambda b, i: (0, b * nq + i)),
            jax.ShapeDtypeStruct((B_WIDTH, n), BF16), scratch,
            (pt, *([p] * nkb), *([pt] * nkb), bias))


def _band(p, pt, bias, bsz, seq):
    in_specs, out_spec, out_shape, scratch, operands = _band_call(p, pt, bias, bsz, seq)
    return pl.pallas_call(
        _band_kernel, grid=(bsz, seq // BAND_TQ), in_specs=in_specs, out_specs=out_spec,
        out_shape=out_shape, scratch_shapes=scratch,
        compiler_params=_cparams(("parallel", "parallel")), name="band_attn",
    )(*operands)


def _toeplitz(base, m, n):
    period = base.shape[-1]
    assert n <= period - 1
    reps = (1,) * (base.ndim - 1) + (m,)
    big = jnp.tile(base, reps)[..., :m * (period - 1)]
    return big.reshape(base.shape[:-1] + (m, period - 1))[..., :n]


def _band_bias(b_rel):
    tq = BAND_TQ
    nk = BAND_NKB * tq
    period = tq + nk
    e = np.arange(period)
    e = np.where(e < nk, e, e - period)
    rel = np.clip((nk - tq) - e, -B_MAX_REL, B_MAX_REL) + B_MAX_REL
    bias = _toeplitz(jnp.transpose(b_rel[rel]).astype(F32), tq, nk)
    qpos = np.arange(tq)
    kpos = np.arange(nk) - (nk - tq)
    qc = qpos[:, None] // CHUNK
    kc = np.floor_divide(kpos[None, :], CHUNK)
    allowed = (kc <= qc) & (kc >= qc - B_LEFT_CHUNKS)
    bias = jnp.where(allowed[None], bias * LOG2E, NEG)
    return jnp.swapaxes(bias.reshape(B_HEADS // 2, 2 * tq, nk), 1, 2)


def _diff_kernel(lam_ref, cfar_ref, qt_ref, k_ref, vt_ref, bias_ref, g_ref, out_ref,
                 qbd_ref, s_ref, mx_ref, acc_ref, m_ref, *, nnear, first_region=None):
    T = DIFF_T
    h = pl.program_id(1)
    qi = pl.program_id(2)
    cfar = cfar_ref[h]

    def prologue():
        qt = qt_ref[...]
        row = lax.broadcasted_iota(jnp.int32, (2 * C_DQK, T), 0)
        zero = jnp.zeros_like(qt)
        qbd_ref[:, 0:T] = jnp.where(row < C_DQK, qt, zero)
        qbd_ref[:, T:2 * T] = jnp.where(row < C_DQK, zero, qt)
        m_ref[...] = jnp.full(m_ref.shape, NEG, F32)
        acc_ref[...] = jnp.zeros_like(acc_ref)

    def near_bias(t):
        return bias_ref[t] if t < nnear else None

    def stage_a(t, slot, bias, maps=(0, 1)):
        j = jnp.maximum(qi - t, 0)
        k = k_ref[pl.ds(pl.multiple_of(j * T, T), T), :]
        for mp in maps:
            sm = jnp.dot(k, qbd_ref[:, mp * T:(mp + 1) * T], preferred_element_type=F32)
            if bias is not None:
                sm = sm + bias
            s_ref[slot, mp] = sm
            mx = jnp.max(sm, axis=0, keepdims=True)
            mx_ref[slot, mp] = mx + cfar if bias is None else mx

    def stage_bc(t, slot, far, maps=(0, 1)):
        j = qi - t
        vt = jnp.concatenate([vt_ref[:, pl.ds(pl.multiple_of(j * T, T), T)],
                              jnp.ones((DIFF_ONES, T), BF16)], axis=0)
        for mp in maps:
            m_old = m_ref[mp]
            m_new = jnp.maximum(m_old, mx_ref[slot, mp])
            shift = m_new - cfar if far else m_new
            pr = jnp.exp2((s_ref[slot, mp] - shift).astype(BF16))
            m_ref[mp] = m_new
            acc_ref[mp] = jnp.exp2(m_old - m_new) * acc_ref[mp] + jnp.dot(
                vt, pr, preferred_element_type=F32)

    def full_step(t, slot, far, next_bias):
        for mp in range(2):
            stage_a(t + 1, 1 - slot, next_bias, (mp,))
            stage_bc(t, slot, far, (mp,))

    def first_steps(other, n_steps=1):
        prologue()
        next(other, None)
        stage_a(0, 0, near_bias(0))
        next(other, None)
        for t in range(n_steps):
            for mp in range(2):
                stage_a(t + 1, 1 - (t & 1), near_bias(t + 1), (mp,))
                stage_bc(t, t & 1, False, (mp,))
                next(other, None)
        _drain(other)

    all_near = qi >= nnear - 1
    if first_region is None:
        first_steps(iter(()))
        near_done = False
    else:
        near_done = first_region(first_steps, all_near, nnear)
    for t in range(1, nnear):
        @pl.when((qi >= t) & jnp.logical_not(near_done))
        def _(t=t):
            full_step(t, t & 1, False, near_bias(t + 1))

    @pl.when(qi >= nnear)
    def _():
        n_full = qi - nnear
        s0 = nnear & 1

        def pair(i, carry):
            full_step(nnear + 2 * i, s0, True, None)
            full_step(nnear + 2 * i + 1, 1 - s0, True, None)
            return carry

        lax.fori_loop(0, n_full // 2, pair, 0)

        @pl.when(n_full % 2 == 1)
        def _():
            full_step(qi - 1, s0, True, None)

        stage_bc(qi, qi & 1, True)

    o1 = acc_ref[0, 0:C_DV, :] / acc_ref[0, C_DV:C_DV + 1, :]
    o2 = acc_ref[1, 0:C_DV, :] / acc_ref[1, C_DV:C_DV + 1, :]
    o = o1 - lam_ref[0] * o2
    on = o * lax.rsqrt(jnp.mean(o * o, axis=0, keepdims=True) + EPS) * g_ref[...]
    out_ref[...] = on.astype(BF16)


def _diff_nnear(nq):
    d_sat = -(-(T5_MAX_DIST - 1 + DIFF_T) // DIFF_T)
    return min(d_sat, nq)


def _diff_call(p, pt, bias, cfar, lam, gsub, bsz, seq):
    T = DIFF_T
    nq = seq // T
    nnear = bias.shape[1]
    kcol = SEG_CK * SEG // LANES
    n = bsz * seq
    in_specs = [
        pl.BlockSpec(memory_space=pltpu.SMEM),
        pl.BlockSpec(memory_space=pltpu.SMEM),
        pl.BlockSpec((None, 2 * C_DQK, T), lambda b, h, i: (TSEG_CQ, h, b * nq + i)),
        pl.BlockSpec((seq, LANES), lambda b, h, i: (b, kcol + h)),
        pl.BlockSpec((None, C_DV, seq), lambda b, h, i: (TSEG_CV, h, b)),
        pl.BlockSpec((None, nnear, T, T), lambda b, h, i: (h, 0, 0, 0)),
        pl.BlockSpec((C_DV, 1), lambda b, h, i: (0, 0)),
    ]
    scratch = [
        pltpu.VMEM((2 * C_DQK, 2 * T), BF16),
        pltpu.VMEM((2, 2, T, T), F32),
        pltpu.VMEM((2, 2, 1, T), F32),
        pltpu.VMEM((2, C_DV + DIFF_ONES, T), F32),
        pltpu.VMEM((2, 1, T), F32),
    ]
    return (in_specs, pl.BlockSpec((C_DV, T), lambda b, h, i: (h, b * nq + i)),
            jax.ShapeDtypeStruct((C_WIDTH, n), BF16), scratch, (lam, cfar, pt, p, pt, bias, gsub))


def _diff(p, pt, bias, cfar, lam, gsub, bsz, seq):
    in_specs, out_spec, out_shape, scratch, operands = _diff_call(p, pt, bias, cfar, lam, gsub, bsz, seq)
    return pl.pallas_call(
        functools.partial(_diff_kernel, nnear=bias.shape[1]),
        grid=(bsz, C_HEADS, seq // DIFF_T), in_specs=in_specs, out_specs=out_spec,
        out_shape=out_shape, scratch_shapes=scratch,
        compiler_params=_cparams(("parallel", "parallel", "arbitrary")), name="diff_attn",
    )(*operands)


def _mixers_kernel(*refs, n_in, n_scr, nnear):
    d_in, m_in, b_in = (refs[sum(n_in[:k]):sum(n_in[:k + 1])] for k in range(3))
    hct_ref, hat_ref, hbt_ref = refs[sum(n_in):sum(n_in) + 3]
    scr = refs[sum(n_in) + 3:]
    d_scr, m_scr, b_scr = (scr[sum(n_scr[:k]):sum(n_scr[:k + 1])] for k in range(3))
    m_refs = (*m_in, hat_ref, *m_scr)
    b_refs = (*b_in, hbt_ref, *b_scr)
    chunk = pl.program_id(1) * pl.num_programs(2) + pl.program_id(2)
    _mlstm_carry(*m_refs, chunk=chunk)

    def first_region(trace, all_near, n_near):
        def other(mask):
            return _interleave(_band_steps(b_refs, chunk, mask), _mlstm_chunk(*m_refs))

        masked = chunk < BAND_NKB - 1
        wide = jnp.logical_not(masked) & all_near

        @pl.when(masked)
        def _():
            trace(other(True), 1)

        @pl.when(wide)
        def _():
            trace(other(False), n_near)

        @pl.when(jnp.logical_not(masked) & jnp.logical_not(all_near))
        def _():
            trace(other(False), 1)

        return wide

    _diff_kernel(*d_in, hct_ref, *d_scr, nnear=nnear, first_region=first_region)


def _mixers(p, pt, gif, gift, mp, bsz, seq):
    nq = seq // DIFF_T
    assert C_HEADS * nq == seq // MLSTM_L and MLSTM_L == BAND_TQ

    def on_diff_grid(spec):
        if spec.index_map is None:
            return spec
        return pl.BlockSpec(spec.block_shape, lambda b, h, i, f=spec.index_map: f(b, h * nq + i))

    d_specs, d_out, d_shape, d_scr, d_ops = _diff_call(p, pt, mp["diff_bias"], mp["diff_far"], mp["lam"],
                                                       mp["gsub"], bsz, seq)
    m_specs, m_out, m_shape, m_scr, m_ops = _mlstm_call(p, pt, gif, gift, mp["cw"], mp["cb"], mp["gbr"],
                                                        mp["gbc"], mp["ag"], bsz, seq)
    b_specs, b_out, b_shape, b_scr, b_ops = _band_call(p, pt, mp["band_bias"], bsz, seq)
    hct, hat, hbt = pl.pallas_call(
        functools.partial(_mixers_kernel, n_in=(len(d_specs), len(m_specs), len(b_specs)),
                          n_scr=(len(d_scr), len(m_scr), len(b_scr)), nnear=mp["diff_bias"].shape[1]),
        grid=(bsz, C_HEADS, nq),
        in_specs=d_specs + [on_diff_grid(s) for s in m_specs + b_specs],
        out_specs=[d_out, on_diff_grid(m_out), on_diff_grid(b_out)],
        out_shape=[d_shape, m_shape, b_shape],
        scratch_shapes=d_scr + m_scr + b_scr,
        compiler_params=_cparams(("parallel", "arbitrary", "arbitrary")),
        name="mixers",
    )(*d_ops, *m_ops, *b_ops)
    return hat, hbt, hct


def _t5_bucket(rel):
    nb = T5_BUCKETS // 2
    max_exact = nb // 2
    ret = (rel > 0).astype(jnp.int32) * nb
    n = jnp.abs(rel)
    large = max_exact + (jnp.log(jnp.maximum(n, max_exact).astype(F32) / max_exact)
                         / math.log(T5_MAX_DIST / max_exact) * (nb - max_exact)).astype(jnp.int32)
    large = jnp.minimum(large, nb - 1)
    return ret + jnp.where(n < max_exact, n, large)


def _diff_bias(t5_table, nnear):
    T = DIFF_T
    e = np.arange(2 * T)
    amc = np.where(e < T, -e, 2 * T - e)
    rel = jnp.asarray(-np.arange(nnear)[:, None] * T + amc[None, :], jnp.int32)
    base = jnp.moveaxis(t5_table[_t5_bucket(rel)], -1, 0).astype(F32) * LOG2E
    tiles = _toeplitz(base, T, T)
    a = np.arange(T)[:, None]
    c = np.arange(T)[None, :]
    allowed = np.ones((nnear, T, T), bool)
    allowed[0] = (a // CHUNK) <= (c // CHUNK)
    far = t5_table[_t5_bucket(jnp.asarray(-T5_MAX_DIST, jnp.int32))].astype(F32) * LOG2E
    return jnp.where(allowed[None], tiles, NEG), far


def _merge_kernel(hat_ref, hbt_ref, hct_ref, x_ref, n1_ref, wg_ref, wb_ref, wo_ref,
                  n2_ref, wr_ref, br_ref, x1_ref, h2_ref, lg_ref, y_ref):
    tn = (((0,), (0,)), ((), ()))
    tm = x_ref.shape[0]
    halves = [slice(i * (tm // 2), (i + 1) * (tm // 2)) for i in range(2)]
    for r in halves:
        xf = x_ref[r, :]
        xn = (xf * lax.rsqrt(jnp.mean(xf * xf, axis=-1, keepdims=True) + EPS) * n1_ref[...]).astype(BF16)
        y = None
        for b, ht_ref in enumerate((hat_ref, hbt_ref, hct_ref)):
            gate = jax.nn.sigmoid(jnp.dot(xn, wg_ref[:, b * D_MODEL:(b + 1) * D_MODEL],
                                          preferred_element_type=F32))
            yb = gate * lax.dot_general(ht_ref[:, r], wb_ref[b], tn, preferred_element_type=F32)
            y = yb if y is None else y + yb
        y_ref[r, :] = y.astype(BF16)
    for r in halves:
        x1 = x_ref[r, :] + jnp.dot(y_ref[r, :], wo_ref[...], preferred_element_type=F32)
        x1_ref[r, :] = x1
        h2 = x1 * lax.rsqrt(jnp.mean(x1 * x1, axis=-1, keepdims=True) + EPS) * n2_ref[...]
        h2_ref[r, :] = h2.astype(BF16)
    for r in halves:
        lg_ref[r, :] = jnp.dot(h2_ref[r, :], wr_ref[...], preferred_element_type=F32) + br_ref[...]


def _merge(ha, hbt, hct, x2, n1, wg, wb, wo, n2, wr, br, tm):
    n = x2.shape[0]

    def rows(width, col=0):
        return pl.BlockSpec((tm, width), lambda i: (i, col))

    def cols():
        return pl.BlockSpec((BRANCH_WIDTH, tm), lambda i: (0, i))

    def full(shape):
        return pl.BlockSpec(shape, lambda i: (0,) * len(shape), pipeline_mode=pl.Buffered(1))

    return pl.pallas_call(
        _merge_kernel,
        grid=(n // tm,),
        in_specs=[
            cols(), cols(), cols(),
            rows(D_MODEL),
            full((1, D_MODEL)), full((D_MODEL, N_BRANCH * D_MODEL)),
            full((N_BRANCH, BRANCH_WIDTH, D_MODEL)), full((D_MODEL, D_MODEL)),
            full((1, D_MODEL)), full((D_MODEL, LANES)), full((1, LANES)),
        ],
        out_specs=[rows(D_MODEL), rows(D_MODEL), rows(LANES)],
        out_shape=[
            jax.ShapeDtypeStruct((n, D_MODEL), F32),
            jax.ShapeDtypeStruct((n, D_MODEL), BF16),
            jax.ShapeDtypeStruct((n, LANES), F32),
        ],
        scratch_shapes=[pltpu.VMEM((tm, D_MODEL), BF16)],
        compiler_params=_cparams(("parallel",)),
        name="merge",
    )(ha, hbt, hct, x2, n1, wg, wb, wo, n2, wr, br)


def _combine_weights(lg):
    lanef = lax.broadcasted_iota(jnp.int32, lg.shape, 1).astype(F32)
    big = 1e9
    is_g = (lanef >= N_EXPERTS) & (lanef < N_EXPERTS + N_GROUPS)
    gl = jnp.where(is_g, lg, -jnp.inf)
    gmax = jnp.max(gl, axis=-1, keepdims=True)
    g_idx = jnp.min(jnp.where(gl == gmax, lanef - N_EXPERTS, big), axis=-1, keepdims=True)
    p_g = 1.0 / jnp.sum(jnp.exp(gl - gmax), axis=-1, keepdims=True)
    in_grp = (lanef >= g_idx * EXPERTS_PER_GROUP) & (lanef < (g_idx + 1.0) * EXPERTS_PER_GROUP)
    el = jnp.where(in_grp, lg, -jnp.inf)
    ee = jnp.exp(el - jnp.max(el, axis=-1, keepdims=True))
    ep = ee / jnp.sum(ee, axis=-1, keepdims=True)
    ep = jnp.where(in_grp, ep, -1.0)
    v1 = jnp.max(ep, axis=-1, keepdims=True)
    i1 = jnp.min(jnp.where(ep == v1, lanef, big), axis=-1, keepdims=True)
    ep2 = jnp.where(lanef == i1, -1.0, ep)
    v2 = jnp.max(ep2, axis=-1, keepdims=True)
    i2 = jnp.min(jnp.where(ep2 == v2, lanef, big), axis=-1, keepdims=True)
    tot = v1 + v2
    comb = jnp.where(lanef == i1, p_g * (v1 / tot), 0.0) + jnp.where(lanef == i2, p_g * (v2 / tot), 0.0)
    return comb, g_idx


def _moe_kernel(x1_hbm, h2_hbm, lg_hbm, tri_ref, wgu_ref, wd_ref, out_ref,
                pt_ref, xs_ref, ws_ref, ys_ref, meta_ref, x1_buf, x1_sem, h2_ref, lg_ref, in_sem, *, rcap):
    C = MOE_C
    blk = pl.program_id(0)
    s = pl.program_id(1)
    tm = out_ref.shape[0]

    def x1_copy():
        rows = pl.ds(pl.multiple_of(blk * tm, tm), tm)
        return pltpu.make_async_copy(x1_hbm.at[rows, :], x1_buf, x1_sem)

    def in_copies(block):
        rows = pl.ds(pl.multiple_of(block * tm, tm), tm)
        return (pltpu.make_async_copy(h2_hbm.at[rows, :], h2_ref, in_sem.at[0]),
                pltpu.make_async_copy(lg_hbm.at[rows, :], lg_ref, in_sem.at[1]))

    @pl.when((s == 0) & (blk == 0))
    def _():
        for cp in in_copies(blk):
            cp.start()

    @pl.when((s == 1) & (blk + 1 < pl.num_programs(0)))
    def _():
        for cp in in_copies(blk + 1):
            cp.start()

    @pl.when(s == 0)
    def _():
        x1_copy().start()
        for cp in in_copies(blk):
            cp.wait()

    @pl.when(s == 0)
    def _():
        comb, g_idx = _combine_weights(lg_ref[...])
        lanef = lax.broadcasted_iota(jnp.int32, comb.shape, 1).astype(F32)
        mine = lanef == g_idx
        onehot = jnp.where(mine, 1.0, 0.0)
        ranks = jnp.dot(tri_ref[...], onehot.astype(BF16), preferred_element_type=F32)
        dest = jnp.sum(jnp.where(mine, ranks, 0.0), axis=-1, keepdims=True)
        off = jnp.int32(0)
        for g in range(N_GROUPS):
            cnt = jnp.sum(onehot[:, g:g + 1]).astype(jnp.int32)
            nchunk = (cnt + (C - 1)) // C
            meta_ref[g] = off
            meta_ref[N_GROUPS + g] = nchunk
            dest = dest + jnp.where(g_idx == float(g), (off * C).astype(F32), 0.0)
            off = off + nchunk
        slot = lax.broadcasted_iota(jnp.int32, (tm, rcap), 1).astype(F32)
        pt = jnp.where(dest == slot, 1.0, 0.0).astype(BF16)
        pt_ref[...] = pt
        comb_hi = comb.astype(BF16)
        comb_lo = (comb - comb_hi.astype(F32)).astype(BF16)
        packed = jnp.concatenate([h2_ref[...], comb_hi, comb_lo], axis=1)
        srt = lax.dot_general(pt, packed, (((0,), (0,)), ((), ())), preferred_element_type=F32)
        xs_ref[...] = srt[:, 0:D_MODEL].astype(BF16)
        ws_ref[...] = srt[:, D_MODEL:D_MODEL + LANES] + srt[:, D_MODEL + LANES:D_MODEL + 2 * LANES]
        ys_ref[...] = jnp.zeros_like(ys_ref)

    def do_chunk(r0, m):
        r0 = pl.multiple_of(r0, C)
        xc = xs_ref[pl.ds(r0, m), :]
        wsc = ws_ref[pl.ds(r0, m), :]
        lane = lax.broadcasted_iota(jnp.int32, (m, LANES), 1)
        y = None
        for k in range(MOE_EPS):
            gu = jnp.dot(xc, wgu_ref[k], preferred_element_type=F32)
            gate = gu[:, 0:D_EXPERT]
            wk = jnp.sum(jnp.where(lane == s * MOE_EPS + k, wsc, 0.0), axis=-1, keepdims=True)
            he = (gate * jax.nn.sigmoid(gate) * gu[:, D_EXPERT:2 * D_EXPERT] * wk).astype(BF16)
            yk = jnp.dot(he, wd_ref[k], preferred_element_type=F32)
            y = yk if y is None else y + yk
        ys_ref[pl.ds(r0, m), :] = y.astype(BF16)

    g = s // (EXPERTS_PER_GROUP // MOE_EPS)
    start = meta_ref[g]
    nchunk = meta_ref[N_GROUPS + g]

    def big(i, carry):
        do_chunk((start + MOE_BIG * i) * C, MOE_BIG * C)
        return carry

    common = nchunk == MOE_BIG + 1

    @pl.when(common)
    def _():
        do_chunk(start * C, (MOE_BIG + 1) * C)

    @pl.when(jnp.logical_not(common))
    def _():
        nbig = nchunk // MOE_BIG
        lax.fori_loop(0, nbig, big, 0)
        done = nbig * MOE_BIG
        size = MOE_BIG // 2
        while size >= 1:
            @pl.when((nchunk & size) != 0)
            def _(done=done, size=size):
                do_chunk((start + done) * C, size * C)
            done = done + (nchunk & size)
            size //= 2

    @pl.when(s == pl.num_programs(1) - 1)
    def _():
        x1_copy().wait()
        out_ref[...] = x1_buf[...] + jnp.dot(pt_ref[...], ys_ref[...], preferred_element_type=F32)


def _moe(x1, h2, lg, wgu, wd, tm):
    n = x1.shape[0]
    rcap = (tm + N_GROUPS * (MOE_C - 1)) // MOE_C * MOE_C
    rcap = -(-rcap // LANES) * LANES
    idx = np.arange(tm)
    tri = jnp.asarray(idx[None, :] < idx[:, None], BF16)
    return pl.pallas_call(
        functools.partial(_moe_kernel, rcap=rcap),
        grid=(n // tm, N_EXPERTS // MOE_EPS),
        in_specs=[
            pl.BlockSpec(memory_space=pl.ANY),
            pl.BlockSpec(memory_space=pl.ANY),
            pl.BlockSpec(memory_space=pl.ANY),
            pl.BlockSpec((tm, tm), lambda i, s: (0, 0), pipeline_mode=pl.Buffered(1)),
            pl.BlockSpec((MOE_EPS, D_MODEL, 2 * D_EXPERT), lambda i, s: (s, 0, 0)),
            pl.BlockSpec((MOE_EPS, D_EXPERT, D_MODEL), lambda i, s: (s, 0, 0)),
        ],
        out_specs=pl.BlockSpec((tm, D_MODEL), lambda i, s: (i, 0), pipeline_mode=pl.Buffered(1)),
        out_shape=jax.ShapeDtypeStruct((n, D_MODEL), F32),
        scratch_shapes=[
            pltpu.VMEM((tm, rcap), BF16),
            pltpu.VMEM((rcap, D_MODEL), BF16),
            pltpu.VMEM((rcap, LANES), F32),
            pltpu.VMEM((rcap, D_MODEL), BF16),
            pltpu.SMEM((2 * N_GROUPS,), jnp.int32),
            pltpu.VMEM((tm, D_MODEL), F32),
            pltpu.SemaphoreType.DMA,
            pltpu.VMEM((tm, D_MODEL), BF16),
            pltpu.VMEM((tm, LANES), F32),
            pltpu.SemaphoreType.DMA((2,)),
        ],
        compiler_params=pltpu.CompilerParams(dimension_semantics=("arbitrary", "arbitrary"),
                                             vmem_limit_bytes=MOE_VMEM_LIMIT),
        name="moe",
    )(x1, h2, lg, tri, wgu, wd)


def _tile(n, pref):
    t = pref
    while n % t:
        t //= 2
    return t


def _mixer_params(layer, norm1_g, w_in, a_conv_w, a_conv_b, a_gate_bias, a_out_norm_g,
                  b_qk_norm_g, b_rel_bias, c_qk_norm_g, c_lambda, c_sub_norm_g, t5_bias,
                  w_branch, w_out, nq_diff):
    n_small = 2 * A_HEADS
    cut = 4 * A_WIDTH
    n_proj = N_SEG * SEG
    w_main = jnp.concatenate([w_in[:, :cut], w_in[:, cut + n_small:n_small + n_proj]], axis=1)
    w_main = w_main.reshape(D_MODEL, N_SEG, SEG)[:, np.asarray(SEG_PERM), :]
    w_main = w_main.reshape(D_MODEL, n_proj).astype(BF16)
    w_gates = w_in[:, n_small + n_proj:].astype(BF16)
    w_if = jnp.pad(w_in[:, cut:cut + n_small], ((0, 0), (0, LANES - n_small))).astype(BF16)
    gain = jnp.ones((N_SEG, SEG), F32)
    gain = gain.at[N_ROW_SEG + TSEG_BQ].set(jnp.tile(b_qk_norm_g[0], B_HEADS) * (B_DH ** -0.5 * LOG2E))
    gain = gain.at[SEG_BK].set(jnp.tile(b_qk_norm_g[1], B_HEADS))
    gain = gain.at[N_ROW_SEG + TSEG_CQ].set(
        jnp.tile(c_qk_norm_g[0], 2 * C_HEADS) * (C_DQK ** -0.5 * LOG2E))
    gain = gain.at[SEG_CK].set(jnp.tile(c_qk_norm_g[1], 2 * C_HEADS))
    diff_bias, diff_far = _diff_bias(t5_bias, _diff_nnear(nq_diff))
    lam_init = 0.8 - 0.6 * math.exp(-0.3 * layer)
    lf32 = c_lambda.astype(F32)
    lam = jnp.exp(jnp.sum(lf32[0] * lf32[1])) - jnp.exp(jnp.sum(lf32[2] * lf32[3])) + lam_init
    return dict(
        g1=norm1_g.reshape(1, D_MODEL), w_main=w_main, w_if=w_if, w_gates=w_gates,
        gain=gain.reshape(N_SEG, 1, SEG),
        cw=a_conv_w, cb=a_conv_b.reshape(1, -1),
        gbr=jnp.pad(a_gate_bias, (0, LANES - n_small)).reshape(1, LANES),
        gbc=a_gate_bias.reshape(n_small, 1),
        ag=a_out_norm_g.reshape(1, A_WIDTH),
        band_bias=_band_bias(b_rel_bias),
        diff_bias=diff_bias, diff_far=diff_far,
        lam=lam.reshape(1).astype(F32),
        gsub=(c_sub_norm_g * (1.0 - lam_init)).reshape(C_DV, 1),
        wb=w_branch.astype(BF16), wo=w_out.astype(BF16),
    )


def _layer(x2, bsz, seq, mp, norm2_g, w_group, b_group, w_router, b_router, w_e_gate, w_e_up, w_e_down):
    n = bsz * seq
    p, pt, gif = _inproj(x2, mp["g1"], mp["w_main"], mp["w_if"], mp["gain"], _tile(n, 512))
    gift = jnp.transpose(gif[:, :2 * A_HEADS])
    ha, hbt, hct = _mixers(p, pt, gif, gift, mp, bsz, seq)

    wr = jnp.concatenate([w_router, w_group], axis=1)
    wr = jnp.pad(wr, ((0, 0), (0, LANES - wr.shape[1]))).astype(BF16)
    br = jnp.pad(jnp.concatenate([b_router, b_group]), (0, LANES - N_EXPERTS - N_GROUPS)).reshape(1, LANES)
    x1, h2, lg = _merge(ha, hbt, hct, x2, mp["g1"], mp["w_gates"], mp["wb"], mp["wo"],
                        norm2_g.reshape(1, D_MODEL), wr, br, _tile(n, 512))
    wgu = jnp.concatenate([w_e_gate, w_e_up], axis=-1).astype(BF16)
    return _moe(x1, h2, lg, wgu, w_e_down.astype(BF16), _tile(n, MOE_TM))


def kernel(x, norm1_g, w_in, a_conv_w, a_conv_b, a_gate_bias, a_out_norm_g, b_qk_norm_g, b_rel_bias,
           c_qk_norm_g, c_lambda, c_sub_norm_g, t5_bias, w_branch, w_out, norm2_g, w_group, b_group,
           w_router, b_router, w_e_gate, w_e_up, w_e_down):
    bsz, seq, _ = x.shape
    assert seq % DIFF_T == 0 and seq % MLSTM_L == 0 and seq % BAND_TQ == 0
    x2 = x.reshape(bsz * seq, D_MODEL)
    for l in range(norm1_g.shape[0]):
        mp = _mixer_params(l, norm1_g[l], w_in[l], a_conv_w[l], a_conv_b[l], a_gate_bias[l],
                           a_out_norm_g[l], b_qk_norm_g[l], b_rel_bias[l], c_qk_norm_g[l], c_lambda[l],
                           c_sub_norm_g[l], t5_bias, w_branch[l], w_out[l], seq // DIFF_T)
        x2 = _layer(x2, bsz, seq, mp, norm2_g[l], w_group[l], b_group[l], w_router[l], b_router[l],
                    w_e_gate[l], w_e_up[l], w_e_down[l])
    return x2.reshape(bsz, seq, D_MODEL)
```

```python
import functools
import math

import numpy as np
import jax
import jax.numpy as jnp
from jax import lax
from jax.experimental import pallas as pl
from jax.experimental.pallas import tpu as pltpu

F32 = jnp.float32
BF16 = jnp.bfloat16

D_MODEL = 1024
CHUNK = 64
EPS = 1e-6
NEG = -1e30
LOG2E = math.log2(math.e)

A_HEADS = 4
A_DH = 128
A_WIDTH = A_HEADS * A_DH
CONV_W = 4
GATE_CAP = 15.0

B_HEADS = 8
B_DH = 64
B_WIDTH = B_HEADS * B_DH
B_LEFT_CHUNKS = 8
B_MAX_REL = 256

C_HEADS = 4
C_DQK = 64
C_DV = 2 * C_DQK
C_WIDTH = C_HEADS * C_DV

T5_BUCKETS = 32
T5_MAX_DIST = 1024

N_BRANCH = 3
BRANCH_WIDTH = 512

N_GROUPS = 4
EXPERTS_PER_GROUP = 8
N_EXPERTS = N_GROUPS * EXPERTS_PER_GROUP
D_EXPERT = D_MODEL // 4

LANES = 128
SEG = 512
N_SEG = 10
VMEM_LIMIT = 48 * 1024 * 1024

SEG_AQ, SEG_AK, SEG_BK, SEG_CK = 0, 1, 2, 3
N_ROW_SEG = 4
TSEG_AV, TSEG_AO, TSEG_BQ, TSEG_BV, TSEG_CQ, TSEG_CV = 0, 1, 2, 3, 4, 5
N_T_SEG = N_SEG - N_ROW_SEG
SEG_PERM = (0, 1, 5, 8, 2, 3, 4, 6, 7, 9)

MLSTM_L = 128
CONV_HIST = 8
BAND_TQ = 128
BAND_NKB = 1 + (B_LEFT_CHUNKS * CHUNK) // BAND_TQ
BAND_ONES = 16
DIFF_T = 512
DIFF_ONES = 16
MOE_TM = 1024
MOE_C = 64
MOE_BIG = 4
MOE_EPS = EXPERTS_PER_GROUP
MOE_VMEM_LIMIT = 56 * 1024 * 1024

def _cparams(sem, flags=None):
    return pltpu.CompilerParams(dimension_semantics=sem, vmem_limit_bytes=VMEM_LIMIT, flags=flags)


NORM_SEGS = (SEG_BK, SEG_CK, N_ROW_SEG + TSEG_BQ, N_ROW_SEG + TSEG_CQ)
SIGMOID_SEGS = (N_ROW_SEG + TSEG_AO,)


def _head_norm_t(acc_t):
    rows, tm = acc_t.shape
    a3 = acc_t.reshape(rows // 64, 64, tm)
    ssq = jnp.sum(a3 * a3, axis=1, keepdims=True)
    return (a3 * lax.rsqrt(ssq * (1.0 / 64.0) + EPS)).reshape(rows, tm)


def _inproj_kernel(x_ref, g_ref, w_ref, wif_ref, gain_ref, gain_t_ref, p_ref, pt_ref, gif_ref):
    xf = x_ref[...]
    xn = (xf * lax.rsqrt(jnp.mean(xf * xf, axis=-1, keepdims=True) + EPS) * g_ref[...]).astype(BF16)
    gif_ref[...] = jnp.dot(xn, wif_ref[...], preferred_element_type=F32)
    for j in range(N_SEG):
        cols = slice(j * SEG, (j + 1) * SEG)
        acc = jnp.dot(xn, w_ref[:, cols], preferred_element_type=F32)
        if j in NORM_SEGS:
            acc_t = _head_norm_t(jnp.transpose(acc))
            if j < N_ROW_SEG:
                p_ref[:, cols] = (jnp.transpose(acc_t) * gain_ref[j]).astype(BF16)
            else:
                pt_ref[j - N_ROW_SEG] = (acc_t * gain_t_ref[NORM_SEGS.index(j) - 2]).astype(BF16)
            continue
        if j in SIGMOID_SEGS:
            acc = jax.nn.sigmoid(acc)
        if j < N_ROW_SEG:
            p_ref[:, cols] = acc.astype(BF16)
        else:
            pt_ref[j - N_ROW_SEG] = jnp.transpose(acc).astype(BF16)


def _inproj(x2, g, w, wif, gain, tm):
    n = x2.shape[0]
    tsegs = np.asarray([N_ROW_SEG + TSEG_BQ, N_ROW_SEG + TSEG_CQ])
    gain_t = jnp.broadcast_to(gain[tsegs, 0, :, None], (2, SEG, tm))

    def const(shape):
        return pl.BlockSpec(shape, lambda i: (0,) * len(shape), pipeline_mode=pl.Buffered(1))

    return pl.pallas_call(
        _inproj_kernel,
        grid=(n // tm,),
        in_specs=[
            pl.BlockSpec((tm, D_MODEL), lambda i: (i, 0)),
            const((1, D_MODEL)),
            const((D_MODEL, N_SEG * SEG)),
            const((D_MODEL, LANES)),
            const((N_SEG, 1, SEG)),
            const((2, SEG, tm)),
        ],
        out_specs=[
            pl.BlockSpec((tm, N_ROW_SEG * SEG), lambda i: (i, 0)),
            pl.BlockSpec((N_T_SEG, SEG, tm), lambda i: (0, 0, i)),
            pl.BlockSpec((tm, LANES), lambda i: (i, 0)),
        ],
        out_shape=[
            jax.ShapeDtypeStruct((n, N_ROW_SEG * SEG), BF16),
            jax.ShapeDtypeStruct((N_T_SEG, SEG, n), BF16),
            jax.ShapeDtypeStruct((n, LANES), F32),
        ],
        compiler_params=_cparams(("parallel",)),
        name="inproj",
    )(x2, g, w, wif, gain, gain_t)


def _log_sigmoid(z):
    return jnp.minimum(z, 0.0) - jnp.log(1.0 + jnp.exp(-jnp.abs(z)))


def _split3(a):
    hi = a.astype(BF16)
    r1 = a - hi.astype(F32)
    mid = r1.astype(BF16)
    lo = (r1 - mid.astype(F32)).astype(BF16)
    return hi, mid, lo


def _mlstm_kernel(*refs):
    _mlstm_carry(*refs)
    for _ in _mlstm_chunk(*refs):
        pass


def _mlstm_carry(aq_ref, ak_ref, vt_ref, aot_ref, gif_ref, gift_ref, cw_ref, cb_ref,
                 gbr_ref, gbc_ref, agt_ref, out_ref, ubuf, kq_ref, st_ref, ct_ref, n_ref, m_ref,
                 *, chunk=None):
    L = MLSTM_L
    c = pl.program_id(1) if chunk is None else chunk

    @pl.when(c == 0)
    def _():
        ubuf[0:CONV_HIST, :] = jnp.zeros((CONV_HIST, 2 * A_WIDTH), F32)
        ct_ref[...] = jnp.zeros_like(ct_ref)
        n_ref[...] = jnp.zeros_like(n_ref)
        m_ref[...] = jnp.zeros_like(m_ref)

    @pl.when(c > 0)
    def _():
        ubuf[0:CONV_HIST, :] = ubuf[L:L + CONV_HIST, :]


def _mlstm_chunk(aq_ref, ak_ref, vt_ref, aot_ref, gif_ref, gift_ref, cw_ref, cb_ref,
                 gbr_ref, gbc_ref, agt_ref, out_ref, ubuf, kq_ref, st_ref, ct_ref, n_ref, m_ref):
    L = MLSTM_L

    H = CONV_HIST
    ubuf[H:L + H, 0:A_WIDTH] = aq_ref[...].astype(F32)
    ubuf[H:L + H, A_WIDTH:2 * A_WIDTH] = ak_ref[...].astype(F32)
    y = cb_ref[...] + cw_ref[0:1, :] * ubuf[H:L + H, :]
    for t in range(1, CONV_W):
        y = y + cw_ref[t:t + 1, :] * ubuf[H - t:H - t + L, :]
    qk = y * jax.nn.sigmoid(y)
    q_t = jnp.transpose(qk[:, 0:A_WIDTH]).astype(BF16)
    k_all = (qk[:, A_WIDTH:2 * A_WIDTH] * (A_DH ** -0.5)).astype(BF16)

    zc = gif_ref[...] + gbr_ref[...]
    ig_c = GATE_CAP * jnp.tanh(zc * (1.0 / GATE_CAP))
    lf_c = _log_sigmoid(zc)
    zr = gift_ref[...] + gbc_ref[...]
    ig_r = GATE_CAP * jnp.tanh(zr * (1.0 / GATE_CAP))
    lf_r = _log_sigmoid(zr)

    row = lax.broadcasted_iota(jnp.int32, (L, L), 0)
    col = lax.broadcasted_iota(jnp.int32, (L, L), 1)
    causal = col <= row
    tril = jnp.where(causal, 1.0, 0.0).astype(BF16)
    triu = jnp.where(row <= col, 1.0, 0.0).astype(BF16)
    b_c = sum(jnp.dot(tril, piece, preferred_element_type=F32) for piece in _split3(lf_c))
    b_r = sum(jnp.dot(piece, triu, preferred_element_type=F32) for piece in _split3(lf_r))

    sub8 = lax.broadcasted_iota(jnp.int32, (8, L), 0)
    heads = [slice(h * A_DH, (h + 1) * A_DH) for h in range(A_HEADS)]
    for h, rows in enumerate(heads):
        kq_ref[h] = jnp.dot(k_all[:, rows], q_t[rows, :], preferred_element_type=F32)
    yield

    stats = []
    for h, rows in enumerate(heads):
        bcol = b_c[:, A_HEADS + h:A_HEADS + h + 1]
        brow = b_r[A_HEADS + h:A_HEADS + h + 1, :]
        igcol = ig_c[:, h:h + 1]
        m_prev = m_ref[h][:, 0:1]
        dmat = jnp.where(row <= col, brow + (igcol - bcol), NEG)
        inter = brow + m_prev
        m_t = jnp.maximum(inter, jnp.max(dmat, axis=0, keepdims=True))
        st = kq_ref[h] * jnp.exp(dmat - m_t)
        st_ref[h] = st.astype(BF16)
        stats.append((brow, m_prev, m_t, jnp.exp(inter - m_t), jnp.sum(st, axis=0, keepdims=True)))
        if h % 2 == 1:
            yield

    for h, rows in enumerate(heads):
        brow, m_prev, m_t, w_inter, st_sum = stats[h]
        qt = q_t[rows, :]
        k = k_all[:, rows]
        vt = vt_ref[rows, :]
        igrow = ig_r[h:h + 1, :]
        b_last = brow[:, L - 1:L]
        ct = ct_ref[h]
        n8 = n_ref[h]

        num = w_inter * jnp.dot(ct.astype(BF16), qt, preferred_element_type=F32)
        num = num + jnp.dot(vt, st_ref[h], preferred_element_type=F32)
        nq = jnp.dot(n8.astype(BF16), qt, preferred_element_type=F32)[0:1, :]
        den = w_inter * nq + st_sum
        hh = num * (1.0 / jnp.maximum(jnp.abs(den), jnp.exp(-m_t)))

        g_end = b_last - brow + igrow
        m_new = jnp.maximum(b_last + m_prev, jnp.max(g_end, axis=-1, keepdims=True))
        dec = jnp.exp(b_last + m_prev - m_new)
        w_end = jnp.exp(g_end - m_new)
        vw = (vt.astype(F32) * w_end).astype(BF16)
        ct_ref[h] = dec * ct + jnp.dot(vw, k, preferred_element_type=F32)
        w8 = jnp.where(sub8 == 0, w_end, 0.0).astype(BF16)
        n_ref[h] = dec * n8 + jnp.dot(w8, k, preferred_element_type=F32)
        m_ref[h] = jnp.broadcast_to(m_new, (1, LANES))

        hn = hh * lax.rsqrt(jnp.mean(hh * hh, axis=0, keepdims=True) + EPS) * agt_ref[rows, :]
        out_ref[rows, :] = (hn * aot_ref[rows, :].astype(F32)).astype(BF16)
        if h % 2 == 1 and h + 1 < A_HEADS:
            yield


def _mlstm_call(p, pt, gif, gift, cw, cb, gbr, gbc, ag, bsz, seq):
    L = MLSTM_L
    nc = seq // L
    n = bsz * seq
    agt = jnp.broadcast_to(ag.reshape(A_WIDTH, 1), (A_WIDTH, L))

    def tseg(j):
        return pl.BlockSpec((None, SEG, L), lambda b, c: (j, 0, b * nc + c))

    def full(shape):
        return pl.BlockSpec(shape, lambda b, c: (0,) * len(shape))

    in_specs = [
        pl.BlockSpec((L, SEG), lambda b, c: (b * nc + c, SEG_AQ)),
        pl.BlockSpec((L, SEG), lambda b, c: (b * nc + c, SEG_AK)),
        tseg(TSEG_AV), tseg(TSEG_AO),
        pl.BlockSpec((L, LANES), lambda b, c: (b * nc + c, 0)),
        pl.BlockSpec((8, L), lambda b, c: (0, b * nc + c)),
        full((CONV_W, 2 * A_WIDTH)), full((1, 2 * A_WIDTH)),
        full((1, LANES)), full((8, 1)), full((A_WIDTH, L)),
    ]
    scratch = [
        pltpu.VMEM((L + CONV_HIST, 2 * A_WIDTH), F32),
        pltpu.VMEM((A_HEADS, L, L), F32),
        pltpu.VMEM((A_HEADS, L, L), BF16),
        pltpu.VMEM((A_HEADS, A_DH, A_DH), F32),
        pltpu.VMEM((A_HEADS, 8, A_DH), F32),
        pltpu.VMEM((A_HEADS, 1, LANES), F32),
    ]
    return (in_specs, pl.BlockSpec((A_WIDTH, L), lambda b, c: (0, b * nc + c)),
            jax.ShapeDtypeStruct((A_WIDTH, n), BF16), scratch,
            (p, p, pt, pt, gif, gift, cw, cb, gbr, gbc, agt))


def _mlstm(p, pt, gif, gift, cw, cb, gbr, gbc, ag, bsz, seq):
    in_specs, out_spec, out_shape, scratch, operands = _mlstm_call(
        p, pt, gif, gift, cw, cb, gbr, gbc, ag, bsz, seq)
    return pl.pallas_call(
        _mlstm_kernel, grid=(bsz, seq // MLSTM_L), in_specs=in_specs, out_specs=out_spec,
        out_shape=out_shape, scratch_shapes=scratch,
        compiler_params=_cparams(("parallel", "arbitrary")), name="mlstm",
    )(*operands)


def _interleave(*gens):
    gens = list(gens)
    while gens:
        for g in list(gens):
            try:
                next(g)
            except StopIteration:
                gens.remove(g)
                continue
            yield


def _band_steps(refs, i, mask_start):
    nkb = BAND_NKB
    qt_ref = refs[0]
    k_refs = refs[1:1 + nkb]
    vt_refs = refs[1 + nkb:1 + 2 * nkb]
    bias_ref = refs[1 + 2 * nkb]
    out_ref = refs[2 + 2 * nkb]
    s_ref, mx_ref = refs[3 + 2 * nkb:5 + 2 * nkb]
    tq = BAND_TQ
    nk = nkb * tq

    k_all = jnp.concatenate([r[...] for r in k_refs], axis=0)
    vt_all = jnp.concatenate([r[...] for r in vt_refs], axis=1)
    ones = jnp.ones((BAND_ONES, nk), BF16)
    row = lax.broadcasted_iota(jnp.int32, (LANES, tq), 0)
    lo = row < B_DH
    if mask_start:
        kidx = lax.broadcasted_iota(jnp.int32, (nk, 1), 0)
        valid = (kidx + (i - (nkb - 1)) * tq) >= 0

    def score(p):
        rows = slice(p * LANES, (p + 1) * LANES)
        qtp = qt_ref[rows, :]
        zero = jnp.zeros_like(qtp)
        qbd = jnp.concatenate([jnp.where(lo, qtp, zero), jnp.where(lo, zero, qtp)], axis=1)
        s = jnp.dot(k_all[:, rows], qbd, preferred_element_type=F32) + bias_ref[p]
        if mask_start:
            s = jnp.where(valid, s, NEG)
        s_ref[p & 1] = s
        mx_ref[p & 1] = jnp.max(s, axis=0, keepdims=True)

    def finish(p):
        rows = slice(p * LANES, (p + 1) * LANES)
        pr = jnp.exp2((s_ref[p & 1] - mx_ref[p & 1]).astype(BF16))
        o = jnp.dot(jnp.concatenate([vt_all[rows, :], ones], axis=0), pr,
                    preferred_element_type=F32)
        o = o[0:LANES, :] / o[LANES:LANES + 1, :]
        out_ref[rows, :] = jnp.where(lo, o[:, 0:tq], o[:, tq:2 * tq]).astype(BF16)

    score(0)
    yield
    for p in range(B_HEADS // 2):
        if p + 1 < B_HEADS // 2:
            score(p + 1)
        finish(p)
        if p + 1 < B_HEADS // 2:
            yield


def _band_variants(i, region):
    @pl.when(i < BAND_NKB - 1)
    def _():
        region(True)

    @pl.when(i >= BAND_NKB - 1)
    def _():
        region(False)


def _drain(gen):
    for _ in gen:
        pass


def _band_kernel(*refs):
    i = pl.program_id(1)
    _band_variants(i, lambda mask: _drain(_band_steps(refs, i, mask)))


def _band_call(p, pt, bias, bsz, seq):
    tq = BAND_TQ
    nkb = BAND_NKB
    nq = seq // tq
    n = bsz * seq

    def kblk(d):
        return pl.BlockSpec((tq, SEG), lambda b, i: (b * nq + jnp.maximum(i - d, 0), SEG_BK))

    def vblk(d):
        return pl.BlockSpec((None, SEG, tq), lambda b, i: (TSEG_BV, 0, b * nq + jnp.maximum(i - d, 0)))

    in_specs = [pl.BlockSpec((None, SEG, tq), lambda b, i: (TSEG_BQ, 0, b * nq + i))]
    in_specs += [kblk(d) for d in range(nkb - 1, -1, -1)]
    in_specs += [vblk(d) for d in range(nkb - 1, -1, -1)]
    in_specs += [pl.BlockSpec(bias.shape, lambda b, i: (0, 0, 0))]
    scratch = [
        pltpu.VMEM((2, nkb * tq, 2 * tq), F32),
        pltpu.VMEM((2, 1, 2 * tq), F32),
    ]
    return (in_specs, pl.BlockSpec((B_WIDTH, tq), lambda b, i: (0, b * nq + i)),
            jax.ShapeDtypeStruct((B_WIDTH, n), BF16), scratch,
            (pt, *([p] * nkb), *([pt] * nkb), bias))


def _band(p, pt, bias, bsz, seq):
    in_specs, out_spec, out_shape, scratch, operands = _band_call(p, pt, bias, bsz, seq)
    return pl.pallas_call(
        _band_kernel, grid=(bsz, seq // BAND_TQ), in_specs=in_specs, out_specs=out_spec,
        out_shape=out_shape, scratch_shapes=scratch,
        compiler_params=_cparams(("parallel", "parallel")), name="band_attn",
    )(*operands)


def _toeplitz(base, m, n):
    period = base.shape[-1]
    assert n <= period - 1
    reps = (1,) * (base.ndim - 1) + (m,)
    big = jnp.tile(base, reps)[..., :m * (period - 1)]
    return big.reshape(base.shape[:-1] + (m, period - 1))[..., :n]


def _band_bias(b_rel):
    tq = BAND_TQ
    nk = BAND_NKB * tq
    period = tq + nk
    e = np.arange(period)
    e = np.where(e < nk, e, e - period)
    rel = np.clip((nk - tq) - e, -B_MAX_REL, B_MAX_REL) + B_MAX_REL
    bias = _toeplitz(jnp.transpose(b_rel[rel]).astype(F32), tq, nk)
    qpos = np.arange(tq)
    kpos = np.arange(nk) - (nk - tq)
    qc = qpos[:, None] // CHUNK
    kc = np.floor_divide(kpos[None, :], CHUNK)
    allowed = (kc <= qc) & (kc >= qc - B_LEFT_CHUNKS)
    bias = jnp.where(allowed[None], bias * LOG2E, NEG)
    return jnp.swapaxes(bias.reshape(B_HEADS // 2, 2 * tq, nk), 1, 2)


def _diff_kernel(lam_ref, cfar_ref, qt_ref, k_ref, vt_ref, bias_ref, g_ref, out_ref,
                 qbd_ref, s_ref, mx_ref, acc_ref, m_ref, *, nnear, first_region=None):
    T = DIFF_T
    h = pl.program_id(1)
    qi = pl.program_id(2)
    cfar = cfar_ref[h]

    def prologue():
        qt = qt_ref[...]
        row = lax.broadcasted_iota(jnp.int32, (2 * C_DQK, T), 0)
        zero = jnp.zeros_like(qt)
        qbd_ref[:, 0:T] = jnp.where(row < C_DQK, qt, zero)
        qbd_ref[:, T:2 * T] = jnp.where(row < C_DQK, zero, qt)
        m_ref[...] = jnp.full(m_ref.shape, NEG, F32)
        acc_ref[...] = jnp.zeros_like(acc_ref)

    def near_bias(t):
        return bias_ref[t] if t < nnear else None

    def stage_a(t, slot, bias, maps=(0, 1)):
        j = jnp.maximum(qi - t, 0)
        k = k_ref[pl.ds(pl.multiple_of(j * T, T), T), :]
        for mp in maps:
            sm = jnp.dot(k, qbd_ref[:, mp * T:(mp + 1) * T], preferred_element_type=F32)
            if bias is not None:
                sm = sm + bias
            s_ref[slot, mp] = sm
            mx = jnp.max(sm, axis=0, keepdims=True)
            mx_ref[slot, mp] = mx + cfar if bias is None else mx

    def stage_bc(t, slot, far, maps=(0, 1)):
        j = qi - t
        vt = jnp.concatenate([vt_ref[:, pl.ds(pl.multiple_of(j * T, T), T)],
                              jnp.ones((DIFF_ONES, T), BF16)], axis=0)
        for mp in maps:
            m_old = m_ref[mp]
            m_new = jnp.maximum(m_old, mx_ref[slot, mp])
            shift = m_new - cfar if far else m_new
            pr = jnp.exp2((s_ref[slot, mp] - shift).astype(BF16))
            m_ref[mp] = m_new
            acc_ref[mp] = jnp.exp2(m_old - m_new) * acc_ref[mp] + jnp.dot(
                vt, pr, preferred_element_type=F32)

    def full_step(t, slot, far, next_bias):
        for mp in range(2):
            stage_a(t + 1, 1 - slot, next_bias, (mp,))
            stage_bc(t, slot, far, (mp,))

    def first_steps(other, n_steps=1):
        prologue()
        next(other, None)
        stage_a(0, 0, near_bias(0))
        next(other, None)
        for t in range(n_steps):
            for mp in range(2):
                stage_a(t + 1, 1 - (t & 1), near_bias(t + 1), (mp,))
                stage_bc(t, t & 1, False, (mp,))
                next(other, None)
        _drain(other)

    all_near = qi >= nnear - 1
    if first_region is None:
        first_steps(iter(()))
        near_done = False
    else:
        near_done = first_region(first_steps, all_near, nnear)
    for t in range(1, nnear):
        @pl.when((qi >= t) & jnp.logical_not(near_done))
        def _(t=t):
            full_step(t, t & 1, False, near_bias(t + 1))

    @pl.when(qi >= nnear)
    def _():
        n_full = qi - nnear
        s0 = nnear & 1

        def pair(i, carry):
            full_step(nnear + 2 * i, s0, True, None)
            full_step(nnear + 2 * i + 1, 1 - s0, True, None)
            return carry

        lax.fori_loop(0, n_full // 2, pair, 0)

        @pl.when(n_full % 2 == 1)
        def _():
            full_step(qi - 1, s0, True, None)

        stage_bc(qi, qi & 1, True)

    o1 = acc_ref[0, 0:C_DV, :] / acc_ref[0, C_DV:C_DV + 1, :]
    o2 = acc_ref[1, 0:C_DV, :] / acc_ref[1, C_DV:C_DV + 1, :]
    o = o1 - lam_ref[0] * o2
    on = o * lax.rsqrt(jnp.mean(o * o, axis=0, keepdims=True) + EPS) * g_ref[...]
    out_ref[...] = on.astype(BF16)


def _diff_nnear(nq):
    d_sat = -(-(T5_MAX_DIST - 1 + DIFF_T) // DIFF_T)
    return min(d_sat, nq)


def _diff_call(p, pt, bias, cfar, lam, gsub, bsz, seq):
    T = DIFF_T
    nq = seq // T
    nnear = bias.shape[1]
    kcol = SEG_CK * SEG // LANES
    n = bsz * seq
    in_specs = [
        pl.BlockSpec(memory_space=pltpu.SMEM),
        pl.BlockSpec(memory_space=pltpu.SMEM),
        pl.BlockSpec((None, 2 * C_DQK, T), lambda b, h, i: (TSEG_CQ, h, b * nq + i)),
        pl.BlockSpec((seq, LANES), lambda b, h, i: (b, kcol + h)),
        pl.BlockSpec((None, C_DV, seq), lambda b, h, i: (TSEG_CV, h, b)),
        pl.BlockSpec((None, nnear, T, T), lambda b, h, i: (h, 0, 0, 0)),
        pl.BlockSpec((C_DV, 1), lambda b, h, i: (0, 0)),
    ]
    scratch = [
        pltpu.VMEM((2 * C_DQK, 2 * T), BF16),
        pltpu.VMEM((2, 2, T, T), F32),
        pltpu.VMEM((2, 2, 1, T), F32),
        pltpu.VMEM((2, C_DV + DIFF_ONES, T), F32),
        pltpu.VMEM((2, 1, T), F32),
    ]
    return (in_specs, pl.BlockSpec((C_DV, T), lambda b, h, i: (h, b * nq + i)),
            jax.ShapeDtypeStruct((C_WIDTH, n), BF16), scratch, (lam, cfar, pt, p, pt, bias, gsub))


def _diff(p, pt, bias, cfar, lam, gsub, bsz, seq):
    in_specs, out_spec, out_shape, scratch, operands = _diff_call(p, pt, bias, cfar, lam, gsub, bsz, seq)
    return pl.pallas_call(
        functools.partial(_diff_kernel, nnear=bias.shape[1]),
        grid=(bsz, C_HEADS, seq // DIFF_T), in_specs=in_specs, out_specs=out_spec,
        out_shape=out_shape, scratch_shapes=scratch,
        compiler_params=_cparams(("parallel", "parallel", "arbitrary")), name="diff_attn",
    )(*operands)


def _mixers_kernel(*refs, n_in, n_scr, nnear):
    d_in, m_in, b_in = (refs[sum(n_in[:k]):sum(n_in[:k + 1])] for k in range(3))
    hct_ref, hat_ref, hbt_ref = refs[sum(n_in):sum(n_in) + 3]
    scr = refs[sum(n_in) + 3:]
    d_scr, m_scr, b_scr = (scr[sum(n_scr[:k]):sum(n_scr[:k + 1])] for k in range(3))
    m_refs = (*m_in, hat_ref, *m_scr)
    b_refs = (*b_in, hbt_ref, *b_scr)
    chunk = pl.program_id(1) * pl.num_programs(2) + pl.program_id(2)
    _mlstm_carry(*m_refs, chunk=chunk)

    def first_region(trace, all_near, n_near):
        def other(mask):
            return _interleave(_band_steps(b_refs, chunk, mask), _mlstm_chunk(*m_refs))

        masked = chunk < BAND_NKB - 1
        wide = jnp.logical_not(masked) & all_near

        @pl.when(masked)
        def _():
            trace(other(True), 1)

        @pl.when(wide)
        def _():
            trace(other(False), n_near)

        @pl.when(jnp.logical_not(masked) & jnp.logical_not(all_near))
        def _():
            trace(other(False), 1)

        return wide

    _diff_kernel(*d_in, hct_ref, *d_scr, nnear=nnear, first_region=first_region)


def _mixers(p, pt, gif, gift, mp, bsz, seq):
    nq = seq // DIFF_T
    assert C_HEADS * nq == seq // MLSTM_L and MLSTM_L == BAND_TQ

    def on_diff_grid(spec):
        if spec.index_map is None:
            return spec
        return pl.BlockSpec(spec.block_shape, lambda b, h, i, f=spec.index_map: f(b, h * nq + i))

    d_specs, d_out, d_shape, d_scr, d_ops = _diff_call(p, pt, mp["diff_bias"], mp["diff_far"], mp["lam"],
                                                       mp["gsub"], bsz, seq)
    m_specs, m_out, m_shape, m_scr, m_ops = _mlstm_call(p, pt, gif, gift, mp["cw"], mp["cb"], mp["gbr"],
                                                        mp["gbc"], mp["ag"], bsz, seq)
    b_specs, b_out, b_shape, b_scr, b_ops = _band_call(p, pt, mp["band_bias"], bsz, seq)
    hct, hat, hbt = pl.pallas_call(
        functools.partial(_mixers_kernel, n_in=(len(d_specs), len(m_specs), len(b_specs)),
                          n_scr=(len(d_scr), len(m_scr), len(b_scr)), nnear=mp["diff_bias"].shape[1]),
        grid=(bsz, C_HEADS, nq),
        in_specs=d_specs + [on_diff_grid(s) for s in m_specs + b_specs],
        out_specs=[d_out, on_diff_grid(m_out), on_diff_grid(b_out)],
        out_shape=[d_shape, m_shape, b_shape],
        scratch_shapes=d_scr + m_scr + b_scr,
        compiler_params=_cparams(("parallel", "arbitrary", "arbitrary")),
        name="mixers",
    )(*d_ops, *m_ops, *b_ops)
    return hat, hbt, hct


def _t5_bucket(rel):
    nb = T5_BUCKETS // 2
    max_exact = nb // 2
    ret = (rel > 0).astype(jnp.int32) * nb
    n = jnp.abs(rel)
    large = max_exact + (jnp.log(jnp.maximum(n, max_exact).astype(F32) / max_exact)
                         / math.log(T5_MAX_DIST / max_exact) * (nb - max_exact)).astype(jnp.int32)
    large = jnp.minimum(large, nb - 1)
    return ret + jnp.where(n < max_exact, n, large)


def _diff_bias(t5_table, nnear):
    T = DIFF_T
    e = np.arange(2 * T)
    amc = np.where(e < T, -e, 2 * T - e)
    rel = jnp.asarray(-np.arange(nnear)[:, None] * T + amc[None, :], jnp.int32)
    base = jnp.moveaxis(t5_table[_t5_bucket(rel)], -1, 0).astype(F32) * LOG2E
    tiles = _toeplitz(base, T, T)
    a = np.arange(T)[:, None]
    c = np.arange(T)[None, :]
    allowed = np.ones((nnear, T, T), bool)
    allowed[0] = (a // CHUNK) <= (c // CHUNK)
    far = t5_table[_t5_bucket(jnp.asarray(-T5_MAX_DIST, jnp.int32))].astype(F32) * LOG2E
    return jnp.where(allowed[None], tiles, NEG), far


def _merge_kernel(hat_ref, hbt_ref, hct_ref, x_ref, n1_ref, wg_ref, wb_ref, wo_ref,
                  n2_ref, wr_ref, br_ref, x1_ref, h2_ref, lg_ref, y_ref):
    tn = (((0,), (0,)), ((), ()))
    tm = x_ref.shape[0]
    halves = [slice(i * (tm // 2), (i + 1) * (tm // 2)) for i in range(2)]
    for r in halves:
        xf = x_ref[r, :]
        xn = (xf * lax.rsqrt(jnp.mean(xf * xf, axis=-1, keepdims=True) + EPS) * n1_ref[...]).astype(BF16)
        y = None
        for b, ht_ref in enumerate((hat_ref, hbt_ref, hct_ref)):
            gate = jax.nn.sigmoid(jnp.dot(xn, wg_ref[:, b * D_MODEL:(b + 1) * D_MODEL],
                                          preferred_element_type=F32))
            yb = gate * lax.dot_general(ht_ref[:, r], wb_ref[b], tn, preferred_element_type=F32)
            y = yb if y is None else y + yb
        y_ref[r, :] = y.astype(BF16)
    for r in halves:
        x1 = x_ref[r, :] + jnp.dot(y_ref[r, :], wo_ref[...], preferred_element_type=F32)
        x1_ref[r, :] = x1
        h2 = x1 * lax.rsqrt(jnp.mean(x1 * x1, axis=-1, keepdims=True) + EPS) * n2_ref[...]
        h2_ref[r, :] = h2.astype(BF16)
    for r in halves:
        lg_ref[r, :] = jnp.dot(h2_ref[r, :], wr_ref[...], preferred_element_type=F32) + br_ref[...]


def _merge(ha, hbt, hct, x2, n1, wg, wb, wo, n2, wr, br, tm):
    n = x2.shape[0]

    def rows(width, col=0):
        return pl.BlockSpec((tm, width), lambda i: (i, col))

    def cols():
        return pl.BlockSpec((BRANCH_WIDTH, tm), lambda i: (0, i))

    def full(shape):
        return pl.BlockSpec(shape, lambda i: (0,) * len(shape), pipeline_mode=pl.Buffered(1))

    return pl.pallas_call(
        _merge_kernel,
        grid=(n // tm,),
        in_specs=[
            cols(), cols(), cols(),
            rows(D_MODEL),
            full((1, D_MODEL)), full((D_MODEL, N_BRANCH * D_MODEL)),
            full((N_BRANCH, BRANCH_WIDTH, D_MODEL)), full((D_MODEL, D_MODEL)),
            full((1, D_MODEL)), full((D_MODEL, LANES)), full((1, LANES)),
        ],
        out_specs=[rows(D_MODEL), rows(D_MODEL), rows(LANES)],
        out_shape=[
            jax.ShapeDtypeStruct((n, D_MODEL), F32),
            jax.ShapeDtypeStruct((n, D_MODEL), BF16),
            jax.ShapeDtypeStruct((n, LANES), F32),
        ],
        scratch_shapes=[pltpu.VMEM((tm, D_MODEL), BF16)],
        compiler_params=_cparams(("parallel",)),
        name="merge",
    )(ha, hbt, hct, x2, n1, wg, wb, wo, n2, wr, br)


def _combine_weights(lg):
    lanef = lax.broadcasted_iota(jnp.int32, lg.shape, 1).astype(F32)
    big = 1e9
    is_g = (lanef >= N_EXPERTS) & (lanef < N_EXPERTS + N_GROUPS)
    gl = jnp.where(is_g, lg, -jnp.inf)
    gmax = jnp.max(gl, axis=-1, keepdims=True)
    g_idx = jnp.min(jnp.where(gl == gmax, lanef - N_EXPERTS, big), axis=-1, keepdims=True)
    p_g = 1.0 / jnp.sum(jnp.exp(gl - gmax), axis=-1, keepdims=True)
    in_grp = (lanef >= g_idx * EXPERTS_PER_GROUP) & (lanef < (g_idx + 1.0) * EXPERTS_PER_GROUP)
    el = jnp.where(in_grp, lg, -jnp.inf)
    ee = jnp.exp(el - jnp.max(el, axis=-1, keepdims=True))
    ep = ee / jnp.sum(ee, axis=-1, keepdims=True)
    ep = jnp.where(in_grp, ep, -1.0)
    v1 = jnp.max(ep, axis=-1, keepdims=True)
    i1 = jnp.min(jnp.where(ep == v1, lanef, big), axis=-1, keepdims=True)
    ep2 = jnp.where(lanef == i1, -1.0, ep)
    v2 = jnp.max(ep2, axis=-1, keepdims=True)
    i2 = jnp.min(jnp.where(ep2 == v2, lanef, big), axis=-1, keepdims=True)
    tot = v1 + v2
    comb = jnp.where(lanef == i1, p_g * (v1 / tot), 0.0) + jnp.where(lanef == i2, p_g * (v2 / tot), 0.0)
    return comb, g_idx


def _moe_kernel(x1_hbm, h2_hbm, lg_hbm, tri_ref, wgu_ref, wd_ref, out_hbm,
                pt_ref, xs_ref, ws_ref, ys_ref, meta_ref, x1_buf, x1_sem, h2_ref, lg_ref, in_sem,
                out_buf, out_sem, *, rcap):
    C = MOE_C
    blk = pl.program_id(0)
    s = pl.program_id(1)
    tm = x1_buf.shape[0]

    def x1_copy():
        rows = pl.ds(pl.multiple_of(blk * tm, tm), tm)
        return pltpu.make_async_copy(x1_hbm.at[rows, :], x1_buf, x1_sem)

    def in_copies(block):
        rows = pl.ds(pl.multiple_of(block * tm, tm), tm)
        return (pltpu.make_async_copy(h2_hbm.at[rows, :], h2_ref, in_sem.at[0]),
                pltpu.make_async_copy(lg_hbm.at[rows, :], lg_ref, in_sem.at[1]))

    @pl.when((s == 0) & (blk == 0))
    def _():
        for cp in in_copies(blk):
            cp.start()

    @pl.when((s == 1) & (blk + 1 < pl.num_programs(0)))
    def _():
        for cp in in_copies(blk + 1):
            cp.start()

    @pl.when(s == 0)
    def _():
        x1_copy().start()
        for cp in in_copies(blk):
            cp.wait()

    @pl.when(s == 0)
    def _():
        comb, g_idx = _combine_weights(lg_ref[...])
        lanef = lax.broadcasted_iota(jnp.int32, comb.shape, 1).astype(F32)
        mine = lanef == g_idx
        onehot = jnp.where(mine, 1.0, 0.0)
        ranks = jnp.dot(tri_ref[...], onehot.astype(BF16), preferred_element_type=F32)
        dest = jnp.sum(jnp.where(mine, ranks, 0.0), axis=-1, keepdims=True)
        off = jnp.int32(0)
        for g in range(N_GROUPS):
            cnt = jnp.sum(onehot[:, g:g + 1]).astype(jnp.int32)
            nchunk = (cnt + (C - 1)) // C
            meta_ref[g] = off
            meta_ref[N_GROUPS + g] = nchunk
            dest = dest + jnp.where(g_idx == float(g), (off * C).astype(F32), 0.0)
            off = off + nchunk
        slot = lax.broadcasted_iota(jnp.int32, (tm, rcap), 1).astype(F32)
        pt = jnp.where(dest == slot, 1.0, 0.0).astype(BF16)
        pt_ref[...] = pt
        comb_hi = comb.astype(BF16)
        comb_lo = (comb - comb_hi.astype(F32)).astype(BF16)
        packed = jnp.concatenate([h2_ref[...], comb_hi, comb_lo], axis=1)
        srt = lax.dot_general(pt, packed, (((0,), (0,)), ((), ())), preferred_element_type=F32)
        xs_ref[...] = srt[:, 0:D_MODEL].astype(BF16)
        ws_ref[...] = srt[:, D_MODEL:D_MODEL + LANES] + srt[:, D_MODEL + LANES:D_MODEL + 2 * LANES]
        ys_ref[...] = jnp.zeros_like(ys_ref)

    def do_chunk(r0, m):
        r0 = pl.multiple_of(r0, C)
        xc = xs_ref[pl.ds(r0, m), :]
        wsc = ws_ref[pl.ds(r0, m), :]
        lane = lax.broadcasted_iota(jnp.int32, (m, LANES), 1)
        y = None
        for k in range(MOE_EPS):
            gu = jnp.dot(xc, wgu_ref[k], preferred_element_type=F32)
            gate = gu[:, 0:D_EXPERT]
            wk = jnp.sum(jnp.where(lane == s * MOE_EPS + k, wsc, 0.0), axis=-1, keepdims=True)
            he = (gate * jax.nn.sigmoid(gate) * gu[:, D_EXPERT:2 * D_EXPERT] * wk).astype(BF16)
            yk = jnp.dot(he, wd_ref[k], preferred_element_type=F32)
            y = yk if y is None else y + yk
        ys_ref[pl.ds(r0, m), :] = y.astype(BF16)

    g = s // (EXPERTS_PER_GROUP // MOE_EPS)
    start = meta_ref[g]
    nchunk = meta_ref[N_GROUPS + g]

    def big(i, carry):
        do_chunk((start + MOE_BIG * i) * C, MOE_BIG * C)
        return carry

    common = nchunk == MOE_BIG + 1

    @pl.when(common)
    def _():
        do_chunk(start * C, (MOE_BIG + 1) * C)

    @pl.when(jnp.logical_not(common))
    def _():
        nbig = nchunk // MOE_BIG
        lax.fori_loop(0, nbig, big, 0)
        done = nbig * MOE_BIG
        size = MOE_BIG // 2
        while size >= 1:
            @pl.when((nchunk & size) != 0)
            def _(done=done, size=size):
                do_chunk((start + done) * C, size * C)
            done = done + (nchunk & size)
            size //= 2

    @pl.when(s == pl.num_programs(1) - 1)
    def _():
        x1_copy().wait()

        def out_copy(block):
            rows = pl.ds(pl.multiple_of(block * tm, tm), tm)
            return pltpu.make_async_copy(out_buf, out_hbm.at[rows, :], out_sem)

        @pl.when(blk > 0)
        def _():
            out_copy(blk - 1).wait()

        out_buf[...] = x1_buf[...] + jnp.dot(pt_ref[...], ys_ref[...], preferred_element_type=F32)
        out_copy(blk).start()

        @pl.when(blk == pl.num_programs(0) - 1)
        def _():
            out_copy(blk).wait()


def _moe(x1, h2, lg, wgu, wd, tm):
    n = x1.shape[0]
    rcap = (tm + N_GROUPS * (MOE_C - 1)) // MOE_C * MOE_C
    rcap = -(-rcap // LANES) * LANES
    idx = np.arange(tm)
    tri = jnp.asarray(idx[None, :] < idx[:, None], BF16)
    return pl.pallas_call(
        functools.partial(_moe_kernel, rcap=rcap),
        grid=(n // tm, N_EXPERTS // MOE_EPS),
        in_specs=[
            pl.BlockSpec(memory_space=pl.ANY),
            pl.BlockSpec(memory_space=pl.ANY),
            pl.BlockSpec(memory_space=pl.ANY),
            pl.BlockSpec((tm, tm), lambda i, s: (0, 0), pipeline_mode=pl.Buffered(1)),
            pl.BlockSpec((MOE_EPS, D_MODEL, 2 * D_EXPERT), lambda i, s: (s, 0, 0)),
            pl.BlockSpec((MOE_EPS, D_EXPERT, D_MODEL), lambda i, s: (s, 0, 0)),
        ],
        out_specs=pl.BlockSpec(memory_space=pl.ANY),
        out_shape=jax.ShapeDtypeStruct((n, D_MODEL), F32),
        scratch_shapes=[
            pltpu.VMEM((tm, rcap), BF16),
            pltpu.VMEM((rcap, D_MODEL), BF16),
            pltpu.VMEM((rcap, LANES), F32),
            pltpu.VMEM((rcap, D_MODEL), BF16),
            pltpu.SMEM((2 * N_GROUPS,), jnp.int32),
            pltpu.VMEM((tm, D_MODEL), F32),
            pltpu.SemaphoreType.DMA,
            pltpu.VMEM((tm, D_MODEL), BF16),
            pltpu.VMEM((tm, LANES), F32),
            pltpu.SemaphoreType.DMA((2,)),
            pltpu.VMEM((tm, D_MODEL), F32),
            pltpu.SemaphoreType.DMA,
        ],
        compiler_params=pltpu.CompilerParams(dimension_semantics=("arbitrary", "arbitrary"),
                                             vmem_limit_bytes=MOE_VMEM_LIMIT),
        name="moe",
    )(x1, h2, lg, tri, wgu, wd)


def _tile(n, pref):
    t = pref
    while n % t:
        t //= 2
    return t


def _mixer_params(layer, norm1_g, w_in, a_conv_w, a_conv_b, a_gate_bias, a_out_norm_g,
                  b_qk_norm_g, b_rel_bias, c_qk_norm_g, c_lambda, c_sub_norm_g, t5_bias,
                  w_branch, w_out, nq_diff):
    n_small = 2 * A_HEADS
    cut = 4 * A_WIDTH
    n_proj = N_SEG * SEG
    w_main = jnp.concatenate([w_in[:, :cut], w_in[:, cut + n_small:n_small + n_proj]], axis=1)
    w_main = w_main.reshape(D_MODEL, N_SEG, SEG)[:, np.asarray(SEG_PERM), :]
    w_main = w_main.reshape(D_MODEL, n_proj).astype(BF16)
    w_gates = w_in[:, n_small + n_proj:].astype(BF16)
    w_if = jnp.pad(w_in[:, cut:cut + n_small], ((0, 0), (0, LANES - n_small))).astype(BF16)
    gain = jnp.ones((N_SEG, SEG), F32)
    gain = gain.at[N_ROW_SEG + TSEG_BQ].set(jnp.tile(b_qk_norm_g[0], B_HEADS) * (B_DH ** -0.5 * LOG2E))
    gain = gain.at[SEG_BK].set(jnp.tile(b_qk_norm_g[1], B_HEADS))
    gain = gain.at[N_ROW_SEG + TSEG_CQ].set(
        jnp.tile(c_qk_norm_g[0], 2 * C_HEADS) * (C_DQK ** -0.5 * LOG2E))
    gain = gain.at[SEG_CK].set(jnp.tile(c_qk_norm_g[1], 2 * C_HEADS))
    diff_bias, diff_far = _diff_bias(t5_bias, _diff_nnear(nq_diff))
    lam_init = 0.8 - 0.6 * math.exp(-0.3 * layer)
    lf32 = c_lambda.astype(F32)
    lam = jnp.exp(jnp.sum(lf32[0] * lf32[1])) - jnp.exp(jnp.sum(lf32[2] * lf32[3])) + lam_init
    return dict(
        g1=norm1_g.reshape(1, D_MODEL), w_main=w_main, w_if=w_if, w_gates=w_gates,
        gain=gain.reshape(N_SEG, 1, SEG),
        cw=a_conv_w, cb=a_conv_b.reshape(1, -1),
        gbr=jnp.pad(a_gate_bias, (0, LANES - n_small)).reshape(1, LANES),
        gbc=a_gate_bias.reshape(n_small, 1),
        ag=a_out_norm_g.reshape(1, A_WIDTH),
        band_bias=_band_bias(b_rel_bias),
        diff_bias=diff_bias, diff_far=diff_far,
        lam=lam.reshape(1).astype(F32),
        gsub=(c_sub_norm_g * (1.0 - lam_init)).reshape(C_DV, 1),
        wb=w_branch.astype(BF16), wo=w_out.astype(BF16),
    )


def _layer(x2, bsz, seq, mp, norm2_g, w_group, b_group, w_router, b_router, w_e_gate, w_e_up, w_e_down):
    n = bsz * seq
    p, pt, gif = _inproj(x2, mp["g1"], mp["w_main"], mp["w_if"], mp["gain"], _tile(n, 512))
    gift = jnp.transpose(gif[:, :2 * A_HEADS])
    ha, hbt, hct = _mixers(p, pt, gif, gift, mp, bsz, seq)

    wr = jnp.concatenate([w_router, w_group], axis=1)
    wr = jnp.pad(wr, ((0, 0), (0, LANES - wr.shape[1]))).astype(BF16)
    br = jnp.pad(jnp.concatenate([b_router, b_group]), (0, LANES - N_EXPERTS - N_GROUPS)).reshape(1, LANES)
    x1, h2, lg = _merge(ha, hbt, hct, x2, mp["g1"], mp["w_gates"], mp["wb"], mp["wo"],
                        norm2_g.reshape(1, D_MODEL), wr, br, _tile(n, 512))
    wgu = jnp.concatenate([w_e_gate, w_e_up], axis=-1).astype(BF16)
    return _moe(x1, h2, lg, wgu, w_e_down.astype(BF16), _tile(n, MOE_TM))


def kernel(x, norm1_g, w_in, a_conv_w, a_conv_b, a_gate_bias, a_out_norm_g, b_qk_norm_g, b_rel_bias,
           c_qk_norm_g, c_lambda, c_sub_norm_g, t5_bias, w_branch, w_out, norm2_g, w_group, b_group,
           w_router, b_router, w_e_gate, w_e_up, w_e_down):
    bsz, seq, _ = x.shape
    assert seq % DIFF_T == 0 and seq % MLSTM_L == 0 and seq % BAND_TQ == 0
    x2 = x.reshape(bsz * seq, D_MODEL)
    for l in range(norm1_g.shape[0]):
        mp = _mixer_params(l, norm1_g[l], w_in[l], a_conv_w[l], a_conv_b[l], a_gate_bias[l],
                           a_out_norm_g[l], b_qk_norm_g[l], b_rel_bias[l], c_qk_norm_g[l], c_lambda[l],
                           c_sub_norm_g[l], t5_bias, w_branch[l], w_out[l], seq // DIFF_T)
        x2 = _layer(x2, bsz, seq, mp, norm2_g[l], w_group[l], b_group[l], w_router[l], b_router[l],
                    w_e_gate[l], w_e_up[l], w_e_down[l])
    return x2.reshape(bsz, seq, D_MODEL)
```
